```python
import jax, jax.numpy as jnp
from jax import lax
import numpy as np

D_MODEL = 2048
BATCH = 8
SEQ = 8192
DEPTH = 4

GRID_W = 64
CTX_LEN = 256
N_MIXERS = 2
HEAD_DIM = 128
N_HEADS = D_MODEL // HEAD_DIM
N_KV_HEADS = N_HEADS // 4
GROUP = N_HEADS // N_KV_HEADS
D_Q = N_HEADS * HEAD_DIM
D_KV = N_KV_HEADS * HEAD_DIM
WINDOW = 128
BLOCK = 128
ROPE_PAIRS = HEAD_DIM // 4
ROPE_BASE = 10000.0
CONV_W = 3
D_FF = ((8 * D_MODEL // 3 + 255) // 256) * 256
N_ATTN_LAYERS = (DEPTH + N_MIXERS - 1) // N_MIXERS
N_CONV_LAYERS = DEPTH // N_MIXERS
EPS = 1e-6
NEG_INF = -1e30

kernel_name = "hybrid_swa_shortconv_dit_prefix"


def rms_norm(x):
    xf = x.astype(jnp.float32)
    return (xf * lax.rsqrt(jnp.mean(xf * xf, axis=-1, keepdims=True) + EPS)).astype(x.dtype)


def modulate(x, shift, scale):
    return rms_norm(x) * (1 + scale) + shift


def adaln(cond_act, w, b):
    return jnp.split(cond_act @ w + b, 6, axis=-1)


def dwconv3(x, w):
    xp = jnp.pad(x, ((0, 0), (1, 1), (0, 0)))
    return xp[:, :-2] * w[0] + xp[:, 1:-1] * w[1] + xp[:, 2:] * w[2]


def rope_tables(n):
    rows = n // GRID_W
    row = jnp.repeat(jnp.arange(rows), GRID_W).astype(jnp.float32)
    col = jnp.tile(jnp.arange(GRID_W), rows).astype(jnp.float32)
    inv = ROPE_BASE ** (-jnp.arange(ROPE_PAIRS, dtype=jnp.float32) / ROPE_PAIRS)
    ang = jnp.stack([row[:, None] * inv, col[:, None] * inv], axis=1)
    ang = jnp.broadcast_to(ang[:, :, None, :], (n, 2, 2, ROPE_PAIRS)).reshape(n, HEAD_DIM)
    return jnp.cos(ang), jnp.sin(ang)


def apply_rope(x, cos, sin):
    xf = x.astype(jnp.float32)
    xr = xf.reshape(*x.shape[:-1], 2, 2, ROPE_PAIRS)
    rot = jnp.stack([-xr[..., 1, :], xr[..., 0, :]], axis=-2).reshape(x.shape)
    return (xf * cos[:, None, :] + rot * sin[:, None, :]).astype(x.dtype)


def band_mask(n):
    nb = n // BLOCK
    qi = jnp.arange(BLOCK)[None, :, None]
    kk = jnp.arange(3 * BLOCK)[None, None, :]
    blk = jnp.arange(nb)[:, None, None]
    kpos = blk * BLOCK - BLOCK + kk
    rel = qi - kk + BLOCK
    return (jnp.abs(rel) <= WINDOW) & (kpos >= 0) & (kpos < n)


def sink_softmax(s, sink):
    m = jnp.maximum(jnp.max(s, axis=-1, keepdims=True), sink)
    e = jnp.exp(s - m)
    return e / (jnp.sum(e, axis=-1, keepdims=True) + jnp.exp(sink - m))


def project_q(h, w_q, gain):
    q = (h @ w_q).reshape(*h.shape[:-1], N_HEADS, HEAD_DIM)
    return rms_norm(q) * gain


def project_kv(h, w_kv, gain):
    k, v = jnp.split(h @ w_kv, 2, axis=-1)
    k = rms_norm(k.reshape(*h.shape[:-1], N_KV_HEADS, HEAD_DIM)) * gain
    v = v.reshape(*h.shape[:-1], N_KV_HEADS, HEAD_DIM)
    return k, v


def attention_mixer(h_x, h_c, w_qkv, w_o, q_gain, k_gain, sink, cos, sin, mask, need_ctx):
    b, n, _ = h_x.shape
    nb = n // BLOCK
    scale = HEAD_DIM ** -0.5
    w_q, w_kv = w_qkv[:, :D_Q], w_qkv[:, D_Q:]
    q = apply_rope(project_q(h_x, w_q, q_gain), cos, sin)
    k, v = project_kv(h_x, w_kv, k_gain)
    k = apply_rope(k, cos, sin)
    kc, vc = project_kv(h_c, w_kv, k_gain)
    sink_f = sink.astype(jnp.float32).reshape(N_KV_HEADS, GROUP)

    qb = q.reshape(b, nb, BLOCK, N_KV_HEADS, GROUP, HEAD_DIM)
    pad = ((0, 0), (BLOCK, BLOCK), (0, 0), (0, 0))
    kp = jnp.pad(k, pad).reshape(b, nb + 2, BLOCK, N_KV_HEADS, HEAD_DIM)
    vp = jnp.pad(v, pad).reshape(b, nb + 2, BLOCK, N_KV_HEADS, HEAD_DIM)
    kb = jnp.concatenate([kp[:, :-2], kp[:, 1:-1], kp[:, 2:]], axis=2)
    vb = jnp.concatenate([vp[:, :-2], vp[:, 1:-1], vp[:, 2:]], axis=2)
    s_loc = jnp.einsum('bnqhgd,bnkhd->bhgnqk', qb, kb).astype(jnp.float32) * scale
    s_loc = jnp.where(mask, s_loc, NEG_INF)
    s_ctx = jnp.einsum('bnqhgd,bkhd->bhgnqk', qb, kc).astype(jnp.float32) * scale
    p = sink_softmax(jnp.concatenate([s_loc, s_ctx], axis=-1),
                     sink_f[None, :, :, None, None, None]).astype(v.dtype)
    o = (jnp.einsum('bhgnqk,bnkhd->bnqhgd', p[..., :3 * BLOCK], vb)
         + jnp.einsum('bhgnqk,bkhd->bnqhgd', p[..., 3 * BLOCK:], vc))
    out_x = o.reshape(b, n, D_Q) @ w_o

    out_c = None
    if need_ctx:
        l = h_c.shape[1]
        qc = project_q(h_c, w_q, q_gain).reshape(b, l, N_KV_HEADS, GROUP, HEAD_DIM)
        s_c = jnp.einsum('bqhgd,bkhd->bhgqk', qc, kc).astype(jnp.float32) * scale
        p_c = sink_softmax(s_c, sink_f[None, :, :, None, None]).astype(vc.dtype)
        oc = jnp.einsum('bhgqk,bkhd->bqhgd', p_c, vc)
        out_c = oc.reshape(b, l, D_Q) @ w_o
    return out_x, out_c


def short_conv_mixer(h, w_in, conv_w, w_out):
    gate_b, gate_c, val = jnp.split(h @ w_in, 3, axis=-1)
    return (gate_b * dwconv3(gate_c * val, conv_w)) @ w_out


def conv_ffn(h, w_up, conv_w, conv_b, w_down):
    gate, val = jnp.split(h @ w_up, 2, axis=-1)
    gate = dwconv3(gate, conv_w) + conv_b
    return (jax.nn.silu(gate) * val) @ w_down


def _fwd_setup_inputs(seed: int = 0) -> dict:
    key = jax.random.key(seed)
    ks = jax.random.split(key, 20)

    def nrm(k, shape, scale):
        return jax.random.normal(k, shape, jnp.float32) * scale

    d = D_MODEL
    return {
        "x": nrm(ks[0], (BATCH, SEQ, d), 1.0),
        "c": nrm(ks[1], (BATCH, d), 1.0),
        "ctx": nrm(ks[2], (BATCH, CTX_LEN, d), 1.0),
        "c_ctx": nrm(ks[3], (d,), 1.0),
        "w_ada": nrm(ks[4], (DEPTH, d, 6 * d), 0.5 * d ** -0.5),
        "b_ada": nrm(ks[5], (DEPTH, 6 * d), 0.02),
        "attn_w_qkv": nrm(ks[6], (N_ATTN_LAYERS, d, D_Q + 2 * D_KV), d ** -0.5),
        "attn_w_o": nrm(ks[7], (N_ATTN_LAYERS, D_Q, d), D_Q ** -0.5),
        "attn_q_gain": 1.0 + nrm(ks[8], (N_ATTN_LAYERS, HEAD_DIM), 0.1),
        "attn_k_gain": 1.0 + nrm(ks[9], (N_ATTN_LAYERS, HEAD_DIM), 0.1),
        "attn_sink": nrm(ks[10], (N_ATTN_LAYERS, N_HEADS), 0.5),
        "sc_w_in": nrm(ks[11], (N_CONV_LAYERS, d, 3 * d), d ** -0.5),
        "sc_conv": nrm(ks[12], (N_CONV_LAYERS, CONV_W, d), CONV_W ** -0.5),
        "sc_w_out": nrm(ks[13], (N_CONV_LAYERS, d, d), d ** -0.5),
        "ffn_w_up": nrm(ks[14], (DEPTH, d, 2 * D_FF), d ** -0.5),
        "ffn_conv": nrm(ks[15], (DEPTH, CONV_W, D_FF), CONV_W ** -0.5),
        "ffn_conv_b": nrm(ks[16], (DEPTH, D_FF), 0.02),
        "ffn_w_down": nrm(ks[17], (DEPTH, D_FF, d), D_FF ** -0.5),
    }


def _fwd_reference(x, c, ctx, c_ctx, w_ada, b_ada, attn_w_qkv, attn_w_o, attn_q_gain, attn_k_gain,
              attn_sink, sc_w_in, sc_conv, sc_w_out, ffn_w_up, ffn_conv, ffn_conv_b, ffn_w_down):
    n = x.shape[1]
    cos, sin = rope_tables(n)
    mask = band_mask(n)
    silu_c = jax.nn.silu(c)
    silu_cc = jax.nn.silu(c_ctx)

    for l in range(DEPTH):
        is_attn = (l % N_MIXERS) == 0
        j = l // N_MIXERS
        need_ctx = l < DEPTH - 1
        sh_m, sc_m, g_m, sh_f, sc_f, g_f = [t[:, None, :] for t in adaln(silu_c, w_ada[l], b_ada[l])]
        h_x = modulate(x, sh_m, sc_m)
        y_c = None
        if need_ctx or is_attn:
            csh_m, csc_m, cg_m, csh_f, csc_f, cg_f = adaln(silu_cc, w_ada[l], b_ada[l])
            h_c = modulate(ctx, csh_m, csc_m)
        if is_attn:
            y_x, y_c = attention_mixer(h_x, h_c, attn_w_qkv[j], attn_w_o[j], attn_q_gain[j],
                                       attn_k_gain[j], attn_sink[j], cos, sin, mask, need_ctx)
        else:
            y_x = short_conv_mixer(h_x, sc_w_in[j], sc_conv[j], sc_w_out[j])
            if need_ctx:
                y_c = short_conv_mixer(h_c, sc_w_in[j], sc_conv[j], sc_w_out[j])
        x = x + g_m * y_x
        x = x + g_f * conv_ffn(modulate(x, sh_f, sc_f), ffn_w_up[l], ffn_conv[l], ffn_conv_b[l], ffn_w_down[l])
        if need_ctx:
            ctx = ctx + cg_m * y_c
            ctx = ctx + cg_f * conv_ffn(modulate(ctx, csh_f, csc_f), ffn_w_up[l], ffn_conv[l],
                                        ffn_conv_b[l], ffn_w_down[l])
    return x


import jax as _jax
import jax.numpy as _jnp

TWIN_FORMAT = 'train_step'
FWD_PARAMS = ['x', 'c', 'ctx', 'c_ctx', 'w_ada', 'b_ada', 'attn_w_qkv', 'attn_w_o', 'attn_q_gain', 'attn_k_gain', 'attn_sink', 'sc_w_in', 'sc_conv', 'sc_w_out', 'ffn_w_up', 'ffn_conv', 'ffn_conv_b', 'ffn_w_down']
TWIN_WEIGHTS = ['c_ctx', 'w_ada', 'b_ada', 'attn_w_qkv', 'attn_w_o', 'attn_q_gain', 'attn_k_gain', 'attn_sink', 'sc_w_in', 'sc_conv', 'sc_w_out', 'ffn_w_up', 'ffn_conv', 'ffn_conv_b', 'ffn_w_down']
TWIN_DIFF_INPUT = 'x'
TWIN_INPUTS = ['x', 'c', 'ctx', 'c_ctx', 'w_ada', 'b_ada', 'attn_w_qkv', 'attn_w_o', 'attn_q_gain', 'attn_k_gain', 'attn_sink', 'sc_w_in', 'sc_conv', 'sc_w_out', 'ffn_w_up', 'ffn_conv', 'ffn_conv_b', 'ffn_w_down', 'loss_target', 'm_c_ctx', 'm_w_ada', 'm_b_ada', 'm_attn_w_qkv', 'm_attn_w_o', 'm_attn_q_gain', 'm_attn_k_gain', 'm_attn_sink', 'm_sc_w_in', 'm_sc_conv', 'm_sc_w_out', 'm_ffn_w_up', 'm_ffn_conv', 'm_ffn_conv_b', 'm_ffn_w_down', 'v_c_ctx', 'v_w_ada', 'v_b_ada', 'v_attn_w_qkv', 'v_attn_w_o', 'v_attn_q_gain', 'v_attn_k_gain', 'v_attn_sink', 'v_sc_w_in', 'v_sc_conv', 'v_sc_w_out', 'v_ffn_w_up', 'v_ffn_conv', 'v_ffn_conv_b', 'v_ffn_w_down']
TWIN_OUTPUTS = ['loss', 'grad_x', 'grad_c_ctx', 'grad_w_ada', 'grad_b_ada', 'grad_attn_w_qkv', 'grad_attn_w_o', 'grad_attn_q_gain', 'grad_attn_k_gain', 'grad_attn_sink', 'grad_sc_w_in', 'grad_sc_conv', 'grad_sc_w_out', 'grad_ffn_w_up', 'grad_ffn_conv', 'grad_ffn_conv_b', 'grad_ffn_w_down', 'delta_c_ctx', 'delta_w_ada', 'delta_b_ada', 'delta_attn_w_qkv', 'delta_attn_w_o', 'delta_attn_q_gain', 'delta_attn_k_gain', 'delta_attn_sink', 'delta_sc_w_in', 'delta_sc_conv', 'delta_sc_w_out', 'delta_ffn_w_up', 'delta_ffn_conv', 'delta_ffn_conv_b', 'delta_ffn_w_down', 'new_m_c_ctx', 'new_m_w_ada', 'new_m_b_ada', 'new_m_attn_w_qkv', 'new_m_attn_w_o', 'new_m_attn_q_gain', 'new_m_attn_k_gain', 'new_m_attn_sink', 'new_m_sc_w_in', 'new_m_sc_conv', 'new_m_sc_w_out', 'new_m_ffn_w_up', 'new_m_ffn_conv', 'new_m_ffn_conv_b', 'new_m_ffn_w_down', 'new_v_c_ctx', 'new_v_w_ada', 'new_v_b_ada', 'new_v_attn_w_qkv', 'new_v_attn_w_o', 'new_v_attn_q_gain', 'new_v_attn_k_gain', 'new_v_attn_sink', 'new_v_sc_w_in', 'new_v_sc_conv', 'new_v_sc_w_out', 'new_v_ffn_w_up', 'new_v_ffn_conv', 'new_v_ffn_conv_b', 'new_v_ffn_w_down']
TWIN_LEAF_KINDS = {'loss': 'loss', 'grad_x': 'grad_x', 'grad_c_ctx': 'grad_w', 'grad_w_ada': 'grad_w', 'grad_b_ada': 'grad_w', 'grad_attn_w_qkv': 'grad_w', 'grad_attn_w_o': 'grad_w', 'grad_attn_q_gain': 'grad_w', 'grad_attn_k_gain': 'grad_w', 'grad_attn_sink': 'grad_w', 'grad_sc_w_in': 'grad_w', 'grad_sc_conv': 'grad_w', 'grad_sc_w_out': 'grad_w', 'grad_ffn_w_up': 'grad_w', 'grad_ffn_conv': 'grad_w', 'grad_ffn_conv_b': 'grad_w', 'grad_ffn_w_down': 'grad_w', 'delta_c_ctx': 'delta_w', 'delta_w_ada': 'delta_w', 'delta_b_ada': 'delta_w', 'delta_attn_w_qkv': 'delta_w', 'delta_attn_w_o': 'delta_w', 'delta_attn_q_gain': 'delta_w', 'delta_attn_k_gain': 'delta_w', 'delta_attn_sink': 'delta_w', 'delta_sc_w_in': 'delta_w', 'delta_sc_conv': 'delta_w', 'delta_sc_w_out': 'delta_w', 'delta_ffn_w_up': 'delta_w', 'delta_ffn_conv': 'delta_w', 'delta_ffn_conv_b': 'delta_w', 'delta_ffn_w_down': 'delta_w', 'new_m_c_ctx': 'new_m', 'new_m_w_ada': 'new_m', 'new_m_b_ada': 'new_m', 'new_m_attn_w_qkv': 'new_m', 'new_m_attn_w_o': 'new_m', 'new_m_attn_q_gain': 'new_m', 'new_m_attn_k_gain': 'new_m', 'new_m_attn_sink': 'new_m', 'new_m_sc_w_in': 'new_m', 'new_m_sc_conv': 'new_m', 'new_m_sc_w_out': 'new_m', 'new_m_ffn_w_up': 'new_m', 'new_m_ffn_conv': 'new_m', 'new_m_ffn_conv_b': 'new_m', 'new_m_ffn_w_down': 'new_m', 'new_v_c_ctx': 'new_v', 'new_v_w_ada': 'new_v', 'new_v_b_ada': 'new_v', 'new_v_attn_w_qkv': 'new_v', 'new_v_attn_w_o': 'new_v', 'new_v_attn_q_gain': 'new_v', 'new_v_attn_k_gain': 'new_v', 'new_v_attn_sink': 'new_v', 'new_v_sc_w_in': 'new_v', 'new_v_sc_conv': 'new_v', 'new_v_sc_w_out': 'new_v', 'new_v_ffn_w_up': 'new_v', 'new_v_ffn_conv': 'new_v', 'new_v_ffn_conv_b': 'new_v', 'new_v_ffn_w_down': 'new_v'}


def _forward(args):
    return _fwd_reference(*[args[k] for k in FWD_PARAMS])


def _output_shape():
    def fwd():
        inp = _fwd_setup_inputs(0)
        return _fwd_reference(*[inp[k] for k in FWD_PARAMS])
    out = _jax.eval_shape(fwd)
    return out.shape, out.dtype

N_MICROBATCH = 1
ADAM_LR = 0.001
ADAM_B1 = 0.9
ADAM_B2 = 0.999
ADAM_EPS = 1e-08
ADAM_WD = 0.01
ADAM_STEP = 10
PER_EXAMPLE_BATCH_AXIS = {'x': 0, 'c': 0, 'ctx': 0, 'loss_target': 0}
SHARED_INPUTS = []
_WEIGHT_DTYPES = {'c_ctx': _jnp.float32, 'w_ada': _jnp.float32, 'b_ada': _jnp.float32, 'attn_w_qkv': _jnp.float32, 'attn_w_o': _jnp.float32, 'attn_q_gain': _jnp.float32, 'attn_k_gain': _jnp.float32, 'attn_sink': _jnp.float32, 'sc_w_in': _jnp.float32, 'sc_conv': _jnp.float32, 'sc_w_out': _jnp.float32, 'ffn_w_up': _jnp.float32, 'ffn_conv': _jnp.float32, 'ffn_conv_b': _jnp.float32, 'ffn_w_down': _jnp.float32}
MOMENT_SCALE = {'c_ctx': 3.622327e-01, 'w_ada': 1.532433e+00, 'b_ada': 4.418860e+00, 'attn_w_qkv': 2.049276e-01, 'attn_w_o': 1.883486e-01, 'attn_q_gain': 2.682263e-01, 'attn_k_gain': 2.653616e-01, 'attn_sink': 3.058587e-02, 'sc_w_in': 1.813598e-01, 'sc_conv': 2.687894e+00, 'sc_w_out': 1.305672e-01, 'ffn_w_up': 6.197897e-02, 'ffn_conv': 3.539836e-01, 'ffn_conv_b': 4.018614e-01, 'ffn_w_down': 7.579130e-02}


def _to_microbatches(a, axis):
    t = _jnp.moveaxis(a, axis, 0)
    t = t.reshape((N_MICROBATCH, t.shape[0] // N_MICROBATCH) + t.shape[1:])
    return _jnp.moveaxis(t, 1, axis + 1)


def setup_inputs(seed: int = 0) -> dict:
    inp = _fwd_setup_inputs(seed)
    key = _jax.random.fold_in(_jax.random.key(seed), 7919)
    shape, _ = _output_shape()
    out = dict(inp)
    out["loss_target"] = _jax.random.normal(_jax.random.fold_in(key, 0), shape, _jnp.float32)
    for i, name in enumerate(TWIN_WEIGHTS):
        w = inp[name].astype(_jnp.float32)
        if MOMENT_SCALE is None:
            s = _jnp.sqrt(_jnp.mean(_jnp.square(w)) + 1e-30)
        else:
            s = MOMENT_SCALE[name]
        km, kv = _jax.random.split(_jax.random.fold_in(key, i + 1))
        out[name] = w
        out["m_" + name] = s * _jax.random.normal(km, w.shape, _jnp.float32)
        out["v_" + name] = (s * s) * _jax.random.uniform(kv, w.shape, _jnp.float32, 0.5, 1.5)
    if N_MICROBATCH > 1:
        for name, axis in PER_EXAMPLE_BATCH_AXIS.items():
            out[name] = _to_microbatches(out[name], axis)
    return {'x': out['x'], 'c': out['c'], 'ctx': out['ctx'], 'c_ctx': out['c_ctx'], 'w_ada': out['w_ada'], 'b_ada': out['b_ada'], 'attn_w_qkv': out['attn_w_qkv'], 'attn_w_o': out['attn_w_o'], 'attn_q_gain': out['attn_q_gain'], 'attn_k_gain': out['attn_k_gain'], 'attn_sink': out['attn_sink'], 'sc_w_in': out['sc_w_in'], 'sc_conv': out['sc_conv'], 'sc_w_out': out['sc_w_out'], 'ffn_w_up': out['ffn_w_up'], 'ffn_conv': out['ffn_conv'], 'ffn_conv_b': out['ffn_conv_b'], 'ffn_w_down': out['ffn_w_down'], 'loss_target': out['loss_target'], 'm_c_ctx': out['m_c_ctx'], 'm_w_ada': out['m_w_ada'], 'm_b_ada': out['m_b_ada'], 'm_attn_w_qkv': out['m_attn_w_qkv'], 'm_attn_w_o': out['m_attn_w_o'], 'm_attn_q_gain': out['m_attn_q_gain'], 'm_attn_k_gain': out['m_attn_k_gain'], 'm_attn_sink': out['m_attn_sink'], 'm_sc_w_in': out['m_sc_w_in'], 'm_sc_conv': out['m_sc_conv'], 'm_sc_w_out': out['m_sc_w_out'], 'm_ffn_w_up': out['m_ffn_w_up'], 'm_ffn_conv': out['m_ffn_conv'], 'm_ffn_conv_b': out['m_ffn_conv_b'], 'm_ffn_w_down': out['m_ffn_w_down'], 'v_c_ctx': out['v_c_ctx'], 'v_w_ada': out['v_w_ada'], 'v_b_ada': out['v_b_ada'], 'v_attn_w_qkv': out['v_attn_w_qkv'], 'v_attn_w_o': out['v_attn_w_o'], 'v_attn_q_gain': out['v_attn_q_gain'], 'v_attn_k_gain': out['v_attn_k_gain'], 'v_attn_sink': out['v_attn_sink'], 'v_sc_w_in': out['v_sc_w_in'], 'v_sc_conv': out['v_sc_conv'], 'v_sc_w_out': out['v_sc_w_out'], 'v_ffn_w_up': out['v_ffn_w_up'], 'v_ffn_conv': out['v_ffn_conv'], 'v_ffn_conv_b': out['v_ffn_conv_b'], 'v_ffn_w_down': out['v_ffn_w_down']}


def _loss(weights, diff, rest, loss_target):
    with _jax.named_scope("forward"):
        args = {**rest, TWIN_DIFF_INPUT: diff, **{k: w.astype(_WEIGHT_DTYPES[k]) for k, w in weights.items()}}
        y = _forward(args)
    with _jax.named_scope("loss_head"):
        err = _jnp.square(y.astype(_jnp.float32) - loss_target)
        return 0.5 * _jnp.sum(_jnp.mean(err, axis=-1)) if err.ndim else 0.5 * err


def _adamw(w, g, m, v):
    m = ADAM_B1 * m + (1.0 - ADAM_B1) * g
    v = ADAM_B2 * v + (1.0 - ADAM_B2) * _jnp.square(g)
    m_hat = m / (1.0 - ADAM_B1 ** ADAM_STEP)
    v_hat = v / (1.0 - ADAM_B2 ** ADAM_STEP)
    delta = -ADAM_LR * (m_hat / (_jnp.sqrt(v_hat) + ADAM_EPS) + ADAM_WD * w)
    return delta, m, v


def reference(x, c, ctx, c_ctx, w_ada, b_ada, attn_w_qkv, attn_w_o, attn_q_gain, attn_k_gain, attn_sink, sc_w_in, sc_conv, sc_w_out, ffn_w_up, ffn_conv, ffn_conv_b, ffn_w_down, loss_target, m_c_ctx, m_w_ada, m_b_ada, m_attn_w_qkv, m_attn_w_o, m_attn_q_gain, m_attn_k_gain, m_attn_sink, m_sc_w_in, m_sc_conv, m_sc_w_out, m_ffn_w_up, m_ffn_conv, m_ffn_conv_b, m_ffn_w_down, v_c_ctx, v_w_ada, v_b_ada, v_attn_w_qkv, v_attn_w_o, v_attn_q_gain, v_attn_k_gain, v_attn_sink, v_sc_w_in, v_sc_conv, v_sc_w_out, v_ffn_w_up, v_ffn_conv, v_ffn_conv_b, v_ffn_w_down):
    given = dict(x=x, c=c, ctx=ctx, c_ctx=c_ctx, w_ada=w_ada, b_ada=b_ada, attn_w_qkv=attn_w_qkv, attn_w_o=attn_w_o, attn_q_gain=attn_q_gain, attn_k_gain=attn_k_gain, attn_sink=attn_sink, sc_w_in=sc_w_in, sc_conv=sc_conv, sc_w_out=sc_w_out, ffn_w_up=ffn_w_up, ffn_conv=ffn_conv, ffn_conv_b=ffn_conv_b, ffn_w_down=ffn_w_down, loss_target=loss_target, m_c_ctx=m_c_ctx, m_w_ada=m_w_ada, m_b_ada=m_b_ada, m_attn_w_qkv=m_attn_w_qkv, m_attn_w_o=m_attn_w_o, m_attn_q_gain=m_attn_q_gain, m_attn_k_gain=m_attn_k_gain, m_attn_sink=m_attn_sink, m_sc_w_in=m_sc_w_in, m_sc_conv=m_sc_conv, m_sc_w_out=m_sc_w_out, m_ffn_w_up=m_ffn_w_up, m_ffn_conv=m_ffn_conv, m_ffn_conv_b=m_ffn_conv_b, m_ffn_w_down=m_ffn_w_down, v_c_ctx=v_c_ctx, v_w_ada=v_w_ada, v_b_ada=v_b_ada, v_attn_w_qkv=v_attn_w_qkv, v_attn_w_o=v_attn_w_o, v_attn_q_gain=v_attn_q_gain, v_attn_k_gain=v_attn_k_gain, v_attn_sink=v_attn_sink, v_sc_w_in=v_sc_w_in, v_sc_conv=v_sc_conv, v_sc_w_out=v_sc_w_out, v_ffn_w_up=v_ffn_w_up, v_ffn_conv=v_ffn_conv, v_ffn_conv_b=v_ffn_conv_b, v_ffn_w_down=v_ffn_w_down)
    weights = {n: given[n] for n in TWIN_WEIGHTS}
    shared = {n: given[n] for n in SHARED_INPUTS}
    per_example = {n: given[n] for n in ['x', 'c', 'ctx']}
    grad_fn = _jax.value_and_grad(_loss, argnums=(0, 1))

    def one_microbatch(ex, loss_target):
        ex = dict(ex)
        diff = ex.pop(TWIN_DIFF_INPUT)
        return grad_fn(weights, diff, {**shared, **ex}, loss_target)

    if N_MICROBATCH == 1:
        loss, (grad_w, grad_x) = one_microbatch(per_example, given["loss_target"])
    else:
        def body(carry, xs):
            loss_sum, grad_sum = carry
            l_k, (gw_k, gx_k) = one_microbatch(xs[0], xs[1])
            with _jax.named_scope("update"):
                return (loss_sum + l_k, _jax.tree.map(_jnp.add, grad_sum, gw_k)), gx_k

        init = (_jnp.zeros((), _jnp.float32), _jax.tree.map(_jnp.zeros_like, weights))
        (loss, grad_w), grad_x = _jax.lax.scan(body, init, (per_example, given["loss_target"]))
    with _jax.named_scope("update"):
        delta_w, new_m, new_v = {}, {}, {}
        for n in TWIN_WEIGHTS:
            delta_w[n], new_m[n], new_v[n] = _adamw(weights[n], grad_w[n], given["m_" + n], given["v_" + n])
    return (loss, grad_x, *[grad_w[n] for n in TWIN_WEIGHTS], *[delta_w[n] for n in TWIN_WEIGHTS],
            *[new_m[n] for n in TWIN_WEIGHTS], *[new_v[n] for n in TWIN_WEIGHTS])
```

```python
import functools

import jax
import jax.numpy as jnp
from jax import lax
from jax.experimental import pallas as pl
from jax.experimental.pallas import tpu as pltpu

F32, BF16 = jnp.float32, jnp.bfloat16
MESH = pl.DeviceIdType.MESH
VMEM_LIMIT = 56 * 1024 * 1024
LANE = 128
HALO = 16
HEAD = 128
GROUP = 4
CTX = 256
BLK = 128
WINDOW = 128
RB = 256
GRID_W = 64
ROPE_BASE = 10000.0
EPS = 1e-6
NEG = -1e30
N_MIX = 2
LR, B1, B2, ADAM_EPS, WD, STEP = 0.001, 0.9, 0.999, 1e-08, 0.01, 10
ADAM_TILE_ELEMS = 400 * 1024
NT = (((1,), (1,)), ((), ()))
TN = (((0,), (0,)), ((), ()))


def _pick(dim, target, mult=LANE):
    best = None
    for t in range(mult, min(dim, target) + 1, mult):
        if dim % t == 0:
            best = t
    return dim if best is None else best


def _cdiv(a, b):
    return -(-a // b)


def _params(sem):
    return pltpu.CompilerParams(dimension_semantics=sem, vmem_limit_bytes=VMEM_LIMIT)


def _sigmoid(g):
    return 1.0 / (1.0 + jnp.exp(-g))


def _row(v, r):
    rows = lax.broadcasted_iota(jnp.int32, v.shape, 0)
    return jnp.sum(jnp.where(rows == r, v, 0.0), axis=0, keepdims=True)


def _get_col(v, c):
    lanes = lax.broadcasted_iota(jnp.int32, v.shape, 1)
    return jnp.sum(jnp.where(lanes == c, v, 0.0), axis=1, keepdims=True)


def _put_col(v, c, col):
    lanes = lax.broadcasted_iota(jnp.int32, v.shape, 1)
    return jnp.where(lanes == c, col, v)


def _rows3(s0, s1, s2, width):
    rows = lax.broadcasted_iota(jnp.int32, (8, width), 0)
    z = jnp.zeros((8, width), F32)
    return jnp.where(rows == 0, s0, jnp.where(rows == 1, s1, jnp.where(rows == 2, s2, z)))


def _shift_rows(w, prev_row, next_row):
    n = w.shape[0]
    rows = lax.broadcasted_iota(jnp.int32, (n, 1), 0)
    down = jnp.where(rows == 0, prev_row, pltpu.roll(w, 1, 0))
    up = jnp.where(rows == n - 1, next_row, pltpu.roll(w, n - 1, 0))
    return down, up


def _seg_flags(i, nt):
    return i <= 1, (i == 0) | (i == nt - 1)


def _halo_specs(width, nrows):
    r = RB // HALO
    nh = nrows // HALO
    prev = pl.BlockSpec((HALO, width), lambda i: (jnp.maximum(i * r - 1, 0), 0))
    nxt = pl.BlockSpec((HALO, width), lambda i: (jnp.minimum((i + 1) * r, nh - 1), 0))
    return prev, nxt


class Layout:
    def __init__(self, d, ff, nqkv, depth):
        n_attn, n_conv = (depth + 1) // 2, depth // 2
        self.nb_up, self.nb_sc, self.nb_qkv = 2 * ff // 4, 3 * d // 4, nqkv // 4
        self.kb_dn, self.kb_o = ff // 4, d // 4
        off = 0
        self.up = [off + l * self.nb_up for l in range(depth)]
        off = _cdiv(depth * self.nb_up, self.nb_sc) * self.nb_sc
        self.scin = [off + l * self.nb_sc for l in range(n_conv)]
        off = _cdiv(off + n_conv * self.nb_sc, self.nb_qkv) * self.nb_qkv
        self.qkv = [off + l * self.nb_qkv for l in range(n_attn)]
        self.ct = _cdiv(off + n_attn * self.nb_qkv, LANE) * LANE
        self.dn = [l * self.kb_dn for l in range(depth)]
        off = _cdiv(depth * self.kb_dn, self.kb_o) * self.kb_o
        self.wo = [off + l * self.kb_o for l in range(n_attn)]
        off += n_attn * self.kb_o
        self.scout = [off + l * self.kb_o for l in range(n_conv)]
        self.rt = _cdiv(off + n_conv * self.kb_o, 2 * HALO) * 2 * HALO


def _mm_call(name, grid, in_specs, out_specs, out_shape, contract, operands, acc_shape, epilogue,
             n_extra=0, aliases=None):
    nk = grid[2]
    n_out = len(out_shape)

    def body(*refs):
        a_ref, b_ref = refs[0], refs[1]
        extra = refs[2:2 + n_extra]
        outs = refs[2 + n_extra:2 + n_extra + n_out]
        ids = (pl.program_id(0), pl.program_id(1))

        def part():
            return lax.dot_general(a_ref[...], b_ref[...], contract, preferred_element_type=F32)

        if nk == 1:
            epilogue(part(), extra, outs, ids)
        else:
            acc = refs[-1]
            k = pl.program_id(2)

            @pl.when(k == 0)
            def _():
                acc[...] = jnp.zeros_like(acc)

            acc[...] += part()

            @pl.when(k == nk - 1)
            def _():
                epilogue(acc[...], extra, outs, ids)

    scratch = [] if nk == 1 else [pltpu.VMEM(acc_shape, F32)]
    return pl.pallas_call(
        body, name=name, grid=grid, in_specs=in_specs, out_specs=out_specs, out_shape=out_shape,
        scratch_shapes=scratch, input_output_aliases=aliases or {},
        compiler_params=_params(("parallel", "parallel", "arbitrary")))(*operands)


def _store(dtype):
    def epilogue(r, extra, outs, ids):
        outs[0][...] = r.astype(dtype)
    return epilogue


def mm_nn_col(name, a, wc, off, nb):
    m, d = a.shape
    s = wc.shape[0]
    tm, tn = _pick(m, 1056, HALO), _pick(nb, 1536)
    npb, ob = nb // tn, off // tn
    assert off % tn == 0
    return _mm_call(
        name, (m // tm, s * npb, 1),
        [pl.BlockSpec((tm, d), lambda i, j, k: (i, 0)),
         pl.BlockSpec((None, d, tn), lambda i, j, k: (j // npb, 0, ob + j % npb))],
        [pl.BlockSpec((tm, tn), lambda i, j, k: (i, j))],
        [jax.ShapeDtypeStruct((m, s * nb), BF16)],
        (((1,), (0,)), ((), ())), (a, wc), None, _store(BF16))[0]


def mm_nn_row(name, a, wr, off, kb, res, mod, gi):
    m = a.shape[0]
    s, _, d = wr.shape
    tm, tk, tn = _pick(m, 1056, HALO), _pick(kb, 1408), _pick(d, 1024)
    kpb, ob = kb // tk, off // tk
    assert off % tk == 0

    def epilogue(r, extra, outs, ids):
        res_ref, mod_ref = extra
        rows = ids[0] * tm + lax.broadcasted_iota(jnp.int32, (tm, 1), 0)
        g = jnp.where(rows < CTX, mod_ref[0, gi:gi + 1, :], mod_ref[1, gi:gi + 1, :])
        outs[0][...] = res_ref[...] + g * r
        outs[1][...] = r.astype(BF16)

    return _mm_call(
        name, (m // tm, d // tn, s * kpb),
        [pl.BlockSpec((tm, tk), lambda i, j, k: (i, k)),
         pl.BlockSpec((None, tk, tn), lambda i, j, k: (k // kpb, ob + k % kpb, j)),
         pl.BlockSpec((tm, tn), lambda i, j, k: (i, j)),
         pl.BlockSpec((2, 8, tn), lambda i, j, k: (0, 0, j))],
        [pl.BlockSpec((tm, tn), lambda i, j, k: (i, j)), pl.BlockSpec((tm, tn), lambda i, j, k: (i, j))],
        [jax.ShapeDtypeStruct((m, d), F32), jax.ShapeDtypeStruct((m, d), BF16)],
        (((1,), (0,)), ((), ())), (a, wr, res, mod), (tm, tn), epilogue, n_extra=2)


def mm_nt_col(name, dy, wc, off, nb):
    m = dy.shape[0]
    s, d, _ = wc.shape
    tm, tc = _pick(m, 768, HALO), _pick(nb, 1536)
    npb, ob = nb // tc, off // tc
    return _mm_call(
        name, (m // tm, 1, s * npb),
        [pl.BlockSpec((tm, tc), lambda i, j, k: (i, k)),
         pl.BlockSpec((None, d, tc), lambda i, j, k: (k // npb, 0, ob + k % npb))],
        [pl.BlockSpec((tm, d), lambda i, j, k: (i, 0))],
        [jax.ShapeDtypeStruct((m, d), F32)],
        NT, (dy, wc), (tm, d), _store(F32))[0]


def mm_nt_row(name, dy, wr, off, kb):
    m, d = dy.shape
    s = wr.shape[0]
    tm, tkb = _pick(m, 1056, HALO), _pick(kb, 1408)
    kpb, ob = kb // tkb, off // tkb
    return _mm_call(
        name, (m // tm, s * kpb, 1),
        [pl.BlockSpec((tm, d), lambda i, j, k: (i, 0)),
         pl.BlockSpec((None, tkb, d), lambda i, j, k: (j // kpb, ob + j % kpb, 0))],
        [pl.BlockSpec((tm, tkb), lambda i, j, k: (i, j))],
        [jax.ShapeDtypeStruct((m, s * kb), BF16)],
        NT, (dy, wr), None, _store(BF16))[0]


def mm_tn_col(name, a, dy, gbuf, off, nb):
    t, d = a.shape
    s = gbuf.shape[0]
    tka, tn, tt = _pick(d, 2048), _pick(nb, 1536), _pick(t, 1056, HALO)
    npb, ob = nb // tn, off // tn
    return _mm_call(
        name, (d // tka, s * npb, t // tt),
        [pl.BlockSpec((tt, tka), lambda i, j, k: (k, i)),
         pl.BlockSpec((tt, tn), lambda i, j, k: (k, j)),
         pl.BlockSpec(memory_space=pl.ANY)],
        [pl.BlockSpec((None, tka, tn), lambda i, j, k: (j // npb, i, ob + j % npb))],
        [jax.ShapeDtypeStruct(gbuf.shape, BF16)],
        TN, (a, dy, gbuf), (tka, tn), _store(BF16), n_extra=1, aliases={2: 0})[0]


def mm_tn_row(name, act, dy, gbuf, off, kb):
    t, d = dy.shape
    s = gbuf.shape[0]
    tka, tn, tt = _pick(kb, 1408), _pick(d, 2048), _pick(t, 1056, HALO)
    kpb, ob = kb // tka, off // tka
    return _mm_call(
        name, (s * kpb, d // tn, t // tt),
        [pl.BlockSpec((tt, tka), lambda i, j, k: (k, i)),
         pl.BlockSpec((tt, tn), lambda i, j, k: (k, j)),
         pl.BlockSpec(memory_space=pl.ANY)],
        [pl.BlockSpec((None, tka, tn), lambda i, j, k: (i // kpb, ob + i % kpb, j))],
        [jax.ShapeDtypeStruct(gbuf.shape, BF16)],
        TN, (act, dy, gbuf), (tka, tn), _store(BF16), n_extra=1, aliases={2: 0})[0]


def cast_pack(name, w, buf, off, col):
    nl, k, n = w.shape
    if col:
        tr = _pick(k, 512, HALO)
        ob = off // n
        out_spec = pl.BlockSpec((tr, n), lambda l, i: (i, ob + l))
    else:
        tr = _pick(k, 704, HALO)
        ob, per = off // tr, k // tr
        out_spec = pl.BlockSpec((tr, n), lambda l, i: (ob + l * per + i, 0))

    def body(w_ref, buf_ref, out_ref):
        out_ref[...] = w_ref[...].astype(BF16)

    return pl.pallas_call(
        body, name=name, grid=(nl, k // tr),
        in_specs=[pl.BlockSpec((None, tr, n), lambda l, i: (l, i, 0)), pl.BlockSpec(memory_space=pl.ANY)],
        out_specs=out_spec, out_shape=jax.ShapeDtypeStruct(buf.shape, BF16),
        input_output_aliases={1: 0}, compiler_params=_params(("parallel", "parallel")))(w, buf)


def norm_mod(name, x, mod, sh, sc):
    t, d = x.shape

    def body(x_ref, mod_ref, h_ref):
        seg = jnp.minimum(pl.program_id(0), 1)
        xv = x_ref[...]
        r = lax.rsqrt(jnp.mean(xv * xv, axis=-1, keepdims=True) + EPS)
        m = mod_ref[seg]
        h_ref[...] = ((xv * r) * (1.0 + m[sc:sc + 1, :]) + m[sh:sh + 1, :]).astype(BF16)

    return pl.pallas_call(
        body, name=name, grid=(t // RB,),
        in_specs=[pl.BlockSpec((RB, d), lambda i: (i, 0)), pl.BlockSpec((2, 8, d), lambda i: (0, 0, 0))],
        out_specs=pl.BlockSpec((RB, d), lambda i: (i, 0)),
        out_shape=jax.ShapeDtypeStruct((t, d), BF16), compiler_params=_params(("parallel",)))(x, mod)


def sc_gate_fwd(name, u, cw):
    t = u.shape[0]
    d = u.shape[1] // 3
    nt, tc = t // RB, _pick(d, 512)
    prev, nxt = _halo_specs(3 * d, t)

    def body(u_ref, up_ref, un_ref, cw_ref, z_ref):
        first, last = _seg_flags(pl.program_id(0), nt)
        for j in range(d // tc):
            c0 = j * tc
            gb = u_ref[:, c0:c0 + tc].astype(F32)
            w = u_ref[:, d + c0:d + c0 + tc].astype(F32) * u_ref[:, 2 * d + c0:2 * d + c0 + tc].astype(F32)
            pw = _row(up_ref[:, d + c0:d + c0 + tc].astype(F32) * up_ref[:, 2 * d + c0:2 * d + c0 + tc].astype(F32), HALO - 1)
            nw = _row(un_ref[:, d + c0:d + c0 + tc].astype(F32) * un_ref[:, 2 * d + c0:2 * d + c0 + tc].astype(F32), 0)
            wd, wu = _shift_rows(w, jnp.where(first, 0.0, pw), jnp.where(last, 0.0, nw))
            cwj = cw_ref[:, c0:c0 + tc]
            conv = wd * cwj[0:1] + w * cwj[1:2] + wu * cwj[2:3]
            z_ref[:, c0:c0 + tc] = (gb * conv).astype(BF16)

    return pl.pallas_call(
        body, name=name, grid=(nt,),
        in_specs=[pl.BlockSpec((RB, 3 * d), lambda i: (i, 0)), prev, nxt, pl.BlockSpec((8, d), lambda i: (0, 0))],
        out_specs=pl.BlockSpec((RB, d), lambda i: (i, 0)),
        out_shape=jax.ShapeDtypeStruct((t, d), BF16), compiler_params=_params(("parallel",)))(u, u, u, cw)


def sc_gate_bwd(name, u, dz, cw):
    t = u.shape[0]
    d = u.shape[1] // 3
    nt, tc = t // RB, _pick(d, 512)
    prev, nxt = _halo_specs(3 * d, t)
    dprev, dnxt = _halo_specs(d, t)

    def body(u_ref, up_ref, un_ref, dz_ref, dzp_ref, dzn_ref, cw_ref, du_ref, dcw_ref):
        i = pl.program_id(0)
        first, last = _seg_flags(i, nt)

        @pl.when(i == 0)
        def _():
            dcw_ref[...] = jnp.zeros_like(dcw_ref)

        for j in range(d // tc):
            c0 = j * tc
            sl0, sl1, sl2 = slice(c0, c0 + tc), slice(d + c0, d + c0 + tc), slice(2 * d + c0, 2 * d + c0 + tc)
            gb, gc, v = u_ref[:, sl0].astype(F32), u_ref[:, sl1].astype(F32), u_ref[:, sl2].astype(F32)
            w = gc * v
            pw = _row(up_ref[:, sl1].astype(F32) * up_ref[:, sl2].astype(F32), HALO - 1)
            nw = _row(un_ref[:, sl1].astype(F32) * un_ref[:, sl2].astype(F32), 0)
            wd, wu = _shift_rows(w, jnp.where(first, 0.0, pw), jnp.where(last, 0.0, nw))
            cwj = cw_ref[:, sl0]
            cw0, cw1, cw2 = cwj[0:1], cwj[1:2], cwj[2:3]
            dzv = dz_ref[:, sl0].astype(F32)
            e = dzv * gb
            pe = _row(dzp_ref[:, sl0].astype(F32) * up_ref[:, sl0].astype(F32), HALO - 1)
            ne = _row(dzn_ref[:, sl0].astype(F32) * un_ref[:, sl0].astype(F32), 0)
            ed, eu = _shift_rows(e, jnp.where(first, 0.0, pe), jnp.where(last, 0.0, ne))
            dw = cw0 * eu + cw1 * e + cw2 * ed
            du_ref[:, sl0] = (dzv * (wd * cw0 + w * cw1 + wu * cw2)).astype(BF16)
            du_ref[:, sl1] = (dw * v).astype(BF16)
            du_ref[:, sl2] = (dw * gc).astype(BF16)
            dcw_ref[:, sl0] += _rows3(jnp.sum(e * wd, axis=0, keepdims=True), jnp.sum(e * w, axis=0, keepdims=True),
                                      jnp.sum(e * wu, axis=0, keepdims=True), tc)

    return pl.pallas_call(
        body, name=name, grid=(nt,),
        in_specs=[pl.BlockSpec((RB, 3 * d), lambda i: (i, 0)), prev, nxt,
                  pl.BlockSpec((RB, d), lambda i: (i, 0)), dprev, dnxt, pl.BlockSpec((8, d), lambda i: (0, 0))],
        out_specs=[pl.BlockSpec((RB, 3 * d), lambda i: (i, 0)), pl.BlockSpec((8, d), lambda i: (0, 0))],
        out_shape=[jax.ShapeDtypeStruct((t, 3 * d), BF16), jax.ShapeDtypeStruct((8, d), F32)],
        compiler_params=_params(("arbitrary",)))(u, u, u, dz, dz, dz, cw)


def ffn_act_fwd(name, up, cw):
    t = up.shape[0]
    ff = up.shape[1] // 2
    nt, tc = t // RB, _pick(ff, 1408)
    prev, nxt = _halo_specs(2 * ff, t)

    def body(up_ref, upp_ref, upn_ref, cw_ref, a_ref):
        first, last = _seg_flags(pl.program_id(0), nt)
        for j in range(ff // tc):
            sg, sv = slice(j * tc, (j + 1) * tc), slice(ff + j * tc, ff + (j + 1) * tc)
            gate = up_ref[:, sg].astype(F32)
            pg = _row(upp_ref[:, sg].astype(F32), HALO - 1)
            ng = _row(upn_ref[:, sg].astype(F32), 0)
            gd, gu = _shift_rows(gate, jnp.where(first, 0.0, pg), jnp.where(last, 0.0, ng))
            cwj = cw_ref[:, sg]
            g = gd * cwj[0:1] + gate * cwj[1:2] + gu * cwj[2:3] + cwj[3:4]
            a_ref[:, sg] = (g * _sigmoid(g) * up_ref[:, sv].astype(F32)).astype(BF16)

    return pl.pallas_call(
        body, name=name, grid=(nt,),
        in_specs=[pl.BlockSpec((RB, 2 * ff), lambda i: (i, 0)), prev, nxt, pl.BlockSpec((8, ff), lambda i: (0, 0))],
        out_specs=pl.BlockSpec((RB, ff), lambda i: (i, 0)),
        out_shape=jax.ShapeDtypeStruct((t, ff), BF16), compiler_params=_params(("parallel",)))(up, up, up, cw)


def ffn_act_bwd(name, up, da, cw):
    t = up.shape[0]
    ff = up.shape[1] // 2
    nt, tc = t // RB, _pick(ff, 1408)
    prev, nxt = _halo_specs(2 * ff, t)
    dprev, dnxt = _halo_specs(ff, t)

    def dsilu(g):
        s = _sigmoid(g)
        return s * (1.0 + g * (1.0 - s))

    def body(up_ref, upp_ref, upn_ref, da_ref, dap_ref, dan_ref, cw_ref, dup_ref, acc_ref):
        i = pl.program_id(0)
        first, last = _seg_flags(i, nt)

        @pl.when(i == 0)
        def _():
            acc_ref[...] = jnp.zeros_like(acc_ref)

        for j in range(ff // tc):
            sg, sv = slice(j * tc, (j + 1) * tc), slice(ff + j * tc, ff + (j + 1) * tc)
            gate, val, dav = up_ref[:, sg].astype(F32), up_ref[:, sv].astype(F32), da_ref[:, sg].astype(F32)
            pgt, ngt = upp_ref[:, sg].astype(F32), upn_ref[:, sg].astype(F32)
            pg1, pg2 = _row(pgt, HALO - 1), _row(pgt, HALO - 2)
            ng1, ng2 = _row(ngt, 0), _row(ngt, 1)
            cwj = cw_ref[:, sg]
            cw0, cw1, cw2, b = cwj[0:1], cwj[1:2], cwj[2:3], cwj[3:4]
            gd, gu = _shift_rows(gate, jnp.where(first, 0.0, pg1), jnp.where(last, 0.0, ng1))
            g = gd * cw0 + gate * cw1 + gu * cw2 + b
            g_p = pg2 * cw0 + pg1 * cw1 + _row(gate, 0) * cw2 + b
            g_n = _row(gate, RB - 1) * cw0 + ng1 * cw1 + ng2 * cw2 + b
            dg = dav * val * dsilu(g)
            dg_p = _row(dap_ref[:, sg].astype(F32) * upp_ref[:, sv].astype(F32), HALO - 1) * dsilu(g_p)
            dg_n = _row(dan_ref[:, sg].astype(F32) * upn_ref[:, sv].astype(F32), 0) * dsilu(g_n)
            dgd, dgu = _shift_rows(dg, jnp.where(first, 0.0, dg_p), jnp.where(last, 0.0, dg_n))
            dup_ref[:, sg] = (cw0 * dgu + cw1 * dg + cw2 * dgd).astype(BF16)
            dup_ref[:, sv] = (dav * g * _sigmoid(g)).astype(BF16)
            rows = lax.broadcasted_iota(jnp.int32, (8, tc), 0)
            acc_ref[:, sg] += (_rows3(jnp.sum(dg * gd, axis=0, keepdims=True), jnp.sum(dg * gate, axis=0, keepdims=True),
                                      jnp.sum(dg * gu, axis=0, keepdims=True), tc)
                               + jnp.where(rows == 3, jnp.sum(dg, axis=0, keepdims=True), 0.0))

    return pl.pallas_call(
        body, name=name, grid=(nt,),
        in_specs=[pl.BlockSpec((RB, 2 * ff), lambda i: (i, 0)), prev, nxt,
                  pl.BlockSpec((RB, ff), lambda i: (i, 0)), dprev, dnxt, pl.BlockSpec((8, ff), lambda i: (0, 0))],
        out_specs=[pl.BlockSpec((RB, 2 * ff), lambda i: (i, 0)), pl.BlockSpec((8, ff), lambda i: (0, 0))],
        out_shape=[jax.ShapeDtypeStruct((t, 2 * ff), BF16), jax.ShapeDtypeStruct((8, ff), F32)],
        compiler_params=_params(("arbitrary",)))(up, up, up, da, da, da, cw)


def _rot(z):
    w = z.shape[1]
    lane = lax.broadcasted_iota(jnp.int32, z.shape, 1)
    return jnp.where((lane % 64) < 32, -pltpu.roll(z, w - 32, 1), pltpu.roll(z, 32, 1))


def rope_fwd(name, qkv, cos, sin, gains, dq, dkv):
    t, nqkv = qkv.shape
    nh, nkv = dq // HEAD, dkv // HEAD

    def body(qkv_ref, cos_ref, sin_ref, g_ref, qr_ref, kr_ref):
        cs, sn = cos_ref[...], sin_ref[...]
        for hd in range(nh + nkv):
            c0 = hd * HEAD
            xh = qkv_ref[:, c0:c0 + HEAD].astype(F32)
            r = lax.rsqrt(jnp.mean(xh * xh, axis=-1, keepdims=True) + EPS)
            y = xh * r * (g_ref[0:1, :] if hd < nh else g_ref[1:2, :])
            yr = (y * cs + _rot(y) * sn).astype(BF16)
            if hd < nh:
                qr_ref[:, c0:c0 + HEAD] = yr
            else:
                kr_ref[:, c0 - dq:c0 - dq + HEAD] = yr

    return pl.pallas_call(
        body, name=name, grid=(t // RB,),
        in_specs=[pl.BlockSpec((RB, nqkv), lambda i: (i, 0)), pl.BlockSpec((RB, HEAD), lambda i: (i, 0)),
                  pl.BlockSpec((RB, HEAD), lambda i: (i, 0)), pl.BlockSpec((8, HEAD), lambda i: (0, 0))],
        out_specs=[pl.BlockSpec((RB, dq), lambda i: (i, 0)), pl.BlockSpec((RB, dkv), lambda i: (i, 0))],
        out_shape=[jax.ShapeDtypeStruct((t, dq), BF16), jax.ShapeDtypeStruct((t, dkv), BF16)],
        compiler_params=_params(("parallel",)))(qkv, cos, sin, gains)


def rope_bwd(name, qkv, dqr, dkr, dv, cos, sin, gains):
    t, nqkv = qkv.shape
    dq, dkv = dqr.shape[1], dkr.shape[1]
    nh, nkv = dq // HEAD, dkv // HEAD

    def body(qkv_ref, dq_ref, dk_ref, dv_ref, cos_ref, sin_ref, g_ref, out_ref, dg_ref):
        @pl.when(pl.program_id(0) == 0)
        def _():
            dg_ref[...] = jnp.zeros_like(dg_ref)

        cs, sn = cos_ref[...], sin_ref[...]
        zero = jnp.zeros((1, HEAD), F32)
        gq, gk = zero, zero
        for hd in range(nh + nkv):
            c0 = hd * HEAD
            xh = qkv_ref[:, c0:c0 + HEAD].astype(F32)
            r = lax.rsqrt(jnp.mean(xh * xh, axis=-1, keepdims=True) + EPS)
            xhat = xh * r
            dy = dq_ref[:, c0:c0 + HEAD] if hd < nh else dk_ref[:, c0 - dq:c0 - dq + HEAD]
            tt = dy * cs - _rot(dy * sn)
            gsum = jnp.sum(tt * xhat, axis=0, keepdims=True)
            if hd < nh:
                gq = gq + gsum
            else:
                gk = gk + gsum
            dxh = tt * (g_ref[0:1, :] if hd < nh else g_ref[1:2, :])
            dx = r * (dxh - xhat * jnp.mean(dxh * xhat, axis=-1, keepdims=True))
            out_ref[:, c0:c0 + HEAD] = dx.astype(BF16)
        out_ref[:, dq + dkv:] = dv_ref[...].astype(BF16)
        dg_ref[...] += _rows3(gq, gk, zero, HEAD)

    return pl.pallas_call(
        body, name=name, grid=(t // RB,),
        in_specs=[pl.BlockSpec((RB, nqkv), lambda i: (i, 0)), pl.BlockSpec((RB, dq), lambda i: (i, 0)),
                  pl.BlockSpec((RB, dkv), lambda i: (i, 0)), pl.BlockSpec((RB, dkv), lambda i: (i, 0)),
                  pl.BlockSpec((RB, HEAD), lambda i: (i, 0)), pl.BlockSpec((RB, HEAD), lambda i: (i, 0)),
                  pl.BlockSpec((8, HEAD), lambda i: (0, 0))],
        out_specs=[pl.BlockSpec((RB, nqkv), lambda i: (i, 0)), pl.BlockSpec((8, HEAD), lambda i: (0, 0))],
        out_shape=[jax.ShapeDtypeStruct((t, nqkv), BF16), jax.ShapeDtypeStruct((8, HEAD), F32)],
        compiler_params=_params(("arbitrary",)))(qkv, dqr, dkr, dv, cos, sin, gains)


def resid_bwd(name, dx, dh, x, mod_n, sh, sc, y_prev=None, mod_g=None, gi=0):
    t, d = x.shape
    has_prev = y_prev is not None

    def body(*refs):
        if has_prev:
            dx_ref, dh_ref, x_ref, mn_ref, y_ref, mg_ref, dxo_ref, dy_ref, acc_ref = refs
        else:
            dx_ref, dh_ref, x_ref, mn_ref, dxo_ref, acc_ref = refs
        i = pl.program_id(0)
        seg = jnp.minimum(i, 1)

        @pl.when(i == 0)
        def _():
            acc_ref[...] = jnp.zeros_like(acc_ref)

        xv, dhv = x_ref[...], dh_ref[...]
        r = lax.rsqrt(jnp.mean(xv * xv, axis=-1, keepdims=True) + EPS)
        xhat = xv * r
        m = mn_ref[seg]
        dxh = dhv * (1.0 + m[sc:sc + 1, :])
        dxo = dx_ref[...] + r * (dxh - xhat * jnp.mean(dxh * xhat, axis=-1, keepdims=True))
        dxo_ref[...] = dxo
        s2 = jnp.zeros((1, d), F32)
        if has_prev:
            dy_ref[...] = (mg_ref[seg][gi:gi + 1, :] * dxo).astype(BF16)
            s2 = jnp.sum(dxo * y_ref[...].astype(F32), axis=0, keepdims=True)
        acc_ref[seg] = acc_ref[seg] + _rows3(jnp.sum(dhv, axis=0, keepdims=True),
                                             jnp.sum(dhv * xhat, axis=0, keepdims=True), s2, d)

    row = pl.BlockSpec((RB, d), lambda i: (i, 0))
    modspec = pl.BlockSpec((2, 8, d), lambda i: (0, 0, 0))
    in_specs, operands = [row, row, row, modspec], [dx, dh, x, mod_n]
    out_specs, out_shape = [row], [jax.ShapeDtypeStruct((t, d), F32)]
    if has_prev:
        in_specs += [row, modspec]
        operands += [y_prev, mod_g]
        out_specs.append(row)
        out_shape.append(jax.ShapeDtypeStruct((t, d), BF16))
    out_specs.append(modspec)
    out_shape.append(jax.ShapeDtypeStruct((2, 8, d), F32))
    return pl.pallas_call(body, name=name, grid=(t // RB,), in_specs=in_specs, out_specs=out_specs,
                          out_shape=out_shape, compiler_params=_params(("arbitrary",)))(*operands)


def loss_head(name, xf, target, y_last, mod, gi):
    t, d = xf.shape

    def body(x_ref, t_ref, y_ref, mod_ref, dx_ref, dy_ref, acc_ref, lp_ref):
        i = pl.program_id(0)
        seg = jnp.minimum(i, 1)

        @pl.when(i == 0)
        def _():
            acc_ref[...] = jnp.zeros_like(acc_ref)
            lp_ref[...] = jnp.zeros_like(lp_ref)

        lat = i >= 1
        err = jnp.where(lat, x_ref[...] - t_ref[...], 0.0)
        dxv = err / d
        dx_ref[...] = dxv
        dy_ref[...] = (mod_ref[seg][gi:gi + 1, :] * dxv).astype(BF16)
        zero = jnp.zeros((1, d), F32)
        lp_ref[...] += _rows3(jnp.sum(err * err, axis=0, keepdims=True), zero, zero, d)
        acc_ref[seg] = acc_ref[seg] + _rows3(zero, zero, jnp.sum(dxv * y_ref[...].astype(F32), axis=0, keepdims=True), d)

    row = pl.BlockSpec((RB, d), lambda i: (i, 0))
    modspec = pl.BlockSpec((2, 8, d), lambda i: (0, 0, 0))
    return pl.pallas_call(
        body, name=name, grid=(t // RB,),
        in_specs=[row, pl.BlockSpec((RB, d), lambda i: (jnp.maximum(i - 1, 0), 0)), row, modspec],
        out_specs=[row, row, modspec, pl.BlockSpec((8, d), lambda i: (0, 0))],
        out_shape=[jax.ShapeDtypeStruct((t, d), F32), jax.ShapeDtypeStruct((t, d), BF16),
                   jax.ShapeDtypeStruct((2, 8, d), F32), jax.ShapeDtypeStruct((8, d), F32)],
        compiler_params=_params(("arbitrary",)))(xf, target, y_last, mod)


def _kv_specs(width, colblk, nbk):
    return [pl.BlockSpec((CTX, width), lambda i: (0, colblk)),
            pl.BlockSpec((BLK, width), lambda i: (jnp.maximum(i - 1, 0), colblk)),
            pl.BlockSpec((BLK, width), lambda i: (i, colblk)),
            pl.BlockSpec((BLK, width), lambda i: (jnp.minimum(i + 1, nbk - 1), colblk))]


def _band_mask(i, seq):
    nk = CTX + 3 * BLK
    qrow = lax.broadcasted_iota(jnp.int32, (GROUP * BLK, nk), 0) % BLK
    col = lax.broadcasted_iota(jnp.int32, (GROUP * BLK, nk), 1)
    cb = col - CTX
    kpos = (i - 3) * BLK + cb
    band = (i >= 2) & (jnp.abs(BLK + qrow - cb) <= WINDOW) & (kpos >= 0) & (kpos < seq)
    return (col < CTX) | band


def _stack_heads(ref, h):
    return jnp.concatenate([ref[:, (h * GROUP + g) * HEAD:(h * GROUP + g + 1) * HEAD] for g in range(GROUP)], axis=0)


def _stack_cols(v, h):
    return jnp.concatenate([_get_col(v, h * GROUP + g) for g in range(GROUP)], axis=0)


def _sink_col(sink_ref, h):
    rowg = lax.broadcasted_iota(jnp.int32, (GROUP * BLK, 1), 0) // BLK
    sk = jnp.full((GROUP * BLK, 1), sink_ref[h * GROUP], F32)
    for g in range(1, GROUP):
        sk = jnp.where(rowg == g, sink_ref[h * GROUP + g], sk)
    return sk


def attn_fwd(name, qr, kr, qkv, sink, seq):
    t, dq = qr.shape
    dkv = kr.shape[1]
    nbk, nkv = t // BLK, dkv // HEAD
    vcol = (dq + dkv) // dkv
    scale = HEAD ** -0.5

    def body(sink_ref, q_ref, kc, kp, ko, kn, vc, vp, vo, vn, o_ref, lse_ref):
        i = pl.program_id(0)
        mask = _band_mask(i, seq)
        lse = jnp.zeros((BLK, LANE), F32)
        for h in range(nkv):
            hs = slice(h * HEAD, (h + 1) * HEAD)
            k = jnp.concatenate([kc[:, hs], kp[:, hs], ko[:, hs], kn[:, hs]], axis=0)
            v = jnp.concatenate([vc[:, hs], vp[:, hs], vo[:, hs], vn[:, hs]], axis=0)
            q4 = _stack_heads(q_ref, h)
            s = jnp.where(mask, lax.dot_general(q4, k, NT, preferred_element_type=F32) * scale, NEG)
            sk = _sink_col(sink_ref, h)
            m = jnp.maximum(jnp.max(s, axis=-1, keepdims=True), sk)
            e = jnp.exp(s - m)
            den = jnp.sum(e, axis=-1, keepdims=True) + jnp.exp(sk - m)
            o4 = jnp.dot((e / den).astype(BF16), v, preferred_element_type=F32)
            l4 = m + jnp.log(den)
            for g in range(GROUP):
                hg = h * GROUP + g
                o_ref[:, hg * HEAD:(hg + 1) * HEAD] = o4[g * BLK:(g + 1) * BLK].astype(BF16)
                lse = _put_col(lse, hg, l4[g * BLK:(g + 1) * BLK])
        lse_ref[...] = lse

    return pl.pallas_call(
        body, name=name, grid=(nbk,),
        in_specs=[pl.BlockSpec(memory_space=pltpu.SMEM), pl.BlockSpec((BLK, dq), lambda i: (i, 0))]
        + _kv_specs(dkv, 0, nbk) + _kv_specs(dkv, vcol, nbk),
        out_specs=[pl.BlockSpec((BLK, dq), lambda i: (i, 0)), pl.BlockSpec((BLK, LANE), lambda i: (i, 0))],
        out_shape=[jax.ShapeDtypeStruct((t, dq), BF16), jax.ShapeDtypeStruct((t, LANE), F32)],
        compiler_params=_params(("parallel",)))(sink, qr, kr, kr, kr, kr, qkv, qkv, qkv, qkv)


def attn_bwd_q(name, qr, kr, qkv, sink, do, o, lse, seq):
    t, dq = qr.shape
    dkv = kr.shape[1]
    nbk, nkv = t // BLK, dkv // HEAD
    vcol = (dq + dkv) // dkv
    scale = HEAD ** -0.5

    def body(sink_ref, q_ref, kc, kp, ko, kn, vc, vp, vo, vn, do_ref, o_ref, lse_ref,
             dq_ref, dl_ref, dkc_ref, dvc_ref, ds_ref):
        i = pl.program_id(0)

        @pl.when(i == 0)
        def _():
            dkc_ref[...] = jnp.zeros_like(dkc_ref)
            dvc_ref[...] = jnp.zeros_like(dvc_ref)
            ds_ref[...] = jnp.zeros_like(ds_ref)

        mask = _band_mask(i, seq)
        lse = lse_ref[...]
        delta = jnp.zeros((BLK, LANE), F32)
        dsink = jnp.zeros((8, LANE), F32)
        for h in range(nkv):
            hs = slice(h * HEAD, (h + 1) * HEAD)
            k = jnp.concatenate([kc[:, hs], kp[:, hs], ko[:, hs], kn[:, hs]], axis=0)
            v = jnp.concatenate([vc[:, hs], vp[:, hs], vo[:, hs], vn[:, hs]], axis=0)
            q4, do4 = _stack_heads(q_ref, h), _stack_heads(do_ref, h)
            d4 = jnp.sum(do4.astype(F32) * _stack_heads(o_ref, h).astype(F32), axis=-1, keepdims=True)
            l4 = _stack_cols(lse, h)
            s = jnp.where(mask, lax.dot_general(q4, k, NT, preferred_element_type=F32) * scale, NEG)
            p = jnp.exp(s - l4)
            dp = lax.dot_general(do4, v, NT, preferred_element_type=F32)
            dsb = (p * (dp - d4) * scale).astype(BF16)
            pb = p.astype(BF16)
            dq4 = jnp.dot(dsb, k, preferred_element_type=F32)
            dkc_ref[:, hs] += lax.dot_general(dsb[:, :CTX], q4, TN, preferred_element_type=F32)
            dvc_ref[:, hs] += lax.dot_general(pb[:, :CTX], do4, TN, preferred_element_type=F32)
            dsk = -jnp.exp(_sink_col(sink_ref, h) - l4) * d4
            for g in range(GROUP):
                hg = h * GROUP + g
                rs = slice(g * BLK, (g + 1) * BLK)
                dq_ref[:, hg * HEAD:(hg + 1) * HEAD] = dq4[rs]
                delta = _put_col(delta, hg, d4[rs])
                dsink = _put_col(dsink, hg, jnp.sum(dsk[rs], axis=0, keepdims=True))
        dl_ref[...] = delta
        rows = lax.broadcasted_iota(jnp.int32, (8, LANE), 0)
        ds_ref[...] += jnp.where(rows == 0, dsink, 0.0)

    blk = lambda w: pl.BlockSpec((BLK, w), lambda i: (i, 0))
    const = lambda r, w: pl.BlockSpec((r, w), lambda i: (0, 0))
    return pl.pallas_call(
        body, name=name, grid=(nbk,),
        in_specs=[pl.BlockSpec(memory_space=pltpu.SMEM), blk(dq)] + _kv_specs(dkv, 0, nbk) + _kv_specs(dkv, vcol, nbk)
        + [blk(dq), blk(dq), blk(LANE)],
        out_specs=[blk(dq), blk(LANE), const(CTX, dkv), const(CTX, dkv), const(8, LANE)],
        out_shape=[jax.ShapeDtypeStruct((t, dq), F32), jax.ShapeDtypeStruct((t, LANE), F32),
                   jax.ShapeDtypeStruct((CTX, dkv), F32), jax.ShapeDtypeStruct((CTX, dkv), F32),
                   jax.ShapeDtypeStruct((8, LANE), F32)],
        compiler_params=_params(("arbitrary",)))(sink, qr, kr, kr, kr, kr, qkv, qkv, qkv, qkv, do, o, lse)


def attn_bwd_kv(name, qr, kr, qkv, do, lse, delta, seq):
    t, dq = qr.shape
    dkv = kr.shape[1]
    nbk, nbl, nkv = t // BLK, seq // BLK, dkv // HEAD
    cb = CTX // BLK
    vcol = (dq + dkv) // dkv
    scale = HEAD ** -0.5

    def qspec(w, d):
        return pl.BlockSpec((BLK, w), lambda j: (jnp.clip(j + cb + d, cb, nbk - 1), 0))

    def body(k_ref, v_ref, *refs):
        dk_ref, dv_ref = refs[-2], refs[-1]
        j = pl.program_id(0)
        qrow = lax.broadcasted_iota(jnp.int32, (GROUP * BLK, BLK), 0) % BLK
        kcol = lax.broadcasted_iota(jnp.int32, (GROUP * BLK, BLK), 1)
        for h in range(nkv):
            hs = slice(h * HEAD, (h + 1) * HEAD)
            kh, vh = k_ref[:, hs], v_ref[:, hs]
            dk_h = jnp.zeros((BLK, HEAD), F32)
            dv_h = jnp.zeros((BLK, HEAD), F32)
            for di, d in enumerate((-1, 0, 1)):
                q_ref, do_ref, lse_ref, dl_ref = refs[4 * di:4 * di + 4]
                n = j + d
                msk = (n >= 0) & (n < nbl) & (jnp.abs(d * BLK + qrow - kcol) <= WINDOW)
                q4, do4 = _stack_heads(q_ref, h), _stack_heads(do_ref, h)
                l4, d4 = _stack_cols(lse_ref[...], h), _stack_cols(dl_ref[...], h)
                s = jnp.where(msk, lax.dot_general(q4, kh, NT, preferred_element_type=F32) * scale, NEG)
                p = jnp.exp(s - l4)
                dv_h += lax.dot_general(p.astype(BF16), do4, TN, preferred_element_type=F32)
                dp = lax.dot_general(do4, vh, NT, preferred_element_type=F32)
                dk_h += lax.dot_general((p * (dp - d4) * scale).astype(BF16), q4, TN, preferred_element_type=F32)
            dk_ref[:, hs] = dk_h
            dv_ref[:, hs] = dv_h

    in_specs = [pl.BlockSpec((BLK, dkv), lambda j: (j + cb, 0)), pl.BlockSpec((BLK, dkv), lambda j: (j + cb, vcol))]
    operands = [kr, qkv]
    for d in (-1, 0, 1):
        in_specs += [qspec(dq, d), qspec(dq, d), qspec(LANE, d), qspec(LANE, d)]
        operands += [qr, do, lse, delta]
    return pl.pallas_call(
        body, name=name, grid=(nbl,), in_specs=in_specs,
        out_specs=[pl.BlockSpec((BLK, dkv), lambda j: (j, 0)), pl.BlockSpec((BLK, dkv), lambda j: (j, 0))],
        out_shape=[jax.ShapeDtypeStruct((seq, dkv), F32), jax.ShapeDtypeStruct((seq, dkv), F32)],
        compiler_params=_params(("parallel",)))(*operands)


def ada_fwd(name, cond, w_ada):
    nl, d, n = w_ada.shape
    tn = _pick(n, 1024)

    def body(c_ref, w_ref, out_ref):
        cv = c_ref[...]
        out_ref[...] = jnp.dot((cv * _sigmoid(cv)).astype(BF16), w_ref[...].astype(BF16), preferred_element_type=F32)

    return pl.pallas_call(
        body, name=name, grid=(nl, n // tn),
        in_specs=[pl.BlockSpec((16, d), lambda l, j: (0, 0)), pl.BlockSpec((None, d, tn), lambda l, j: (l, 0, j))],
        out_specs=pl.BlockSpec((None, 16, tn), lambda l, j: (l, 0, j)),
        out_shape=jax.ShapeDtypeStruct((nl, 16, n), F32), compiler_params=_params(("parallel", "parallel")))(cond, w_ada)


def ada_bwd_cond(name, dsum, w_ada):
    nl, d, n = w_ada.shape
    tn = _pick(n, 1024)

    def body(g_ref, w_ref, out_ref):
        @pl.when((pl.program_id(0) == 0) & (pl.program_id(1) == 0))
        def _():
            out_ref[...] = jnp.zeros_like(out_ref)

        out_ref[...] += lax.dot_general(g_ref[...].astype(BF16), w_ref[...].astype(BF16), NT, preferred_element_type=F32)

    return pl.pallas_call(
        body, name=name, grid=(nl, n // tn),
        in_specs=[pl.BlockSpec((None, 8, tn), lambda l, j: (l, 0, j)), pl.BlockSpec((None, d, tn), lambda l, j: (l, 0, j))],
        out_specs=pl.BlockSpec((8, d), lambda l, j: (0, 0)),
        out_shape=jax.ShapeDtypeStruct((8, d), F32), compiler_params=_params(("arbitrary", "arbitrary")))(dsum, w_ada)


def ada_grad_w(name, cond, rhs):
    nl, _, n = rhs.shape
    d = cond.shape[1]
    tr, tn = _pick(d, 512), _pick(n, 1024)

    def body(c_ref, r_ref, out_ref):
        cv = c_ref[...]
        out_ref[...] = lax.dot_general((cv * _sigmoid(cv)).astype(BF16), r_ref[...].astype(BF16), TN, preferred_element_type=F32)

    return pl.pallas_call(
        body, name=name, grid=(nl, d // tr, n // tn),
        in_specs=[pl.BlockSpec((16, tr), lambda l, i, j: (0, i)), pl.BlockSpec((None, 16, tn), lambda l, i, j: (l, 0, j))],
        out_specs=pl.BlockSpec((None, tr, tn), lambda l, i, j: (l, i, j)),
        out_shape=jax.ShapeDtypeStruct((nl, d, n), F32),
        compiler_params=_params(("parallel", "parallel", "parallel")))(cond, rhs)


def adamw(name, g, g_spec, w, m, v, tr):
    nl, r, c = w.shape
    spec = pl.BlockSpec((None, tr, c), lambda l, i: (l, i, 0))

    def body(g_ref, w_ref, m_ref, v_ref, go_ref, d_ref, mo_ref, vo_ref):
        gv = g_ref[...]
        mn = B1 * m_ref[...] + (1.0 - B1) * gv
        vn = B2 * v_ref[...] + (1.0 - B2) * (gv * gv)
        m_hat = mn / (1.0 - B1 ** STEP)
        v_hat = vn / (1.0 - B2 ** STEP)
        go_ref[...] = gv
        d_ref[...] = -LR * (m_hat / (jnp.sqrt(v_hat) + ADAM_EPS) + WD * w_ref[...])
        mo_ref[...] = mn
        vo_ref[...] = vn

    return pl.pallas_call(
        body, name=name, grid=(nl, r // tr), in_specs=[g_spec, spec, spec, spec], out_specs=[spec] * 4,
        out_shape=[jax.ShapeDtypeStruct(w.shape, F32)] * 4, compiler_params=_params(("parallel", "parallel")))(g, w, m, v)


def adamw_small(name, g, w, m, v):
    shape = w.shape
    r3 = lambda a: a.reshape(1, -1, shape[-1]).astype(F32)
    rows = r3(w).shape[1]
    outs = adamw(name, r3(g), pl.BlockSpec((None, rows, shape[-1]), lambda l, i: (l, i, 0)), r3(w), r3(m), r3(v), rows)
    return [o.reshape(shape) for o in outs]


def _place():
    x, y, c = lax.axis_index("x"), lax.axis_index("y"), lax.axis_index("c")
    return x, y, c, [(1 - x, y), (x, 1 - y), (1 - x, 1 - y)]


def small_allgather(name, v):
    r, w = v.shape

    def body(x_ref, out_ref, send_sems, recv_sems, local_sem):
        x, y, c, chips = _place()
        me, sibling = (x, y, c), (x, y, 1 - c)

        def slot(px, py, pc):
            return out_ref.at[4 * px + 2 * py + pc]

        def copy(k, block, to, src=None):
            return pltpu.make_async_remote_copy(
                src_ref=slot(*block) if src is None else src, dst_ref=slot(*block),
                send_sem=send_sems.at[k], recv_sem=recv_sems.at[k], device_id=to, device_id_type=MESH)

        mine = pltpu.make_async_copy(x_ref, slot(*me), local_sem)
        mine.start()
        first = [copy(0, me, sibling, src=x_ref)]
        first += [copy(1 + j, me, (*chip, c), src=x_ref) for j, chip in enumerate(chips)]
        for cp in first:
            cp.start()
        passed = [copy(4 + j, (*chip, c), sibling) for j, chip in enumerate(chips)]
        for j, chip in enumerate(chips):
            copy(1 + j, (*chip, c), me).wait_recv()
            passed[j].start()
        copy(0, sibling, me).wait_recv()
        for j, chip in enumerate(chips):
            copy(4 + j, (*chip, 1 - c), me).wait_recv()
        for cp in first + passed:
            cp.wait_send()
        mine.wait()

    return pl.pallas_call(
        body, name=name, out_shape=jax.ShapeDtypeStruct((8, r, w), v.dtype),
        in_specs=[pl.BlockSpec(memory_space=pltpu.VMEM)], out_specs=pl.BlockSpec(memory_space=pltpu.VMEM),
        scratch_shapes=[pltpu.SemaphoreType.DMA((7,)), pltpu.SemaphoreType.DMA((7,)), pltpu.SemaphoreType.DMA],
        compiler_params=pltpu.CompilerParams(vmem_limit_bytes=VMEM_LIMIT))(v)


def gather_flat(name, parts):
    flat = jnp.concatenate([p.reshape(-1).astype(F32) for p in parts])
    n = flat.shape[0]
    rows = _cdiv(n, 8 * LANE) * 8
    flat = jnp.pad(flat, (0, rows * LANE - n))
    return small_allgather(name, flat.reshape(rows, LANE)).reshape(8, rows * LANE)


def sum8(name, g):
    p = g.shape[1]
    g3 = g.reshape(8, p // LANE, LANE)
    tr = _pick(p // LANE, 1024, 8)

    def body(g_ref, out_ref):
        acc = g_ref[0]
        for k in range(1, 8):
            acc = acc + g_ref[k]
        out_ref[...] = acc

    return pl.pallas_call(
        body, name=name, grid=(p // LANE // tr,),
        in_specs=[pl.BlockSpec((8, tr, LANE), lambda i: (0, i, 0))], out_specs=pl.BlockSpec((tr, LANE), lambda i: (i, 0)),
        out_shape=jax.ShapeDtypeStruct((p // LANE, LANE), F32), compiler_params=_params(("parallel",)))(g3).reshape(p)


HBM_SPEC = pl.BlockSpec(memory_space=pltpu.HBM)


def gather_weights(name, wc_loc, wr_loc):
    bufs = (wc_loc, wr_loc)

    def body(wc_ref, wr_ref, oc_ref, or_ref, send_sems, recv_sems, local_sems):
        x, y, c, chips = _place()
        sibling = (x, y, 1 - c)
        me = 2 * x + y
        copies = []
        for a, (src, out) in enumerate(((wc_ref, oc_ref), (wr_ref, or_ref))):
            half = src.shape[0] // 2

            def rows(pc, half=half):
                return pl.ds(pc * half, half)

            def copy(k, chip, pc, to, out=out, src=None, rows=rows):
                dst = out.at[2 * chip[0] + chip[1], rows(pc), :]
                return pltpu.make_async_remote_copy(
                    src_ref=dst if src is None else src, dst_ref=dst, send_sem=send_sems.at[k],
                    recv_sem=recv_sems.at[k], device_id=to, device_id_type=MESH)

            mine = pltpu.make_async_copy(src, out.at[me], local_sems.at[a])
            mine.start()
            first = [copy(6 * a + j, (x, y), c, (*chip, c), src=src.at[rows(c), :]) for j, chip in enumerate(chips)]
            for cp in first:
                cp.start()
            passed = [copy(6 * a + 3 + j, chip, c, sibling) for j, chip in enumerate(chips)]
            copies.append((mine, first, passed, copy))
        for a, (mine, first, passed, copy) in enumerate(copies):
            for j, chip in enumerate(chips):
                copy(6 * a + j, chip, c, (x, y, c)).wait_recv()
                passed[j].start()
        for a, (mine, first, passed, copy) in enumerate(copies):
            for j, chip in enumerate(chips):
                copy(6 * a + 3 + j, chip, 1 - c, (x, y, c)).wait_recv()
            for cp in first + passed:
                cp.wait_send()
            mine.wait()

    return pl.pallas_call(
        body, name=name,
        out_shape=[jax.ShapeDtypeStruct((4,) + b.shape, b.dtype) for b in bufs],
        in_specs=[HBM_SPEC, HBM_SPEC], out_specs=[HBM_SPEC, HBM_SPEC],
        scratch_shapes=[pltpu.SemaphoreType.DMA((12,)), pltpu.SemaphoreType.DMA((12,)), pltpu.SemaphoreType.DMA((2,))],
    )(*bufs)


def pair_exchange(name, bufs):
    def body(*refs):
        ins, outs, (send_sems, recv_sems) = refs[:len(bufs)], refs[len(bufs):2 * len(bufs)], refs[2 * len(bufs):]
        x, y, c, _ = _place()
        cps = []
        for a, (src, out) in enumerate(zip(ins, outs)):
            half = src.shape[1] // 2
            cp = pltpu.make_async_remote_copy(
                src_ref=src.at[:, pl.ds((1 - c) * half, half), :], dst_ref=out, send_sem=send_sems.at[a],
                recv_sem=recv_sems.at[a], device_id=(x, y, 1 - c), device_id_type=MESH)
            cp.start()
            cps.append(cp)
        for cp in cps:
            cp.wait()

    return pl.pallas_call(
        body, name=name,
        out_shape=[jax.ShapeDtypeStruct((b.shape[0], b.shape[1] // 2, b.shape[2]), b.dtype) for b in bufs],
        in_specs=[HBM_SPEC] * len(bufs), out_specs=[HBM_SPEC] * len(bufs),
        scratch_shapes=[pltpu.SemaphoreType.DMA((len(bufs),)), pltpu.SemaphoreType.DMA((len(bufs),))])(*bufs)


def pair_add(name, buf, got, cidx):
    s, r, c = buf.shape
    half = r // 2
    tr = _pick(half, max(HALO, (4 * 1024 * 1024) // (2 * c)), HALO)
    per = half // tr

    def body(c_ref, a_ref, b_ref, out_ref):
        out_ref[...] = (a_ref[...].astype(F32) + b_ref[...].astype(F32)).astype(BF16)

    return pl.pallas_call(
        body, name=name,
        grid_spec=pltpu.PrefetchScalarGridSpec(
            num_scalar_prefetch=1, grid=(s, per),
            in_specs=[pl.BlockSpec((None, tr, c), lambda k, i, cr: (k, cr[0] * per + i, 0)),
                      pl.BlockSpec((None, tr, c), lambda k, i, cr: (k, i, 0))],
            out_specs=pl.BlockSpec((None, tr, c), lambda k, i, cr: (k, i, 0))),
        out_shape=jax.ShapeDtypeStruct((s, half, c), BF16),
        compiler_params=_params(("parallel", "parallel")))(cidx, buf, got)


def chip_scatter(name, bufs):
    def body(*refs):
        ins, outs = refs[:len(bufs)], refs[len(bufs):2 * len(bufs)]
        send_sems, recv_sems, local_sems = refs[2 * len(bufs):]
        x, y, c, chips = _place()
        me = 2 * x + y
        started = []
        for a, (src, out) in enumerate(zip(ins, outs)):
            mine = pltpu.make_async_copy(src.at[me], out.at[me], local_sems.at[a])
            mine.start()
            started.append(mine)
            for j, chip in enumerate(chips):
                cp = pltpu.make_async_remote_copy(
                    src_ref=src.at[2 * chip[0] + chip[1]], dst_ref=out.at[me], send_sem=send_sems.at[3 * a + j],
                    recv_sem=recv_sems.at[3 * a + j], device_id=(*chip, c), device_id_type=MESH)
                cp.start()
                started.append(cp)
        for cp in started:
            cp.wait()

    n = len(bufs)
    return pl.pallas_call(
        body, name=name, out_shape=[jax.ShapeDtypeStruct(b.shape, b.dtype) for b in bufs],
        in_specs=[HBM_SPEC] * n, out_specs=[HBM_SPEC] * n,
        scratch_shapes=[pltpu.SemaphoreType.DMA((3 * n,)), pltpu.SemaphoreType.DMA((3 * n,)),
                        pltpu.SemaphoreType.DMA((n,))])(*bufs)


def chip_add(name, got):
    s, h, c = got.shape
    tr = _pick(h, max(HALO, (2 * 1024 * 1024) // (2 * c)), HALO)

    def body(g_ref, out_ref):
        acc = g_ref[0].astype(F32)
        for k in range(1, s):
            acc = acc + g_ref[k].astype(F32)
        out_ref[...] = acc

    return pl.pallas_call(
        body, name=name, grid=(h // tr,),
        in_specs=[pl.BlockSpec((s, tr, c), lambda i: (0, i, 0))], out_specs=pl.BlockSpec((tr, c), lambda i: (i, 0)),
        out_shape=jax.ShapeDtypeStruct((h, c), F32), compiler_params=_params(("parallel",)))(got)


def pair_join(name, halves):
    def body(*refs):
        ins, outs = refs[:len(halves)], refs[len(halves):2 * len(halves)]
        send_sems, recv_sems, local_sems = refs[2 * len(halves):]
        x, y, c, _ = _place()
        started = []
        for a, (src, out) in enumerate(zip(ins, outs)):
            half = src.shape[0]
            dst = out.at[pl.ds(c * half, half), :]
            mine = pltpu.make_async_copy(src, dst, local_sems.at[a])
            mine.start()
            cp = pltpu.make_async_remote_copy(src_ref=src, dst_ref=dst, send_sem=send_sems.at[a], recv_sem=recv_sems.at[a],
                                              device_id=(x, y, 1 - c), device_id_type=MESH)
            cp.start()
            started += [mine, cp]
        for cp in started:
            cp.wait()

    n = len(halves)
    return pl.pallas_call(
        body, name=name, out_shape=[jax.ShapeDtypeStruct((2 * h.shape[0], h.shape[1]), h.dtype) for h in halves],
        in_specs=[HBM_SPEC] * n, out_specs=[HBM_SPEC] * n,
        scratch_shapes=[pltpu.SemaphoreType.DMA((n,)), pltpu.SemaphoreType.DMA((n,)), pltpu.SemaphoreType.DMA((n,))])(*halves)


def _rope_tables(seq):
    rows = seq // GRID_W
    row = jnp.repeat(jnp.arange(rows), GRID_W).astype(F32)
    col = jnp.tile(jnp.arange(GRID_W), rows).astype(F32)
    pairs = HEAD // 4
    inv = ROPE_BASE ** (-jnp.arange(pairs, dtype=F32) / pairs)
    ang = jnp.stack([row[:, None] * inv, col[:, None] * inv], axis=1)
    ang = jnp.broadcast_to(ang[:, :, None, :], (seq, 2, 2, pairs)).reshape(seq, HEAD)
    cos = jnp.concatenate([jnp.ones((CTX, HEAD), F32), jnp.cos(ang)], axis=0)
    sin = jnp.concatenate([jnp.zeros((CTX, HEAD), F32), jnp.sin(ang)], axis=0)
    return cos, sin


def _pad8(a):
    return jnp.pad(a, ((0, 8 - a.shape[0]), (0, 0)))


def kernel(x, c, ctx, c_ctx, w_ada, b_ada, attn_w_qkv, attn_w_o, attn_q_gain, attn_k_gain, attn_sink, sc_w_in, sc_conv, sc_w_out, ffn_w_up, ffn_conv, ffn_conv_b, ffn_w_down, loss_target, m_c_ctx, m_w_ada, m_b_ada, m_attn_w_qkv, m_attn_w_o, m_attn_q_gain, m_attn_k_gain, m_attn_sink, m_sc_w_in, m_sc_conv, m_sc_w_out, m_ffn_w_up, m_ffn_conv, m_ffn_conv_b, m_ffn_w_down, v_c_ctx, v_w_ada, v_b_ada, v_attn_w_qkv, v_attn_w_o, v_attn_q_gain, v_attn_k_gain, v_attn_sink, v_sc_w_in, v_sc_conv, v_sc_w_out, v_ffn_w_up, v_ffn_conv, v_ffn_conv_b, v_ffn_w_down):
    seq, d = x.shape[1], x.shape[2]
    depth, nada = w_ada.shape[0], w_ada.shape[2]
    n_attn, n_conv = attn_w_qkv.shape[0], sc_w_in.shape[0]
    ff = ffn_conv_b.shape[1]
    dq, dkv = d, d // GROUP
    nqkv = dq + 2 * dkv
    nh = dq // HEAD
    assert ctx.shape[1] == CTX and seq % RB == 0 and 6 * d == 4 * nada
    lay = Layout(d, ff, nqkv, depth)
    ax, ay, ac = lax.axis_index("x"), lax.axis_index("y"), lax.axis_index("c")
    chip, dev = 2 * ax + ay, 4 * ax + 2 * ay + ac
    cidx = jnp.reshape(ac, (1,)).astype(jnp.int32)

    wc_loc, wr_loc = lax.empty((d, lay.ct), BF16), lax.empty((lay.rt, d), BF16)
    wc_loc = cast_pack("pack_up", ffn_w_up, wc_loc, lay.up[0], True)
    wc_loc = cast_pack("pack_scin", sc_w_in, wc_loc, lay.scin[0], True)
    wc_loc = cast_pack("pack_qkv", attn_w_qkv, wc_loc, lay.qkv[0], True)
    wr_loc = cast_pack("pack_down", ffn_w_down, wr_loc, lay.dn[0], False)
    wr_loc = cast_pack("pack_wo", attn_w_o, wr_loc, lay.wo[0], False)
    wr_loc = cast_pack("pack_scout", sc_w_out, wr_loc, lay.scout[0], False)
    wc, wr = gather_weights("gather_weights", wc_loc, wr_loc)

    g1 = gather_flat("gather_cond", [c, sc_conv, ffn_conv])
    c_all = g1[:, :d]
    o1 = d + sc_conv.size
    sc_conv_full = jnp.concatenate([g1[2 * s, d:o1].reshape(sc_conv.shape) for s in range(4)], axis=-1)
    ffn_conv_full = jnp.concatenate([g1[2 * s, o1:o1 + ffn_conv.size].reshape(ffn_conv.shape) for s in range(4)], axis=-1)
    cond = jnp.concatenate([c_all, c_ctx[None, :], jnp.zeros((7, d), F32)], axis=0)
    ada_part = ada_fwd("ada_fwd", cond, w_ada)
    g2 = gather_flat("gather_ada", [ada_part])
    ada_all = jnp.concatenate([g2[2 * s, :ada_part.size].reshape(ada_part.shape) for s in range(4)], axis=-1)
    ada_own = jnp.stack([lax.dynamic_index_in_dim(ada_all, 8, 1, False),
                         lax.dynamic_index_in_dim(ada_all, dev, 1, False)], axis=1) + b_ada[:, None, :]
    mods = jnp.pad(ada_own.reshape(depth, 2, 6, d), ((0, 0), (0, 0), (0, 2), (0, 0)))

    cos, sin = _rope_tables(seq)
    xa = jnp.concatenate([ctx[0], x[0]], axis=0)
    cws = [_pad8(sc_conv_full[j]) for j in range(n_conv)]
    cwf = [_pad8(jnp.concatenate([ffn_conv_full[l], ffn_conv_b[l][None, :]], axis=0)) for l in range(depth)]
    gains = [_pad8(jnp.stack([attn_q_gain[j], attn_k_gain[j]])) for j in range(n_attn)]

    saved = []
    for l in range(depth):
        j, is_attn, mod = l // N_MIX, l % N_MIX == 0, mods[l]
        sv = {"x_in": xa}
        h1 = norm_mod(f"norm1_{l}", xa, mod, 0, 1)
        if is_attn:
            qkv = mm_nn_col(f"qkv_{l}", h1, wc, lay.qkv[j], lay.nb_qkv)
            qr, kr = rope_fwd(f"rope_{l}", qkv, cos, sin, gains[j], dq, dkv)
            o, lse = attn_fwd(f"attn_{l}", qr, kr, qkv, attn_sink[j], seq)
            xa, y_m = mm_nn_row(f"wo_{l}", o, wr, lay.wo[j], lay.kb_o, xa, mod, 2)
            sv.update(qkv=qkv, qr=qr, kr=kr, o=o, lse=lse)
        else:
            u = mm_nn_col(f"scin_{l}", h1, wc, lay.scin[j], lay.nb_sc)
            z = sc_gate_fwd(f"scgate_{l}", u, cws[j])
            xa, y_m = mm_nn_row(f"scout_{l}", z, wr, lay.scout[j], lay.kb_o, xa, mod, 2)
            sv.update(u=u, z=z)
        h2 = norm_mod(f"norm2_{l}", xa, mod, 3, 4)
        up = mm_nn_col(f"up_{l}", h2, wc, lay.up[l], lay.nb_up)
        act = ffn_act_fwd(f"act_{l}", up, cwf[l])
        sv.update(h1=h1, y_m=y_m, x_mid=xa, h2=h2, up=up, act=act)
        xa, y_f = mm_nn_row(f"down_{l}", act, wr, lay.dn[l], lay.kb_dn, xa, mod, 5)
        sv["y_f"] = y_f
        saved.append(sv)

    dx, dy, acc, lp = loss_head("loss", xa, loss_target[0], saved[-1]["y_f"], mods[-1], 5)
    loss = lax.psum(0.5 * jnp.sum(lp[0]) / d, ("x", "y", "c"))
    gc, gr = lax.empty((4, d, lay.ct), BF16), lax.empty((4, lay.rt, d), BF16)
    d_mod = [jnp.zeros((2, 6, d), F32) for _ in range(depth)]

    def add_mod(l, acc, idx):
        upd = jnp.zeros((2, 6, d), F32)
        for row, k in idx:
            upd = upd.at[:, k, :].set(acc[:, row, :])
        d_mod[l] = d_mod[l] + upd

    add_mod(depth - 1, acc, [(2, 5)])
    d_conv_f, d_conv_s = [None] * depth, [None] * n_conv
    d_gq, d_gk, d_sink = [None] * n_attn, [None] * n_attn, [None] * n_attn
    for l in reversed(range(depth)):
        j, is_attn, sv = l // N_MIX, l % N_MIX == 0, saved[l]
        gr = mm_tn_row(f"g_down_{l}", sv["act"], dy, gr, lay.dn[l], lay.kb_dn)
        da = mm_nt_row(f"d_act_{l}", dy, wr, lay.dn[l], lay.kb_dn)
        d_up, d_conv_f[l] = ffn_act_bwd(f"act_bwd_{l}", sv["up"], da, cwf[l])
        gc = mm_tn_col(f"g_up_{l}", sv["h2"], d_up, gc, lay.up[l], lay.nb_up)
        dh2 = mm_nt_col(f"d_h2_{l}", d_up, wc, lay.up[l], lay.nb_up)
        dx, dy, acc = resid_bwd(f"norm2_bwd_{l}", dx, dh2, sv["x_mid"], mods[l], 3, 4, sv["y_m"], mods[l], 2)
        add_mod(l, acc, [(0, 3), (1, 4), (2, 2)])
        if is_attn:
            gr = mm_tn_row(f"g_wo_{l}", sv["o"], dy, gr, lay.wo[j], lay.kb_o)
            do = mm_nt_row(f"d_o_{l}", dy, wr, lay.wo[j], lay.kb_o)
            dqr, delta, dkc, dvc, dsk = attn_bwd_q(f"attn_bwd_q_{l}", sv["qr"], sv["kr"], sv["qkv"], attn_sink[j],
                                                   do, sv["o"], sv["lse"], seq)
            dkl, dvl = attn_bwd_kv(f"attn_bwd_kv_{l}", sv["qr"], sv["kr"], sv["qkv"], do, sv["lse"], delta, seq)
            dqkv, dgn = rope_bwd(f"rope_bwd_{l}", sv["qkv"], dqr, jnp.concatenate([dkc, dkl], axis=0),
                                 jnp.concatenate([dvc, dvl], axis=0), cos, sin, gains[j])
            d_gq[j], d_gk[j], d_sink[j] = dgn[0], dgn[1], dsk[0, :nh]
            gc = mm_tn_col(f"g_qkv_{l}", sv["h1"], dqkv, gc, lay.qkv[j], lay.nb_qkv)
            dh1 = mm_nt_col(f"d_h1_{l}", dqkv, wc, lay.qkv[j], lay.nb_qkv)
        else:
            gr = mm_tn_row(f"g_scout_{l}", sv["z"], dy, gr, lay.scout[j], lay.kb_o)
            dz = mm_nt_row(f"d_z_{l}", dy, wr, lay.scout[j], lay.kb_o)
            du, dcw = sc_gate_bwd(f"scgate_bwd_{l}", sv["u"], dz, cws[j])
            d_conv_s[j] = dcw[:3]
            gc = mm_tn_col(f"g_scin_{l}", sv["h1"], du, gc, lay.scin[j], lay.nb_sc)
            dh1 = mm_nt_col(f"d_h1_{l}", du, wc, lay.scin[j], lay.nb_sc)
        if l > 0:
            dx, dy, acc = resid_bwd(f"norm1_bwd_{l}", dx, dh1, sv["x_in"], mods[l], 0, 1, saved[l - 1]["y_f"], mods[l - 1], 5)
            add_mod(l - 1, acc, [(2, 5)])
        else:
            dx, acc = resid_bwd(f"norm1_bwd_{l}", dx, dh1, sv["x_in"], mods[l], 0, 1)
        add_mod(l, acc, [(0, 0), (1, 1)])
    grad_x = dx[CTX:][None]

    ra_c, ra_r = pair_exchange("pair_exchange", [gc, gr])
    hc, hr = pair_add("pair_add_c", gc, ra_c, cidx), pair_add("pair_add_r", gr, ra_r, cidx)
    rb_c, rb_r = chip_scatter("chip_scatter", [hc, hr])
    fc, fr = chip_add("chip_add_c", rb_c), chip_add("chip_add_r", rb_r)
    gfc, gfr = pair_join("pair_join", [fc, fr])

    d_ada = jnp.stack(d_mod).reshape(depth, 2, 6 * d)
    small = [d_ada, jnp.stack(d_gq), jnp.stack(d_gk), jnp.stack(d_sink), jnp.stack(d_conv_s),
             jnp.stack([t[:3] for t in d_conv_f]), jnp.stack([t[3] for t in d_conv_f])]
    g3 = gather_flat("gather_small", small)
    tot = sum8("sum_small", g3)
    sizes = [s.size for s in small]
    offs = [sum(sizes[:k]) for k in range(len(sizes) + 1)]
    part = lambda k: tot[offs[k]:offs[k + 1]].reshape(small[k].shape)
    g_b_ada = part(0)[:, 0] + part(0)[:, 1]
    g_q_gain, g_k_gain, g_sink = part(1), part(2), part(3)
    g_sc_conv = lax.dynamic_slice_in_dim(part(4), chip * sc_conv.shape[2], sc_conv.shape[2], 2)
    g_ffn_conv = lax.dynamic_slice_in_dim(part(5), chip * ffn_conv.shape[2], ffn_conv.shape[2], 2)
    g_conv_b = part(6)

    d_ada_all = g3[:, :d_ada.size].reshape(8, depth, 2, 6 * d)
    cols = lambda a: lax.dynamic_slice_in_dim(a, chip * nada, nada, a.ndim - 1)
    d_lat = cols(jnp.moveaxis(d_ada_all[:, :, 1], 0, 1))
    d_ctx = cols(part(0)[:, 0])
    rhs = jnp.concatenate([d_lat, d_ctx[:, None], jnp.zeros((depth, 7, nada), F32)], axis=1)
    g_w_ada = ada_grad_w("ada_grad_w", cond, rhs)
    dcc = ada_bwd_cond("ada_bwd_cond", jnp.pad(d_ctx[:, None], ((0, 0), (0, 7), (0, 0))), w_ada)[0]
    g4 = gather_flat("gather_dcc", [dcc])
    d_silu = g4[0, :d] + g4[2, :d] + g4[4, :d] + g4[6, :d]
    sg = _sigmoid(c_ctx)
    g_c_ctx = d_silu * (sg * (1.0 + c_ctx * (1.0 - sg)))

    def adam_rows(k, n):
        return _pick(k, max(8, ADAM_TILE_ELEMS // n), 8)

    def big(name, gbuf, off, col, w, m, v):
        nl, k, n = w.shape
        tr = adam_rows(k, n)
        if col:
            ob = off // n
            spec = pl.BlockSpec((tr, n), lambda l, i: (i, ob + l))
        else:
            ob, per = off // tr, k // tr
            spec = pl.BlockSpec((tr, n), lambda l, i: (ob + l * per + i, 0))
        return adamw(name, gbuf, spec, w, m, v, tr)

    ada_tr = adam_rows(d, nada)
    res = {
        "c_ctx": adamw_small("adam_c_ctx", g_c_ctx, c_ctx, m_c_ctx, v_c_ctx),
        "w_ada": adamw("adam_w_ada", g_w_ada, pl.BlockSpec((None, ada_tr, nada), lambda l, i: (l, i, 0)), w_ada, m_w_ada, v_w_ada, ada_tr),
        "b_ada": adamw_small("adam_b_ada", g_b_ada, b_ada, m_b_ada, v_b_ada),
        "attn_w_qkv": big("adam_qkv", gfc, lay.qkv[0], True, attn_w_qkv, m_attn_w_qkv, v_attn_w_qkv),
        "attn_w_o": big("adam_wo", gfr, lay.wo[0], False, attn_w_o, m_attn_w_o, v_attn_w_o),
        "attn_q_gain": adamw_small("adam_q_gain", g_q_gain, attn_q_gain, m_attn_q_gain, v_attn_q_gain),
        "attn_k_gain": adamw_small("adam_k_gain", g_k_gain, attn_k_gain, m_attn_k_gain, v_attn_k_gain),
        "attn_sink": adamw_small("adam_sink", g_sink, attn_sink, m_attn_sink, v_attn_sink),
        "sc_w_in": big("adam_scin", gfc, lay.scin[0], True, sc_w_in, m_sc_w_in, v_sc_w_in),
        "sc_conv": adamw_small("adam_sc_conv", g_sc_conv, sc_conv, m_sc_conv, v_sc_conv),
        "sc_w_out": big("adam_scout", gfr, lay.scout[0], False, sc_w_out, m_sc_w_out, v_sc_w_out),
        "ffn_w_up": big("adam_up", gfc, lay.up[0], True, ffn_w_up, m_ffn_w_up, v_ffn_w_up),
        "ffn_conv": adamw_small("adam_ffn_conv", g_ffn_conv, ffn_conv, m_ffn_conv, v_ffn_conv),
        "ffn_conv_b": adamw_small("adam_conv_b", g_conv_b, ffn_conv_b, m_ffn_conv_b, v_ffn_conv_b),
        "ffn_w_down": big("adam_down", gfr, lay.dn[0], False, ffn_w_down, m_ffn_w_down, v_ffn_w_down),
    }
    names = list(res)
    return (loss, grad_x, *[res[n][0] for n in names], *[res[n][1] for n in names],
            *[res[n][2] for n in names], *[res[n][3] for n in names])
```

```python
import functools

import jax
import jax.numpy as jnp
from jax import lax
from jax.experimental import pallas as pl
from jax.experimental.pallas import tpu as pltpu

F32, BF16 = jnp.float32, jnp.bfloat16
MESH = pl.DeviceIdType.MESH
VMEM_LIMIT = 56 * 1024 * 1024
LANE = 128
HALO = 16
HEAD = 128
GROUP = 4
CTX = 256
BLK = 128
WINDOW = 128
RB = 256
GRID_W = 64
ROPE_BASE = 10000.0
EPS = 1e-6
NEG = -1e30
N_MIX = 2
LR, B1, B2, ADAM_EPS, WD, STEP = 0.001, 0.9, 0.999, 1e-08, 0.01, 10
ADAM_TILE_ELEMS = 400 * 1024
NT = (((1,), (1,)), ((), ()))
TN = (((0,), (0,)), ((), ()))


def _pick(dim, target, mult=LANE):
    best = None
    for t in range(mult, min(dim, target) + 1, mult):
        if dim % t == 0:
            best = t
    return dim if best is None else best


def _cdiv(a, b):
    return -(-a // b)


def _params(sem):
    return pltpu.CompilerParams(dimension_semantics=sem, vmem_limit_bytes=VMEM_LIMIT)


def _sigmoid(g):
    return 1.0 / (1.0 + jnp.exp(-g))


def _row(v, r):
    rows = lax.broadcasted_iota(jnp.int32, v.shape, 0)
    return jnp.sum(jnp.where(rows == r, v, 0.0), axis=0, keepdims=True)


def _get_col(v, c):
    lanes = lax.broadcasted_iota(jnp.int32, v.shape, 1)
    return jnp.sum(jnp.where(lanes == c, v, 0.0), axis=1, keepdims=True)


def _put_col(v, c, col):
    lanes = lax.broadcasted_iota(jnp.int32, v.shape, 1)
    return jnp.where(lanes == c, col, v)


def _rows3(s0, s1, s2, width):
    rows = lax.broadcasted_iota(jnp.int32, (8, width), 0)
    z = jnp.zeros((8, width), F32)
    return jnp.where(rows == 0, s0, jnp.where(rows == 1, s1, jnp.where(rows == 2, s2, z)))


def _shift_rows(w, prev_row, next_row):
    n = w.shape[0]
    rows = lax.broadcasted_iota(jnp.int32, (n, 1), 0)
    down = jnp.where(rows == 0, prev_row, pltpu.roll(w, 1, 0))
    up = jnp.where(rows == n - 1, next_row, pltpu.roll(w, n - 1, 0))
    return down, up


def _seg_flags(i, nt):
    return i <= 1, (i == 0) | (i == nt - 1)


def _halo_specs(width, nrows):
    r = RB // HALO
    nh = nrows // HALO
    prev = pl.BlockSpec((HALO, width), lambda i: (jnp.maximum(i * r - 1, 0), 0))
    nxt = pl.BlockSpec((HALO, width), lambda i: (jnp.minimum((i + 1) * r, nh - 1), 0))
    return prev, nxt


class Layout:
    def __init__(self, d, ff, nqkv, depth):
        n_attn, n_conv = (depth + 1) // 2, depth // 2
        self.nb_up, self.nb_sc, self.nb_qkv = 2 * ff // 4, 3 * d // 4, nqkv // 4
        self.kb_dn, self.kb_o = ff // 4, d // 4
        off = 0
        self.up = [off + l * self.nb_up for l in range(depth)]
        off = _cdiv(depth * self.nb_up, self.nb_sc) * self.nb_sc
        self.scin = [off + l * self.nb_sc for l in range(n_conv)]
        off = _cdiv(off + n_conv * self.nb_sc, self.nb_qkv) * self.nb_qkv
        self.qkv = [off + l * self.nb_qkv for l in range(n_attn)]
        self.ct = _cdiv(off + n_attn * self.nb_qkv, LANE) * LANE
        self.dn = [l * self.kb_dn for l in range(depth)]
        off = _cdiv(depth * self.kb_dn, self.kb_o) * self.kb_o
        self.wo = [off + l * self.kb_o for l in range(n_attn)]
        off += n_attn * self.kb_o
        self.scout = [off + l * self.kb_o for l in range(n_conv)]
        self.rt = _cdiv(off + n_conv * self.kb_o, 2 * HALO) * 2 * HALO


def _mm_call(name, grid, in_specs, out_specs, out_shape, contract, operands, acc_shape, epilogue,
             n_extra=0, aliases=None):
    nk = grid[2]
    n_out = len(out_shape)

    def body(*refs):
        a_ref, b_ref = refs[0], refs[1]
        extra = refs[2:2 + n_extra]
        outs = refs[2 + n_extra:2 + n_extra + n_out]
        ids = (pl.program_id(0), pl.program_id(1))

        def part():
            return lax.dot_general(a_ref[...], b_ref[...], contract, preferred_element_type=F32)

        if nk == 1:
            epilogue(part(), extra, outs, ids)
        else:
            acc = refs[-1]
            k = pl.program_id(2)

            @pl.when(k == 0)
            def _():
                acc[...] = jnp.zeros_like(acc)

            acc[...] += part()

            @pl.when(k == nk - 1)
            def _():
                epilogue(acc[...], extra, outs, ids)

    scratch = [] if nk == 1 else [pltpu.VMEM(acc_shape, F32)]
    return pl.pallas_call(
        body, name=name, grid=grid, in_specs=in_specs, out_specs=out_specs, out_shape=out_shape,
        scratch_shapes=scratch, input_output_aliases=aliases or {},
        compiler_params=_params(("parallel", "parallel", "arbitrary")))(*operands)


def _store(dtype):
    def epilogue(r, extra, outs, ids):
        outs[0][...] = r.astype(dtype)
    return epilogue


def mm_nn_col(name, a, wc, off, nb):
    m, d = a.shape
    s = wc.shape[0]
    tm, tn = _pick(m, 1056, HALO), _pick(nb, 1536)
    npb, ob = nb // tn, off // tn
    assert off % tn == 0
    return _mm_call(
        name, (m // tm, s * npb, 1),
        [pl.BlockSpec((tm, d), lambda i, j, k: (i, 0)),
         pl.BlockSpec((None, d, tn), lambda i, j, k: (j // npb, 0, ob + j % npb))],
        [pl.BlockSpec((tm, tn), lambda i, j, k: (i, j))],
        [jax.ShapeDtypeStruct((m, s * nb), BF16)],
        (((1,), (0,)), ((), ())), (a, wc), None, _store(BF16))[0]


def mm_nn_row(name, a, wr, off, kb, res, mod, gi):
    m = a.shape[0]
    s, _, d = wr.shape
    tm, tk, tn = _pick(m, 1056, HALO), _pick(kb, 1408), _pick(d, 1024)
    kpb, ob = kb // tk, off // tk
    assert off % tk == 0

    def epilogue(r, extra, outs, ids):
        res_ref, mod_ref = extra
        rows = ids[0] * tm + lax.broadcasted_iota(jnp.int32, (tm, 1), 0)
        g = jnp.where(rows < CTX, mod_ref[0, gi:gi + 1, :], mod_ref[1, gi:gi + 1, :])
        outs[0][...] = res_ref[...] + g * r
        outs[1][...] = r.astype(BF16)

    return _mm_call(
        name, (m // tm, d // tn, s * kpb),
        [pl.BlockSpec((tm, tk), lambda i, j, k: (i, k)),
         pl.BlockSpec((None, tk, tn), lambda i, j, k: (k // kpb, ob + k % kpb, j)),
         pl.BlockSpec((tm, tn), lambda i, j, k: (i, j)),
         pl.BlockSpec((2, 8, tn), lambda i, j, k: (0, 0, j))],
        [pl.BlockSpec((tm, tn), lambda i, j, k: (i, j)), pl.BlockSpec((tm, tn), lambda i, j, k: (i, j))],
        [jax.ShapeDtypeStruct((m, d), F32), jax.ShapeDtypeStruct((m, d), BF16)],
        (((1,), (0,)), ((), ())), (a, wr, res, mod), (tm, tn), epilogue, n_extra=2)


def mm_nt_col(name, dy, wc, off, nb):
    m = dy.shape[0]
    s, d, _ = wc.shape
    tm, tc = _pick(m, 768, HALO), _pick(nb, 1536)
    npb, ob = nb // tc, off // tc
    return _mm_call(
        name, (m // tm, 1, s * npb),
        [pl.BlockSpec((tm, tc), lambda i, j, k: (i, k)),
         pl.BlockSpec((None, d, tc), lambda i, j, k: (k // npb, 0, ob + k % npb))],
        [pl.BlockSpec((tm, d), lambda i, j, k: (i, 0))],
        [jax.ShapeDtypeStruct((m, d), F32)],
        NT, (dy, wc), (tm, d), _store(F32))[0]


def mm_nt_row(name, dy, wr, off, kb):
    m, d = dy.shape
    s = wr.shape[0]
    tm, tkb = _pick(m, 1056, HALO), _pick(kb, 1408)
    kpb, ob = kb // tkb, off // tkb
    return _mm_call(
        name, (m // tm, s * kpb, 1),
        [pl.BlockSpec((tm, d), lambda i, j, k: (i, 0)),
         pl.BlockSpec((None, tkb, d), lambda i, j, k: (j // kpb, ob + j % kpb, 0))],
        [pl.BlockSpec((tm, tkb), lambda i, j, k: (i, j))],
        [jax.ShapeDtypeStruct((m, s * kb), BF16)],
        NT, (dy, wr), None, _store(BF16))[0]


def mm_tn_col(name, a, dy, gbuf, off, nb):
    t, d = a.shape
    s = gbuf.shape[0]
    tka, tn, tt = _pick(d, 2048), _pick(nb, 1536), _pick(t, 1056, HALO)
    npb, ob = nb // tn, off // tn
    return _mm_call(
        name, (d // tka, s * npb, t // tt),
        [pl.BlockSpec((tt, tka), lambda i, j, k: (k, i)),
         pl.BlockSpec((tt, tn), lambda i, j, k: (k, j)),
         pl.BlockSpec(memory_space=pl.ANY)],
        [pl.BlockSpec((None, tka, tn), lambda i, j, k: (j // npb, i, ob + j % npb))],
        [jax.ShapeDtypeStruct(gbuf.shape, BF16)],
        TN, (a, dy, gbuf), (tka, tn), _store(BF16), n_extra=1, aliases={2: 0})[0]


def mm_tn_row(name, act, dy, gbuf, off, kb):
    t, d = dy.shape
    s = gbuf.shape[0]
    tka, tn, tt = _pick(kb, 1408), _pick(d, 2048), _pick(t, 1056, HALO)
    kpb, ob = kb // tka, off // tka
    return _mm_call(
        name, (s * kpb, d // tn, t // tt),
        [pl.BlockSpec((tt, tka), lambda i, j, k: (k, i)),
         pl.BlockSpec((tt, tn), lambda i, j, k: (k, j)),
         pl.BlockSpec(memory_space=pl.ANY)],
        [pl.BlockSpec((None, tka, tn), lambda i, j, k: (i // kpb, ob + i % kpb, j))],
        [jax.ShapeDtypeStruct(gbuf.shape, BF16)],
        TN, (act, dy, gbuf), (tka, tn), _store(BF16), n_extra=1, aliases={2: 0})[0]


def cast_pack(name, w, buf, off, col, chip):
    nl, k, n = w.shape
    if col:
        tr = _pick(k, 512, HALO)
        ob = off // n
        out_spec = pl.BlockSpec((None, tr, n), lambda l, i, s: (s[0], i, ob + l))
    else:
        tr = _pick(k, 704, HALO)
        ob, per = off // tr, k // tr
        out_spec = pl.BlockSpec((None, tr, n), lambda l, i, s: (s[0], ob + l * per + i, 0))

    def body(s_ref, w_ref, buf_ref, out_ref):
        out_ref[...] = w_ref[...].astype(BF16)

    return pl.pallas_call(
        body, name=name,
        grid_spec=pltpu.PrefetchScalarGridSpec(
            num_scalar_prefetch=1, grid=(nl, k // tr),
            in_specs=[pl.BlockSpec((None, tr, n), lambda l, i, s: (l, i, 0)), pl.BlockSpec(memory_space=pl.ANY)],
            out_specs=out_spec),
        out_shape=jax.ShapeDtypeStruct(buf.shape, BF16),
        input_output_aliases={2: 0}, compiler_params=_params(("parallel", "parallel")))(chip, w, buf)


def norm_mod(name, x, mod, sh, sc):
    t, d = x.shape

    def body(x_ref, mod_ref, h_ref):
        seg = jnp.minimum(pl.program_id(0), 1)
        xv = x_ref[...]
        r = lax.rsqrt(jnp.mean(xv * xv, axis=-1, keepdims=True) + EPS)
        m = mod_ref[seg]
        h_ref[...] = ((xv * r) * (1.0 + m[sc:sc + 1, :]) + m[sh:sh + 1, :]).astype(BF16)

    return pl.pallas_call(
        body, name=name, grid=(t // RB,),
        in_specs=[pl.BlockSpec((RB, d), lambda i: (i, 0)), pl.BlockSpec((2, 8, d), lambda i: (0, 0, 0))],
        out_specs=pl.BlockSpec((RB, d), lambda i: (i, 0)),
        out_shape=jax.ShapeDtypeStruct((t, d), BF16), compiler_params=_params(("parallel",)))(x, mod)


def sc_gate_fwd(name, u, cw):
    t = u.shape[0]
    d = u.shape[1] // 3
    nt, tc = t // RB, _pick(d, 512)
    prev, nxt = _halo_specs(3 * d, t)

    def body(u_ref, up_ref, un_ref, cw_ref, z_ref):
        first, last = _seg_flags(pl.program_id(0), nt)
        for j in range(d // tc):
            c0 = j * tc
            gb = u_ref[:, c0:c0 + tc].astype(F32)
            w = u_ref[:, d + c0:d + c0 + tc].astype(F32) * u_ref[:, 2 * d + c0:2 * d + c0 + tc].astype(F32)
            pw = _row(up_ref[:, d + c0:d + c0 + tc].astype(F32) * up_ref[:, 2 * d + c0:2 * d + c0 + tc].astype(F32), HALO - 1)
            nw = _row(un_ref[:, d + c0:d + c0 + tc].astype(F32) * un_ref[:, 2 * d + c0:2 * d + c0 + tc].astype(F32), 0)
            wd, wu = _shift_rows(w, jnp.where(first, 0.0, pw), jnp.where(last, 0.0, nw))
            cwj = cw_ref[:, c0:c0 + tc]
            conv = wd * cwj[0:1] + w * cwj[1:2] + wu * cwj[2:3]
            z_ref[:, c0:c0 + tc] = (gb * conv).astype(BF16)

    return pl.pallas_call(
        body, name=name, grid=(nt,),
        in_specs=[pl.BlockSpec((RB, 3 * d), lambda i: (i, 0)), prev, nxt, pl.BlockSpec((8, d), lambda i: (0, 0))],
        out_specs=pl.BlockSpec((RB, d), lambda i: (i, 0)),
        out_shape=jax.ShapeDtypeStruct((t, d), BF16), compiler_params=_params(("parallel",)))(u, u, u, cw)


def sc_gate_bwd(name, u, dz, cw):
    t = u.shape[0]
    d = u.shape[1] // 3
    nt, tc = t // RB, _pick(d, 512)
    prev, nxt = _halo_specs(3 * d, t)
    dprev, dnxt = _halo_specs(d, t)

    def body(u_ref, up_ref, un_ref, dz_ref, dzp_ref, dzn_ref, cw_ref, du_ref, dcw_ref):
        i = pl.program_id(0)
        first, last = _seg_flags(i, nt)

        @pl.when(i == 0)
        def _():
            dcw_ref[...] = jnp.zeros_like(dcw_ref)

        for j in range(d // tc):
            c0 = j * tc
            sl0, sl1, sl2 = slice(c0, c0 + tc), slice(d + c0, d + c0 + tc), slice(2 * d + c0, 2 * d + c0 + tc)
            gb, gc, v = u_ref[:, sl0].astype(F32), u_ref[:, sl1].astype(F32), u_ref[:, sl2].astype(F32)
            w = gc * v
            pw = _row(up_ref[:, sl1].astype(F32) * up_ref[:, sl2].astype(F32), HALO - 1)
            nw = _row(un_ref[:, sl1].astype(F32) * un_ref[:, sl2].astype(F32), 0)
            wd, wu = _shift_rows(w, jnp.where(first, 0.0, pw), jnp.where(last, 0.0, nw))
            cwj = cw_ref[:, sl0]
            cw0, cw1, cw2 = cwj[0:1], cwj[1:2], cwj[2:3]
            dzv = dz_ref[:, sl0].astype(F32)
            e = dzv * gb
            pe = _row(dzp_ref[:, sl0].astype(F32) * up_ref[:, sl0].astype(F32), HALO - 1)
            ne = _row(dzn_ref[:, sl0].astype(F32) * un_ref[:, sl0].astype(F32), 0)
            ed, eu = _shift_rows(e, jnp.where(first, 0.0, pe), jnp.where(last, 0.0, ne))
            dw = cw0 * eu + cw1 * e + cw2 * ed
            du_ref[:, sl0] = (dzv * (wd * cw0 + w * cw1 + wu * cw2)).astype(BF16)
            du_ref[:, sl1] = (dw * v).astype(BF16)
            du_ref[:, sl2] = (dw * gc).astype(BF16)
            dcw_ref[:, sl0] += _rows3(jnp.sum(e * wd, axis=0, keepdims=True), jnp.sum(e * w, axis=0, keepdims=True),
                                      jnp.sum(e * wu, axis=0, keepdims=True), tc)

    return pl.pallas_call(
        body, name=name, grid=(nt,),
        in_specs=[pl.BlockSpec((RB, 3 * d), lambda i: (i, 0)), prev, nxt,
                  pl.BlockSpec((RB, d), lambda i: (i, 0)), dprev, dnxt, pl.BlockSpec((8, d), lambda i: (0, 0))],
        out_specs=[pl.BlockSpec((RB, 3 * d), lambda i: (i, 0)), pl.BlockSpec((8, d), lambda i: (0, 0))],
        out_shape=[jax.ShapeDtypeStruct((t, 3 * d), BF16), jax.ShapeDtypeStruct((8, d), F32)],
        compiler_params=_params(("arbitrary",)))(u, u, u, dz, dz, dz, cw)


def ffn_act_fwd(name, up, cw):
    t = up.shape[0]
    ff = up.shape[1] // 2
    nt, tc = t // RB, _pick(ff, 1408)
    prev, nxt = _halo_specs(2 * ff, t)

    def body(up_ref, upp_ref, upn_ref, cw_ref, a_ref):
        first, last = _seg_flags(pl.program_id(0), nt)
        for j in range(ff // tc):
            sg, sv = slice(j * tc, (j + 1) * tc), slice(ff + j * tc, ff + (j + 1) * tc)
            gate = up_ref[:, sg].astype(F32)
            pg = _row(upp_ref[:, sg].astype(F32), HALO - 1)
            ng = _row(upn_ref[:, sg].astype(F32), 0)
            gd, gu = _shift_rows(gate, jnp.where(first, 0.0, pg), jnp.where(last, 0.0, ng))
            cwj = cw_ref[:, sg]
            g = gd * cwj[0:1] + gate * cwj[1:2] + gu * cwj[2:3] + cwj[3:4]
            a_ref[:, sg] = (g * _sigmoid(g) * up_ref[:, sv].astype(F32)).astype(BF16)

    return pl.pallas_call(
        body, name=name, grid=(nt,),
        in_specs=[pl.BlockSpec((RB, 2 * ff), lambda i: (i, 0)), prev, nxt, pl.BlockSpec((8, ff), lambda i: (0, 0))],
        out_specs=pl.BlockSpec((RB, ff), lambda i: (i, 0)),
        out_shape=jax.ShapeDtypeStruct((t, ff), BF16), compiler_params=_params(("parallel",)))(up, up, up, cw)


def ffn_act_bwd(name, up, da, cw):
    t = up.shape[0]
    ff = up.shape[1] // 2
    nt, tc = t // RB, _pick(ff, 1408)
    prev, nxt = _halo_specs(2 * ff, t)
    dprev, dnxt = _halo_specs(ff, t)

    def dsilu(g):
        s = _sigmoid(g)
        return s * (1.0 + g * (1.0 - s))

    def body(up_ref, upp_ref, upn_ref, da_ref, dap_ref, dan_ref, cw_ref, dup_ref, acc_ref):
        i = pl.program_id(0)
        first, last = _seg_flags(i, nt)

        @pl.when(i == 0)
        def _():
            acc_ref[...] = jnp.zeros_like(acc_ref)

        for j in range(ff // tc):
            sg, sv = slice(j * tc, (j + 1) * tc), slice(ff + j * tc, ff + (j + 1) * tc)
            gate, val, dav = up_ref[:, sg].astype(F32), up_ref[:, sv].astype(F32), da_ref[:, sg].astype(F32)
            pgt, ngt = upp_ref[:, sg].astype(F32), upn_ref[:, sg].astype(F32)
            pg1, pg2 = _row(pgt, HALO - 1), _row(pgt, HALO - 2)
            ng1, ng2 = _row(ngt, 0), _row(ngt, 1)
            cwj = cw_ref[:, sg]
            cw0, cw1, cw2, b = cwj[0:1], cwj[1:2], cwj[2:3], cwj[3:4]
            gd, gu = _shift_rows(gate, jnp.where(first, 0.0, pg1), jnp.where(last, 0.0, ng1))
            g = gd * cw0 + gate * cw1 + gu * cw2 + b
            g_p = pg2 * cw0 + pg1 * cw1 + _row(gate, 0) * cw2 + b
            g_n = _row(gate, RB - 1) * cw0 + ng1 * cw1 + ng2 * cw2 + b
            dg = dav * val * dsilu(g)
            dg_p = _row(dap_ref[:, sg].astype(F32) * upp_ref[:, sv].astype(F32), HALO - 1) * dsilu(g_p)
            dg_n = _row(dan_ref[:, sg].astype(F32) * upn_ref[:, sv].astype(F32), 0) * dsilu(g_n)
            dgd, dgu = _shift_rows(dg, jnp.where(first, 0.0, dg_p), jnp.where(last, 0.0, dg_n))
            dup_ref[:, sg] = (cw0 * dgu + cw1 * dg + cw2 * dgd).astype(BF16)
            dup_ref[:, sv] = (dav * g * _sigmoid(g)).astype(BF16)
            rows = lax.broadcasted_iota(jnp.int32, (8, tc), 0)
            acc_ref[:, sg] += (_rows3(jnp.sum(dg * gd, axis=0, keepdims=True), jnp.sum(dg * gate, axis=0, keepdims=True),
                                      jnp.sum(dg * gu, axis=0, keepdims=True), tc)
                               + jnp.where(rows == 3, jnp.sum(dg, axis=0, keepdims=True), 0.0))

    return pl.pallas_call(
        body, name=name, grid=(nt,),
        in_specs=[pl.BlockSpec((RB, 2 * ff), lambda i: (i, 0)), prev, nxt,
                  pl.BlockSpec((RB, ff), lambda i: (i, 0)), dprev, dnxt, pl.BlockSpec((8, ff), lambda i: (0, 0))],
        out_specs=[pl.BlockSpec((RB, 2 * ff), lambda i: (i, 0)), pl.BlockSpec((8, ff), lambda i: (0, 0))],
        out_shape=[jax.ShapeDtypeStruct((t, 2 * ff), BF16), jax.ShapeDtypeStruct((8, ff), F32)],
        compiler_params=_params(("arbitrary",)))(up, up, up, da, da, da, cw)


def _rot(z):
    w = z.shape[1]
    lane = lax.broadcasted_iota(jnp.int32, z.shape, 1)
    return jnp.where((lane % 64) < 32, -pltpu.roll(z, w - 32, 1), pltpu.roll(z, 32, 1))


def rope_fwd(name, qkv, cos, sin, gains, dq, dkv):
    t, nqkv = qkv.shape
    nh, nkv = dq // HEAD, dkv // HEAD

    def body(qkv_ref, cos_ref, sin_ref, g_ref, qr_ref, kr_ref):
        cs, sn = cos_ref[...], sin_ref[...]
        for hd in range(nh + nkv):
            c0 = hd * HEAD
            xh = qkv_ref[:, c0:c0 + HEAD].astype(F32)
            r = lax.rsqrt(jnp.mean(xh * xh, axis=-1, keepdims=True) + EPS)
            y = xh * r * (g_ref[0:1, :] if hd < nh else g_ref[1:2, :])
            yr = (y * cs + _rot(y) * sn).astype(BF16)
            if hd < nh:
                qr_ref[:, c0:c0 + HEAD] = yr
            else:
                kr_ref[:, c0 - dq:c0 - dq + HEAD] = yr

    return pl.pallas_call(
        body, name=name, grid=(t // RB,),
        in_specs=[pl.BlockSpec((RB, nqkv), lambda i: (i, 0)), pl.BlockSpec((RB, HEAD), lambda i: (i, 0)),
                  pl.BlockSpec((RB, HEAD), lambda i: (i, 0)), pl.BlockSpec((8, HEAD), lambda i: (0, 0))],
        out_specs=[pl.BlockSpec((RB, dq), lambda i: (i, 0)), pl.BlockSpec((RB, dkv), lambda i: (i, 0))],
        out_shape=[jax.ShapeDtypeStruct((t, dq), BF16), jax.ShapeDtypeStruct((t, dkv), BF16)],
        compiler_params=_params(("parallel",)))(qkv, cos, sin, gains)


def rope_bwd(name, qkv, dqr, dkr, dv, cos, sin, gains):
    t, nqkv = qkv.shape
    dq, dkv = dqr.shape[1], dkr.shape[1]
    nh, nkv = dq // HEAD, dkv // HEAD

    def body(qkv_ref, dq_ref, dk_ref, dv_ref, cos_ref, sin_ref, g_ref, out_ref, dg_ref):
        @pl.when(pl.program_id(0) == 0)
        def _():
            dg_ref[...] = jnp.zeros_like(dg_ref)

        cs, sn = cos_ref[...], sin_ref[...]
        zero = jnp.zeros((1, HEAD), F32)
        gq, gk = zero, zero
        for hd in range(nh + nkv):
            c0 = hd * HEAD
            xh = qkv_ref[:, c0:c0 + HEAD].astype(F32)
            r = lax.rsqrt(jnp.mean(xh * xh, axis=-1, keepdims=True) + EPS)
            xhat = xh * r
            dy = dq_ref[:, c0:c0 + HEAD] if hd < nh else dk_ref[:, c0 - dq:c0 - dq + HEAD]
            tt = dy * cs - _rot(dy * sn)
            gsum = jnp.sum(tt * xhat, axis=0, keepdims=True)
            if hd < nh:
                gq = gq + gsum
            else:
                gk = gk + gsum
            dxh = tt * (g_ref[0:1, :] if hd < nh else g_ref[1:2, :])
            dx = r * (dxh - xhat * jnp.mean(dxh * xhat, axis=-1, keepdims=True))
            out_ref[:, c0:c0 + HEAD] = dx.astype(BF16)
        out_ref[:, dq + dkv:] = dv_ref[...].astype(BF16)
        dg_ref[...] += _rows3(gq, gk, zero, HEAD)

    return pl.pallas_call(
        body, name=name, grid=(t // RB,),
        in_specs=[pl.BlockSpec((RB, nqkv), lambda i: (i, 0)), pl.BlockSpec((RB, dq), lambda i: (i, 0)),
                  pl.BlockSpec((RB, dkv), lambda i: (i, 0)), pl.BlockSpec((RB, dkv), lambda i: (i, 0)),
                  pl.BlockSpec((RB, HEAD), lambda i: (i, 0)), pl.BlockSpec((RB, HEAD), lambda i: (i, 0)),
                  pl.BlockSpec((8, HEAD), lambda i: (0, 0))],
        out_specs=[pl.BlockSpec((RB, nqkv), lambda i: (i, 0)), pl.BlockSpec((8, HEAD), lambda i: (0, 0))],
        out_shape=[jax.ShapeDtypeStruct((t, nqkv), BF16), jax.ShapeDtypeStruct((8, HEAD), F32)],
        compiler_params=_params(("arbitrary",)))(qkv, dqr, dkr, dv, cos, sin, gains)


def resid_bwd(name, dx, dh, x, mod_n, sh, sc, y_prev=None, mod_g=None, gi=0):
    t, d = x.shape
    has_prev = y_prev is not None

    def body(*refs):
        if has_prev:
            dx_ref, dh_ref, x_ref, mn_ref, y_ref, mg_ref, dxo_ref, dy_ref, acc_ref = refs
        else:
            dx_ref, dh_ref, x_ref, mn_ref, dxo_ref, acc_ref = refs
        i = pl.program_id(0)
        seg = jnp.minimum(i, 1)

        @pl.when(i == 0)
        def _():
            acc_ref[...] = jnp.zeros_like(acc_ref)

        xv, dhv = x_ref[...], dh_ref[...]
        r = lax.rsqrt(jnp.mean(xv * xv, axis=-1, keepdims=True) + EPS)
        xhat = xv * r
        m = mn_ref[seg]
        dxh = dhv * (1.0 + m[sc:sc + 1, :])
        dxo = dx_ref[...] + r * (dxh - xhat * jnp.mean(dxh * xhat, axis=-1, keepdims=True))
        dxo_ref[...] = dxo
        s2 = jnp.zeros((1, d), F32)
        if has_prev:
            dy_ref[...] = (mg_ref[seg][gi:gi + 1, :] * dxo).astype(BF16)
            s2 = jnp.sum(dxo * y_ref[...].astype(F32), axis=0, keepdims=True)
        acc_ref[seg] = acc_ref[seg] + _rows3(jnp.sum(dhv, axis=0, keepdims=True),
                                             jnp.sum(dhv * xhat, axis=0, keepdims=True), s2, d)

    row = pl.BlockSpec((RB, d), lambda i: (i, 0))
    modspec = pl.BlockSpec((2, 8, d), lambda i: (0, 0, 0))
    in_specs, operands = [row, row, row, modspec], [dx, dh, x, mod_n]
    out_specs, out_shape = [row], [jax.ShapeDtypeStruct((t, d), F32)]
    if has_prev:
        in_specs += [row, modspec]
        operands += [y_prev, mod_g]
        out_specs.append(row)
        out_shape.append(jax.ShapeDtypeStruct((t, d), BF16))
    out_specs.append(modspec)
    out_shape.append(jax.ShapeDtypeStruct((2, 8, d), F32))
    return pl.pallas_call(body, name=name, grid=(t // RB,), in_specs=in_specs, out_specs=out_specs,
                          out_shape=out_shape, compiler_params=_params(("arbitrary",)))(*operands)


def loss_head(name, xf, target, y_last, mod, gi):
    t, d = xf.shape

    def body(x_ref, t_ref, y_ref, mod_ref, dx_ref, dy_ref, acc_ref, lp_ref):
        i = pl.program_id(0)
        seg = jnp.minimum(i, 1)

        @pl.when(i == 0)
        def _():
            acc_ref[...] = jnp.zeros_like(acc_ref)
            lp_ref[...] = jnp.zeros_like(lp_ref)

        lat = i >= 1
        err = jnp.where(lat, x_ref[...] - t_ref[...], 0.0)
        dxv = err / d
        dx_ref[...] = dxv
        dy_ref[...] = (mod_ref[seg][gi:gi + 1, :] * dxv).astype(BF16)
        zero = jnp.zeros((1, d), F32)
        lp_ref[...] += _rows3(jnp.sum(err * err, axis=0, keepdims=True), zero, zero, d)
        acc_ref[seg] = acc_ref[seg] + _rows3(zero, zero, jnp.sum(dxv * y_ref[...].astype(F32), axis=0, keepdims=True), d)

    row = pl.BlockSpec((RB, d), lambda i: (i, 0))
    modspec = pl.BlockSpec((2, 8, d), lambda i: (0, 0, 0))
    return pl.pallas_call(
        body, name=name, grid=(t // RB,),
        in_specs=[row, pl.BlockSpec((RB, d), lambda i: (jnp.maximum(i - 1, 0), 0)), row, modspec],
        out_specs=[row, row, modspec, pl.BlockSpec((8, d), lambda i: (0, 0))],
        out_shape=[jax.ShapeDtypeStruct((t, d), F32), jax.ShapeDtypeStruct((t, d), BF16),
                   jax.ShapeDtypeStruct((2, 8, d), F32), jax.ShapeDtypeStruct((8, d), F32)],
        compiler_params=_params(("arbitrary",)))(xf, target, y_last, mod)


def _kv_specs(width, colblk, nbk):
    return [pl.BlockSpec((CTX, width), lambda i: (0, colblk)),
            pl.BlockSpec((BLK, width), lambda i: (jnp.maximum(i - 1, 0), colblk)),
            pl.BlockSpec((BLK, width), lambda i: (i, colblk)),
            pl.BlockSpec((BLK, width), lambda i: (jnp.minimum(i + 1, nbk - 1), colblk))]


def _band_mask(i, seq):
    nk = CTX + 3 * BLK
    qrow = lax.broadcasted_iota(jnp.int32, (GROUP * BLK, nk), 0) % BLK
    col = lax.broadcasted_iota(jnp.int32, (GROUP * BLK, nk), 1)
    cb = col - CTX
    kpos = (i - 3) * BLK + cb
    band = (i >= 2) & (jnp.abs(BLK + qrow - cb) <= WINDOW) & (kpos >= 0) & (kpos < seq)
    return (col < CTX) | band


def _stack_heads(ref, h):
    return jnp.concatenate([ref[:, (h * GROUP + g) * HEAD:(h * GROUP + g + 1) * HEAD] for g in range(GROUP)], axis=0)


def _stack_cols(v, h):
    return jnp.concatenate([_get_col(v, h * GROUP + g) for g in range(GROUP)], axis=0)


def _sink_col(sink_ref, h):
    rowg = lax.broadcasted_iota(jnp.int32, (GROUP * BLK, 1), 0) // BLK
    sk = jnp.full((GROUP * BLK, 1), sink_ref[h * GROUP], F32)
    for g in range(1, GROUP):
        sk = jnp.where(rowg == g, sink_ref[h * GROUP + g], sk)
    return sk


def attn_fwd(name, qr, kr, qkv, sink, seq):
    t, dq = qr.shape
    dkv = kr.shape[1]
    nbk, nkv = t // BLK, dkv // HEAD
    vcol = (dq + dkv) // dkv
    scale = HEAD ** -0.5

    def body(sink_ref, q_ref, kc, kp, ko, kn, vc, vp, vo, vn, o_ref, lse_ref):
        i = pl.program_id(0)
        mask = _band_mask(i, seq)
        lse = jnp.zeros((BLK, LANE), F32)
        for h in range(nkv):
            hs = slice(h * HEAD, (h + 1) * HEAD)
            k = jnp.concatenate([kc[:, hs], kp[:, hs], ko[:, hs], kn[:, hs]], axis=0)
            v = jnp.concatenate([vc[:, hs], vp[:, hs], vo[:, hs], vn[:, hs]], axis=0)
            q4 = _stack_heads(q_ref, h)
            s = jnp.where(mask, lax.dot_general(q4, k, NT, preferred_element_type=F32) * scale, NEG)
            sk = _sink_col(sink_ref, h)
            m = jnp.maximum(jnp.max(s, axis=-1, keepdims=True), sk)
            e = jnp.exp(s - m)
            den = jnp.sum(e, axis=-1, keepdims=True) + jnp.exp(sk - m)
            o4 = jnp.dot((e / den).astype(BF16), v, preferred_element_type=F32)
            l4 = m + jnp.log(den)
            for g in range(GROUP):
                hg = h * GROUP + g
                o_ref[:, hg * HEAD:(hg + 1) * HEAD] = o4[g * BLK:(g + 1) * BLK].astype(BF16)
                lse = _put_col(lse, hg, l4[g * BLK:(g + 1) * BLK])
        lse_ref[...] = lse

    return pl.pallas_call(
        body, name=name, grid=(nbk,),
        in_specs=[pl.BlockSpec(memory_space=pltpu.SMEM), pl.BlockSpec((BLK, dq), lambda i: (i, 0))]
        + _kv_specs(dkv, 0, nbk) + _kv_specs(dkv, vcol, nbk),
        out_specs=[pl.BlockSpec((BLK, dq), lambda i: (i, 0)), pl.BlockSpec((BLK, LANE), lambda i: (i, 0))],
        out_shape=[jax.ShapeDtypeStruct((t, dq), BF16), jax.ShapeDtypeStruct((t, LANE), F32)],
        compiler_params=_params(("parallel",)))(sink, qr, kr, kr, kr, kr, qkv, qkv, qkv, qkv)


def attn_bwd_q(name, qr, kr, qkv, sink, do, o, lse, seq):
    t, dq = qr.shape
    dkv = kr.shape[1]
    nbk, nkv = t // BLK, dkv // HEAD
    vcol = (dq + dkv) // dkv
    scale = HEAD ** -0.5

    def body(sink_ref, q_ref, kc, kp, ko, kn, vc, vp, vo, vn, do_ref, o_ref, lse_ref,
             dq_ref, dl_ref, dkc_ref, dvc_ref, ds_ref):
        i = pl.program_id(0)

        @pl.when(i == 0)
        def _():
            dkc_ref[...] = jnp.zeros_like(dkc_ref)
            dvc_ref[...] = jnp.zeros_like(dvc_ref)
            ds_ref[...] = jnp.zeros_like(ds_ref)

        mask = _band_mask(i, seq)
        lse = lse_ref[...]
        delta = jnp.zeros((BLK, LANE), F32)
        dsink = jnp.zeros((8, LANE), F32)
        for h in range(nkv):
            hs = slice(h * HEAD, (h + 1) * HEAD)
            k = jnp.concatenate([kc[:, hs], kp[:, hs], ko[:, hs], kn[:, hs]], axis=0)
            v = jnp.concatenate([vc[:, hs], vp[:, hs], vo[:, hs], vn[:, hs]], axis=0)
            q4, do4 = _stack_heads(q_ref, h), _stack_heads(do_ref, h)
            d4 = jnp.sum(do4.astype(F32) * _stack_heads(o_ref, h).astype(F32), axis=-1, keepdims=True)
            l4 = _stack_cols(lse, h)
            s = jnp.where(mask, lax.dot_general(q4, k, NT, preferred_element_type=F32) * scale, NEG)
            p = jnp.exp(s - l4)
            dp = lax.dot_general(do4, v, NT, preferred_element_type=F32)
            dsb = (p * (dp - d4) * scale).astype(BF16)
            pb = p.astype(BF16)
            dq4 = jnp.dot(dsb, k, preferred_element_type=F32)
            dkc_ref[:, hs] += lax.dot_general(dsb[:, :CTX], q4, TN, preferred_element_type=F32)
            dvc_ref[:, hs] += lax.dot_general(pb[:, :CTX], do4, TN, preferred_element_type=F32)
            dsk = -jnp.exp(_sink_col(sink_ref, h) - l4) * d4
            for g in range(GROUP):
                hg = h * GROUP + g
                rs = slice(g * BLK, (g + 1) * BLK)
                dq_ref[:, hg * HEAD:(hg + 1) * HEAD] = dq4[rs]
                delta = _put_col(delta, hg, d4[rs])
                dsink = _put_col(dsink, hg, jnp.sum(dsk[rs], axis=0, keepdims=True))
        dl_ref[...] = delta
        rows = lax.broadcasted_iota(jnp.int32, (8, LANE), 0)
        ds_ref[...] += jnp.where(rows == 0, dsink, 0.0)

    blk = lambda w: pl.BlockSpec((BLK, w), lambda i: (i, 0))
    const = lambda r, w: pl.BlockSpec((r, w), lambda i: (0, 0))
    return pl.pallas_call(
        body, name=name, grid=(nbk,),
        in_specs=[pl.BlockSpec(memory_space=pltpu.SMEM), blk(dq)] + _kv_specs(dkv, 0, nbk) + _kv_specs(dkv, vcol, nbk)
        + [blk(dq), blk(dq), blk(LANE)],
        out_specs=[blk(dq), blk(LANE), const(CTX, dkv), const(CTX, dkv), const(8, LANE)],
        out_shape=[jax.ShapeDtypeStruct((t, dq), F32), jax.ShapeDtypeStruct((t, LANE), F32),
                   jax.ShapeDtypeStruct((CTX, dkv), F32), jax.ShapeDtypeStruct((CTX, dkv), F32),
                   jax.ShapeDtypeStruct((8, LANE), F32)],
        compiler_params=_params(("arbitrary",)))(sink, qr, kr, kr, kr, kr, qkv, qkv, qkv, qkv, do, o, lse)


def attn_bwd_kv(name, qr, kr, qkv, do, lse, delta, seq):
    t, dq = qr.shape
    dkv = kr.shape[1]
    nbk, nbl, nkv = t // BLK, seq // BLK, dkv // HEAD
    cb = CTX // BLK
    vcol = (dq + dkv) // dkv
    scale = HEAD ** -0.5

    def qspec(w, d):
        return pl.BlockSpec((BLK, w), lambda j: (jnp.clip(j + cb + d, cb, nbk - 1), 0))

    def body(k_ref, v_ref, *refs):
        dk_ref, dv_ref = refs[-2], refs[-1]
        j = pl.program_id(0)
        qrow = lax.broadcasted_iota(jnp.int32, (GROUP * BLK, BLK), 0) % BLK
        kcol = lax.broadcasted_iota(jnp.int32, (GROUP * BLK, BLK), 1)
        for h in range(nkv):
            hs = slice(h * HEAD, (h + 1) * HEAD)
            kh, vh = k_ref[:, hs], v_ref[:, hs]
            dk_h = jnp.zeros((BLK, HEAD), F32)
            dv_h = jnp.zeros((BLK, HEAD), F32)
            for di, d in enumerate((-1, 0, 1)):
                q_ref, do_ref, lse_ref, dl_ref = refs[4 * di:4 * di + 4]
                n = j + d
                msk = (n >= 0) & (n < nbl) & (jnp.abs(d * BLK + qrow - kcol) <= WINDOW)
                q4, do4 = _stack_heads(q_ref, h), _stack_heads(do_ref, h)
                l4, d4 = _stack_cols(lse_ref[...], h), _stack_cols(dl_ref[...], h)
                s = jnp.where(msk, lax.dot_general(q4, kh, NT, preferred_element_type=F32) * scale, NEG)
                p = jnp.exp(s - l4)
                dv_h += lax.dot_general(p.astype(BF16), do4, TN, preferred_element_type=F32)
                dp = lax.dot_general(do4, vh, NT, preferred_element_type=F32)
                dk_h += lax.dot_general((p * (dp - d4) * scale).astype(BF16), q4, TN, preferred_element_type=F32)
            dk_ref[:, hs] = dk_h
            dv_ref[:, hs] = dv_h

    in_specs = [pl.BlockSpec((BLK, dkv), lambda j: (j + cb, 0)), pl.BlockSpec((BLK, dkv), lambda j: (j + cb, vcol))]
    operands = [kr, qkv]
    for d in (-1, 0, 1):
        in_specs += [qspec(dq, d), qspec(dq, d), qspec(LANE, d), qspec(LANE, d)]
        operands += [qr, do, lse, delta]
    return pl.pallas_call(
        body, name=name, grid=(nbl,), in_specs=in_specs,
        out_specs=[pl.BlockSpec((BLK, dkv), lambda j: (j, 0)), pl.BlockSpec((BLK, dkv), lambda j: (j, 0))],
        out_shape=[jax.ShapeDtypeStruct((seq, dkv), F32), jax.ShapeDtypeStruct((seq, dkv), F32)],
        compiler_params=_params(("parallel",)))(*operands)


def ada_fwd(name, cond, w_ada):
    nl, d, n = w_ada.shape
    tn = _pick(n, 1024)

    def body(c_ref, w_ref, out_ref):
        cv = c_ref[...]
        out_ref[...] = jnp.dot((cv * _sigmoid(cv)).astype(BF16), w_ref[...].astype(BF16), preferred_element_type=F32)

    return pl.pallas_call(
        body, name=name, grid=(nl, n // tn),
        in_specs=[pl.BlockSpec((16, d), lambda l, j: (0, 0)), pl.BlockSpec((None, d, tn), lambda l, j: (l, 0, j))],
        out_specs=pl.BlockSpec((None, 16, tn), lambda l, j: (l, 0, j)),
        out_shape=jax.ShapeDtypeStruct((nl, 16, n), F32), compiler_params=_params(("parallel", "parallel")))(cond, w_ada)


def ada_bwd_cond(name, dsum, w_ada):
    nl, d, n = w_ada.shape
    tn = _pick(n, 1024)

    def body(g_ref, w_ref, out_ref):
        @pl.when((pl.program_id(0) == 0) & (pl.program_id(1) == 0))
        def _():
            out_ref[...] = jnp.zeros_like(out_ref)

        out_ref[...] += lax.dot_general(g_ref[...].astype(BF16), w_ref[...].astype(BF16), NT, preferred_element_type=F32)

    return pl.pallas_call(
        body, name=name, grid=(nl, n // tn),
        in_specs=[pl.BlockSpec((None, 8, tn), lambda l, j: (l, 0, j)), pl.BlockSpec((None, d, tn), lambda l, j: (l, 0, j))],
        out_specs=pl.BlockSpec((8, d), lambda l, j: (0, 0)),
        out_shape=jax.ShapeDtypeStruct((8, d), F32), compiler_params=_params(("arbitrary", "arbitrary")))(dsum, w_ada)


def ada_grad_w(name, cond, rhs):
    nl, _, n = rhs.shape
    d = cond.shape[1]
    tr, tn = _pick(d, 512), _pick(n, 1024)

    def body(c_ref, r_ref, out_ref):
        cv = c_ref[...]
        out_ref[...] = lax.dot_general((cv * _sigmoid(cv)).astype(BF16), r_ref[...].astype(BF16), TN, preferred_element_type=F32)

    return pl.pallas_call(
        body, name=name, grid=(nl, d // tr, n // tn),
        in_specs=[pl.BlockSpec((16, tr), lambda l, i, j: (0, i)), pl.BlockSpec((None, 16, tn), lambda l, i, j: (l, 0, j))],
        out_specs=pl.BlockSpec((None, tr, tn), lambda l, i, j: (l, i, j)),
        out_shape=jax.ShapeDtypeStruct((nl, d, n), F32),
        compiler_params=_params(("parallel", "parallel", "parallel")))(cond, rhs)


def adamw(name, g, g_spec, w, m, v, tr):
    nl, r, c = w.shape
    spec = pl.BlockSpec((None, tr, c), lambda l, i: (l, i, 0))

    def body(g_ref, w_ref, m_ref, v_ref, go_ref, d_ref, mo_ref, vo_ref):
        gv = g_ref[...]
        mn = B1 * m_ref[...] + (1.0 - B1) * gv
        vn = B2 * v_ref[...] + (1.0 - B2) * (gv * gv)
        m_hat = mn / (1.0 - B1 ** STEP)
        v_hat = vn / (1.0 - B2 ** STEP)
        go_ref[...] = gv
        d_ref[...] = -LR * (m_hat / (jnp.sqrt(v_hat) + ADAM_EPS) + WD * w_ref[...])
        mo_ref[...] = mn
        vo_ref[...] = vn

    return pl.pallas_call(
        body, name=name, grid=(nl, r // tr), in_specs=[g_spec, spec, spec, spec], out_specs=[spec] * 4,
        out_shape=[jax.ShapeDtypeStruct(w.shape, F32)] * 4, compiler_params=_params(("parallel", "parallel")))(g, w, m, v)


def adamw_small(name, g, w, m, v):
    shape = w.shape
    r3 = lambda a: a.reshape(1, -1, shape[-1]).astype(F32)
    rows = r3(w).shape[1]
    outs = adamw(name, r3(g), pl.BlockSpec((None, rows, shape[-1]), lambda l, i: (l, i, 0)), r3(w), r3(m), r3(v), rows)
    return [o.reshape(shape) for o in outs]


def _place():
    x, y, c = lax.axis_index("x"), lax.axis_index("y"), lax.axis_index("c")
    return x, y, c, [(1 - x, y), (x, 1 - y), (1 - x, 1 - y)]


def small_allgather(name, v):
    r, w = v.shape

    def body(x_ref, out_ref, send_sems, recv_sems, local_sem):
        x, y, c, chips = _place()
        me, sibling = (x, y, c), (x, y, 1 - c)

        def slot(px, py, pc):
            return out_ref.at[4 * px + 2 * py + pc]

        def copy(k, block, to, src=None):
            return pltpu.make_async_remote_copy(
                src_ref=slot(*block) if src is None else src, dst_ref=slot(*block),
                send_sem=send_sems.at[k], recv_sem=recv_sems.at[k], device_id=to, device_id_type=MESH)

        mine = pltpu.make_async_copy(x_ref, slot(*me), local_sem)
        mine.start()
        first = [copy(0, me, sibling, src=x_ref)]
        first += [copy(1 + j, me, (*chip, c), src=x_ref) for j, chip in enumerate(chips)]
        for cp in first:
            cp.start()
        passed = [copy(4 + j, (*chip, c), sibling) for j, chip in enumerate(chips)]
        for j, chip in enumerate(chips):
            copy(1 + j, (*chip, c), me).wait_recv()
            passed[j].start()
        copy(0, sibling, me).wait_recv()
        for j, chip in enumerate(chips):
            copy(4 + j, (*chip, 1 - c), me).wait_recv()
        for cp in first + passed:
            cp.wait_send()
        mine.wait()

    return pl.pallas_call(
        body, name=name, out_shape=jax.ShapeDtypeStruct((8, r, w), v.dtype),
        in_specs=[pl.BlockSpec(memory_space=pltpu.VMEM)], out_specs=pl.BlockSpec(memory_space=pltpu.VMEM),
        scratch_shapes=[pltpu.SemaphoreType.DMA((7,)), pltpu.SemaphoreType.DMA((7,)), pltpu.SemaphoreType.DMA],
        compiler_params=pltpu.CompilerParams(vmem_limit_bytes=VMEM_LIMIT))(v)


def gather_flat(name, parts):
    flat = jnp.concatenate([p.reshape(-1).astype(F32) for p in parts])
    n = flat.shape[0]
    rows = _cdiv(n, 8 * LANE) * 8
    flat = jnp.pad(flat, (0, rows * LANE - n))
    return small_allgather(name, flat.reshape(rows, LANE)).reshape(8, rows * LANE)


def sum8(name, g):
    p = g.shape[1]
    g3 = g.reshape(8, p // LANE, LANE)
    tr = _pick(p // LANE, 1024, 8)

    def body(g_ref, out_ref):
        acc = g_ref[0]
        for k in range(1, 8):
            acc = acc + g_ref[k]
        out_ref[...] = acc

    return pl.pallas_call(
        body, name=name, grid=(p // LANE // tr,),
        in_specs=[pl.BlockSpec((8, tr, LANE), lambda i: (0, i, 0))], out_specs=pl.BlockSpec((tr, LANE), lambda i: (i, 0)),
        out_shape=jax.ShapeDtypeStruct((p // LANE, LANE), F32), compiler_params=_params(("parallel",)))(g3).reshape(p)


HBM_SPEC = pl.BlockSpec(memory_space=pltpu.HBM)


def gather_weights(name, wc, wr):
    def body(wc_in, wr_in, oc_ref, or_ref, send_sems, recv_sems):
        x, y, c, chips = _place()
        sibling = (x, y, 1 - c)
        copies = []
        for a, out in enumerate((oc_ref, or_ref)):
            half = out.shape[1] // 2

            def copy(k, chip, pc, to, out=out, half=half):
                blk = out.at[2 * chip[0] + chip[1], pl.ds(pc * half, half), :]
                return pltpu.make_async_remote_copy(
                    src_ref=blk, dst_ref=blk, send_sem=send_sems.at[k], recv_sem=recv_sems.at[k],
                    device_id=to, device_id_type=MESH)

            first = [copy(6 * a + j, (x, y), c, (*chip, c)) for j, chip in enumerate(chips)]
            for cp in first:
                cp.start()
            passed = [copy(6 * a + 3 + j, chip, c, sibling) for j, chip in enumerate(chips)]
            copies.append((first, passed, copy))
        for a, (first, passed, copy) in enumerate(copies):
            for j, chip in enumerate(chips):
                copy(6 * a + j, chip, c, (x, y, c)).wait_recv()
                passed[j].start()
        for a, (first, passed, copy) in enumerate(copies):
            for j, chip in enumerate(chips):
                copy(6 * a + 3 + j, chip, 1 - c, (x, y, c)).wait_recv()
            for cp in first + passed:
                cp.wait_send()

    return pl.pallas_call(
        body, name=name, out_shape=[jax.ShapeDtypeStruct(b.shape, b.dtype) for b in (wc, wr)],
        in_specs=[HBM_SPEC, HBM_SPEC], out_specs=[HBM_SPEC, HBM_SPEC], input_output_aliases={0: 0, 1: 1},
        scratch_shapes=[pltpu.SemaphoreType.DMA((12,)), pltpu.SemaphoreType.DMA((12,))])(wc, wr)


def pair_exchange(name, bufs):
    def body(*refs):
        ins, outs, (send_sems, recv_sems) = refs[:len(bufs)], refs[len(bufs):2 * len(bufs)], refs[2 * len(bufs):]
        x, y, c, _ = _place()
        cps = []
        for a, (src, out) in enumerate(zip(ins, outs)):
            half = src.shape[1] // 2
            cp = pltpu.make_async_remote_copy(
                src_ref=src.at[:, pl.ds((1 - c) * half, half), :], dst_ref=out, send_sem=send_sems.at[a],
                recv_sem=recv_sems.at[a], device_id=(x, y, 1 - c), device_id_type=MESH)
            cp.start()
            cps.append(cp)
        for cp in cps:
            cp.wait()

    return pl.pallas_call(
        body, name=name,
        out_shape=[jax.ShapeDtypeStruct((b.shape[0], b.shape[1] // 2, b.shape[2]), b.dtype) for b in bufs],
        in_specs=[HBM_SPEC] * len(bufs), out_specs=[HBM_SPEC] * len(bufs),
        scratch_shapes=[pltpu.SemaphoreType.DMA((len(bufs),)), pltpu.SemaphoreType.DMA((len(bufs),))])(*bufs)


def pair_add(name, buf, got, cidx):
    s, r, c = buf.shape
    half = r // 2
    tr = _pick(half, max(HALO, (4 * 1024 * 1024) // (2 * c)), HALO)
    per = half // tr

    def body(c_ref, a_ref, b_ref, out_ref):
        out_ref[...] = (a_ref[...].astype(F32) + b_ref[...].astype(F32)).astype(BF16)

    return pl.pallas_call(
        body, name=name,
        grid_spec=pltpu.PrefetchScalarGridSpec(
            num_scalar_prefetch=1, grid=(s, per),
            in_specs=[pl.BlockSpec((None, tr, c), lambda k, i, cr: (k, cr[0] * per + i, 0)),
                      pl.BlockSpec((None, tr, c), lambda k, i, cr: (k, i, 0))],
            out_specs=pl.BlockSpec((None, tr, c), lambda k, i, cr: (k, i, 0))),
        out_shape=jax.ShapeDtypeStruct((s, half, c), BF16),
        compiler_params=_params(("parallel", "parallel")))(cidx, buf, got)


def chip_scatter(name, bufs):
    def body(*refs):
        ins, outs = refs[:len(bufs)], refs[len(bufs):2 * len(bufs)]
        send_sems, recv_sems = refs[2 * len(bufs):]
        x, y, c, chips = _place()
        me = 2 * x + y
        started = []
        for a, (src, out) in enumerate(zip(ins, outs)):
            for j, chip in enumerate(chips):
                cp = pltpu.make_async_remote_copy(
                    src_ref=src.at[2 * chip[0] + chip[1]], dst_ref=out.at[me], send_sem=send_sems.at[3 * a + j],
                    recv_sem=recv_sems.at[3 * a + j], device_id=(*chip, c), device_id_type=MESH)
                cp.start()
                started.append(cp)
        for cp in started:
            cp.wait()

    n = len(bufs)
    return pl.pallas_call(
        body, name=name, out_shape=[jax.ShapeDtypeStruct(b.shape, b.dtype) for b in bufs],
        in_specs=[HBM_SPEC] * n, out_specs=[HBM_SPEC] * n,
        scratch_shapes=[pltpu.SemaphoreType.DMA((3 * n,)), pltpu.SemaphoreType.DMA((3 * n,))])(*bufs)


def chip_add(name, own, got, place):
    s, h, c = got.shape
    tr = _pick(h, max(HALO, (2 * 1024 * 1024) // (2 * c)), HALO)
    per = h // tr

    def body(p_ref, own_ref, g_ref, out_ref):
        acc = jnp.zeros((tr, c), F32)
        for k in range(s):
            acc = acc + jnp.where(p_ref[0] == k, own_ref[...], g_ref[k]).astype(F32)
        out_ref[...] = acc

    return pl.pallas_call(
        body, name=name,
        grid_spec=pltpu.PrefetchScalarGridSpec(
            num_scalar_prefetch=1, grid=(per,),
            in_specs=[pl.BlockSpec((None, tr, c), lambda i, p: (p[0], i, 0)),
                      pl.BlockSpec((s, tr, c), lambda i, p: (0, i, 0))],
            out_specs=pl.BlockSpec((tr, c), lambda i, p: (p[1] * per + i, 0))),
        out_shape=jax.ShapeDtypeStruct((2 * h, c), F32), compiler_params=_params(("parallel",)))(place, own, got)


def pair_join(name, bufs):
    def body(*refs):
        outs = refs[len(bufs):2 * len(bufs)]
        send_sems, recv_sems = refs[2 * len(bufs):]
        x, y, c, _ = _place()
        started = []
        for a, out in enumerate(outs):
            half = out.shape[0] // 2
            blk = out.at[pl.ds(c * half, half), :]
            cp = pltpu.make_async_remote_copy(src_ref=blk, dst_ref=blk, send_sem=send_sems.at[a], recv_sem=recv_sems.at[a],
                                              device_id=(x, y, 1 - c), device_id_type=MESH)
            cp.start()
            started.append(cp)
        for cp in started:
            cp.wait()

    n = len(bufs)
    return pl.pallas_call(
        body, name=name, out_shape=[jax.ShapeDtypeStruct(b.shape, b.dtype) for b in bufs],
        in_specs=[HBM_SPEC] * n, out_specs=[HBM_SPEC] * n, input_output_aliases={a: a for a in range(n)},
        scratch_shapes=[pltpu.SemaphoreType.DMA((n,)), pltpu.SemaphoreType.DMA((n,))])(*bufs)


def _rope_tables(seq):
    rows = seq // GRID_W
    row = jnp.repeat(jnp.arange(rows), GRID_W).astype(F32)
    col = jnp.tile(jnp.arange(GRID_W), rows).astype(F32)
    pairs = HEAD // 4
    inv = ROPE_BASE ** (-jnp.arange(pairs, dtype=F32) / pairs)
    ang = jnp.stack([row[:, None] * inv, col[:, None] * inv], axis=1)
    ang = jnp.broadcast_to(ang[:, :, None, :], (seq, 2, 2, pairs)).reshape(seq, HEAD)
    cos = jnp.concatenate([jnp.ones((CTX, HEAD), F32), jnp.cos(ang)], axis=0)
    sin = jnp.concatenate([jnp.zeros((CTX, HEAD), F32), jnp.sin(ang)], axis=0)
    return cos, sin


def _pad8(a):
    return jnp.pad(a, ((0, 8 - a.shape[0]), (0, 0)))


def kernel(x, c, ctx, c_ctx, w_ada, b_ada, attn_w_qkv, attn_w_o, attn_q_gain, attn_k_gain, attn_sink, sc_w_in, sc_conv, sc_w_out, ffn_w_up, ffn_conv, ffn_conv_b, ffn_w_down, loss_target, m_c_ctx, m_w_ada, m_b_ada, m_attn_w_qkv, m_attn_w_o, m_attn_q_gain, m_attn_k_gain, m_attn_sink, m_sc_w_in, m_sc_conv, m_sc_w_out, m_ffn_w_up, m_ffn_conv, m_ffn_conv_b, m_ffn_w_down, v_c_ctx, v_w_ada, v_b_ada, v_attn_w_qkv, v_attn_w_o, v_attn_q_gain, v_attn_k_gain, v_attn_sink, v_sc_w_in, v_sc_conv, v_sc_w_out, v_ffn_w_up, v_ffn_conv, v_ffn_conv_b, v_ffn_w_down):
    seq, d = x.shape[1], x.shape[2]
    depth, nada = w_ada.shape[0], w_ada.shape[2]
    n_attn, n_conv = attn_w_qkv.shape[0], sc_w_in.shape[0]
    ff = ffn_conv_b.shape[1]
    dq, dkv = d, d // GROUP
    nqkv = dq + 2 * dkv
    nh = dq // HEAD
    assert ctx.shape[1] == CTX and seq % RB == 0 and 6 * d == 4 * nada
    lay = Layout(d, ff, nqkv, depth)
    ax, ay, ac = lax.axis_index("x"), lax.axis_index("y"), lax.axis_index("c")
    chip, dev = 2 * ax + ay, 4 * ax + 2 * ay + ac
    cidx = jnp.reshape(ac, (1,)).astype(jnp.int32)

    chip1 = jnp.reshape(chip, (1,)).astype(jnp.int32)
    wc, wr = lax.empty((4, d, lay.ct), BF16), lax.empty((4, lay.rt, d), BF16)
    wc = cast_pack("pack_up", ffn_w_up, wc, lay.up[0], True, chip1)
    wc = cast_pack("pack_scin", sc_w_in, wc, lay.scin[0], True, chip1)
    wc = cast_pack("pack_qkv", attn_w_qkv, wc, lay.qkv[0], True, chip1)
    wr = cast_pack("pack_down", ffn_w_down, wr, lay.dn[0], False, chip1)
    wr = cast_pack("pack_wo", attn_w_o, wr, lay.wo[0], False, chip1)
    wr = cast_pack("pack_scout", sc_w_out, wr, lay.scout[0], False, chip1)
    wc, wr = gather_weights("gather_weights", wc, wr)

    g1 = gather_flat("gather_cond", [c, sc_conv, ffn_conv])
    c_all = g1[:, :d]
    o1 = d + sc_conv.size
    sc_conv_full = jnp.concatenate([g1[2 * s, d:o1].reshape(sc_conv.shape) for s in range(4)], axis=-1)
    ffn_conv_full = jnp.concatenate([g1[2 * s, o1:o1 + ffn_conv.size].reshape(ffn_conv.shape) for s in range(4)], axis=-1)
    cond = jnp.concatenate([c_all, c_ctx[None, :], jnp.zeros((7, d), F32)], axis=0)
    ada_part = ada_fwd("ada_fwd", cond, w_ada)
    g2 = gather_flat("gather_ada", [ada_part])
    ada_all = jnp.concatenate([g2[2 * s, :ada_part.size].reshape(ada_part.shape) for s in range(4)], axis=-1)
    ada_own = jnp.stack([lax.dynamic_index_in_dim(ada_all, 8, 1, False),
                         lax.dynamic_index_in_dim(ada_all, dev, 1, False)], axis=1) + b_ada[:, None, :]
    mods = jnp.pad(ada_own.reshape(depth, 2, 6, d), ((0, 0), (0, 0), (0, 2), (0, 0)))

    cos, sin = _rope_tables(seq)
    xa = jnp.concatenate([ctx[0], x[0]], axis=0)
    cws = [_pad8(sc_conv_full[j]) for j in range(n_conv)]
    cwf = [_pad8(jnp.concatenate([ffn_conv_full[l], ffn_conv_b[l][None, :]], axis=0)) for l in range(depth)]
    gains = [_pad8(jnp.stack([attn_q_gain[j], attn_k_gain[j]])) for j in range(n_attn)]

    saved = []
    for l in range(depth):
        j, is_attn, mod = l // N_MIX, l % N_MIX == 0, mods[l]
        sv = {"x_in": xa}
        h1 = norm_mod(f"norm1_{l}", xa, mod, 0, 1)
        if is_attn:
            qkv = mm_nn_col(f"qkv_{l}", h1, wc, lay.qkv[j], lay.nb_qkv)
            qr, kr = rope_fwd(f"rope_{l}", qkv, cos, sin, gains[j], dq, dkv)
            o, lse = attn_fwd(f"attn_{l}", qr, kr, qkv, attn_sink[j], seq)
            xa, y_m = mm_nn_row(f"wo_{l}", o, wr, lay.wo[j], lay.kb_o, xa, mod, 2)
            sv.update(qkv=qkv, qr=qr, kr=kr, o=o, lse=lse)
        else:
            u = mm_nn_col(f"scin_{l}", h1, wc, lay.scin[j], lay.nb_sc)
            z = sc_gate_fwd(f"scgate_{l}", u, cws[j])
            xa, y_m = mm_nn_row(f"scout_{l}", z, wr, lay.scout[j], lay.kb_o, xa, mod, 2)
            sv.update(u=u, z=z)
        h2 = norm_mod(f"norm2_{l}", xa, mod, 3, 4)
        up = mm_nn_col(f"up_{l}", h2, wc, lay.up[l], lay.nb_up)
        act = ffn_act_fwd(f"act_{l}", up, cwf[l])
        sv.update(h1=h1, y_m=y_m, x_mid=xa, h2=h2, up=up, act=act)
        xa, y_f = mm_nn_row(f"down_{l}", act, wr, lay.dn[l], lay.kb_dn, xa, mod, 5)
        sv["y_f"] = y_f
        saved.append(sv)

    dx, dy, acc, lp = loss_head("loss", xa, loss_target[0], saved[-1]["y_f"], mods[-1], 5)
    loss = lax.psum(0.5 * jnp.sum(lp[0]) / d, ("x", "y", "c"))
    gc, gr = lax.empty((4, d, lay.ct), BF16), lax.empty((4, lay.rt, d), BF16)
    d_mod = [jnp.zeros((2, 6, d), F32) for _ in range(depth)]

    def add_mod(l, acc, idx):
        upd = jnp.zeros((2, 6, d), F32)
        for row, k in idx:
            upd = upd.at[:, k, :].set(acc[:, row, :])
        d_mod[l] = d_mod[l] + upd

    add_mod(depth - 1, acc, [(2, 5)])
    d_conv_f, d_conv_s = [None] * depth, [None] * n_conv
    d_gq, d_gk, d_sink = [None] * n_attn, [None] * n_attn, [None] * n_attn
    for l in reversed(range(depth)):
        j, is_attn, sv = l // N_MIX, l % N_MIX == 0, saved[l]
        gr = mm_tn_row(f"g_down_{l}", sv["act"], dy, gr, lay.dn[l], lay.kb_dn)
        da = mm_nt_row(f"d_act_{l}", dy, wr, lay.dn[l], lay.kb_dn)
        d_up, d_conv_f[l] = ffn_act_bwd(f"act_bwd_{l}", sv["up"], da, cwf[l])
        gc = mm_tn_col(f"g_up_{l}", sv["h2"], d_up, gc, lay.up[l], lay.nb_up)
        dh2 = mm_nt_col(f"d_h2_{l}", d_up, wc, lay.up[l], lay.nb_up)
        dx, dy, acc = resid_bwd(f"norm2_bwd_{l}", dx, dh2, sv["x_mid"], mods[l], 3, 4, sv["y_m"], mods[l], 2)
        add_mod(l, acc, [(0, 3), (1, 4), (2, 2)])
        if is_attn:
            gr = mm_tn_row(f"g_wo_{l}", sv["o"], dy, gr, lay.wo[j], lay.kb_o)
            do = mm_nt_row(f"d_o_{l}", dy, wr, lay.wo[j], lay.kb_o)
            dqr, delta, dkc, dvc, dsk = attn_bwd_q(f"attn_bwd_q_{l}", sv["qr"], sv["kr"], sv["qkv"], attn_sink[j],
                                                   do, sv["o"], sv["lse"], seq)
            dkl, dvl = attn_bwd_kv(f"attn_bwd_kv_{l}", sv["qr"], sv["kr"], sv["qkv"], do, sv["lse"], delta, seq)
            dqkv, dgn = rope_bwd(f"rope_bwd_{l}", sv["qkv"], dqr, jnp.concatenate([dkc, dkl], axis=0),
                                 jnp.concatenate([dvc, dvl], axis=0), cos, sin, gains[j])
            d_gq[j], d_gk[j], d_sink[j] = dgn[0], dgn[1], dsk[0, :nh]
            gc = mm_tn_col(f"g_qkv_{l}", sv["h1"], dqkv, gc, lay.qkv[j], lay.nb_qkv)
            dh1 = mm_nt_col(f"d_h1_{l}", dqkv, wc, lay.qkv[j], lay.nb_qkv)
        else:
            gr = mm_tn_row(f"g_scout_{l}", sv["z"], dy, gr, lay.scout[j], lay.kb_o)
            dz = mm_nt_row(f"d_z_{l}", dy, wr, lay.scout[j], lay.kb_o)
            du, dcw = sc_gate_bwd(f"scgate_bwd_{l}", sv["u"], dz, cws[j])
            d_conv_s[j] = dcw[:3]
            gc = mm_tn_col(f"g_scin_{l}", sv["h1"], du, gc, lay.scin[j], lay.nb_sc)
            dh1 = mm_nt_col(f"d_h1_{l}", du, wc, lay.scin[j], lay.nb_sc)
        if l > 0:
            dx, dy, acc = resid_bwd(f"norm1_bwd_{l}", dx, dh1, sv["x_in"], mods[l], 0, 1, saved[l - 1]["y_f"], mods[l - 1], 5)
            add_mod(l - 1, acc, [(2, 5)])
        else:
            dx, acc = resid_bwd(f"norm1_bwd_{l}", dx, dh1, sv["x_in"], mods[l], 0, 1)
        add_mod(l, acc, [(0, 0), (1, 1)])
    grad_x = dx[CTX:][None]

    ra_c, ra_r = pair_exchange("pair_exchange", [gc, gr])
    hc, hr = pair_add("pair_add_c", gc, ra_c, cidx), pair_add("pair_add_r", gr, ra_r, cidx)
    rb_c, rb_r = chip_scatter("chip_scatter", [hc, hr])
    place = jnp.stack([chip, ac]).astype(jnp.int32)
    fc, fr = chip_add("chip_add_c", hc, rb_c, place), chip_add("chip_add_r", hr, rb_r, place)
    gfc, gfr = pair_join("pair_join", [fc, fr])

    d_ada = jnp.stack(d_mod).reshape(depth, 2, 6 * d)
    small = [d_ada, jnp.stack(d_gq), jnp.stack(d_gk), jnp.stack(d_sink), jnp.stack(d_conv_s),
             jnp.stack([t[:3] for t in d_conv_f]), jnp.stack([t[3] for t in d_conv_f])]
    g3 = gather_flat("gather_small", small)
    tot = sum8("sum_small", g3)
    sizes = [s.size for s in small]
    offs = [sum(sizes[:k]) for k in range(len(sizes) + 1)]
    part = lambda k: tot[offs[k]:offs[k + 1]].reshape(small[k].shape)
    g_b_ada = part(0)[:, 0] + part(0)[:, 1]
    g_q_gain, g_k_gain, g_sink = part(1), part(2), part(3)
    g_sc_conv = lax.dynamic_slice_in_dim(part(4), chip * sc_conv.shape[2], sc_conv.shape[2], 2)
    g_ffn_conv = lax.dynamic_slice_in_dim(part(5), chip * ffn_conv.shape[2], ffn_conv.shape[2], 2)
    g_conv_b = part(6)

    d_ada_all = g3[:, :d_ada.size].reshape(8, depth, 2, 6 * d)
    cols = lambda a: lax.dynamic_slice_in_dim(a, chip * nada, nada, a.ndim - 1)
    d_lat = cols(jnp.moveaxis(d_ada_all[:, :, 1], 0, 1))
    d_ctx = cols(part(0)[:, 0])
    rhs = jnp.concatenate([d_lat, d_ctx[:, None], jnp.zeros((depth, 7, nada), F32)], axis=1)
    g_w_ada = ada_grad_w("ada_grad_w", cond, rhs)
    dcc = ada_bwd_cond("ada_bwd_cond", jnp.pad(d_ctx[:, None], ((0, 0), (0, 7), (0, 0))), w_ada)[0]
    g4 = gather_flat("gather_dcc", [dcc])
    d_silu = g4[0, :d] + g4[2, :d] + g4[4, :d] + g4[6, :d]
    sg = _sigmoid(c_ctx)
    g_c_ctx = d_silu * (sg * (1.0 + c_ctx * (1.0 - sg)))

    def adam_rows(k, n):
        return _pick(k, max(8, ADAM_TILE_ELEMS // n), 8)

    def big(name, gbuf, off, col, w, m, v):
        nl, k, n = w.shape
        tr = adam_rows(k, n)
        if col:
            ob = off // n
            spec = pl.BlockSpec((tr, n), lambda l, i: (i, ob + l))
        else:
            ob, per = off // tr, k // tr
            spec = pl.BlockSpec((tr, n), lambda l, i: (ob + l * per + i, 0))
        return adamw(name, gbuf, spec, w, m, v, tr)

    ada_tr = adam_rows(d, nada)
    res = {
        "c_ctx": adamw_small("adam_c_ctx", g_c_ctx, c_ctx, m_c_ctx, v_c_ctx),
        "w_ada": adamw("adam_w_ada", g_w_ada, pl.BlockSpec((None, ada_tr, nada), lambda l, i: (l, i, 0)), w_ada, m_w_ada, v_w_ada, ada_tr),
        "b_ada": adamw_small("adam_b_ada", g_b_ada, b_ada, m_b_ada, v_b_ada),
        "attn_w_qkv": big("adam_qkv", gfc, lay.qkv[0], True, attn_w_qkv, m_attn_w_qkv, v_attn_w_qkv),
        "attn_w_o": big("adam_wo", gfr, lay.wo[0], False, attn_w_o, m_attn_w_o, v_attn_w_o),
        "attn_q_gain": adamw_small("adam_q_gain", g_q_gain, attn_q_gain, m_attn_q_gain, v_attn_q_gain),
        "attn_k_gain": adamw_small("adam_k_gain", g_k_gain, attn_k_gain, m_attn_k_gain, v_attn_k_gain),
        "attn_sink": adamw_small("adam_sink", g_sink, attn_sink, m_attn_sink, v_attn_sink),
        "sc_w_in": big("adam_scin", gfc, lay.scin[0], True, sc_w_in, m_sc_w_in, v_sc_w_in),
        "sc_conv": adamw_small("adam_sc_conv", g_sc_conv, sc_conv, m_sc_conv, v_sc_conv),
        "sc_w_out": big("adam_scout", gfr, lay.scout[0], False, sc_w_out, m_sc_w_out, v_sc_w_out),
        "ffn_w_up": big("adam_up", gfc, lay.up[0], True, ffn_w_up, m_ffn_w_up, v_ffn_w_up),
        "ffn_conv": adamw_small("adam_ffn_conv", g_ffn_conv, ffn_conv, m_ffn_conv, v_ffn_conv),
        "ffn_conv_b": adamw_small("adam_conv_b", g_conv_b, ffn_conv_b, m_ffn_conv_b, v_ffn_conv_b),
        "ffn_w_down": big("adam_down", gfr, lay.dn[0], False, ffn_w_down, m_ffn_w_down, v_ffn_w_down),
    }
    names = list(res)
    return (loss, grad_x, *[res[n][0] for n in names], *[res[n][1] for n in names],
            *[res[n][2] for n in names], *[res[n][3] for n in names])
```

```python
import functools

import jax
import jax.numpy as jnp
from jax import lax
from jax.experimental import pallas as pl
from jax.experimental.pallas import tpu as pltpu

F32, BF16 = jnp.float32, jnp.bfloat16
MESH = pl.DeviceIdType.MESH
VMEM_LIMIT = 56 * 1024 * 1024
LANE = 128
HALO = 16
HEAD = 128
GROUP = 4
CTX = 256
BLK = 128
WINDOW = 128
RB = 256
GRID_W = 64
ROPE_BASE = 10000.0
EPS = 1e-6
NEG = -1e30
N_MIX = 2
LR, B1, B2, ADAM_EPS, WD, STEP = 0.001, 0.9, 0.999, 1e-08, 0.01, 10
ADAM_TILE_ELEMS = 400 * 1024
NT = (((1,), (1,)), ((), ()))
TN = (((0,), (0,)), ((), ()))


def _pick(dim, target, mult=LANE):
    best = None
    for t in range(mult, min(dim, target) + 1, mult):
        if dim % t == 0:
            best = t
    return dim if best is None else best


def _cdiv(a, b):
    return -(-a // b)


def _params(sem):
    return pltpu.CompilerParams(dimension_semantics=sem, vmem_limit_bytes=VMEM_LIMIT)


def _sigmoid(g):
    return 1.0 / (1.0 + jnp.exp(-g))


def _row(v, r):
    rows = lax.broadcasted_iota(jnp.int32, v.shape, 0)
    return jnp.sum(jnp.where(rows == r, v, 0.0), axis=0, keepdims=True)


def _get_col(v, c):
    lanes = lax.broadcasted_iota(jnp.int32, v.shape, 1)
    return jnp.sum(jnp.where(lanes == c, v, 0.0), axis=1, keepdims=True)


def _put_col(v, c, col):
    lanes = lax.broadcasted_iota(jnp.int32, v.shape, 1)
    return jnp.where(lanes == c, col, v)


def _rows3(s0, s1, s2, width):
    rows = lax.broadcasted_iota(jnp.int32, (8, width), 0)
    z = jnp.zeros((8, width), F32)
    return jnp.where(rows == 0, s0, jnp.where(rows == 1, s1, jnp.where(rows == 2, s2, z)))


def _shift_rows(w, prev_row, next_row):
    n = w.shape[0]
    rows = lax.broadcasted_iota(jnp.int32, (n, 1), 0)
    down = jnp.where(rows == 0, prev_row, pltpu.roll(w, 1, 0))
    up = jnp.where(rows == n - 1, next_row, pltpu.roll(w, n - 1, 0))
    return down, up


def _seg_flags(i, nt):
    return i <= 1, (i == 0) | (i == nt - 1)


def _halo_specs(width, nrows):
    r = RB // HALO
    nh = nrows // HALO
    prev = pl.BlockSpec((HALO, width), lambda i: (jnp.maximum(i * r - 1, 0), 0))
    nxt = pl.BlockSpec((HALO, width), lambda i: (jnp.minimum((i + 1) * r, nh - 1), 0))
    return prev, nxt


class Layout:
    def __init__(self, d, ff, nqkv, depth):
        self.nb_up, self.kb_dn, self.kb_o = 2 * ff // 4, ff // 4, d // 4
        self.nb_mix = [(nqkv if l % N_MIX == 0 else 3 * d) // 4 for l in range(depth)]
        self.mix = [_cdiv(self.nb_up, nb) * nb for nb in self.nb_mix]
        self.ct = [m + nb for m, nb in zip(self.mix, self.nb_mix)]
        self.out = _cdiv(self.kb_dn, self.kb_o) * self.kb_o
        self.rt = _cdiv(self.out + self.kb_o, 2 * HALO) * 2 * HALO


def _mm_call(name, grid, in_specs, out_specs, out_shape, contract, operands, acc_shape, epilogue,
             n_extra=0, aliases=None, comm=None):
    nk = grid[2]
    n_out = len(out_shape)
    n_cin = len(comm["ins"]) if comm else 0
    n_cout = len(comm["out_shape"]) if comm else 0
    aliases = dict(aliases or {})
    in_specs, out_specs, out_shape, operands = list(in_specs), list(out_specs), list(out_shape), list(operands)
    scratch = [] if nk == 1 else [pltpu.VMEM(acc_shape, F32)]
    if comm:
        for i_in, i_out in comm["aliases"].items():
            aliases[len(operands) + i_in] = n_out + i_out
        in_specs += [HBM_SPEC] * n_cin
        out_specs += [HBM_SPEC] * n_cout
        out_shape += comm["out_shape"]
        operands += comm["ins"]
        scratch += [pltpu.SemaphoreType.DMA((comm["n_sems"],)), pltpu.SemaphoreType.DMA((comm["n_sems"],))]

    def body(*refs):
        a_ref, b_ref = refs[0], refs[1]
        extra = refs[2:2 + n_extra]
        p = 2 + n_extra
        cin, outs = refs[p:p + n_cin], refs[p + n_cin:p + n_cin + n_out]
        couts = refs[p + n_cin + n_out:p + n_cin + n_out + n_cout]
        scr = refs[p + n_cin + n_out + n_cout:]
        ids = (pl.program_id(0), pl.program_id(1))
        k = pl.program_id(2)
        if comm:
            start, finish = comm["ops"](cin, couts, scr[-2], scr[-1])

            @pl.when((ids[0] == 0) & (ids[1] == 0) & (k == 0))
            def _():
                start()

        def part():
            return lax.dot_general(a_ref[...], b_ref[...], contract, preferred_element_type=F32)

        if nk == 1:
            epilogue(part(), extra, outs, ids)
        else:
            acc = scr[0]

            @pl.when(k == 0)
            def _():
                acc[...] = jnp.zeros_like(acc)

            acc[...] += part()

            @pl.when(k == nk - 1)
            def _():
                epilogue(acc[...], extra, outs, ids)

        if comm:
            @pl.when((ids[0] == grid[0] - 1) & (ids[1] == grid[1] - 1) & (k == nk - 1))
            def _():
                finish()

    sem = ("arbitrary",) * 3 if comm else ("parallel", "parallel", "arbitrary")
    res = pl.pallas_call(
        body, name=name, grid=grid, in_specs=in_specs, out_specs=out_specs, out_shape=out_shape,
        scratch_shapes=scratch, input_output_aliases=aliases, compiler_params=_params(sem))(*operands)
    return (res[:n_out], res[n_out:]) if comm else res


def _store(dtype):
    def epilogue(r, extra, outs, ids):
        outs[0][...] = r.astype(dtype)
    return epilogue


def _ret(res, comm, single=True):
    if comm:
        return (res[0][0] if single else res[0]), res[1]
    return res[0] if single else res


def mm_nn_col(name, a, wc, off, nb, comm=None):
    m, d = a.shape
    s = wc.shape[0]
    tm, tn = _pick(m, 1056, HALO), _pick(nb, 1536)
    npb, ob = nb // tn, off // tn
    assert off % tn == 0
    return _ret(_mm_call(
        name, (m // tm, s * npb, 1),
        [pl.BlockSpec((tm, d), lambda i, j, k: (i, 0)),
         pl.BlockSpec((None, d, tn), lambda i, j, k: (j // npb, 0, ob + j % npb))],
        [pl.BlockSpec((tm, tn), lambda i, j, k: (i, j))],
        [jax.ShapeDtypeStruct((m, s * nb), BF16)],
        (((1,), (0,)), ((), ())), (a, wc), None, _store(BF16), comm=comm), comm)


def mm_nn_row(name, a, wr, off, kb, res, mod, gi, comm=None):
    m = a.shape[0]
    s, _, d = wr.shape
    tm, tk, tn = _pick(m, 1056, HALO), _pick(kb, 1408), _pick(d, 1024)
    kpb, ob = kb // tk, off // tk
    assert off % tk == 0

    def epilogue(r, extra, outs, ids):
        res_ref, mod_ref = extra
        rows = ids[0] * tm + lax.broadcasted_iota(jnp.int32, (tm, 1), 0)
        g = jnp.where(rows < CTX, mod_ref[0, gi:gi + 1, :], mod_ref[1, gi:gi + 1, :])
        outs[0][...] = res_ref[...] + g * r
        outs[1][...] = r.astype(BF16)

    return _ret(_mm_call(
        name, (m // tm, d // tn, s * kpb),
        [pl.BlockSpec((tm, tk), lambda i, j, k: (i, k)),
         pl.BlockSpec((None, tk, tn), lambda i, j, k: (k // kpb, ob + k % kpb, j)),
         pl.BlockSpec((tm, tn), lambda i, j, k: (i, j)),
         pl.BlockSpec((2, 8, tn), lambda i, j, k: (0, 0, j))],
        [pl.BlockSpec((tm, tn), lambda i, j, k: (i, j)), pl.BlockSpec((tm, tn), lambda i, j, k: (i, j))],
        [jax.ShapeDtypeStruct((m, d), F32), jax.ShapeDtypeStruct((m, d), BF16)],
        (((1,), (0,)), ((), ())), (a, wr, res, mod), (tm, tn), epilogue, n_extra=2, comm=comm), comm, single=False)


def mm_nt_col(name, dy, wc, off, nb):
    m = dy.shape[0]
    s, d, _ = wc.shape
    tm, tc = _pick(m, 768, HALO), _pick(nb, 1536)
    npb, ob = nb // tc, off // tc
    return _mm_call(
        name, (m // tm, 1, s * npb),
        [pl.BlockSpec((tm, tc), lambda i, j, k: (i, k)),
         pl.BlockSpec((None, d, tc), lambda i, j, k: (k // npb, 0, ob + k % npb))],
        [pl.BlockSpec((tm, d), lambda i, j, k: (i, 0))],
        [jax.ShapeDtypeStruct((m, d), F32)],
        NT, (dy, wc), (tm, d), _store(F32))[0]


def mm_nt_row(name, dy, wr, off, kb):
    m, d = dy.shape
    s = wr.shape[0]
    tm, tkb = _pick(m, 1056, HALO), _pick(kb, 1408)
    kpb, ob = kb // tkb, off // tkb
    return _mm_call(
        name, (m // tm, s * kpb, 1),
        [pl.BlockSpec((tm, d), lambda i, j, k: (i, 0)),
         pl.BlockSpec((None, tkb, d), lambda i, j, k: (j // kpb, ob + j % kpb, 0))],
        [pl.BlockSpec((tm, tkb), lambda i, j, k: (i, j))],
        [jax.ShapeDtypeStruct((m, s * kb), BF16)],
        NT, (dy, wr), None, _store(BF16))[0]


def mm_tn_col(name, a, dy, gbuf, off, nb, comm=None):
    t, d = a.shape
    s = gbuf.shape[0]
    tka, tn, tt = _pick(d, 2048), _pick(nb, 1536), _pick(t, 1056, HALO)
    npb, ob = nb // tn, off // tn
    return _ret(_mm_call(
        name, (d // tka, s * npb, t // tt),
        [pl.BlockSpec((tt, tka), lambda i, j, k: (k, i)),
         pl.BlockSpec((tt, tn), lambda i, j, k: (k, j)),
         pl.BlockSpec(memory_space=pl.ANY)],
        [pl.BlockSpec((None, tka, tn), lambda i, j, k: (j // npb, i, ob + j % npb))],
        [jax.ShapeDtypeStruct(gbuf.shape, BF16)],
        TN, (a, dy, gbuf), (tka, tn), _store(BF16), n_extra=1, aliases={2: 0}, comm=comm), comm)


def mm_tn_row(name, act, dy, gbuf, off, kb, comm=None):
    t, d = dy.shape
    s = gbuf.shape[0]
    tka, tn, tt = _pick(kb, 1408), _pick(d, 2048), _pick(t, 1056, HALO)
    kpb, ob = kb // tka, off // tka
    return _ret(_mm_call(
        name, (s * kpb, d // tn, t // tt),
        [pl.BlockSpec((tt, tka), lambda i, j, k: (k, i)),
         pl.BlockSpec((tt, tn), lambda i, j, k: (k, j)),
         pl.BlockSpec(memory_space=pl.ANY)],
        [pl.BlockSpec((None, tka, tn), lambda i, j, k: (i // kpb, ob + i % kpb, j))],
        [jax.ShapeDtypeStruct(gbuf.shape, BF16)],
        TN, (act, dy, gbuf), (tka, tn), _store(BF16), n_extra=1, aliases={2: 0}, comm=comm), comm)


def cast_pack(name, w, l, buf, off, col, chip):
    _, k, n = w.shape
    if col:
        tr = _pick(k, 512, HALO)
        ob = off // n
        out_spec = pl.BlockSpec((None, tr, n), lambda i, s: (s[0], i, ob))
    else:
        tr = _pick(k, 704, HALO)
        ob = off // tr
        out_spec = pl.BlockSpec((None, tr, n), lambda i, s: (s[0], ob + i, 0))

    def body(s_ref, w_ref, buf_ref, out_ref):
        out_ref[...] = w_ref[...].astype(BF16)

    return pl.pallas_call(
        body, name=name,
        grid_spec=pltpu.PrefetchScalarGridSpec(
            num_scalar_prefetch=1, grid=(k // tr,),
            in_specs=[pl.BlockSpec((None, tr, n), lambda i, s: (l, i, 0)), pl.BlockSpec(memory_space=pl.ANY)],
            out_specs=out_spec),
        out_shape=jax.ShapeDtypeStruct(buf.shape, BF16),
        input_output_aliases={2: 0}, compiler_params=_params(("parallel",)))(chip, w, buf)


def norm_mod(name, x, mod, sh, sc):
    t, d = x.shape

    def body(x_ref, mod_ref, h_ref):
        seg = jnp.minimum(pl.program_id(0), 1)
        xv = x_ref[...]
        r = lax.rsqrt(jnp.mean(xv * xv, axis=-1, keepdims=True) + EPS)
        m = mod_ref[seg]
        h_ref[...] = ((xv * r) * (1.0 + m[sc:sc + 1, :]) + m[sh:sh + 1, :]).astype(BF16)

    return pl.pallas_call(
        body, name=name, grid=(t // RB,),
        in_specs=[pl.BlockSpec((RB, d), lambda i: (i, 0)), pl.BlockSpec((2, 8, d), lambda i: (0, 0, 0))],
        out_specs=pl.BlockSpec((RB, d), lambda i: (i, 0)),
        out_shape=jax.ShapeDtypeStruct((t, d), BF16), compiler_params=_params(("parallel",)))(x, mod)


def sc_gate_fwd(name, u, cw):
    t = u.shape[0]
    d = u.shape[1] // 3
    nt, tc = t // RB, _pick(d, 512)
    prev, nxt = _halo_specs(3 * d, t)

    def body(u_ref, up_ref, un_ref, cw_ref, z_ref):
        first, last = _seg_flags(pl.program_id(0), nt)
        for j in range(d // tc):
            c0 = j * tc
            gb = u_ref[:, c0:c0 + tc].astype(F32)
            w = u_ref[:, d + c0:d + c0 + tc].astype(F32) * u_ref[:, 2 * d + c0:2 * d + c0 + tc].astype(F32)
            pw = _row(up_ref[:, d + c0:d + c0 + tc].astype(F32) * up_ref[:, 2 * d + c0:2 * d + c0 + tc].astype(F32), HALO - 1)
            nw = _row(un_ref[:, d + c0:d + c0 + tc].astype(F32) * un_ref[:, 2 * d + c0:2 * d + c0 + tc].astype(F32), 0)
            wd, wu = _shift_rows(w, jnp.where(first, 0.0, pw), jnp.where(last, 0.0, nw))
            cwj = cw_ref[:, c0:c0 + tc]
            conv = wd * cwj[0:1] + w * cwj[1:2] + wu * cwj[2:3]
            z_ref[:, c0:c0 + tc] = (gb * conv).astype(BF16)

    return pl.pallas_call(
        body, name=name, grid=(nt,),
        in_specs=[pl.BlockSpec((RB, 3 * d), lambda i: (i, 0)), prev, nxt, pl.BlockSpec((8, d), lambda i: (0, 0))],
        out_specs=pl.BlockSpec((RB, d), lambda i: (i, 0)),
        out_shape=jax.ShapeDtypeStruct((t, d), BF16), compiler_params=_params(("parallel",)))(u, u, u, cw)


def sc_gate_bwd(name, u, dz, cw):
    t = u.shape[0]
    d = u.shape[1] // 3
    nt, tc = t // RB, _pick(d, 512)
    prev, nxt = _halo_specs(3 * d, t)
    dprev, dnxt = _halo_specs(d, t)

    def body(u_ref, up_ref, un_ref, dz_ref, dzp_ref, dzn_ref, cw_ref, du_ref, dcw_ref):
        i = pl.program_id(0)
        first, last = _seg_flags(i, nt)

        @pl.when(i == 0)
        def _():
            dcw_ref[...] = jnp.zeros_like(dcw_ref)

        for j in range(d // tc):
            c0 = j * tc
            sl0, sl1, sl2 = slice(c0, c0 + tc), slice(d + c0, d + c0 + tc), slice(2 * d + c0, 2 * d + c0 + tc)
            gb, gc, v = u_ref[:, sl0].astype(F32), u_ref[:, sl1].astype(F32), u_ref[:, sl2].astype(F32)
            w = gc * v
            pw = _row(up_ref[:, sl1].astype(F32) * up_ref[:, sl2].astype(F32), HALO - 1)
            nw = _row(un_ref[:, sl1].astype(F32) * un_ref[:, sl2].astype(F32), 0)
            wd, wu = _shift_rows(w, jnp.where(first, 0.0, pw), jnp.where(last, 0.0, nw))
            cwj = cw_ref[:, sl0]
            cw0, cw1, cw2 = cwj[0:1], cwj[1:2], cwj[2:3]
            dzv = dz_ref[:, sl0].astype(F32)
            e = dzv * gb
            pe = _row(dzp_ref[:, sl0].astype(F32) * up_ref[:, sl0].astype(F32), HALO - 1)
            ne = _row(dzn_ref[:, sl0].astype(F32) * un_ref[:, sl0].astype(F32), 0)
            ed, eu = _shift_rows(e, jnp.where(first, 0.0, pe), jnp.where(last, 0.0, ne))
            dw = cw0 * eu + cw1 * e + cw2 * ed
            du_ref[:, sl0] = (dzv * (wd * cw0 + w * cw1 + wu * cw2)).astype(BF16)
            du_ref[:, sl1] = (dw * v).astype(BF16)
            du_ref[:, sl2] = (dw * gc).astype(BF16)
            dcw_ref[:, sl0] += _rows3(jnp.sum(e * wd, axis=0, keepdims=True), jnp.sum(e * w, axis=0, keepdims=True),
                                      jnp.sum(e * wu, axis=0, keepdims=True), tc)

    return pl.pallas_call(
        body, name=name, grid=(nt,),
        in_specs=[pl.BlockSpec((RB, 3 * d), lambda i: (i, 0)), prev, nxt,
                  pl.BlockSpec((RB, d), lambda i: (i, 0)), dprev, dnxt, pl.BlockSpec((8, d), lambda i: (0, 0))],
        out_specs=[pl.BlockSpec((RB, 3 * d), lambda i: (i, 0)), pl.BlockSpec((8, d), lambda i: (0, 0))],
        out_shape=[jax.ShapeDtypeStruct((t, 3 * d), BF16), jax.ShapeDtypeStruct((8, d), F32)],
        compiler_params=_params(("arbitrary",)))(u, u, u, dz, dz, dz, cw)


def ffn_act_fwd(name, up, cw):
    t = up.shape[0]
    ff = up.shape[1] // 2
    nt, tc = t // RB, _pick(ff, 1408)
    prev, nxt = _halo_specs(2 * ff, t)

    def body(up_ref, upp_ref, upn_ref, cw_ref, a_ref):
        first, last = _seg_flags(pl.program_id(0), nt)
        for j in range(ff // tc):
            sg, sv = slice(j * tc, (j + 1) * tc), slice(ff + j * tc, ff + (j + 1) * tc)
            gate = up_ref[:, sg].astype(F32)
            pg = _row(upp_ref[:, sg].astype(F32), HALO - 1)
            ng = _row(upn_ref[:, sg].astype(F32), 0)
            gd, gu = _shift_rows(gate, jnp.where(first, 0.0, pg), jnp.where(last, 0.0, ng))
            cwj = cw_ref[:, sg]
            g = gd * cwj[0:1] + gate * cwj[1:2] + gu * cwj[2:3] + cwj[3:4]
            a_ref[:, sg] = (g * _sigmoid(g) * up_ref[:, sv].astype(F32)).astype(BF16)

    return pl.pallas_call(
        body, name=name, grid=(nt,),
        in_specs=[pl.BlockSpec((RB, 2 * ff), lambda i: (i, 0)), prev, nxt, pl.BlockSpec((8, ff), lambda i: (0, 0))],
        out_specs=pl.BlockSpec((RB, ff), lambda i: (i, 0)),
        out_shape=jax.ShapeDtypeStruct((t, ff), BF16), compiler_params=_params(("parallel",)))(up, up, up, cw)


def ffn_act_bwd(name, up, da, cw):
    t = up.shape[0]
    ff = up.shape[1] // 2
    nt, tc = t // RB, _pick(ff, 1408)
    prev, nxt = _halo_specs(2 * ff, t)
    dprev, dnxt = _halo_specs(ff, t)

    def dsilu(g):
        s = _sigmoid(g)
        return s * (1.0 + g * (1.0 - s))

    def body(up_ref, upp_ref, upn_ref, da_ref, dap_ref, dan_ref, cw_ref, dup_ref, acc_ref):
        i = pl.program_id(0)
        first, last = _seg_flags(i, nt)

        @pl.when(i == 0)
        def _():
            acc_ref[...] = jnp.zeros_like(acc_ref)

        for j in range(ff // tc):
            sg, sv = slice(j * tc, (j + 1) * tc), slice(ff + j * tc, ff + (j + 1) * tc)
            gate, val, dav = up_ref[:, sg].astype(F32), up_ref[:, sv].astype(F32), da_ref[:, sg].astype(F32)
            pgt, ngt = upp_ref[:, sg].astype(F32), upn_ref[:, sg].astype(F32)
            pg1, pg2 = _row(pgt, HALO - 1), _row(pgt, HALO - 2)
            ng1, ng2 = _row(ngt, 0), _row(ngt, 1)
            cwj = cw_ref[:, sg]
            cw0, cw1, cw2, b = cwj[0:1], cwj[1:2], cwj[2:3], cwj[3:4]
            gd, gu = _shift_rows(gate, jnp.where(first, 0.0, pg1), jnp.where(last, 0.0, ng1))
            g = gd * cw0 + gate * cw1 + gu * cw2 + b
            g_p = pg2 * cw0 + pg1 * cw1 + _row(gate, 0) * cw2 + b
            g_n = _row(gate, RB - 1) * cw0 + ng1 * cw1 + ng2 * cw2 + b
            dg = dav * val * dsilu(g)
            dg_p = _row(dap_ref[:, sg].astype(F32) * upp_ref[:, sv].astype(F32), HALO - 1) * dsilu(g_p)
            dg_n = _row(dan_ref[:, sg].astype(F32) * upn_ref[:, sv].astype(F32), 0) * dsilu(g_n)
            dgd, dgu = _shift_rows(dg, jnp.where(first, 0.0, dg_p), jnp.where(last, 0.0, dg_n))
            dup_ref[:, sg] = (cw0 * dgu + cw1 * dg + cw2 * dgd).astype(BF16)
            dup_ref[:, sv] = (dav * g * _sigmoid(g)).astype(BF16)
            rows = lax.broadcasted_iota(jnp.int32, (8, tc), 0)
            acc_ref[:, sg] += (_rows3(jnp.sum(dg * gd, axis=0, keepdims=True), jnp.sum(dg * gate, axis=0, keepdims=True),
                                      jnp.sum(dg * gu, axis=0, keepdims=True), tc)
                               + jnp.where(rows == 3, jnp.sum(dg, axis=0, keepdims=True), 0.0))

    return pl.pallas_call(
        body, name=name, grid=(nt,),
        in_specs=[pl.BlockSpec((RB, 2 * ff), lambda i: (i, 0)), prev, nxt,
                  pl.BlockSpec((RB, ff), lambda i: (i, 0)), dprev, dnxt, pl.BlockSpec((8, ff), lambda i: (0, 0))],
        out_specs=[pl.BlockSpec((RB, 2 * ff), lambda i: (i, 0)), pl.BlockSpec((8, ff), lambda i: (0, 0))],
        out_shape=[jax.ShapeDtypeStruct((t, 2 * ff), BF16), jax.ShapeDtypeStruct((8, ff), F32)],
        compiler_params=_params(("arbitrary",)))(up, up, up, da, da, da, cw)


def _rot(z):
    w = z.shape[1]
    lane = lax.broadcasted_iota(jnp.int32, z.shape, 1)
    return jnp.where((lane % 64) < 32, -pltpu.roll(z, w - 32, 1), pltpu.roll(z, 32, 1))


def rope_fwd(name, qkv, cos, sin, gains, dq, dkv):
    t, nqkv = qkv.shape
    nh, nkv = dq // HEAD, dkv // HEAD

    def body(qkv_ref, cos_ref, sin_ref, g_ref, qr_ref, kr_ref):
        cs, sn = cos_ref[...], sin_ref[...]
        for hd in range(nh + nkv):
            c0 = hd * HEAD
            xh = qkv_ref[:, c0:c0 + HEAD].astype(F32)
            r = lax.rsqrt(jnp.mean(xh * xh, axis=-1, keepdims=True) + EPS)
            y = xh * r * (g_ref[0:1, :] if hd < nh else g_ref[1:2, :])
            yr = (y * cs + _rot(y) * sn).astype(BF16)
            if hd < nh:
                qr_ref[:, c0:c0 + HEAD] = yr
            else:
                kr_ref[:, c0 - dq:c0 - dq + HEAD] = yr

    return pl.pallas_call(
        body, name=name, grid=(t // RB,),
        in_specs=[pl.BlockSpec((RB, nqkv), lambda i: (i, 0)), pl.BlockSpec((RB, HEAD), lambda i: (i, 0)),
                  pl.BlockSpec((RB, HEAD), lambda i: (i, 0)), pl.BlockSpec((8, HEAD), lambda i: (0, 0))],
        out_specs=[pl.BlockSpec((RB, dq), lambda i: (i, 0)), pl.BlockSpec((RB, dkv), lambda i: (i, 0))],
        out_shape=[jax.ShapeDtypeStruct((t, dq), BF16), jax.ShapeDtypeStruct((t, dkv), BF16)],
        compiler_params=_params(("parallel",)))(qkv, cos, sin, gains)


def rope_bwd(name, qkv, dqr, dkr, dv, cos, sin, gains):
    t, nqkv = qkv.shape
    dq, dkv = dqr.shape[1], dkr.shape[1]
    nh, nkv = dq // HEAD, dkv // HEAD

    def body(qkv_ref, dq_ref, dk_ref, dv_ref, cos_ref, sin_ref, g_ref, out_ref, dg_ref):
        @pl.when(pl.program_id(0) == 0)
        def _():
            dg_ref[...] = jnp.zeros_like(dg_ref)

        cs, sn = cos_ref[...], sin_ref[...]
        zero = jnp.zeros((1, HEAD), F32)
        gq, gk = zero, zero
        for hd in range(nh + nkv):
            c0 = hd * HEAD
            xh = qkv_ref[:, c0:c0 + HEAD].astype(F32)
            r = lax.rsqrt(jnp.mean(xh * xh, axis=-1, keepdims=True) + EPS)
            xhat = xh * r
            dy = dq_ref[:, c0:c0 + HEAD] if hd < nh else dk_ref[:, c0 - dq:c0 - dq + HEAD]
            tt = dy * cs - _rot(dy * sn)
            gsum = jnp.sum(tt * xhat, axis=0, keepdims=True)
            if hd < nh:
                gq = gq + gsum
            else:
                gk = gk + gsum
            dxh = tt * (g_ref[0:1, :] if hd < nh else g_ref[1:2, :])
            dx = r * (dxh - xhat * jnp.mean(dxh * xhat, axis=-1, keepdims=True))
            out_ref[:, c0:c0 + HEAD] = dx.astype(BF16)
        out_ref[:, dq + dkv:] = dv_ref[...].astype(BF16)
        dg_ref[...] += _rows3(gq, gk, zero, HEAD)

    return pl.pallas_call(
        body, name=name, grid=(t // RB,),
        in_specs=[pl.BlockSpec((RB, nqkv), lambda i: (i, 0)), pl.BlockSpec((RB, dq), lambda i: (i, 0)),
                  pl.BlockSpec((RB, dkv), lambda i: (i, 0)), pl.BlockSpec((RB, dkv), lambda i: (i, 0)),
                  pl.BlockSpec((RB, HEAD), lambda i: (i, 0)), pl.BlockSpec((RB, HEAD), lambda i: (i, 0)),
                  pl.BlockSpec((8, HEAD), lambda i: (0, 0))],
        out_specs=[pl.BlockSpec((RB, nqkv), lambda i: (i, 0)), pl.BlockSpec((8, HEAD), lambda i: (0, 0))],
        out_shape=[jax.ShapeDtypeStruct((t, nqkv), BF16), jax.ShapeDtypeStruct((8, HEAD), F32)],
        compiler_params=_params(("arbitrary",)))(qkv, dqr, dkr, dv, cos, sin, gains)


def resid_bwd(name, dx, dh, x, mod_n, sh, sc, y_prev=None, mod_g=None, gi=0):
    t, d = x.shape
    has_prev = y_prev is not None

    def body(*refs):
        if has_prev:
            dx_ref, dh_ref, x_ref, mn_ref, y_ref, mg_ref, dxo_ref, dy_ref, acc_ref = refs
        else:
            dx_ref, dh_ref, x_ref, mn_ref, dxo_ref, acc_ref = refs
        i = pl.program_id(0)
        seg = jnp.minimum(i, 1)

        @pl.when(i == 0)
        def _():
            acc_ref[...] = jnp.zeros_like(acc_ref)

        xv, dhv = x_ref[...], dh_ref[...]
        r = lax.rsqrt(jnp.mean(xv * xv, axis=-1, keepdims=True) + EPS)
        xhat = xv * r
        m = mn_ref[seg]
        dxh = dhv * (1.0 + m[sc:sc + 1, :])
        dxo = dx_ref[...] + r * (dxh - xhat * jnp.mean(dxh * xhat, axis=-1, keepdims=True))
        dxo_ref[...] = dxo
        s2 = jnp.zeros((1, d), F32)
        if has_prev:
            dy_ref[...] = (mg_ref[seg][gi:gi + 1, :] * dxo).astype(BF16)
            s2 = jnp.sum(dxo * y_ref[...].astype(F32), axis=0, keepdims=True)
        acc_ref[seg] = acc_ref[seg] + _rows3(jnp.sum(dhv, axis=0, keepdims=True),
                                             jnp.sum(dhv * xhat, axis=0, keepdims=True), s2, d)

    row = pl.BlockSpec((RB, d), lambda i: (i, 0))
    modspec = pl.BlockSpec((2, 8, d), lambda i: (0, 0, 0))
    in_specs, operands = [row, row, row, modspec], [dx, dh, x, mod_n]
    out_specs, out_shape = [row], [jax.ShapeDtypeStruct((t, d), F32)]
    if has_prev:
        in_specs += [row, modspec]
        operands += [y_prev, mod_g]
        out_specs.append(row)
        out_shape.append(jax.ShapeDtypeStruct((t, d), BF16))
    out_specs.append(modspec)
    out_shape.append(jax.ShapeDtypeStruct((2, 8, d), F32))
    return pl.pallas_call(body, name=name, grid=(t // RB,), in_specs=in_specs, out_specs=out_specs,
                          out_shape=out_shape, compiler_params=_params(("arbitrary",)))(*operands)


def loss_head(name, xf, target, y_last, mod, gi):
    t, d = xf.shape

    def body(x_ref, t_ref, y_ref, mod_ref, dx_ref, dy_ref, acc_ref, lp_ref):
        i = pl.program_id(0)
        seg = jnp.minimum(i, 1)

        @pl.when(i == 0)
        def _():
            acc_ref[...] = jnp.zeros_like(acc_ref)
            lp_ref[...] = jnp.zeros_like(lp_ref)

        lat = i >= 1
        err = jnp.where(lat, x_ref[...] - t_ref[...], 0.0)
        dxv = err / d
        dx_ref[...] = dxv
        dy_ref[...] = (mod_ref[seg][gi:gi + 1, :] * dxv).astype(BF16)
        zero = jnp.zeros((1, d), F32)
        lp_ref[...] += _rows3(jnp.sum(err * err, axis=0, keepdims=True), zero, zero, d)
        acc_ref[seg] = acc_ref[seg] + _rows3(zero, zero, jnp.sum(dxv * y_ref[...].astype(F32), axis=0, keepdims=True), d)

    row = pl.BlockSpec((RB, d), lambda i: (i, 0))
    modspec = pl.BlockSpec((2, 8, d), lambda i: (0, 0, 0))
    return pl.pallas_call(
        body, name=name, grid=(t // RB,),
        in_specs=[row, pl.BlockSpec((RB, d), lambda i: (jnp.maximum(i - 1, 0), 0)), row, modspec],
        out_specs=[row, row, modspec, pl.BlockSpec((8, d), lambda i: (0, 0))],
        out_shape=[jax.ShapeDtypeStruct((t, d), F32), jax.ShapeDtypeStruct((t, d), BF16),
                   jax.ShapeDtypeStruct((2, 8, d), F32), jax.ShapeDtypeStruct((8, d), F32)],
        compiler_params=_params(("arbitrary",)))(xf, target, y_last, mod)


def _kv_specs(width, colblk, nbk):
    return [pl.BlockSpec((CTX, width), lambda i: (0, colblk)),
            pl.BlockSpec((BLK, width), lambda i: (jnp.maximum(i - 1, 0), colblk)),
            pl.BlockSpec((BLK, width), lambda i: (i, colblk)),
            pl.BlockSpec((BLK, width), lambda i: (jnp.minimum(i + 1, nbk - 1), colblk))]


def _band_mask(i, seq):
    nk = CTX + 3 * BLK
    qrow = lax.broadcasted_iota(jnp.int32, (GROUP * BLK, nk), 0) % BLK
    col = lax.broadcasted_iota(jnp.int32, (GROUP * BLK, nk), 1)
    cb = col - CTX
    kpos = (i - 3) * BLK + cb
    band = (i >= 2) & (jnp.abs(BLK + qrow - cb) <= WINDOW) & (kpos >= 0) & (kpos < seq)
    return (col < CTX) | band


def _stack_heads(ref, h):
    return jnp.concatenate([ref[:, (h * GROUP + g) * HEAD:(h * GROUP + g + 1) * HEAD] for g in range(GROUP)], axis=0)


def _stack_cols(v, h):
    return jnp.concatenate([_get_col(v, h * GROUP + g) for g in range(GROUP)], axis=0)


def _sink_col(sink_ref, h):
    rowg = lax.broadcasted_iota(jnp.int32, (GROUP * BLK, 1), 0) // BLK
    sk = jnp.full((GROUP * BLK, 1), sink_ref[h * GROUP], F32)
    for g in range(1, GROUP):
        sk = jnp.where(rowg == g, sink_ref[h * GROUP + g], sk)
    return sk


def attn_fwd(name, qr, kr, qkv, sink, seq):
    t, dq = qr.shape
    dkv = kr.shape[1]
    nbk, nkv = t // BLK, dkv // HEAD
    vcol = (dq + dkv) // dkv
    scale = HEAD ** -0.5

    def body(sink_ref, q_ref, kc, kp, ko, kn, vc, vp, vo, vn, o_ref, lse_ref):
        i = pl.program_id(0)
        mask = _band_mask(i, seq)
        lse = jnp.zeros((BLK, LANE), F32)
        for h in range(nkv):
            hs = slice(h * HEAD, (h + 1) * HEAD)
            k = jnp.concatenate([kc[:, hs], kp[:, hs], ko[:, hs], kn[:, hs]], axis=0)
            v = jnp.concatenate([vc[:, hs], vp[:, hs], vo[:, hs], vn[:, hs]], axis=0)
            q4 = _stack_heads(q_ref, h)
            s = jnp.where(mask, lax.dot_general(q4, k, NT, preferred_element_type=F32) * scale, NEG)
            sk = _sink_col(sink_ref, h)
            m = jnp.maximum(jnp.max(s, axis=-1, keepdims=True), sk)
            e = jnp.exp(s - m)
            den = jnp.sum(e, axis=-1, keepdims=True) + jnp.exp(sk - m)
            o4 = jnp.dot((e / den).astype(BF16), v, preferred_element_type=F32)
            l4 = m + jnp.log(den)
            for g in range(GROUP):
                hg = h * GROUP + g
                o_ref[:, hg * HEAD:(hg + 1) * HEAD] = o4[g * BLK:(g + 1) * BLK].astype(BF16)
                lse = _put_col(lse, hg, l4[g * BLK:(g + 1) * BLK])
        lse_ref[...] = lse

    return pl.pallas_call(
        body, name=name, grid=(nbk,),
        in_specs=[pl.BlockSpec(memory_space=pltpu.SMEM), pl.BlockSpec((BLK, dq), lambda i: (i, 0))]
        + _kv_specs(dkv, 0, nbk) + _kv_specs(dkv, vcol, nbk),
        out_specs=[pl.BlockSpec((BLK, dq), lambda i: (i, 0)), pl.BlockSpec((BLK, LANE), lambda i: (i, 0))],
        out_shape=[jax.ShapeDtypeStruct((t, dq), BF16), jax.ShapeDtypeStruct((t, LANE), F32)],
        compiler_params=_params(("parallel",)))(sink, qr, kr, kr, kr, kr, qkv, qkv, qkv, qkv)


def attn_bwd_q(name, qr, kr, qkv, sink, do, o, lse, seq):
    t, dq = qr.shape
    dkv = kr.shape[1]
    nbk, nkv = t // BLK, dkv // HEAD
    vcol = (dq + dkv) // dkv
    scale = HEAD ** -0.5

    def body(sink_ref, q_ref, kc, kp, ko, kn, vc, vp, vo, vn, do_ref, o_ref, lse_ref,
             dq_ref, dl_ref, dkc_ref, dvc_ref, ds_ref):
        i = pl.program_id(0)

        @pl.when(i == 0)
        def _():
            dkc_ref[...] = jnp.zeros_like(dkc_ref)
            dvc_ref[...] = jnp.zeros_like(dvc_ref)
            ds_ref[...] = jnp.zeros_like(ds_ref)

        mask = _band_mask(i, seq)
        lse = lse_ref[...]
        delta = jnp.zeros((BLK, LANE), F32)
        dsink = jnp.zeros((8, LANE), F32)
        for h in range(nkv):
            hs = slice(h * HEAD, (h + 1) * HEAD)
            k = jnp.concatenate([kc[:, hs], kp[:, hs], ko[:, hs], kn[:, hs]], axis=0)
            v = jnp.concatenate([vc[:, hs], vp[:, hs], vo[:, hs], vn[:, hs]], axis=0)
            q4, do4 = _stack_heads(q_ref, h), _stack_heads(do_ref, h)
            d4 = jnp.sum(do4.astype(F32) * _stack_heads(o_ref, h).astype(F32), axis=-1, keepdims=True)
            l4 = _stack_cols(lse, h)
            s = jnp.where(mask, lax.dot_general(q4, k, NT, preferred_element_type=F32) * scale, NEG)
            p = jnp.exp(s - l4)
            dp = lax.dot_general(do4, v, NT, preferred_element_type=F32)
            dsb = (p * (dp - d4) * scale).astype(BF16)
            pb = p.astype(BF16)
            dq4 = jnp.dot(dsb, k, preferred_element_type=F32)
            dkc_ref[:, hs] += lax.dot_general(dsb[:, :CTX], q4, TN, preferred_element_type=F32)
            dvc_ref[:, hs] += lax.dot_general(pb[:, :CTX], do4, TN, preferred_element_type=F32)
            dsk = -jnp.exp(_sink_col(sink_ref, h) - l4) * d4
            for g in range(GROUP):
                hg = h * GROUP + g
                rs = slice(g * BLK, (g + 1) * BLK)
                dq_ref[:, hg * HEAD:(hg + 1) * HEAD] = dq4[rs]
                delta = _put_col(delta, hg, d4[rs])
                dsink = _put_col(dsink, hg, jnp.sum(dsk[rs], axis=0, keepdims=True))
        dl_ref[...] = delta
        rows = lax.broadcasted_iota(jnp.int32, (8, LANE), 0)
        ds_ref[...] += jnp.where(rows == 0, dsink, 0.0)

    blk = lambda w: pl.BlockSpec((BLK, w), lambda i: (i, 0))
    const = lambda r, w: pl.BlockSpec((r, w), lambda i: (0, 0))
    return pl.pallas_call(
        body, name=name, grid=(nbk,),
        in_specs=[pl.BlockSpec(memory_space=pltpu.SMEM), blk(dq)] + _kv_specs(dkv, 0, nbk) + _kv_specs(dkv, vcol, nbk)
        + [blk(dq), blk(dq), blk(LANE)],
        out_specs=[blk(dq), blk(LANE), const(CTX, dkv), const(CTX, dkv), const(8, LANE)],
        out_shape=[jax.ShapeDtypeStruct((t, dq), F32), jax.ShapeDtypeStruct((t, LANE), F32),
                   jax.ShapeDtypeStruct((CTX, dkv), F32), jax.ShapeDtypeStruct((CTX, dkv), F32),
                   jax.ShapeDtypeStruct((8, LANE), F32)],
        compiler_params=_params(("arbitrary",)))(sink, qr, kr, kr, kr, kr, qkv, qkv, qkv, qkv, do, o, lse)


def attn_bwd_kv(name, qr, kr, qkv, do, lse, delta, seq):
    t, dq = qr.shape
    dkv = kr.shape[1]
    nbk, nbl, nkv = t // BLK, seq // BLK, dkv // HEAD
    cb = CTX // BLK
    vcol = (dq + dkv) // dkv
    scale = HEAD ** -0.5

    def qspec(w, d):
        return pl.BlockSpec((BLK, w), lambda j: (jnp.clip(j + cb + d, cb, nbk - 1), 0))

    def body(k_ref, v_ref, *refs):
        dk_ref, dv_ref = refs[-2], refs[-1]
        j = pl.program_id(0)
        qrow = lax.broadcasted_iota(jnp.int32, (GROUP * BLK, BLK), 0) % BLK
        kcol = lax.broadcasted_iota(jnp.int32, (GROUP * BLK, BLK), 1)
        for h in range(nkv):
            hs = slice(h * HEAD, (h + 1) * HEAD)
            kh, vh = k_ref[:, hs], v_ref[:, hs]
            dk_h = jnp.zeros((BLK, HEAD), F32)
            dv_h = jnp.zeros((BLK, HEAD), F32)
            for di, d in enumerate((-1, 0, 1)):
                q_ref, do_ref, lse_ref, dl_ref = refs[4 * di:4 * di + 4]
                n = j + d
                msk = (n >= 0) & (n < nbl) & (jnp.abs(d * BLK + qrow - kcol) <= WINDOW)
                q4, do4 = _stack_heads(q_ref, h), _stack_heads(do_ref, h)
                l4, d4 = _stack_cols(lse_ref[...], h), _stack_cols(dl_ref[...], h)
                s = jnp.where(msk, lax.dot_general(q4, kh, NT, preferred_element_type=F32) * scale, NEG)
                p = jnp.exp(s - l4)
                dv_h += lax.dot_general(p.astype(BF16), do4, TN, preferred_element_type=F32)
                dp = lax.dot_general(do4, vh, NT, preferred_element_type=F32)
                dk_h += lax.dot_general((p * (dp - d4) * scale).astype(BF16), q4, TN, preferred_element_type=F32)
            dk_ref[:, hs] = dk_h
            dv_ref[:, hs] = dv_h

    in_specs = [pl.BlockSpec((BLK, dkv), lambda j: (j + cb, 0)), pl.BlockSpec((BLK, dkv), lambda j: (j + cb, vcol))]
    operands = [kr, qkv]
    for d in (-1, 0, 1):
        in_specs += [qspec(dq, d), qspec(dq, d), qspec(LANE, d), qspec(LANE, d)]
        operands += [qr, do, lse, delta]
    return pl.pallas_call(
        body, name=name, grid=(nbl,), in_specs=in_specs,
        out_specs=[pl.BlockSpec((BLK, dkv), lambda j: (j, 0)), pl.BlockSpec((BLK, dkv), lambda j: (j, 0))],
        out_shape=[jax.ShapeDtypeStruct((seq, dkv), F32), jax.ShapeDtypeStruct((seq, dkv), F32)],
        compiler_params=_params(("parallel",)))(*operands)


def ada_fwd(name, cond, w_ada):
    nl, d, n = w_ada.shape
    tn = _pick(n, 1024)

    def body(c_ref, w_ref, out_ref):
        cv = c_ref[...]
        out_ref[...] = jnp.dot((cv * _sigmoid(cv)).astype(BF16), w_ref[...].astype(BF16), preferred_element_type=F32)

    return pl.pallas_call(
        body, name=name, grid=(nl, n // tn),
        in_specs=[pl.BlockSpec((16, d), lambda l, j: (0, 0)), pl.BlockSpec((None, d, tn), lambda l, j: (l, 0, j))],
        out_specs=pl.BlockSpec((None, 16, tn), lambda l, j: (l, 0, j)),
        out_shape=jax.ShapeDtypeStruct((nl, 16, n), F32), compiler_params=_params(("parallel", "parallel")))(cond, w_ada)


def ada_bwd_cond(name, dsum, w_ada):
    nl, d, n = w_ada.shape
    tn = _pick(n, 1024)

    def body(g_ref, w_ref, out_ref):
        @pl.when((pl.program_id(0) == 0) & (pl.program_id(1) == 0))
        def _():
            out_ref[...] = jnp.zeros_like(out_ref)

        out_ref[...] += lax.dot_general(g_ref[...].astype(BF16), w_ref[...].astype(BF16), NT, preferred_element_type=F32)

    return pl.pallas_call(
        body, name=name, grid=(nl, n // tn),
        in_specs=[pl.BlockSpec((None, 8, tn), lambda l, j: (l, 0, j)), pl.BlockSpec((None, d, tn), lambda l, j: (l, 0, j))],
        out_specs=pl.BlockSpec((8, d), lambda l, j: (0, 0)),
        out_shape=jax.ShapeDtypeStruct((8, d), F32), compiler_params=_params(("arbitrary", "arbitrary")))(dsum, w_ada)


def ada_grad_w(name, cond, rhs):
    nl, _, n = rhs.shape
    d = cond.shape[1]
    tr, tn = _pick(d, 512), _pick(n, 1024)

    def body(c_ref, r_ref, out_ref):
        cv = c_ref[...]
        out_ref[...] = lax.dot_general((cv * _sigmoid(cv)).astype(BF16), r_ref[...].astype(BF16), TN, preferred_element_type=F32)

    return pl.pallas_call(
        body, name=name, grid=(nl, d // tr, n // tn),
        in_specs=[pl.BlockSpec((16, tr), lambda l, i, j: (0, i)), pl.BlockSpec((None, 16, tn), lambda l, i, j: (l, 0, j))],
        out_specs=pl.BlockSpec((None, tr, tn), lambda l, i, j: (l, i, j)),
        out_shape=jax.ShapeDtypeStruct((nl, d, n), F32),
        compiler_params=_params(("parallel", "parallel", "parallel")))(cond, rhs)


def adamw(name, g, g_spec, w, m, v, tr):
    nl, r, c = w.shape
    spec = pl.BlockSpec((None, tr, c), lambda l, i: (l, i, 0))

    def body(g_ref, w_ref, m_ref, v_ref, go_ref, d_ref, mo_ref, vo_ref):
        gv = g_ref[...]
        mn = B1 * m_ref[...] + (1.0 - B1) * gv
        vn = B2 * v_ref[...] + (1.0 - B2) * (gv * gv)
        m_hat = mn / (1.0 - B1 ** STEP)
        v_hat = vn / (1.0 - B2 ** STEP)
        go_ref[...] = gv
        d_ref[...] = -LR * (m_hat / (jnp.sqrt(v_hat) + ADAM_EPS) + WD * w_ref[...])
        mo_ref[...] = mn
        vo_ref[...] = vn

    return pl.pallas_call(
        body, name=name, grid=(nl, r // tr), in_specs=[g_spec, spec, spec, spec], out_specs=[spec] * 4,
        out_shape=[jax.ShapeDtypeStruct(w.shape, F32)] * 4, compiler_params=_params(("parallel", "parallel")))(g, w, m, v)


def adamw_small(name, g, w, m, v):
    shape = w.shape
    r3 = lambda a: a.reshape(1, -1, shape[-1]).astype(F32)
    rows = r3(w).shape[1]
    outs = adamw(name, r3(g), pl.BlockSpec((None, rows, shape[-1]), lambda l, i: (l, i, 0)), r3(w), r3(m), r3(v), rows)
    return [o.reshape(shape) for o in outs]


def _place():
    x, y, c = lax.axis_index("x"), lax.axis_index("y"), lax.axis_index("c")
    return x, y, c, [(1 - x, y), (x, 1 - y), (1 - x, 1 - y)]


def small_allgather(name, v):
    r, w = v.shape

    def body(x_ref, out_ref, send_sems, recv_sems, local_sem):
        x, y, c, chips = _place()
        me, sibling = (x, y, c), (x, y, 1 - c)

        def slot(px, py, pc):
            return out_ref.at[4 * px + 2 * py + pc]

        def copy(k, block, to, src=None):
            return pltpu.make_async_remote_copy(
                src_ref=slot(*block) if src is None else src, dst_ref=slot(*block),
                send_sem=send_sems.at[k], recv_sem=recv_sems.at[k], device_id=to, device_id_type=MESH)

        mine = pltpu.make_async_copy(x_ref, slot(*me), local_sem)
        mine.start()
        first = [copy(0, me, sibling, src=x_ref)]
        first += [copy(1 + j, me, (*chip, c), src=x_ref) for j, chip in enumerate(chips)]
        for cp in first:
            cp.start()
        passed = [copy(4 + j, (*chip, c), sibling) for j, chip in enumerate(chips)]
        for j, chip in enumerate(chips):
            copy(1 + j, (*chip, c), me).wait_recv()
            passed[j].start()
        copy(0, sibling, me).wait_recv()
        for j, chip in enumerate(chips):
            copy(4 + j, (*chip, 1 - c), me).wait_recv()
        for cp in first + passed:
            cp.wait_send()
        mine.wait()

    return pl.pallas_call(
        body, name=name, out_shape=jax.ShapeDtypeStruct((8, r, w), v.dtype),
        in_specs=[pl.BlockSpec(memory_space=pltpu.VMEM)], out_specs=pl.BlockSpec(memory_space=pltpu.VMEM),
        scratch_shapes=[pltpu.SemaphoreType.DMA((7,)), pltpu.SemaphoreType.DMA((7,)), pltpu.SemaphoreType.DMA],
        compiler_params=pltpu.CompilerParams(vmem_limit_bytes=VMEM_LIMIT))(v)


def gather_flat(name, parts):
    flat = jnp.concatenate([p.reshape(-1).astype(F32) for p in parts])
    n = flat.shape[0]
    rows = _cdiv(n, 8 * LANE) * 8
    flat = jnp.pad(flat, (0, rows * LANE - n))
    return small_allgather(name, flat.reshape(rows, LANE)).reshape(8, rows * LANE)


def sum8(name, g):
    p = g.shape[1]
    g3 = g.reshape(8, p // LANE, LANE)
    tr = _pick(p // LANE, 1024, 8)

    def body(g_ref, out_ref):
        acc = g_ref[0]
        for k in range(1, 8):
            acc = acc + g_ref[k]
        out_ref[...] = acc

    return pl.pallas_call(
        body, name=name, grid=(p // LANE // tr,),
        in_specs=[pl.BlockSpec((8, tr, LANE), lambda i: (0, i, 0))], out_specs=pl.BlockSpec((tr, LANE), lambda i: (i, 0)),
        out_shape=jax.ShapeDtypeStruct((p // LANE, LANE), F32), compiler_params=_params(("parallel",)))(g3).reshape(p)


HBM_SPEC = pl.BlockSpec(memory_space=pltpu.HBM)


def _half(ref, lead, c, axis):
    h = ref.shape[axis] // 2
    return ref.at[lead, pl.ds(c * h, h), :] if axis == 1 else ref.at[lead, :, pl.ds(c * h, h)]


def _gather_ops(outs, axes, send_sems, recv_sems):
    x, y, c, chips = _place()
    me, sibling = (x, y, c), (x, y, 1 - c)

    def copy(a, k, chip, pc, to):
        blk = _half(outs[a], 2 * chip[0] + chip[1], pc, axes[a])
        return pltpu.make_async_remote_copy(src_ref=blk, dst_ref=blk, send_sem=send_sems.at[6 * a + k],
                                            recv_sem=recv_sems.at[6 * a + k], device_id=to, device_id_type=MESH)

    def start():
        for a in range(len(outs)):
            for j, chip in enumerate(chips):
                copy(a, j, (x, y), c, (*chip, c)).start()

    def finish():
        for a in range(len(outs)):
            for j, chip in enumerate(chips):
                copy(a, j, chip, c, me).wait_recv()
                copy(a, 3 + j, chip, c, sibling).start()
        for a in range(len(outs)):
            for j, chip in enumerate(chips):
                copy(a, 3 + j, chip, 1 - c, me).wait_recv()
            for j, chip in enumerate(chips):
                copy(a, j, (x, y), c, (*chip, c)).wait_send()
                copy(a, 3 + j, chip, c, sibling).wait_send()

    return start, finish


def gather_comm(bufs, axes):
    return dict(ins=list(bufs), out_shape=[jax.ShapeDtypeStruct(b.shape, b.dtype) for b in bufs],
                aliases={a: a for a in range(len(bufs))}, n_sems=6 * len(bufs),
                ops=lambda cin, couts, ss, rs: _gather_ops(couts, axes, ss, rs))


def gather_weights(name, bufs, axes):
    n = len(bufs)

    def body(*refs):
        start, finish = _gather_ops(refs[n:2 * n], axes, refs[2 * n], refs[2 * n + 1])
        start()
        finish()

    return pl.pallas_call(
        body, name=name, out_shape=[jax.ShapeDtypeStruct(b.shape, b.dtype) for b in bufs],
        in_specs=[HBM_SPEC] * n, out_specs=[HBM_SPEC] * n, input_output_aliases={a: a for a in range(n)},
        scratch_shapes=[pltpu.SemaphoreType.DMA((6 * n,)), pltpu.SemaphoreType.DMA((6 * n,))])(*bufs)


def _scatter_ops(ins, outs, send_sems, recv_sems):
    x, y, c, chips = _place()
    me = 2 * x + y

    def copy(a, j, chip):
        return pltpu.make_async_remote_copy(
            src_ref=ins[a].at[2 * chip[0] + chip[1]], dst_ref=outs[a].at[me], send_sem=send_sems.at[3 * a + j],
            recv_sem=recv_sems.at[3 * a + j], device_id=(*chip, c), device_id_type=MESH)

    def start():
        for a in range(len(ins)):
            for j, chip in enumerate(chips):
                copy(a, j, chip).start()

    def finish():
        for a in range(len(ins)):
            for j, chip in enumerate(chips):
                copy(a, j, chip).wait()

    return start, finish


def scatter_comm(bufs):
    return dict(ins=list(bufs), out_shape=[jax.ShapeDtypeStruct(b.shape, b.dtype) for b in bufs], aliases={},
                n_sems=3 * len(bufs), ops=_scatter_ops)


def chip_scatter(name, bufs):
    n = len(bufs)

    def body(*refs):
        start, finish = _scatter_ops(refs[:n], refs[n:2 * n], refs[2 * n], refs[2 * n + 1])
        start()
        finish()

    return pl.pallas_call(
        body, name=name, out_shape=[jax.ShapeDtypeStruct(b.shape, b.dtype) for b in bufs],
        in_specs=[HBM_SPEC] * n, out_specs=[HBM_SPEC] * n,
        scratch_shapes=[pltpu.SemaphoreType.DMA((3 * n,)), pltpu.SemaphoreType.DMA((3 * n,))])(*bufs)


def pair_exchange(name, bufs, axes):
    n = len(bufs)

    def body(*refs):
        ins, outs, (send_sems, recv_sems) = refs[:n], refs[n:2 * n], refs[2 * n:]
        x, y, c, _ = _place()
        cps = []
        for a, (src, out) in enumerate(zip(ins, outs)):
            cp = pltpu.make_async_remote_copy(
                src_ref=_half(src, slice(None), 1 - c, axes[a]), dst_ref=out, send_sem=send_sems.at[a],
                recv_sem=recv_sems.at[a], device_id=(x, y, 1 - c), device_id_type=MESH)
            cp.start()
            cps.append(cp)
        for cp in cps:
            cp.wait()

    def halved(b, axis):
        shape = list(b.shape)
        shape[axis] //= 2
        return jax.ShapeDtypeStruct(tuple(shape), b.dtype)

    return pl.pallas_call(
        body, name=name, out_shape=[halved(b, ax) for b, ax in zip(bufs, axes)],
        in_specs=[HBM_SPEC] * n, out_specs=[HBM_SPEC] * n,
        scratch_shapes=[pltpu.SemaphoreType.DMA((n,)), pltpu.SemaphoreType.DMA((n,))])(*bufs)


def pair_add(name, buf, got, cidx, axis):
    s, r, c = got.shape
    tr = _pick(r, max(HALO, (4 * 1024 * 1024) // (2 * c)), HALO)
    per = r // tr
    if axis == 1:
        mine = pl.BlockSpec((None, tr, c), lambda k, i, cr: (k, cr[0] * per + i, 0))
    else:
        mine = pl.BlockSpec((None, tr, c), lambda k, i, cr: (k, i, cr[0]))

    def body(c_ref, a_ref, b_ref, out_ref):
        out_ref[...] = (a_ref[...].astype(F32) + b_ref[...].astype(F32)).astype(BF16)

    return pl.pallas_call(
        body, name=name,
        grid_spec=pltpu.PrefetchScalarGridSpec(
            num_scalar_prefetch=1, grid=(s, per),
            in_specs=[mine, pl.BlockSpec((None, tr, c), lambda k, i, cr: (k, i, 0))],
            out_specs=pl.BlockSpec((None, tr, c), lambda k, i, cr: (k, i, 0))),
        out_shape=jax.ShapeDtypeStruct((s, r, c), BF16),
        compiler_params=_params(("parallel", "parallel")))(cidx, buf, got)


def chip_add(name, own, got, place, dst, l, off, size, col):
    s = got.shape[0]
    if col:
        h, n = got.shape[1], size
        tr = _pick(h, max(HALO, (2 * 1024 * 1024) // (2 * n)), HALO)
        per, ob = h // tr, off // n
        own_spec = pl.BlockSpec((None, tr, n), lambda i, p: (p[0], i, ob))
        got_spec = pl.BlockSpec((s, tr, n), lambda i, p: (0, i, ob))
        out_spec = pl.BlockSpec((None, tr, n), lambda i, p: (l, p[1] * per + i, 0))
        grid = (per,)
    else:
        n = got.shape[2]
        tr = _pick(size, max(HALO, (2 * 1024 * 1024) // (2 * n)), HALO)
        ob = off // tr
        own_spec = pl.BlockSpec((None, tr, n), lambda i, p: (p[0], ob + i, 0))
        got_spec = pl.BlockSpec((s, tr, n), lambda i, p: (0, ob + i, 0))
        out_spec = pl.BlockSpec((None, tr, n), lambda i, p: (l, i, p[1]))
        grid = (size // tr,)

    def body(p_ref, own_ref, g_ref, dst_ref, out_ref):
        acc = jnp.zeros((tr, n), F32)
        for k in range(s):
            acc = acc + jnp.where(p_ref[0] == k, own_ref[...], g_ref[k]).astype(F32)
        out_ref[...] = acc

    return pl.pallas_call(
        body, name=name,
        grid_spec=pltpu.PrefetchScalarGridSpec(
            num_scalar_prefetch=1, grid=grid,
            in_specs=[own_spec, got_spec, pl.BlockSpec(memory_space=pl.ANY)], out_specs=out_spec),
        out_shape=jax.ShapeDtypeStruct(dst.shape, F32), input_output_aliases={3: 0},
        compiler_params=_params(("parallel",)))(place, own, got, dst)


def pair_join(name, bufs, axes):
    n = len(bufs)

    def body(*refs):
        outs = refs[n:2 * n]
        send_sems, recv_sems = refs[2 * n:]
        x, y, c, _ = _place()
        started = []
        for a, out in enumerate(outs):
            blk = _half(out, slice(None), c, axes[a])
            cp = pltpu.make_async_remote_copy(src_ref=blk, dst_ref=blk, send_sem=send_sems.at[a], recv_sem=recv_sems.at[a],
                                              device_id=(x, y, 1 - c), device_id_type=MESH)
            cp.start()
            started.append(cp)
        for cp in started:
            cp.wait()

    return pl.pallas_call(
        body, name=name, out_shape=[jax.ShapeDtypeStruct(b.shape, b.dtype) for b in bufs],
        in_specs=[HBM_SPEC] * n, out_specs=[HBM_SPEC] * n, input_output_aliases={a: a for a in range(n)},
        scratch_shapes=[pltpu.SemaphoreType.DMA((n,)), pltpu.SemaphoreType.DMA((n,))])(*bufs)


def _rope_tables(seq):
    rows = seq // GRID_W
    row = jnp.repeat(jnp.arange(rows), GRID_W).astype(F32)
    col = jnp.tile(jnp.arange(GRID_W), rows).astype(F32)
    pairs = HEAD // 4
    inv = ROPE_BASE ** (-jnp.arange(pairs, dtype=F32) / pairs)
    ang = jnp.stack([row[:, None] * inv, col[:, None] * inv], axis=1)
    ang = jnp.broadcast_to(ang[:, :, None, :], (seq, 2, 2, pairs)).reshape(seq, HEAD)
    cos = jnp.concatenate([jnp.ones((CTX, HEAD), F32), jnp.cos(ang)], axis=0)
    sin = jnp.concatenate([jnp.zeros((CTX, HEAD), F32), jnp.sin(ang)], axis=0)
    return cos, sin


def _pad8(a):
    return jnp.pad(a, ((0, 8 - a.shape[0]), (0, 0)))


def kernel(x, c, ctx, c_ctx, w_ada, b_ada, attn_w_qkv, attn_w_o, attn_q_gain, attn_k_gain, attn_sink, sc_w_in, sc_conv, sc_w_out, ffn_w_up, ffn_conv, ffn_conv_b, ffn_w_down, loss_target, m_c_ctx, m_w_ada, m_b_ada, m_attn_w_qkv, m_attn_w_o, m_attn_q_gain, m_attn_k_gain, m_attn_sink, m_sc_w_in, m_sc_conv, m_sc_w_out, m_ffn_w_up, m_ffn_conv, m_ffn_conv_b, m_ffn_w_down, v_c_ctx, v_w_ada, v_b_ada, v_attn_w_qkv, v_attn_w_o, v_attn_q_gain, v_attn_k_gain, v_attn_sink, v_sc_w_in, v_sc_conv, v_sc_w_out, v_ffn_w_up, v_ffn_conv, v_ffn_conv_b, v_ffn_w_down):
    seq, d = x.shape[1], x.shape[2]
    depth, nada = w_ada.shape[0], w_ada.shape[2]
    n_attn, n_conv = attn_w_qkv.shape[0], sc_w_in.shape[0]
    ff = ffn_conv_b.shape[1]
    dq, dkv = d, d // GROUP
    nqkv = dq + 2 * dkv
    nh = dq // HEAD
    assert ctx.shape[1] == CTX and seq % RB == 0 and 6 * d == 4 * nada
    lay = Layout(d, ff, nqkv, depth)
    ax, ay, ac = lax.axis_index("x"), lax.axis_index("y"), lax.axis_index("c")
    chip, dev = 2 * ax + ay, 4 * ax + 2 * ay + ac
    cidx = jnp.reshape(ac, (1,)).astype(jnp.int32)

    chip1 = jnp.reshape(chip, (1,)).astype(jnp.int32)
    wcs, wrs = [], []
    for l in range(depth):
        j, is_attn = l // N_MIX, l % N_MIX == 0
        wc_l, wr_l = lax.empty((4, d, lay.ct[l]), BF16), lax.empty((4, lay.rt, d), BF16)
        wc_l = cast_pack(f"pack_up_{l}", ffn_w_up, l, wc_l, 0, True, chip1)
        wc_l = cast_pack(f"pack_mix_{l}", attn_w_qkv if is_attn else sc_w_in, j, wc_l, lay.mix[l], True, chip1)
        wr_l = cast_pack(f"pack_down_{l}", ffn_w_down, l, wr_l, 0, False, chip1)
        wr_l = cast_pack(f"pack_out_{l}", attn_w_o if is_attn else sc_w_out, j, wr_l, lay.out, False, chip1)
        wcs.append(wc_l)
        wrs.append(wr_l)
    wcs[0], wrs[0] = gather_weights("gather_w0", [wcs[0], wrs[0]], (1, 2))

    g1 = gather_flat("gather_cond", [c, sc_conv, ffn_conv])
    c_all = g1[:, :d]
    o1 = d + sc_conv.size
    sc_conv_full = jnp.concatenate([g1[2 * s, d:o1].reshape(sc_conv.shape) for s in range(4)], axis=-1)
    ffn_conv_full = jnp.concatenate([g1[2 * s, o1:o1 + ffn_conv.size].reshape(ffn_conv.shape) for s in range(4)], axis=-1)
    cond = jnp.concatenate([c_all, c_ctx[None, :], jnp.zeros((7, d), F32)], axis=0)
    ada_part = ada_fwd("ada_fwd", cond, w_ada)
    g2 = gather_flat("gather_ada", [ada_part])
    ada_all = jnp.concatenate([g2[2 * s, :ada_part.size].reshape(ada_part.shape) for s in range(4)], axis=-1)
    ada_own = jnp.stack([lax.dynamic_index_in_dim(ada_all, 8, 1, False),
                         lax.dynamic_index_in_dim(ada_all, dev, 1, False)], axis=1) + b_ada[:, None, :]
    mods = jnp.pad(ada_own.reshape(depth, 2, 6, d), ((0, 0), (0, 0), (0, 2), (0, 0)))

    cos, sin = _rope_tables(seq)
    xa = jnp.concatenate([ctx[0], x[0]], axis=0)
    cws = [_pad8(sc_conv_full[j]) for j in range(n_conv)]
    cwf = [_pad8(jnp.concatenate([ffn_conv_full[l], ffn_conv_b[l][None, :]], axis=0)) for l in range(depth)]
    gains = [_pad8(jnp.stack([attn_q_gain[j], attn_k_gain[j]])) for j in range(n_attn)]

    saved = []
    for l in range(depth):
        j, is_attn, mod = l // N_MIX, l % N_MIX == 0, mods[l]
        wc, wr, last = wcs[l], wrs[l], l == depth - 1
        sv = {"x_in": xa}
        h1 = norm_mod(f"norm1_{l}", xa, mod, 0, 1)
        if is_attn:
            qkv = mm_nn_col(f"qkv_{l}", h1, wc, lay.mix[l], lay.nb_mix[l])
            qr, kr = rope_fwd(f"rope_{l}", qkv, cos, sin, gains[j], dq, dkv)
            o, lse = attn_fwd(f"attn_{l}", qr, kr, qkv, attn_sink[j], seq)
            xa, y_m = mm_nn_row(f"wo_{l}", o, wr, lay.out, lay.kb_o, xa, mod, 2)
            sv.update(qkv=qkv, qr=qr, kr=kr, o=o, lse=lse)
        else:
            u = mm_nn_col(f"scin_{l}", h1, wc, lay.mix[l], lay.nb_mix[l])
            z = sc_gate_fwd(f"scgate_{l}", u, cws[j])
            xa, y_m = mm_nn_row(f"scout_{l}", z, wr, lay.out, lay.kb_o, xa, mod, 2)
            sv.update(u=u, z=z)
        h2 = norm_mod(f"norm2_{l}", xa, mod, 3, 4)
        if last:
            up = mm_nn_col(f"up_{l}", h2, wc, 0, lay.nb_up)
        else:
            up, (wcs[l + 1],) = mm_nn_col(f"up_{l}", h2, wc, 0, lay.nb_up, comm=gather_comm([wcs[l + 1]], (1,)))
        act = ffn_act_fwd(f"act_{l}", up, cwf[l])
        sv.update(h1=h1, y_m=y_m, x_mid=xa, h2=h2, up=up, act=act)
        if last:
            xa, y_f = mm_nn_row(f"down_{l}", act, wr, 0, lay.kb_dn, xa, mod, 5)
        else:
            (xa, y_f), (wrs[l + 1],) = mm_nn_row(f"down_{l}", act, wr, 0, lay.kb_dn, xa, mod, 5,
                                                 comm=gather_comm([wrs[l + 1]], (2,)))
        sv["y_f"] = y_f
        saved.append(sv)

    dx, dy, acc, lp = loss_head("loss", xa, loss_target[0], saved[-1]["y_f"], mods[-1], 5)
    loss = lax.psum(0.5 * jnp.sum(lp[0]) / d, ("x", "y", "c"))
    d_mod = [jnp.zeros((2, 6, d), F32) for _ in range(depth)]
    place = jnp.stack([chip, ac]).astype(jnp.int32)
    gf = {"up": lax.empty(ffn_w_up.shape, F32), "down": lax.empty(ffn_w_down.shape, F32),
          "qkv": lax.empty(attn_w_qkv.shape, F32), "wo": lax.empty(attn_w_o.shape, F32),
          "scin": lax.empty(sc_w_in.shape, F32), "scout": lax.empty(sc_w_out.shape, F32)}

    def chip_adds(l, hc, hr, rb_c, rb_r):
        j, mix = l // N_MIX, ("qkv", "wo") if l % N_MIX == 0 else ("scin", "scout")
        gf["up"] = chip_add(f"sum_up_{l}", hc, rb_c, place, gf["up"], l, 0, lay.nb_up, True)
        gf[mix[0]] = chip_add(f"sum_mix_{l}", hc, rb_c, place, gf[mix[0]], j, lay.mix[l], lay.nb_mix[l], True)
        gf["down"] = chip_add(f"sum_down_{l}", hr, rb_r, place, gf["down"], l, 0, lay.kb_dn, False)
        gf[mix[1]] = chip_add(f"sum_out_{l}", hr, rb_r, place, gf[mix[1]], j, lay.out, lay.kb_o, False)

    pending = None

    def add_mod(l, acc, idx):
        upd = jnp.zeros((2, 6, d), F32)
        for row, k in idx:
            upd = upd.at[:, k, :].set(acc[:, row, :])
        d_mod[l] = d_mod[l] + upd

    add_mod(depth - 1, acc, [(2, 5)])
    d_conv_f, d_conv_s = [None] * depth, [None] * n_conv
    d_gq, d_gk, d_sink = [None] * n_attn, [None] * n_attn, [None] * n_attn
    for l in reversed(range(depth)):
        j, is_attn, sv = l // N_MIX, l % N_MIX == 0, saved[l]
        wc, wr = wcs[l], wrs[l]
        gc, gr = lax.empty((4, d, lay.ct[l]), BF16), lax.empty((4, lay.rt, d), BF16)
        if pending is None:
            gr = mm_tn_row(f"g_down_{l}", sv["act"], dy, gr, 0, lay.kb_dn)
        else:
            gr, (rb_r,) = mm_tn_row(f"g_down_{l}", sv["act"], dy, gr, 0, lay.kb_dn, comm=scatter_comm([pending[2]]))
        da = mm_nt_row(f"d_act_{l}", dy, wr, 0, lay.kb_dn)
        d_up, d_conv_f[l] = ffn_act_bwd(f"act_bwd_{l}", sv["up"], da, cwf[l])
        if pending is None:
            gc = mm_tn_col(f"g_up_{l}", sv["h2"], d_up, gc, 0, lay.nb_up)
        else:
            gc, (rb_c,) = mm_tn_col(f"g_up_{l}", sv["h2"], d_up, gc, 0, lay.nb_up, comm=scatter_comm([pending[1]]))
            chip_adds(pending[0], pending[1], pending[2], rb_c, rb_r)
        dh2 = mm_nt_col(f"d_h2_{l}", d_up, wc, 0, lay.nb_up)
        dx, dy, acc = resid_bwd(f"norm2_bwd_{l}", dx, dh2, sv["x_mid"], mods[l], 3, 4, sv["y_m"], mods[l], 2)
        add_mod(l, acc, [(0, 3), (1, 4), (2, 2)])
        if is_attn:
            gr = mm_tn_row(f"g_wo_{l}", sv["o"], dy, gr, lay.out, lay.kb_o)
            do = mm_nt_row(f"d_o_{l}", dy, wr, lay.out, lay.kb_o)
            dqr, delta, dkc, dvc, dsk = attn_bwd_q(f"attn_bwd_q_{l}", sv["qr"], sv["kr"], sv["qkv"], attn_sink[j],
                                                   do, sv["o"], sv["lse"], seq)
            dkl, dvl = attn_bwd_kv(f"attn_bwd_kv_{l}", sv["qr"], sv["kr"], sv["qkv"], do, sv["lse"], delta, seq)
            dqkv, dgn = rope_bwd(f"rope_bwd_{l}", sv["qkv"], dqr, jnp.concatenate([dkc, dkl], axis=0),
                                 jnp.concatenate([dvc, dvl], axis=0), cos, sin, gains[j])
            d_gq[j], d_gk[j], d_sink[j] = dgn[0], dgn[1], dsk[0, :nh]
            gc = mm_tn_col(f"g_qkv_{l}", sv["h1"], dqkv, gc, lay.mix[l], lay.nb_mix[l])
            dh1 = mm_nt_col(f"d_h1_{l}", dqkv, wc, lay.mix[l], lay.nb_mix[l])
        else:
            gr = mm_tn_row(f"g_scout_{l}", sv["z"], dy, gr, lay.out, lay.kb_o)
            dz = mm_nt_row(f"d_z_{l}", dy, wr, lay.out, lay.kb_o)
            du, dcw = sc_gate_bwd(f"scgate_bwd_{l}", sv["u"], dz, cws[j])
            d_conv_s[j] = dcw[:3]
            gc = mm_tn_col(f"g_scin_{l}", sv["h1"], du, gc, lay.mix[l], lay.nb_mix[l])
            dh1 = mm_nt_col(f"d_h1_{l}", du, wc, lay.mix[l], lay.nb_mix[l])
        if l > 0:
            dx, dy, acc = resid_bwd(f"norm1_bwd_{l}", dx, dh1, sv["x_in"], mods[l], 0, 1, saved[l - 1]["y_f"], mods[l - 1], 5)
            add_mod(l - 1, acc, [(2, 5)])
        else:
            dx, acc = resid_bwd(f"norm1_bwd_{l}", dx, dh1, sv["x_in"], mods[l], 0, 1)
        add_mod(l, acc, [(0, 0), (1, 1)])
        ra_c, ra_r = pair_exchange(f"pair_exchange_{l}", [gc, gr], (1, 2))
        pending = (l, pair_add(f"pair_add_c_{l}", gc, ra_c, cidx, 1), pair_add(f"pair_add_r_{l}", gr, ra_r, cidx, 2))
    grad_x = dx[CTX:][None]
    rb_c, rb_r = chip_scatter("chip_scatter_0", [pending[1], pending[2]])
    chip_adds(pending[0], pending[1], pending[2], rb_c, rb_r)
    order = ["up", "down", "qkv", "wo", "scin", "scout"]
    joined = pair_join("pair_join", [gf[k] for k in order], (1, 2, 1, 2, 1, 2))
    gf = dict(zip(order, joined))

    d_ada = jnp.stack(d_mod).reshape(depth, 2, 6 * d)
    small = [d_ada, jnp.stack(d_gq), jnp.stack(d_gk), jnp.stack(d_sink), jnp.stack(d_conv_s),
             jnp.stack([t[:3] for t in d_conv_f]), jnp.stack([t[3] for t in d_conv_f])]
    g3 = gather_flat("gather_small", small)
    tot = sum8("sum_small", g3)
    sizes = [s.size for s in small]
    offs = [sum(sizes[:k]) for k in range(len(sizes) + 1)]
    part = lambda k: tot[offs[k]:offs[k + 1]].reshape(small[k].shape)
    g_b_ada = part(0)[:, 0] + part(0)[:, 1]
    g_q_gain, g_k_gain, g_sink = part(1), part(2), part(3)
    g_sc_conv = lax.dynamic_slice_in_dim(part(4), chip * sc_conv.shape[2], sc_conv.shape[2], 2)
    g_ffn_conv = lax.dynamic_slice_in_dim(part(5), chip * ffn_conv.shape[2], ffn_conv.shape[2], 2)
    g_conv_b = part(6)

    d_ada_all = g3[:, :d_ada.size].reshape(8, depth, 2, 6 * d)
    cols = lambda a: lax.dynamic_slice_in_dim(a, chip * nada, nada, a.ndim - 1)
    d_lat = cols(jnp.moveaxis(d_ada_all[:, :, 1], 0, 1))
    d_ctx = cols(part(0)[:, 0])
    rhs = jnp.concatenate([d_lat, d_ctx[:, None], jnp.zeros((depth, 7, nada), F32)], axis=1)
    g_w_ada = ada_grad_w("ada_grad_w", cond, rhs)
    dcc = ada_bwd_cond("ada_bwd_cond", jnp.pad(d_ctx[:, None], ((0, 0), (0, 7), (0, 0))), w_ada)[0]
    g4 = gather_flat("gather_dcc", [dcc])
    d_silu = g4[0, :d] + g4[2, :d] + g4[4, :d] + g4[6, :d]
    sg = _sigmoid(c_ctx)
    g_c_ctx = d_silu * (sg * (1.0 + c_ctx * (1.0 - sg)))

    def adam_rows(k, n):
        return _pick(k, max(8, ADAM_TILE_ELEMS // n), 8)

    def big(name, g, w, m, v):
        _, k, n = w.shape
        tr = adam_rows(k, n)
        return adamw(name, g, pl.BlockSpec((None, tr, n), lambda l, i: (l, i, 0)), w, m, v, tr)

    ada_tr = adam_rows(d, nada)
    res = {
        "c_ctx": adamw_small("adam_c_ctx", g_c_ctx, c_ctx, m_c_ctx, v_c_ctx),
        "w_ada": adamw("adam_w_ada", g_w_ada, pl.BlockSpec((None, ada_tr, nada), lambda l, i: (l, i, 0)), w_ada, m_w_ada, v_w_ada, ada_tr),
        "b_ada": adamw_small("adam_b_ada", g_b_ada, b_ada, m_b_ada, v_b_ada),
        "attn_w_qkv": big("adam_qkv", gf["qkv"], attn_w_qkv, m_attn_w_qkv, v_attn_w_qkv),
        "attn_w_o": big("adam_wo", gf["wo"], attn_w_o, m_attn_w_o, v_attn_w_o),
        "attn_q_gain": adamw_small("adam_q_gain", g_q_gain, attn_q_gain, m_attn_q_gain, v_attn_q_gain),
        "attn_k_gain": adamw_small("adam_k_gain", g_k_gain, attn_k_gain, m_attn_k_gain, v_attn_k_gain),
        "attn_sink": adamw_small("adam_sink", g_sink, attn_sink, m_attn_sink, v_attn_sink),
        "sc_w_in": big("adam_scin", gf["scin"], sc_w_in, m_sc_w_in, v_sc_w_in),
        "sc_conv": adamw_small("adam_sc_conv", g_sc_conv, sc_conv, m_sc_conv, v_sc_conv),
        "sc_w_out": big("adam_scout", gf["scout"], sc_w_out, m_sc_w_out, v_sc_w_out),
        "ffn_w_up": big("adam_up", gf["up"], ffn_w_up, m_ffn_w_up, v_ffn_w_up),
        "ffn_conv": adamw_small("adam_ffn_conv", g_ffn_conv, ffn_conv, m_ffn_conv, v_ffn_conv),
        "ffn_conv_b": adamw_small("adam_conv_b", g_conv_b, ffn_conv_b, m_ffn_conv_b, v_ffn_conv_b),
        "ffn_w_down": big("adam_down", gf["down"], ffn_w_down, m_ffn_w_down, v_ffn_w_down),
    }
    names = list(res)
    return (loss, grad_x, *[res[n][0] for n in names], *[res[n][1] for n in names],
            *[res[n][2] for n in names], *[res[n][3] for n in names])
```

```python
import functools

import jax
import jax.numpy as jnp
from jax import lax
from jax.experimental import pallas as pl
from jax.experimental.pallas import tpu as pltpu

F32, BF16 = jnp.float32, jnp.bfloat16
MESH = pl.DeviceIdType.MESH
VMEM_LIMIT = 56 * 1024 * 1024
LANE = 128
MXU = 256
WIDE = 2816
HALO = 16
HEAD = 128
GROUP = 4
CTX = 256
BLK = 128
WINDOW = 128
RB = 256
GRID_W = 64
ROPE_BASE = 10000.0
EPS = 1e-6
NEG = -1e30
N_MIX = 2
LR, B1, B2, ADAM_EPS, WD, STEP = 0.001, 0.9, 0.999, 1e-08, 0.01, 10
ADAM_TILE_ELEMS = 400 * 1024
NT = (((1,), (1,)), ((), ()))
TN = (((0,), (0,)), ((), ()))


def _pick(dim, target, mult=LANE):
    best = None
    for t in range(mult, min(dim, target) + 1, mult):
        if dim % t == 0:
            best = t
    return dim if best is None else best


def _cdiv(a, b):
    return -(-a // b)


def _params(sem):
    return pltpu.CompilerParams(dimension_semantics=sem, vmem_limit_bytes=VMEM_LIMIT)


def _sigmoid(g):
    return 1.0 / (1.0 + jnp.exp(-g))


def _row(v, r):
    rows = lax.broadcasted_iota(jnp.int32, v.shape, 0)
    return jnp.sum(jnp.where(rows == r, v, 0.0), axis=0, keepdims=True)


def _get_col(v, c):
    lanes = lax.broadcasted_iota(jnp.int32, v.shape, 1)
    return jnp.sum(jnp.where(lanes == c, v, 0.0), axis=1, keepdims=True)


def _put_col(v, c, col):
    lanes = lax.broadcasted_iota(jnp.int32, v.shape, 1)
    return jnp.where(lanes == c, col, v)


def _rows3(s0, s1, s2, width):
    rows = lax.broadcasted_iota(jnp.int32, (8, width), 0)
    z = jnp.zeros((8, width), F32)
    return jnp.where(rows == 0, s0, jnp.where(rows == 1, s1, jnp.where(rows == 2, s2, z)))


def _shift_rows(w, prev_row, next_row):
    n = w.shape[0]
    rows = lax.broadcasted_iota(jnp.int32, (n, 1), 0)
    down = jnp.where(rows == 0, prev_row, pltpu.roll(w, 1, 0))
    up = jnp.where(rows == n - 1, next_row, pltpu.roll(w, n - 1, 0))
    return down, up


def _seg_flags(i, nt):
    return i <= 1, (i == 0) | (i == nt - 1)


def _halo_specs(width, nrows):
    r = RB // HALO
    nh = nrows // HALO
    prev = pl.BlockSpec((HALO, width), lambda i: (jnp.maximum(i * r - 1, 0), 0))
    nxt = pl.BlockSpec((HALO, width), lambda i: (jnp.minimum((i + 1) * r, nh - 1), 0))
    return prev, nxt


class Layout:
    def __init__(self, d, ff, nqkv, depth):
        self.nb_up, self.kb_dn, self.kb_o = 2 * ff // 4, ff // 4, d // 4
        self.nb_mix = [(nqkv if l % N_MIX == 0 else 3 * d) // 4 for l in range(depth)]
        self.mix = [_cdiv(self.nb_up, nb) * nb for nb in self.nb_mix]
        self.ct = [m + nb for m, nb in zip(self.mix, self.nb_mix)]
        self.out = _cdiv(self.kb_dn, self.kb_o) * self.kb_o
        self.rt = _cdiv(self.out + self.kb_o, 2 * HALO) * 2 * HALO


def _mm_call(name, grid, in_specs, out_specs, out_shape, contract, operands, acc_shape, epilogue,
             n_extra=0, aliases=None, comm=None, merge_b=False):
    nk = grid[2]
    n_out = len(out_shape)
    n_cin = len(comm["ins"]) if comm else 0
    n_cout = len(comm["out_shape"]) if comm else 0
    aliases = dict(aliases or {})
    in_specs, out_specs, out_shape, operands = list(in_specs), list(out_specs), list(out_shape), list(operands)
    scratch = [] if nk == 1 else [pltpu.VMEM(acc_shape, F32)]
    if comm:
        for i_in, i_out in comm["aliases"].items():
            aliases[len(operands) + i_in] = n_out + i_out
        in_specs += [HBM_SPEC] * n_cin
        out_specs += [HBM_SPEC] * n_cout
        out_shape += comm["out_shape"]
        operands += comm["ins"]
        scratch += [pltpu.SemaphoreType.DMA((comm["n_sems"],)), pltpu.SemaphoreType.DMA((comm["n_sems"],))]

    def body(*refs):
        a_ref, b_ref = refs[0], refs[1]
        extra = refs[2:2 + n_extra]
        p = 2 + n_extra
        cin, outs = refs[p:p + n_cin], refs[p + n_cin:p + n_cin + n_out]
        couts = refs[p + n_cin + n_out:p + n_cin + n_out + n_cout]
        scr = refs[p + n_cin + n_out + n_cout:]
        ids = (pl.program_id(0), pl.program_id(1))
        k = pl.program_id(2)
        if comm:
            start, finish = comm["ops"](cin, couts, scr[-2], scr[-1])

            @pl.when((ids[0] == 0) & (ids[1] == 0) & (k == 0))
            def _():
                start()

        def part():
            b = b_ref[...]
            if merge_b:
                b = b.reshape(b.shape[0] * b.shape[1], b.shape[2])
            return lax.dot_general(a_ref[...], b, contract, preferred_element_type=F32)

        if nk == 1:
            epilogue(part(), extra, outs, ids)
        else:
            acc = scr[0]

            @pl.when(k == 0)
            def _():
                acc[...] = jnp.zeros_like(acc)

            acc[...] += part()

            @pl.when(k == nk - 1)
            def _():
                epilogue(acc[...], extra, outs, ids)

        if comm:
            @pl.when((ids[0] == grid[0] - 1) & (ids[1] == grid[1] - 1) & (k == nk - 1))
            def _():
                finish()

    sem = ("arbitrary",) * 3 if comm else ("parallel", "parallel", "arbitrary")
    res = pl.pallas_call(
        body, name=name, grid=grid, in_specs=in_specs, out_specs=out_specs, out_shape=out_shape,
        scratch_shapes=scratch, input_output_aliases=aliases, compiler_params=_params(sem))(*operands)
    return (res[:n_out], res[n_out:]) if comm else res


def _store(dtype):
    def epilogue(r, extra, outs, ids):
        outs[0][...] = r.astype(dtype)
    return epilogue


def _ret(res, comm, single=True):
    if comm:
        return (res[0][0] if single else res[0]), res[1]
    return res[0] if single else res


def _slots_per_step(s, kb):
    return max(n for n in (1, 2, 4) if s % n == 0 and (n == 1 or n * kb <= WIDE))


def mm_nn_col(name, a, wc, off, nb, comm=None):
    m, d = a.shape
    s = wc.shape[0]
    tn = _pick(nb, WIDE, MXU)
    tm = _pick(m, 528 if tn > 1536 else 1056, HALO)
    npb, ob = nb // tn, off // tn
    assert off % tn == 0
    return _ret(_mm_call(
        name, (s * npb, m // tm, 1),
        [pl.BlockSpec((tm, d), lambda j, i, k: (i, 0)),
         pl.BlockSpec((None, d, tn), lambda j, i, k: (j // npb, 0, ob + j % npb))],
        [pl.BlockSpec((tm, tn), lambda j, i, k: (i, j))],
        [jax.ShapeDtypeStruct((m, s * nb), BF16)],
        (((1,), (0,)), ((), ())), (a, wc), None, _store(BF16), comm=comm), comm)


def mm_nn_row(name, a, wr, off, kb, res, mod, gi, comm=None):
    m = a.shape[0]
    s, _, d = wr.shape
    sp = _slots_per_step(s, kb)
    tm, tn = _pick(m, 528, HALO), _pick(d, 1024)
    ob = off // kb
    assert off % kb == 0

    def epilogue(r, extra, outs, ids):
        res_ref, mod_ref = extra
        rows = ids[0] * tm + lax.broadcasted_iota(jnp.int32, (tm, 1), 0)
        g = jnp.where(rows < CTX, mod_ref[0, gi:gi + 1, :], mod_ref[1, gi:gi + 1, :])
        outs[0][...] = res_ref[...] + g * r
        outs[1][...] = r.astype(BF16)

    return _ret(_mm_call(
        name, (m // tm, d // tn, s // sp),
        [pl.BlockSpec((tm, sp * kb), lambda i, j, k: (i, k)),
         pl.BlockSpec((sp, kb, tn), lambda i, j, k: (k, ob, j)),
         pl.BlockSpec((tm, tn), lambda i, j, k: (i, j)),
         pl.BlockSpec((2, 8, tn), lambda i, j, k: (0, 0, j))],
        [pl.BlockSpec((tm, tn), lambda i, j, k: (i, j)), pl.BlockSpec((tm, tn), lambda i, j, k: (i, j))],
        [jax.ShapeDtypeStruct((m, d), F32), jax.ShapeDtypeStruct((m, d), BF16)],
        (((1,), (0,)), ((), ())), (a, wr, res, mod), (tm, tn), epilogue, n_extra=2, comm=comm, merge_b=True),
        comm, single=False)


def mm_nt_col(name, dy, wc, off, nb):
    m = dy.shape[0]
    s, d, _ = wc.shape
    tc = _pick(nb, WIDE, MXU)
    tm = _pick(m, 528 if tc > 1536 else 768, HALO)
    npb, ob = nb // tc, off // tc
    return _mm_call(
        name, (m // tm, 1, s * npb),
        [pl.BlockSpec((tm, tc), lambda i, j, k: (i, k)),
         pl.BlockSpec((None, d, tc), lambda i, j, k: (k // npb, 0, ob + k % npb))],
        [pl.BlockSpec((tm, d), lambda i, j, k: (i, 0))],
        [jax.ShapeDtypeStruct((m, d), F32)],
        NT, (dy, wc), (tm, d), _store(F32))[0]


def mm_nt_row(name, dy, wr, off, kb):
    m, d = dy.shape
    s = wr.shape[0]
    sp = _slots_per_step(s, kb)
    tm = _pick(m, 528 if sp * kb > 2048 else 1056, HALO)
    ob = off // kb
    return _mm_call(
        name, (s // sp, m // tm, 1),
        [pl.BlockSpec((tm, d), lambda j, i, k: (i, 0)),
         pl.BlockSpec((sp, kb, d), lambda j, i, k: (j, ob, 0))],
        [pl.BlockSpec((tm, sp * kb), lambda j, i, k: (i, j))],
        [jax.ShapeDtypeStruct((m, s * kb), BF16)],
        NT, (dy, wr), None, _store(BF16), merge_b=True)[0]


def mm_tn_col(name, a, dy, gbuf, off, nb, comm=None):
    t, d = a.shape
    s = gbuf.shape[0]
    tn, tt = _pick(nb, WIDE, MXU), _pick(t, 768, MXU)
    tka = _pick(d, 1024 if tn > 1536 else 2048)
    npb, ob = nb // tn, off // tn
    return _ret(_mm_call(
        name, (d // tka, s * npb, t // tt),
        [pl.BlockSpec((tt, tka), lambda i, j, k: (k, i)),
         pl.BlockSpec((tt, tn), lambda i, j, k: (k, j)),
         pl.BlockSpec(memory_space=pl.ANY)],
        [pl.BlockSpec((None, tka, tn), lambda i, j, k: (j // npb, i, ob + j % npb))],
        [jax.ShapeDtypeStruct(gbuf.shape, BF16)],
        TN, (a, dy, gbuf), (tka, tn), _store(BF16), n_extra=1, aliases={2: 0}, comm=comm), comm)


def mm_tn_row(name, act, dy, gbuf, off, kb, comm=None):
    t, d = dy.shape
    s = gbuf.shape[0]
    tt = _pick(t, 768, MXU)
    ob = off // kb
    return _ret(_mm_call(
        name, (s, 1, t // tt),
        [pl.BlockSpec((tt, kb), lambda i, j, k: (k, i)),
         pl.BlockSpec((tt, d), lambda i, j, k: (k, 0)),
         pl.BlockSpec(memory_space=pl.ANY)],
        [pl.BlockSpec((None, kb, d), lambda i, j, k: (i, ob, 0))],
        [jax.ShapeDtypeStruct(gbuf.shape, BF16)],
        TN, (act, dy, gbuf), (kb, d), _store(BF16), n_extra=1, aliases={2: 0}, comm=comm), comm)


def cast_pack(name, w, l, buf, off, col, chip):
    _, k, n = w.shape
    if col:
        tr = _pick(k, 512, HALO)
        ob = off // n
        out_spec = pl.BlockSpec((None, tr, n), lambda i, s: (s[0], i, ob))
    else:
        tr = _pick(k, 704, HALO)
        ob = off // tr
        out_spec = pl.BlockSpec((None, tr, n), lambda i, s: (s[0], ob + i, 0))

    def body(s_ref, w_ref, buf_ref, out_ref):
        out_ref[...] = w_ref[...].astype(BF16)

    return pl.pallas_call(
        body, name=name,
        grid_spec=pltpu.PrefetchScalarGridSpec(
            num_scalar_prefetch=1, grid=(k // tr,),
            in_specs=[pl.BlockSpec((None, tr, n), lambda i, s: (l, i, 0)), pl.BlockSpec(memory_space=pl.ANY)],
            out_specs=out_spec),
        out_shape=jax.ShapeDtypeStruct(buf.shape, BF16),
        input_output_aliases={2: 0}, compiler_params=_params(("parallel",)))(chip, w, buf)


def norm_mod(name, x, mod, sh, sc):
    t, d = x.shape

    def body(x_ref, mod_ref, h_ref):
        seg = jnp.minimum(pl.program_id(0), 1)
        xv = x_ref[...]
        r = lax.rsqrt(jnp.mean(xv * xv, axis=-1, keepdims=True) + EPS)
        m = mod_ref[seg]
        h_ref[...] = ((xv * r) * (1.0 + m[sc:sc + 1, :]) + m[sh:sh + 1, :]).astype(BF16)

    return pl.pallas_call(
        body, name=name, grid=(t // RB,),
        in_specs=[pl.BlockSpec((RB, d), lambda i: (i, 0)), pl.BlockSpec((2, 8, d), lambda i: (0, 0, 0))],
        out_specs=pl.BlockSpec((RB, d), lambda i: (i, 0)),
        out_shape=jax.ShapeDtypeStruct((t, d), BF16), compiler_params=_params(("parallel",)))(x, mod)


def sc_gate_fwd(name, u, cw):
    t = u.shape[0]
    d = u.shape[1] // 3
    nt, tc = t // RB, _pick(d, 512)
    prev, nxt = _halo_specs(3 * d, t)

    def body(u_ref, up_ref, un_ref, cw_ref, z_ref):
        first, last = _seg_flags(pl.program_id(0), nt)
        for j in range(d // tc):
            c0 = j * tc
            gb = u_ref[:, c0:c0 + tc].astype(F32)
            w = u_ref[:, d + c0:d + c0 + tc].astype(F32) * u_ref[:, 2 * d + c0:2 * d + c0 + tc].astype(F32)
            pw = _row(up_ref[:, d + c0:d + c0 + tc].astype(F32) * up_ref[:, 2 * d + c0:2 * d + c0 + tc].astype(F32), HALO - 1)
            nw = _row(un_ref[:, d + c0:d + c0 + tc].astype(F32) * un_ref[:, 2 * d + c0:2 * d + c0 + tc].astype(F32), 0)
            wd, wu = _shift_rows(w, jnp.where(first, 0.0, pw), jnp.where(last, 0.0, nw))
            cwj = cw_ref[:, c0:c0 + tc]
            conv = wd * cwj[0:1] + w * cwj[1:2] + wu * cwj[2:3]
            z_ref[:, c0:c0 + tc] = (gb * conv).astype(BF16)

    return pl.pallas_call(
        body, name=name, grid=(nt,),
        in_specs=[pl.BlockSpec((RB, 3 * d), lambda i: (i, 0)), prev, nxt, pl.BlockSpec((8, d), lambda i: (0, 0))],
        out_specs=pl.BlockSpec((RB, d), lambda i: (i, 0)),
        out_shape=jax.ShapeDtypeStruct((t, d), BF16), compiler_params=_params(("parallel",)))(u, u, u, cw)


def sc_gate_bwd(name, u, dz, cw):
    t = u.shape[0]
    d = u.shape[1] // 3
    nt, tc = t // RB, _pick(d, 512)
    prev, nxt = _halo_specs(3 * d, t)
    dprev, dnxt = _halo_specs(d, t)

    def body(u_ref, up_ref, un_ref, dz_ref, dzp_ref, dzn_ref, cw_ref, du_ref, dcw_ref):
        i = pl.program_id(0)
        first, last = _seg_flags(i, nt)

        @pl.when(i == 0)
        def _():
            dcw_ref[...] = jnp.zeros_like(dcw_ref)

        for j in range(d // tc):
            c0 = j * tc
            sl0, sl1, sl2 = slice(c0, c0 + tc), slice(d + c0, d + c0 + tc), slice(2 * d + c0, 2 * d + c0 + tc)
            gb, gc, v = u_ref[:, sl0].astype(F32), u_ref[:, sl1].astype(F32), u_ref[:, sl2].astype(F32)
            w = gc * v
            pw = _row(up_ref[:, sl1].astype(F32) * up_ref[:, sl2].astype(F32), HALO - 1)
            nw = _row(un_ref[:, sl1].astype(F32) * un_ref[:, sl2].astype(F32), 0)
            wd, wu = _shift_rows(w, jnp.where(first, 0.0, pw), jnp.where(last, 0.0, nw))
            cwj = cw_ref[:, sl0]
            cw0, cw1, cw2 = cwj[0:1], cwj[1:2], cwj[2:3]
            dzv = dz_ref[:, sl0].astype(F32)
            e = dzv * gb
            pe = _row(dzp_ref[:, sl0].astype(F32) * up_ref[:, sl0].astype(F32), HALO - 1)
            ne = _row(dzn_ref[:, sl0].astype(F32) * un_ref[:, sl0].astype(F32), 0)
            ed, eu = _shift_rows(e, jnp.where(first, 0.0, pe), jnp.where(last, 0.0, ne))
            dw = cw0 * eu + cw1 * e + cw2 * ed
            du_ref[:, sl0] = (dzv * (wd * cw0 + w * cw1 + wu * cw2)).astype(BF16)
            du_ref[:, sl1] = (dw * v).astype(BF16)
            du_ref[:, sl2] = (dw * gc).astype(BF16)
            dcw_ref[:, sl0] += _rows3(jnp.sum(e * wd, axis=0, keepdims=True), jnp.sum(e * w, axis=0, keepdims=True),
                                      jnp.sum(e * wu, axis=0, keepdims=True), tc)

    return pl.pallas_call(
        body, name=name, grid=(nt,),
        in_specs=[pl.BlockSpec((RB, 3 * d), lambda i: (i, 0)), prev, nxt,
                  pl.BlockSpec((RB, d), lambda i: (i, 0)), dprev, dnxt, pl.BlockSpec((8, d), lambda i: (0, 0))],
        out_specs=[pl.BlockSpec((RB, 3 * d), lambda i: (i, 0)), pl.BlockSpec((8, d), lambda i: (0, 0))],
        out_shape=[jax.ShapeDtypeStruct((t, 3 * d), BF16), jax.ShapeDtypeStruct((8, d), F32)],
        compiler_params=_params(("arbitrary",)))(u, u, u, dz, dz, dz, cw)


def ffn_act_fwd(name, up, cw):
    t = up.shape[0]
    ff = up.shape[1] // 2
    nt, tc = t // RB, _pick(ff, 1408)
    prev, nxt = _halo_specs(2 * ff, t)

    def body(up_ref, upp_ref, upn_ref, cw_ref, a_ref):
        first, last = _seg_flags(pl.program_id(0), nt)
        for j in range(ff // tc):
            sg, sv = slice(j * tc, (j + 1) * tc), slice(ff + j * tc, ff + (j + 1) * tc)
            gate = up_ref[:, sg].astype(F32)
            pg = _row(upp_ref[:, sg].astype(F32), HALO - 1)
            ng = _row(upn_ref[:, sg].astype(F32), 0)
            gd, gu = _shift_rows(gate, jnp.where(first, 0.0, pg), jnp.where(last, 0.0, ng))
            cwj = cw_ref[:, sg]
            g = gd * cwj[0:1] + gate * cwj[1:2] + gu * cwj[2:3] + cwj[3:4]
            a_ref[:, sg] = (g * _sigmoid(g) * up_ref[:, sv].astype(F32)).astype(BF16)

    return pl.pallas_call(
        body, name=name, grid=(nt,),
        in_specs=[pl.BlockSpec((RB, 2 * ff), lambda i: (i, 0)), prev, nxt, pl.BlockSpec((8, ff), lambda i: (0, 0))],
        out_specs=pl.BlockSpec((RB, ff), lambda i: (i, 0)),
        out_shape=jax.ShapeDtypeStruct((t, ff), BF16), compiler_params=_params(("parallel",)))(up, up, up, cw)


def ffn_act_bwd(name, up, da, cw):
    t = up.shape[0]
    ff = up.shape[1] // 2
    nt, tc = t // RB, _pick(ff, 1408)
    prev, nxt = _halo_specs(2 * ff, t)
    dprev, dnxt = _halo_specs(ff, t)

    def dsilu(g):
        s = _sigmoid(g)
        return s * (1.0 + g * (1.0 - s))

    def body(up_ref, upp_ref, upn_ref, da_ref, dap_ref, dan_ref, cw_ref, dup_ref, acc_ref):
        i = pl.program_id(0)
        first, last = _seg_flags(i, nt)

        @pl.when(i == 0)
        def _():
            acc_ref[...] = jnp.zeros_like(acc_ref)

        for j in range(ff // tc):
            sg, sv = slice(j * tc, (j + 1) * tc), slice(ff + j * tc, ff + (j + 1) * tc)
            gate, val, dav = up_ref[:, sg].astype(F32), up_ref[:, sv].astype(F32), da_ref[:, sg].astype(F32)
            pgt, ngt = upp_ref[:, sg].astype(F32), upn_ref[:, sg].astype(F32)
            pg1, pg2 = _row(pgt, HALO - 1), _row(pgt, HALO - 2)
            ng1, ng2 = _row(ngt, 0), _row(ngt, 1)
            cwj = cw_ref[:, sg]
            cw0, cw1, cw2, b = cwj[0:1], cwj[1:2], cwj[2:3], cwj[3:4]
            gd, gu = _shift_rows(gate, jnp.where(first, 0.0, pg1), jnp.where(last, 0.0, ng1))
            g = gd * cw0 + gate * cw1 + gu * cw2 + b
            g_p = pg2 * cw0 + pg1 * cw1 + _row(gate, 0) * cw2 + b
            g_n = _row(gate, RB - 1) * cw0 + ng1 * cw1 + ng2 * cw2 + b
            dg = dav * val * dsilu(g)
            dg_p = _row(dap_ref[:, sg].astype(F32) * upp_ref[:, sv].astype(F32), HALO - 1) * dsilu(g_p)
            dg_n = _row(dan_ref[:, sg].astype(F32) * upn_ref[:, sv].astype(F32), 0) * dsilu(g_n)
            dgd, dgu = _shift_rows(dg, jnp.where(first, 0.0, dg_p), jnp.where(last, 0.0, dg_n))
            dup_ref[:, sg] = (cw0 * dgu + cw1 * dg + cw2 * dgd).astype(BF16)
            dup_ref[:, sv] = (dav * g * _sigmoid(g)).astype(BF16)
            rows = lax.broadcasted_iota(jnp.int32, (8, tc), 0)
            acc_ref[:, sg] += (_rows3(jnp.sum(dg * gd, axis=0, keepdims=True), jnp.sum(dg * gate, axis=0, keepdims=True),
                                      jnp.sum(dg * gu, axis=0, keepdims=True), tc)
                               + jnp.where(rows == 3, jnp.sum(dg, axis=0, keepdims=True), 0.0))

    return pl.pallas_call(
        body, name=name, grid=(nt,),
        in_specs=[pl.BlockSpec((RB, 2 * ff), lambda i: (i, 0)), prev, nxt,
                  pl.BlockSpec((RB, ff), lambda i: (i, 0)), dprev, dnxt, pl.BlockSpec((8, ff), lambda i: (0, 0))],
        out_specs=[pl.BlockSpec((RB, 2 * ff), lambda i: (i, 0)), pl.BlockSpec((8, ff), lambda i: (0, 0))],
        out_shape=[jax.ShapeDtypeStruct((t, 2 * ff), BF16), jax.ShapeDtypeStruct((8, ff), F32)],
        compiler_params=_params(("arbitrary",)))(up, up, up, da, da, da, cw)


def _rot(z):
    w = z.shape[1]
    lane = lax.broadcasted_iota(jnp.int32, z.shape, 1)
    return jnp.where((lane % 64) < 32, -pltpu.roll(z, w - 32, 1), pltpu.roll(z, 32, 1))


def rope_fwd(name, qkv, cos, sin, gains, dq, dkv):
    t, nqkv = qkv.shape
    nh, nkv = dq // HEAD, dkv // HEAD

    def body(qkv_ref, cos_ref, sin_ref, g_ref, qr_ref, kr_ref):
        cs, sn = cos_ref[...], sin_ref[...]
        for hd in range(nh + nkv):
            c0 = hd * HEAD
            xh = qkv_ref[:, c0:c0 + HEAD].astype(F32)
            r = lax.rsqrt(jnp.mean(xh * xh, axis=-1, keepdims=True) + EPS)
            y = xh * r * (g_ref[0:1, :] if hd < nh else g_ref[1:2, :])
            yr = (y * cs + _rot(y) * sn).astype(BF16)
            if hd < nh:
                qr_ref[:, c0:c0 + HEAD] = yr
            else:
                kr_ref[:, c0 - dq:c0 - dq + HEAD] = yr

    return pl.pallas_call(
        body, name=name, grid=(t // RB,),
        in_specs=[pl.BlockSpec((RB, nqkv), lambda i: (i, 0)), pl.BlockSpec((RB, HEAD), lambda i: (i, 0)),
                  pl.BlockSpec((RB, HEAD), lambda i: (i, 0)), pl.BlockSpec((8, HEAD), lambda i: (0, 0))],
        out_specs=[pl.BlockSpec((RB, dq), lambda i: (i, 0)), pl.BlockSpec((RB, dkv), lambda i: (i, 0))],
        out_shape=[jax.ShapeDtypeStruct((t, dq), BF16), jax.ShapeDtypeStruct((t, dkv), BF16)],
        compiler_params=_params(("parallel",)))(qkv, cos, sin, gains)


def rope_bwd(name, qkv, dqr, dkr, dv, cos, sin, gains):
    t, nqkv = qkv.shape
    dq, dkv = dqr.shape[1], dkr.shape[1]
    nh, nkv = dq // HEAD, dkv // HEAD

    def body(qkv_ref, dq_ref, dk_ref, dv_ref, cos_ref, sin_ref, g_ref, out_ref, dg_ref):
        @pl.when(pl.program_id(0) == 0)
        def _():
            dg_ref[...] = jnp.zeros_like(dg_ref)

        cs, sn = cos_ref[...], sin_ref[...]
        zero = jnp.zeros((1, HEAD), F32)
        gq, gk = zero, zero
        for hd in range(nh + nkv):
            c0 = hd * HEAD
            xh = qkv_ref[:, c0:c0 + HEAD].astype(F32)
            r = lax.rsqrt(jnp.mean(xh * xh, axis=-1, keepdims=True) + EPS)
            xhat = xh * r
            dy = dq_ref[:, c0:c0 + HEAD] if hd < nh else dk_ref[:, c0 - dq:c0 - dq + HEAD]
            tt = dy * cs - _rot(dy * sn)
            gsum = jnp.sum(tt * xhat, axis=0, keepdims=True)
            if hd < nh:
                gq = gq + gsum
            else:
                gk = gk + gsum
            dxh = tt * (g_ref[0:1, :] if hd < nh else g_ref[1:2, :])
            dx = r * (dxh - xhat * jnp.mean(dxh * xhat, axis=-1, keepdims=True))
            out_ref[:, c0:c0 + HEAD] = dx.astype(BF16)
        out_ref[:, dq + dkv:] = dv_ref[...].astype(BF16)
        dg_ref[...] += _rows3(gq, gk, zero, HEAD)

    return pl.pallas_call(
        body, name=name, grid=(t // RB,),
        in_specs=[pl.BlockSpec((RB, nqkv), lambda i: (i, 0)), pl.BlockSpec((RB, dq), lambda i: (i, 0)),
                  pl.BlockSpec((RB, dkv), lambda i: (i, 0)), pl.BlockSpec((RB, dkv), lambda i: (i, 0)),
                  pl.BlockSpec((RB, HEAD), lambda i: (i, 0)), pl.BlockSpec((RB, HEAD), lambda i: (i, 0)),
                  pl.BlockSpec((8, HEAD), lambda i: (0, 0))],
        out_specs=[pl.BlockSpec((RB, nqkv), lambda i: (i, 0)), pl.BlockSpec((8, HEAD), lambda i: (0, 0))],
        out_shape=[jax.ShapeDtypeStruct((t, nqkv), BF16), jax.ShapeDtypeStruct((8, HEAD), F32)],
        compiler_params=_params(("arbitrary",)))(qkv, dqr, dkr, dv, cos, sin, gains)


def resid_bwd(name, dx, dh, x, mod_n, sh, sc, y_prev=None, mod_g=None, gi=0):
    t, d = x.shape
    has_prev = y_prev is not None

    def body(*refs):
        if has_prev:
            dx_ref, dh_ref, x_ref, mn_ref, y_ref, mg_ref, dxo_ref, dy_ref, acc_ref = refs
        else:
            dx_ref, dh_ref, x_ref, mn_ref, dxo_ref, acc_ref = refs
        i = pl.program_id(0)
        seg = jnp.minimum(i, 1)

        @pl.when(i == 0)
        def _():
            acc_ref[...] = jnp.zeros_like(acc_ref)

        xv, dhv = x_ref[...], dh_ref[...]
        r = lax.rsqrt(jnp.mean(xv * xv, axis=-1, keepdims=True) + EPS)
        xhat = xv * r
        m = mn_ref[seg]
        dxh = dhv * (1.0 + m[sc:sc + 1, :])
        dxo = dx_ref[...] + r * (dxh - xhat * jnp.mean(dxh * xhat, axis=-1, keepdims=True))
        dxo_ref[...] = dxo
        s2 = jnp.zeros((1, d), F32)
        if has_prev:
            dy_ref[...] = (mg_ref[seg][gi:gi + 1, :] * dxo).astype(BF16)
            s2 = jnp.sum(dxo * y_ref[...].astype(F32), axis=0, keepdims=True)
        acc_ref[seg] = acc_ref[seg] + _rows3(jnp.sum(dhv, axis=0, keepdims=True),
                                             jnp.sum(dhv * xhat, axis=0, keepdims=True), s2, d)

    row = pl.BlockSpec((RB, d), lambda i: (i, 0))
    modspec = pl.BlockSpec((2, 8, d), lambda i: (0, 0, 0))
    in_specs, operands = [row, row, row, modspec], [dx, dh, x, mod_n]
    out_specs, out_shape = [row], [jax.ShapeDtypeStruct((t, d), F32)]
    if has_prev:
        in_specs += [row, modspec]
        operands += [y_prev, mod_g]
        out_specs.append(row)
        out_shape.append(jax.ShapeDtypeStruct((t, d), BF16))
    out_specs.append(modspec)
    out_shape.append(jax.ShapeDtypeStruct((2, 8, d), F32))
    return pl.pallas_call(body, name=name, grid=(t // RB,), in_specs=in_specs, out_specs=out_specs,
                          out_shape=out_shape, compiler_params=_params(("arbitrary",)))(*operands)


def loss_head(name, xf, target, y_last, mod, gi):
    t, d = xf.shape

    def body(x_ref, t_ref, y_ref, mod_ref, dx_ref, dy_ref, acc_ref, lp_ref):
        i = pl.program_id(0)
        seg = jnp.minimum(i, 1)

        @pl.when(i == 0)
        def _():
            acc_ref[...] = jnp.zeros_like(acc_ref)
            lp_ref[...] = jnp.zeros_like(lp_ref)

        lat = i >= 1
        err = jnp.where(lat, x_ref[...] - t_ref[...], 0.0)
        dxv = err / d
        dx_ref[...] = dxv
        dy_ref[...] = (mod_ref[seg][gi:gi + 1, :] * dxv).astype(BF16)
        zero = jnp.zeros((1, d), F32)
        lp_ref[...] += _rows3(jnp.sum(err * err, axis=0, keepdims=True), zero, zero, d)
        acc_ref[seg] = acc_ref[seg] + _rows3(zero, zero, jnp.sum(dxv * y_ref[...].astype(F32), axis=0, keepdims=True), d)

    row = pl.BlockSpec((RB, d), lambda i: (i, 0))
    modspec = pl.BlockSpec((2, 8, d), lambda i: (0, 0, 0))
    return pl.pallas_call(
        body, name=name, grid=(t // RB,),
        in_specs=[row, pl.BlockSpec((RB, d), lambda i: (jnp.maximum(i - 1, 0), 0)), row, modspec],
        out_specs=[row, row, modspec, pl.BlockSpec((8, d), lambda i: (0, 0))],
        out_shape=[jax.ShapeDtypeStruct((t, d), F32), jax.ShapeDtypeStruct((t, d), BF16),
                   jax.ShapeDtypeStruct((2, 8, d), F32), jax.ShapeDtypeStruct((8, d), F32)],
        compiler_params=_params(("arbitrary",)))(xf, target, y_last, mod)


def _kv_specs(width, colblk, nbk):
    return [pl.BlockSpec((CTX, width), lambda i: (0, colblk)),
            pl.BlockSpec((BLK, width), lambda i: (jnp.maximum(i - 1, 0), colblk)),
            pl.BlockSpec((BLK, width), lambda i: (i, colblk)),
            pl.BlockSpec((BLK, width), lambda i: (jnp.minimum(i + 1, nbk - 1), colblk))]


def _band_mask(i, seq):
    nk = CTX + 3 * BLK
    qrow = lax.broadcasted_iota(jnp.int32, (GROUP * BLK, nk), 0) % BLK
    col = lax.broadcasted_iota(jnp.int32, (GROUP * BLK, nk), 1)
    cb = col - CTX
    kpos = (i - 3) * BLK + cb
    band = (i >= 2) & (jnp.abs(BLK + qrow - cb) <= WINDOW) & (kpos >= 0) & (kpos < seq)
    return (col < CTX) | band


def _stack_heads(ref, h):
    return jnp.concatenate([ref[:, (h * GROUP + g) * HEAD:(h * GROUP + g + 1) * HEAD] for g in range(GROUP)], axis=0)


def _stack_cols(v, h):
    return jnp.concatenate([_get_col(v, h * GROUP + g) for g in range(GROUP)], axis=0)


def _sink_col(sink_ref, h):
    rowg = lax.broadcasted_iota(jnp.int32, (GROUP * BLK, 1), 0) // BLK
    sk = jnp.full((GROUP * BLK, 1), sink_ref[h * GROUP], F32)
    for g in range(1, GROUP):
        sk = jnp.where(rowg == g, sink_ref[h * GROUP + g], sk)
    return sk


def attn_fwd(name, qr, kr, qkv, sink, seq):
    t, dq = qr.shape
    dkv = kr.shape[1]
    nbk, nkv = t // BLK, dkv // HEAD
    vcol = (dq + dkv) // dkv
    scale = HEAD ** -0.5

    def body(sink_ref, q_ref, kc, kp, ko, kn, vc, vp, vo, vn, o_ref, lse_ref):
        i = pl.program_id(0)
        mask = _band_mask(i, seq)
        lse = jnp.zeros((BLK, LANE), F32)
        for h in range(nkv):
            hs = slice(h * HEAD, (h + 1) * HEAD)
            k = jnp.concatenate([kc[:, hs], kp[:, hs], ko[:, hs], kn[:, hs]], axis=0)
            v = jnp.concatenate([vc[:, hs], vp[:, hs], vo[:, hs], vn[:, hs]], axis=0)
            q4 = _stack_heads(q_ref, h)
            s = jnp.where(mask, lax.dot_general(q4, k, NT, preferred_element_type=F32) * scale, NEG)
            sk = _sink_col(sink_ref, h)
            m = jnp.maximum(jnp.max(s, axis=-1, keepdims=True), sk)
            e = jnp.exp(s - m)
            den = jnp.sum(e, axis=-1, keepdims=True) + jnp.exp(sk - m)
            o4 = jnp.dot((e / den).astype(BF16), v, preferred_element_type=F32)
            l4 = m + jnp.log(den)
            for g in range(GROUP):
                hg = h * GROUP + g
                o_ref[:, hg * HEAD:(hg + 1) * HEAD] = o4[g * BLK:(g + 1) * BLK].astype(BF16)
                lse = _put_col(lse, hg, l4[g * BLK:(g + 1) * BLK])
        lse_ref[...] = lse

    return pl.pallas_call(
        body, name=name, grid=(nbk,),
        in_specs=[pl.BlockSpec(memory_space=pltpu.SMEM), pl.BlockSpec((BLK, dq), lambda i: (i, 0))]
        + _kv_specs(dkv, 0, nbk) + _kv_specs(dkv, vcol, nbk),
        out_specs=[pl.BlockSpec((BLK, dq), lambda i: (i, 0)), pl.BlockSpec((BLK, LANE), lambda i: (i, 0))],
        out_shape=[jax.ShapeDtypeStruct((t, dq), BF16), jax.ShapeDtypeStruct((t, LANE), F32)],
        compiler_params=_params(("parallel",)))(sink, qr, kr, kr, kr, kr, qkv, qkv, qkv, qkv)


def attn_bwd_q(name, qr, kr, qkv, sink, do, o, lse, seq):
    t, dq = qr.shape
    dkv = kr.shape[1]
    nbk, nkv = t // BLK, dkv // HEAD
    vcol = (dq + dkv) // dkv
    scale = HEAD ** -0.5

    def body(sink_ref, q_ref, kc, kp, ko, kn, vc, vp, vo, vn, do_ref, o_ref, lse_ref,
             dq_ref, dl_ref, dkc_ref, dvc_ref, ds_ref):
        i = pl.program_id(0)

        @pl.when(i == 0)
        def _():
            dkc_ref[...] = jnp.zeros_like(dkc_ref)
            dvc_ref[...] = jnp.zeros_like(dvc_ref)
            ds_ref[...] = jnp.zeros_like(ds_ref)

        mask = _band_mask(i, seq)
        lse = lse_ref[...]
        delta = jnp.zeros((BLK, LANE), F32)
        dsink = jnp.zeros((8, LANE), F32)
        for h in range(nkv):
            hs = slice(h * HEAD, (h + 1) * HEAD)
            k = jnp.concatenate([kc[:, hs], kp[:, hs], ko[:, hs], kn[:, hs]], axis=0)
            v = jnp.concatenate([vc[:, hs], vp[:, hs], vo[:, hs], vn[:, hs]], axis=0)
            q4, do4 = _stack_heads(q_ref, h), _stack_heads(do_ref, h)
            d4 = jnp.sum(do4.astype(F32) * _stack_heads(o_ref, h).astype(F32), axis=-1, keepdims=True)
            l4 = _stack_cols(lse, h)
            s = jnp.where(mask, lax.dot_general(q4, k, NT, preferred_element_type=F32) * scale, NEG)
            p = jnp.exp(s - l4)
            dp = lax.dot_general(do4, v, NT, preferred_element_type=F32)
            dsb = (p * (dp - d4) * scale).astype(BF16)
            pb = p.astype(BF16)
            dq4 = jnp.dot(dsb, k, preferred_element_type=F32)
            dkc_ref[:, hs] += lax.dot_general(dsb[:, :CTX], q4, TN, preferred_element_type=F32)
            dvc_ref[:, hs] += lax.dot_general(pb[:, :CTX], do4, TN, preferred_element_type=F32)
            dsk = -jnp.exp(_sink_col(sink_ref, h) - l4) * d4
            for g in range(GROUP):
                hg = h * GROUP + g
                rs = slice(g * BLK, (g + 1) * BLK)
                dq_ref[:, hg * HEAD:(hg + 1) * HEAD] = dq4[rs]
                delta = _put_col(delta, hg, d4[rs])
                dsink = _put_col(dsink, hg, jnp.sum(dsk[rs], axis=0, keepdims=True))
        dl_ref[...] = delta
        rows = lax.broadcasted_iota(jnp.int32, (8, LANE), 0)
        ds_ref[...] += jnp.where(rows == 0, dsink, 0.0)

    blk = lambda w: pl.BlockSpec((BLK, w), lambda i: (i, 0))
    const = lambda r, w: pl.BlockSpec((r, w), lambda i: (0, 0))
    return pl.pallas_call(
        body, name=name, grid=(nbk,),
        in_specs=[pl.BlockSpec(memory_space=pltpu.SMEM), blk(dq)] + _kv_specs(dkv, 0, nbk) + _kv_specs(dkv, vcol, nbk)
        + [blk(dq), blk(dq), blk(LANE)],
        out_specs=[blk(dq), blk(LANE), const(CTX, dkv), const(CTX, dkv), const(8, LANE)],
        out_shape=[jax.ShapeDtypeStruct((t, dq), F32), jax.ShapeDtypeStruct((t, LANE), F32),
                   jax.ShapeDtypeStruct((CTX, dkv), F32), jax.ShapeDtypeStruct((CTX, dkv), F32),
                   jax.ShapeDtypeStruct((8, LANE), F32)],
        compiler_params=_params(("arbitrary",)))(sink, qr, kr, kr, kr, kr, qkv, qkv, qkv, qkv, do, o, lse)


def attn_bwd_kv(name, qr, kr, qkv, do, lse, delta, seq):
    t, dq = qr.shape
    dkv = kr.shape[1]
    nbk, nbl, nkv = t // BLK, seq // BLK, dkv // HEAD
    cb = CTX // BLK
    vcol = (dq + dkv) // dkv
    scale = HEAD ** -0.5

    def qspec(w, d):
        return pl.BlockSpec((BLK, w), lambda j: (jnp.clip(j + cb + d, cb, nbk - 1), 0))

    def body(k_ref, v_ref, *refs):
        dk_ref, dv_ref = refs[-2], refs[-1]
        j = pl.program_id(0)
        qrow = lax.broadcasted_iota(jnp.int32, (GROUP * BLK, BLK), 0) % BLK
        kcol = lax.broadcasted_iota(jnp.int32, (GROUP * BLK, BLK), 1)
        for h in range(nkv):
            hs = slice(h * HEAD, (h + 1) * HEAD)
            kh, vh = k_ref[:, hs], v_ref[:, hs]
            dk_h = jnp.zeros((BLK, HEAD), F32)
            dv_h = jnp.zeros((BLK, HEAD), F32)
            for di, d in enumerate((-1, 0, 1)):
                q_ref, do_ref, lse_ref, dl_ref = refs[4 * di:4 * di + 4]
                n = j + d
                msk = (n >= 0) & (n < nbl) & (jnp.abs(d * BLK + qrow - kcol) <= WINDOW)
                q4, do4 = _stack_heads(q_ref, h), _stack_heads(do_ref, h)
                l4, d4 = _stack_cols(lse_ref[...], h), _stack_cols(dl_ref[...], h)
                s = jnp.where(msk, lax.dot_general(q4, kh, NT, preferred_element_type=F32) * scale, NEG)
                p = jnp.exp(s - l4)
                dv_h += lax.dot_general(p.astype(BF16), do4, TN, preferred_element_type=F32)
                dp = lax.dot_general(do4, vh, NT, preferred_element_type=F32)
                dk_h += lax.dot_general((p * (dp - d4) * scale).astype(BF16), q4, TN, preferred_element_type=F32)
            dk_ref[:, hs] = dk_h
            dv_ref[:, hs] = dv_h

    in_specs = [pl.BlockSpec((BLK, dkv), lambda j: (j + cb, 0)), pl.BlockSpec((BLK, dkv), lambda j: (j + cb, vcol))]
    operands = [kr, qkv]
    for d in (-1, 0, 1):
        in_specs += [qspec(dq, d), qspec(dq, d), qspec(LANE, d), qspec(LANE, d)]
        operands += [qr, do, lse, delta]
    return pl.pallas_call(
        body, name=name, grid=(nbl,), in_specs=in_specs,
        out_specs=[pl.BlockSpec((BLK, dkv), lambda j: (j, 0)), pl.BlockSpec((BLK, dkv), lambda j: (j, 0))],
        out_shape=[jax.ShapeDtypeStruct((seq, dkv), F32), jax.ShapeDtypeStruct((seq, dkv), F32)],
        compiler_params=_params(("parallel",)))(*operands)


def ada_fwd(name, cond, w_ada):
    nl, d, n = w_ada.shape
    tn = _pick(n, 1024)

    def body(c_ref, w_ref, out_ref):
        cv = c_ref[...]
        out_ref[...] = jnp.dot((cv * _sigmoid(cv)).astype(BF16), w_ref[...].astype(BF16), preferred_element_type=F32)

    return pl.pallas_call(
        body, name=name, grid=(nl, n // tn),
        in_specs=[pl.BlockSpec((16, d), lambda l, j: (0, 0)), pl.BlockSpec((None, d, tn), lambda l, j: (l, 0, j))],
        out_specs=pl.BlockSpec((None, 16, tn), lambda l, j: (l, 0, j)),
        out_shape=jax.ShapeDtypeStruct((nl, 16, n), F32), compiler_params=_params(("parallel", "parallel")))(cond, w_ada)


def ada_bwd_cond(name, dsum, w_ada):
    nl, d, n = w_ada.shape
    tn = _pick(n, 1024)

    def body(g_ref, w_ref, out_ref):
        @pl.when((pl.program_id(0) == 0) & (pl.program_id(1) == 0))
        def _():
            out_ref[...] = jnp.zeros_like(out_ref)

        out_ref[...] += lax.dot_general(g_ref[...].astype(BF16), w_ref[...].astype(BF16), NT, preferred_element_type=F32)

    return pl.pallas_call(
        body, name=name, grid=(nl, n // tn),
        in_specs=[pl.BlockSpec((None, 8, tn), lambda l, j: (l, 0, j)), pl.BlockSpec((None, d, tn), lambda l, j: (l, 0, j))],
        out_specs=pl.BlockSpec((8, d), lambda l, j: (0, 0)),
        out_shape=jax.ShapeDtypeStruct((8, d), F32), compiler_params=_params(("arbitrary", "arbitrary")))(dsum, w_ada)


def ada_grad_w(name, cond, rhs):
    nl, _, n = rhs.shape
    d = cond.shape[1]
    tr, tn = _pick(d, 512), _pick(n, 1024)

    def body(c_ref, r_ref, out_ref):
        cv = c_ref[...]
        out_ref[...] = lax.dot_general((cv * _sigmoid(cv)).astype(BF16), r_ref[...].astype(BF16), TN, preferred_element_type=F32)

    return pl.pallas_call(
        body, name=name, grid=(nl, d // tr, n // tn),
        in_specs=[pl.BlockSpec((16, tr), lambda l, i, j: (0, i)), pl.BlockSpec((None, 16, tn), lambda l, i, j: (l, 0, j))],
        out_specs=pl.BlockSpec((None, tr, tn), lambda l, i, j: (l, i, j)),
        out_shape=jax.ShapeDtypeStruct((nl, d, n), F32),
        compiler_params=_params(("parallel", "parallel", "parallel")))(cond, rhs)


def adamw(name, g, g_spec, w, m, v, tr):
    nl, r, c = w.shape
    spec = pl.BlockSpec((None, tr, c), lambda l, i: (l, i, 0))

    def body(g_ref, w_ref, m_ref, v_ref, go_ref, d_ref, mo_ref, vo_ref):
        gv = g_ref[...]
        mn = B1 * m_ref[...] + (1.0 - B1) * gv
        vn = B2 * v_ref[...] + (1.0 - B2) * (gv * gv)
        m_hat = mn / (1.0 - B1 ** STEP)
        v_hat = vn / (1.0 - B2 ** STEP)
        go_ref[...] = gv
        d_ref[...] = -LR * (m_hat / (jnp.sqrt(v_hat) + ADAM_EPS) + WD * w_ref[...])
        mo_ref[...] = mn
        vo_ref[...] = vn

    return pl.pallas_call(
        body, name=name, grid=(nl, r // tr), in_specs=[g_spec, spec, spec, spec], out_specs=[spec] * 4,
        out_shape=[jax.ShapeDtypeStruct(w.shape, F32)] * 4, compiler_params=_params(("parallel", "parallel")))(g, w, m, v)


def adamw_small(name, g, w, m, v):
    shape = w.shape
    r3 = lambda a: a.reshape(1, -1, shape[-1]).astype(F32)
    rows = r3(w).shape[1]
    outs = adamw(name, r3(g), pl.BlockSpec((None, rows, shape[-1]), lambda l, i: (l, i, 0)), r3(w), r3(m), r3(v), rows)
    return [o.reshape(shape) for o in outs]


def _place():
    x, y, c = lax.axis_index("x"), lax.axis_index("y"), lax.axis_index("c")
    return x, y, c, [(1 - x, y), (x, 1 - y), (1 - x, 1 - y)]


def small_allgather(name, v):
    r, w = v.shape

    def body(x_ref, out_ref, send_sems, recv_sems, local_sem):
        x, y, c, chips = _place()
        me, sibling = (x, y, c), (x, y, 1 - c)

        def slot(px, py, pc):
            return out_ref.at[4 * px + 2 * py + pc]

        def copy(k, block, to, src=None):
            return pltpu.make_async_remote_copy(
                src_ref=slot(*block) if src is None else src, dst_ref=slot(*block),
                send_sem=send_sems.at[k], recv_sem=recv_sems.at[k], device_id=to, device_id_type=MESH)

        mine = pltpu.make_async_copy(x_ref, slot(*me), local_sem)
        mine.start()
        first = [copy(0, me, sibling, src=x_ref)]
        first += [copy(1 + j, me, (*chip, c), src=x_ref) for j, chip in enumerate(chips)]
        for cp in first:
            cp.start()
        passed = [copy(4 + j, (*chip, c), sibling) for j, chip in enumerate(chips)]
        for j, chip in enumerate(chips):
            copy(1 + j, (*chip, c), me).wait_recv()
            passed[j].start()
        copy(0, sibling, me).wait_recv()
        for j, chip in enumerate(chips):
            copy(4 + j, (*chip, 1 - c), me).wait_recv()
        for cp in first + passed:
            cp.wait_send()
        mine.wait()

    return pl.pallas_call(
        body, name=name, out_shape=jax.ShapeDtypeStruct((8, r, w), v.dtype),
        in_specs=[pl.BlockSpec(memory_space=pltpu.VMEM)], out_specs=pl.BlockSpec(memory_space=pltpu.VMEM),
        scratch_shapes=[pltpu.SemaphoreType.DMA((7,)), pltpu.SemaphoreType.DMA((7,)), pltpu.SemaphoreType.DMA],
        compiler_params=pltpu.CompilerParams(vmem_limit_bytes=VMEM_LIMIT))(v)


def gather_flat(name, parts):
    flat = jnp.concatenate([p.reshape(-1).astype(F32) for p in parts])
    n = flat.shape[0]
    rows = _cdiv(n, MXU * LANE) * MXU
    flat = jnp.pad(flat, (0, rows * LANE - n))
    return small_allgather(name, flat.reshape(rows, LANE)).reshape(8, rows * LANE)


def sum8(name, g):
    p = g.shape[1]
    g3 = g.reshape(8, p // LANE, LANE)
    tr = _pick(p // LANE, 1024, MXU)

    def body(g_ref, out_ref):
        acc = g_ref[0]
        for k in range(1, 8):
            acc = acc + g_ref[k]
        out_ref[...] = acc

    return pl.pallas_call(
        body, name=name, grid=(p // LANE // tr,),
        in_specs=[pl.BlockSpec((8, tr, LANE), lambda i: (0, i, 0))], out_specs=pl.BlockSpec((tr, LANE), lambda i: (i, 0)),
        out_shape=jax.ShapeDtypeStruct((p // LANE, LANE), F32), compiler_params=_params(("parallel",)))(g3).reshape(p)


HBM_SPEC = pl.BlockSpec(memory_space=pltpu.HBM)


def _half(ref, lead, c, axis):
    h = ref.shape[axis] // 2
    return ref.at[lead, pl.ds(c * h, h), :] if axis == 1 else ref.at[lead, :, pl.ds(c * h, h)]


def _gather_ops(outs, axes, send_sems, recv_sems):
    x, y, c, chips = _place()
    me, sibling = (x, y, c), (x, y, 1 - c)

    def copy(a, k, chip, pc, to):
        blk = _half(outs[a], 2 * chip[0] + chip[1], pc, axes[a])
        return pltpu.make_async_remote_copy(src_ref=blk, dst_ref=blk, send_sem=send_sems.at[6 * a + k],
                                            recv_sem=recv_sems.at[6 * a + k], device_id=to, device_id_type=MESH)

    def start():
        for a in range(len(outs)):
            for j, chip in enumerate(chips):
                copy(a, j, (x, y), c, (*chip, c)).start()

    def finish():
        for a in range(len(outs)):
            for j, chip in enumerate(chips):
                copy(a, j, chip, c, me).wait_recv()
                copy(a, 3 + j, chip, c, sibling).start()
        for a in range(len(outs)):
            for j, chip in enumerate(chips):
                copy(a, 3 + j, chip, 1 - c, me).wait_recv()
            for j, chip in enumerate(chips):
                copy(a, j, (x, y), c, (*chip, c)).wait_send()
                copy(a, 3 + j, chip, c, sibling).wait_send()

    return start, finish


def gather_comm(bufs, axes):
    return dict(ins=list(bufs), out_shape=[jax.ShapeDtypeStruct(b.shape, b.dtype) for b in bufs],
                aliases={a: a for a in range(len(bufs))}, n_sems=6 * len(bufs),
                ops=lambda cin, couts, ss, rs: _gather_ops(couts, axes, ss, rs))


def gather_weights(name, bufs, axes):
    n = len(bufs)

    def body(*refs):
        start, finish = _gather_ops(refs[n:2 * n], axes, refs[2 * n], refs[2 * n + 1])
        start()
        finish()

    return pl.pallas_call(
        body, name=name, out_shape=[jax.ShapeDtypeStruct(b.shape, b.dtype) for b in bufs],
        in_specs=[HBM_SPEC] * n, out_specs=[HBM_SPEC] * n, input_output_aliases={a: a for a in range(n)},
        scratch_shapes=[pltpu.SemaphoreType.DMA((6 * n,)), pltpu.SemaphoreType.DMA((6 * n,))])(*bufs)


def _scatter_ops(ins, outs, send_sems, recv_sems):
    x, y, c, chips = _place()
    me = 2 * x + y

    def copy(a, j, chip):
        return pltpu.make_async_remote_copy(
            src_ref=ins[a].at[2 * chip[0] + chip[1]], dst_ref=outs[a].at[me], send_sem=send_sems.at[3 * a + j],
            recv_sem=recv_sems.at[3 * a + j], device_id=(*chip, c), device_id_type=MESH)

    def start():
        for a in range(len(ins)):
            for j, chip in enumerate(chips):
                copy(a, j, chip).start()

    def finish():
        for a in range(len(ins)):
            for j, chip in enumerate(chips):
                copy(a, j, chip).wait()

    return start, finish


def scatter_comm(bufs):
    return dict(ins=list(bufs), out_shape=[jax.ShapeDtypeStruct(b.shape, b.dtype) for b in bufs], aliases={},
                n_sems=3 * len(bufs), ops=_scatter_ops)


def chip_scatter(name, bufs):
    n = len(bufs)

    def body(*refs):
        start, finish = _scatter_ops(refs[:n], refs[n:2 * n], refs[2 * n], refs[2 * n + 1])
        start()
        finish()

    return pl.pallas_call(
        body, name=name, out_shape=[jax.ShapeDtypeStruct(b.shape, b.dtype) for b in bufs],
        in_specs=[HBM_SPEC] * n, out_specs=[HBM_SPEC] * n,
        scratch_shapes=[pltpu.SemaphoreType.DMA((3 * n,)), pltpu.SemaphoreType.DMA((3 * n,))])(*bufs)


def pair_exchange(name, bufs, axes):
    n = len(bufs)

    def body(*refs):
        ins, outs, (send_sems, recv_sems) = refs[:n], refs[n:2 * n], refs[2 * n:]
        x, y, c, _ = _place()
        cps = []
        for a, (src, out) in enumerate(zip(ins, outs)):
            cp = pltpu.make_async_remote_copy(
                src_ref=_half(src, slice(None), 1 - c, axes[a]), dst_ref=out, send_sem=send_sems.at[a],
                recv_sem=recv_sems.at[a], device_id=(x, y, 1 - c), device_id_type=MESH)
            cp.start()
            cps.append(cp)
        for cp in cps:
            cp.wait()

    def halved(b, axis):
        shape = list(b.shape)
        shape[axis] //= 2
        return jax.ShapeDtypeStruct(tuple(shape), b.dtype)

    return pl.pallas_call(
        body, name=name, out_shape=[halved(b, ax) for b, ax in zip(bufs, axes)],
        in_specs=[HBM_SPEC] * n, out_specs=[HBM_SPEC] * n,
        scratch_shapes=[pltpu.SemaphoreType.DMA((n,)), pltpu.SemaphoreType.DMA((n,))])(*bufs)


def pair_add(name, buf, got, cidx, axis):
    s, r, c = got.shape
    tr = _pick(r, max(HALO, (4 * 1024 * 1024) // (2 * c)), HALO)
    per = r // tr
    if axis == 1:
        mine = pl.BlockSpec((None, tr, c), lambda k, i, cr: (k, cr[0] * per + i, 0))
    else:
        mine = pl.BlockSpec((None, tr, c), lambda k, i, cr: (k, i, cr[0]))

    def body(c_ref, a_ref, b_ref, out_ref):
        out_ref[...] = (a_ref[...].astype(F32) + b_ref[...].astype(F32)).astype(BF16)

    return pl.pallas_call(
        body, name=name,
        grid_spec=pltpu.PrefetchScalarGridSpec(
            num_scalar_prefetch=1, grid=(s, per),
            in_specs=[mine, pl.BlockSpec((None, tr, c), lambda k, i, cr: (k, i, 0))],
            out_specs=pl.BlockSpec((None, tr, c), lambda k, i, cr: (k, i, 0))),
        out_shape=jax.ShapeDtypeStruct((s, r, c), BF16),
        compiler_params=_params(("parallel", "parallel")))(cidx, buf, got)


def chip_add(name, own, got, place, dst, l, off, size, col):
    s = got.shape[0]
    if col:
        h, n = got.shape[1], size
        tr = _pick(h, max(HALO, (2 * 1024 * 1024) // (2 * n)), HALO)
        per, ob = h // tr, off // n
        own_spec = pl.BlockSpec((None, tr, n), lambda i, p: (p[0], i, ob))
        got_spec = pl.BlockSpec((s, tr, n), lambda i, p: (0, i, ob))
        out_spec = pl.BlockSpec((None, tr, n), lambda i, p: (l, p[1] * per + i, 0))
        grid = (per,)
    else:
        n = got.shape[2]
        tr = _pick(size, max(HALO, (2 * 1024 * 1024) // (2 * n)), HALO)
        ob = off // tr
        own_spec = pl.BlockSpec((None, tr, n), lambda i, p: (p[0], ob + i, 0))
        got_spec = pl.BlockSpec((s, tr, n), lambda i, p: (0, ob + i, 0))
        out_spec = pl.BlockSpec((None, tr, n), lambda i, p: (l, i, p[1]))
        grid = (size // tr,)

    def body(p_ref, own_ref, g_ref, dst_ref, out_ref):
        acc = jnp.zeros((tr, n), F32)
        for k in range(s):
            acc = acc + jnp.where(p_ref[0] == k, own_ref[...], g_ref[k]).astype(F32)
        out_ref[...] = acc

    return pl.pallas_call(
        body, name=name,
        grid_spec=pltpu.PrefetchScalarGridSpec(
            num_scalar_prefetch=1, grid=grid,
            in_specs=[own_spec, got_spec, pl.BlockSpec(memory_space=pl.ANY)], out_specs=out_spec),
        out_shape=jax.ShapeDtypeStruct(dst.shape, F32), input_output_aliases={3: 0},
        compiler_params=_params(("parallel",)))(place, own, got, dst)


def pair_join(name, bufs, axes):
    n = len(bufs)

    def body(*refs):
        outs = refs[n:2 * n]
        send_sems, recv_sems = refs[2 * n:]
        x, y, c, _ = _place()
        started = []
        for a, out in enumerate(outs):
            blk = _half(out, slice(None), c, axes[a])
            cp = pltpu.make_async_remote_copy(src_ref=blk, dst_ref=blk, send_sem=send_sems.at[a], recv_sem=recv_sems.at[a],
                                              device_id=(x, y, 1 - c), device_id_type=MESH)
            cp.start()
            started.append(cp)
        for cp in started:
            cp.wait()

    return pl.pallas_call(
        body, name=name, out_shape=[jax.ShapeDtypeStruct(b.shape, b.dtype) for b in bufs],
        in_specs=[HBM_SPEC] * n, out_specs=[HBM_SPEC] * n, input_output_aliases={a: a for a in range(n)},
        scratch_shapes=[pltpu.SemaphoreType.DMA((n,)), pltpu.SemaphoreType.DMA((n,))])(*bufs)


def _rope_tables(seq):
    rows = seq // GRID_W
    row = jnp.repeat(jnp.arange(rows), GRID_W).astype(F32)
    col = jnp.tile(jnp.arange(GRID_W), rows).astype(F32)
    pairs = HEAD // 4
    inv = ROPE_BASE ** (-jnp.arange(pairs, dtype=F32) / pairs)
    ang = jnp.stack([row[:, None] * inv, col[:, None] * inv], axis=1)
    ang = jnp.broadcast_to(ang[:, :, None, :], (seq, 2, 2, pairs)).reshape(seq, HEAD)
    cos = jnp.concatenate([jnp.ones((CTX, HEAD), F32), jnp.cos(ang)], axis=0)
    sin = jnp.concatenate([jnp.zeros((CTX, HEAD), F32), jnp.sin(ang)], axis=0)
    return cos, sin


def _pad8(a):
    return jnp.pad(a, ((0, 8 - a.shape[0]), (0, 0)))


def kernel(x, c, ctx, c_ctx, w_ada, b_ada, attn_w_qkv, attn_w_o, attn_q_gain, attn_k_gain, attn_sink, sc_w_in, sc_conv, sc_w_out, ffn_w_up, ffn_conv, ffn_conv_b, ffn_w_down, loss_target, m_c_ctx, m_w_ada, m_b_ada, m_attn_w_qkv, m_attn_w_o, m_attn_q_gain, m_attn_k_gain, m_attn_sink, m_sc_w_in, m_sc_conv, m_sc_w_out, m_ffn_w_up, m_ffn_conv, m_ffn_conv_b, m_ffn_w_down, v_c_ctx, v_w_ada, v_b_ada, v_attn_w_qkv, v_attn_w_o, v_attn_q_gain, v_attn_k_gain, v_attn_sink, v_sc_w_in, v_sc_conv, v_sc_w_out, v_ffn_w_up, v_ffn_conv, v_ffn_conv_b, v_ffn_w_down):
    seq, d = x.shape[1], x.shape[2]
    depth, nada = w_ada.shape[0], w_ada.shape[2]
    n_attn, n_conv = attn_w_qkv.shape[0], sc_w_in.shape[0]
    ff = ffn_conv_b.shape[1]
    dq, dkv = d, d // GROUP
    nqkv = dq + 2 * dkv
    nh = dq // HEAD
    assert ctx.shape[1] == CTX and seq % RB == 0 and 6 * d == 4 * nada
    lay = Layout(d, ff, nqkv, depth)
    ax, ay, ac = lax.axis_index("x"), lax.axis_index("y"), lax.axis_index("c")
    chip, dev = 2 * ax + ay, 4 * ax + 2 * ay + ac
    cidx = jnp.reshape(ac, (1,)).astype(jnp.int32)

    chip1 = jnp.reshape(chip, (1,)).astype(jnp.int32)
    wcs, wrs = [], []
    for l in range(depth):
        j, is_attn = l // N_MIX, l % N_MIX == 0
        wc_l, wr_l = lax.empty((4, d, lay.ct[l]), BF16), lax.empty((4, lay.rt, d), BF16)
        wc_l = cast_pack(f"pack_up_{l}", ffn_w_up, l, wc_l, 0, True, chip1)
        wc_l = cast_pack(f"pack_mix_{l}", attn_w_qkv if is_attn else sc_w_in, j, wc_l, lay.mix[l], True, chip1)
        wr_l = cast_pack(f"pack_down_{l}", ffn_w_down, l, wr_l, 0, False, chip1)
        wr_l = cast_pack(f"pack_out_{l}", attn_w_o if is_attn else sc_w_out, j, wr_l, lay.out, False, chip1)
        wcs.append(wc_l)
        wrs.append(wr_l)
    wcs[0], wrs[0] = gather_weights("gather_w0", [wcs[0], wrs[0]], (1, 2))

    g1 = gather_flat("gather_cond", [c, sc_conv, ffn_conv])
    c_all = g1[:, :d]
    o1 = d + sc_conv.size
    sc_conv_full = jnp.concatenate([g1[2 * s, d:o1].reshape(sc_conv.shape) for s in range(4)], axis=-1)
    ffn_conv_full = jnp.concatenate([g1[2 * s, o1:o1 + ffn_conv.size].reshape(ffn_conv.shape) for s in range(4)], axis=-1)
    cond = jnp.concatenate([c_all, c_ctx[None, :], jnp.zeros((7, d), F32)], axis=0)
    ada_part = ada_fwd("ada_fwd", cond, w_ada)
    g2 = gather_flat("gather_ada", [ada_part])
    ada_all = jnp.concatenate([g2[2 * s, :ada_part.size].reshape(ada_part.shape) for s in range(4)], axis=-1)
    ada_own = jnp.stack([lax.dynamic_index_in_dim(ada_all, 8, 1, False),
                         lax.dynamic_index_in_dim(ada_all, dev, 1, False)], axis=1) + b_ada[:, None, :]
    mods = jnp.pad(ada_own.reshape(depth, 2, 6, d), ((0, 0), (0, 0), (0, 2), (0, 0)))

    cos, sin = _rope_tables(seq)
    xa = jnp.concatenate([ctx[0], x[0]], axis=0)
    cws = [_pad8(sc_conv_full[j]) for j in range(n_conv)]
    cwf = [_pad8(jnp.concatenate([ffn_conv_full[l], ffn_conv_b[l][None, :]], axis=0)) for l in range(depth)]
    gains = [_pad8(jnp.stack([attn_q_gain[j], attn_k_gain[j]])) for j in range(n_attn)]

    saved = []
    for l in range(depth):
        j, is_attn, mod = l // N_MIX, l % N_MIX == 0, mods[l]
        wc, wr, last = wcs[l], wrs[l], l == depth - 1
        sv = {"x_in": xa}
        h1 = norm_mod(f"norm1_{l}", xa, mod, 0, 1)
        if is_attn:
            qkv = mm_nn_col(f"qkv_{l}", h1, wc, lay.mix[l], lay.nb_mix[l])
            qr, kr = rope_fwd(f"rope_{l}", qkv, cos, sin, gains[j], dq, dkv)
            o, lse = attn_fwd(f"attn_{l}", qr, kr, qkv, attn_sink[j], seq)
            xa, y_m = mm_nn_row(f"wo_{l}", o, wr, lay.out, lay.kb_o, xa, mod, 2)
            sv.update(qkv=qkv, qr=qr, kr=kr, o=o, lse=lse)
        else:
            u = mm_nn_col(f"scin_{l}", h1, wc, lay.mix[l], lay.nb_mix[l])
            z = sc_gate_fwd(f"scgate_{l}", u, cws[j])
            xa, y_m = mm_nn_row(f"scout_{l}", z, wr, lay.out, lay.kb_o, xa, mod, 2)
            sv.update(u=u, z=z)
        h2 = norm_mod(f"norm2_{l}", xa, mod, 3, 4)
        if last:
            up = mm_nn_col(f"up_{l}", h2, wc, 0, lay.nb_up)
        else:
            up, (wcs[l + 1],) = mm_nn_col(f"up_{l}", h2, wc, 0, lay.nb_up, comm=gather_comm([wcs[l + 1]], (1,)))
        act = ffn_act_fwd(f"act_{l}", up, cwf[l])
        sv.update(h1=h1, y_m=y_m, x_mid=xa, h2=h2, up=up, act=act)
        if last:
            xa, y_f = mm_nn_row(f"down_{l}", act, wr, 0, lay.kb_dn, xa, mod, 5)
        else:
            (xa, y_f), (wrs[l + 1],) = mm_nn_row(f"down_{l}", act, wr, 0, lay.kb_dn, xa, mod, 5,
                                                 comm=gather_comm([wrs[l + 1]], (2,)))
        sv["y_f"] = y_f
        saved.append(sv)

    dx, dy, acc, lp = loss_head("loss", xa, loss_target[0], saved[-1]["y_f"], mods[-1], 5)
    loss = lax.psum(0.5 * jnp.sum(lp[0]) / d, ("x", "y", "c"))
    d_mod = [jnp.zeros((2, 6, d), F32) for _ in range(depth)]
    place = jnp.stack([chip, ac]).astype(jnp.int32)
    gf = {"up": lax.empty(ffn_w_up.shape, F32), "down": lax.empty(ffn_w_down.shape, F32),
          "qkv": lax.empty(attn_w_qkv.shape, F32), "wo": lax.empty(attn_w_o.shape, F32),
          "scin": lax.empty(sc_w_in.shape, F32), "scout": lax.empty(sc_w_out.shape, F32)}

    def chip_adds(l, hc, hr, rb_c, rb_r):
        j, mix = l // N_MIX, ("qkv", "wo") if l % N_MIX == 0 else ("scin", "scout")
        gf["up"] = chip_add(f"sum_up_{l}", hc, rb_c, place, gf["up"], l, 0, lay.nb_up, True)
        gf[mix[0]] = chip_add(f"sum_mix_{l}", hc, rb_c, place, gf[mix[0]], j, lay.mix[l], lay.nb_mix[l], True)
        gf["down"] = chip_add(f"sum_down_{l}", hr, rb_r, place, gf["down"], l, 0, lay.kb_dn, False)
        gf[mix[1]] = chip_add(f"sum_out_{l}", hr, rb_r, place, gf[mix[1]], j, lay.out, lay.kb_o, False)

    pending = None

    def add_mod(l, acc, idx):
        upd = jnp.zeros((2, 6, d), F32)
        for row, k in idx:
            upd = upd.at[:, k, :].set(acc[:, row, :])
        d_mod[l] = d_mod[l] + upd

    add_mod(depth - 1, acc, [(2, 5)])
    d_conv_f, d_conv_s = [None] * depth, [None] * n_conv
    d_gq, d_gk, d_sink = [None] * n_attn, [None] * n_attn, [None] * n_attn
    for l in reversed(range(depth)):
        j, is_attn, sv = l // N_MIX, l % N_MIX == 0, saved[l]
        wc, wr = wcs[l], wrs[l]
        gc, gr = lax.empty((4, d, lay.ct[l]), BF16), lax.empty((4, lay.rt, d), BF16)
        if pending is None:
            gr = mm_tn_row(f"g_down_{l}", sv["act"], dy, gr, 0, lay.kb_dn)
        else:
            gr, (rb_r,) = mm_tn_row(f"g_down_{l}", sv["act"], dy, gr, 0, lay.kb_dn, comm=scatter_comm([pending[2]]))
        da = mm_nt_row(f"d_act_{l}", dy, wr, 0, lay.kb_dn)
        d_up, d_conv_f[l] = ffn_act_bwd(f"act_bwd_{l}", sv["up"], da, cwf[l])
        if pending is None:
            gc = mm_tn_col(f"g_up_{l}", sv["h2"], d_up, gc, 0, lay.nb_up)
        else:
            gc, (rb_c,) = mm_tn_col(f"g_up_{l}", sv["h2"], d_up, gc, 0, lay.nb_up, comm=scatter_comm([pending[1]]))
            chip_adds(pending[0], pending[1], pending[2], rb_c, rb_r)
        dh2 = mm_nt_col(f"d_h2_{l}", d_up, wc, 0, lay.nb_up)
        dx, dy, acc = resid_bwd(f"norm2_bwd_{l}", dx, dh2, sv["x_mid"], mods[l], 3, 4, sv["y_m"], mods[l], 2)
        add_mod(l, acc, [(0, 3), (1, 4), (2, 2)])
        if is_attn:
            gr = mm_tn_row(f"g_wo_{l}", sv["o"], dy, gr, lay.out, lay.kb_o)
            do = mm_nt_row(f"d_o_{l}", dy, wr, lay.out, lay.kb_o)
            dqr, delta, dkc, dvc, dsk = attn_bwd_q(f"attn_bwd_q_{l}", sv["qr"], sv["kr"], sv["qkv"], attn_sink[j],
                                                   do, sv["o"], sv["lse"], seq)
            dkl, dvl = attn_bwd_kv(f"attn_bwd_kv_{l}", sv["qr"], sv["kr"], sv["qkv"], do, sv["lse"], delta, seq)
            dqkv, dgn = rope_bwd(f"rope_bwd_{l}", sv["qkv"], dqr, jnp.concatenate([dkc, dkl], axis=0),
                                 jnp.concatenate([dvc, dvl], axis=0), cos, sin, gains[j])
            d_gq[j], d_gk[j], d_sink[j] = dgn[0], dgn[1], dsk[0, :nh]
            gc = mm_tn_col(f"g_qkv_{l}", sv["h1"], dqkv, gc, lay.mix[l], lay.nb_mix[l])
            dh1 = mm_nt_col(f"d_h1_{l}", dqkv, wc, lay.mix[l], lay.nb_mix[l])
        else:
            gr = mm_tn_row(f"g_scout_{l}", sv["z"], dy, gr, lay.out, lay.kb_o)
            dz = mm_nt_row(f"d_z_{l}", dy, wr, lay.out, lay.kb_o)
            du, dcw = sc_gate_bwd(f"scgate_bwd_{l}", sv["u"], dz, cws[j])
            d_conv_s[j] = dcw[:3]
            gc = mm_tn_col(f"g_scin_{l}", sv["h1"], du, gc, lay.mix[l], lay.nb_mix[l])
            dh1 = mm_nt_col(f"d_h1_{l}", du, wc, lay.mix[l], lay.nb_mix[l])
        if l > 0:
            dx, dy, acc = resid_bwd(f"norm1_bwd_{l}", dx, dh1, sv["x_in"], mods[l], 0, 1, saved[l - 1]["y_f"], mods[l - 1], 5)
            add_mod(l - 1, acc, [(2, 5)])
        else:
            dx, acc = resid_bwd(f"norm1_bwd_{l}", dx, dh1, sv["x_in"], mods[l], 0, 1)
        add_mod(l, acc, [(0, 0), (1, 1)])
        ra_c, ra_r = pair_exchange(f"pair_exchange_{l}", [gc, gr], (1, 2))
        pending = (l, pair_add(f"pair_add_c_{l}", gc, ra_c, cidx, 1), pair_add(f"pair_add_r_{l}", gr, ra_r, cidx, 2))
    grad_x = dx[CTX:][None]
    rb_c, rb_r = chip_scatter("chip_scatter_0", [pending[1], pending[2]])
    chip_adds(pending[0], pending[1], pending[2], rb_c, rb_r)
    order = ["up", "down", "qkv", "wo", "scin", "scout"]
    joined = pair_join("pair_join", [gf[k] for k in order], (1, 2, 1, 2, 1, 2))
    gf = dict(zip(order, joined))

    d_ada = jnp.stack(d_mod).reshape(depth, 2, 6 * d)
    small = [d_ada, jnp.stack(d_gq), jnp.stack(d_gk), jnp.stack(d_sink), jnp.stack(d_conv_s),
             jnp.stack([t[:3] for t in d_conv_f]), jnp.stack([t[3] for t in d_conv_f])]
    g3 = gather_flat("gather_small", small)
    tot = sum8("sum_small", g3)
    sizes = [s.size for s in small]
    offs = [sum(sizes[:k]) for k in range(len(sizes) + 1)]
    part = lambda k: tot[offs[k]:offs[k + 1]].reshape(small[k].shape)
    g_b_ada = part(0)[:, 0] + part(0)[:, 1]
    g_q_gain, g_k_gain, g_sink = part(1), part(2), part(3)
    g_sc_conv = lax.dynamic_slice_in_dim(part(4), chip * sc_conv.shape[2], sc_conv.shape[2], 2)
    g_ffn_conv = lax.dynamic_slice_in_dim(part(5), chip * ffn_conv.shape[2], ffn_conv.shape[2], 2)
    g_conv_b = part(6)

    d_ada_all = g3[:, :d_ada.size].reshape(8, depth, 2, 6 * d)
    cols = lambda a: lax.dynamic_slice_in_dim(a, chip * nada, nada, a.ndim - 1)
    d_lat = cols(jnp.moveaxis(d_ada_all[:, :, 1], 0, 1))
    d_ctx = cols(part(0)[:, 0])
    rhs = jnp.concatenate([d_lat, d_ctx[:, None], jnp.zeros((depth, 7, nada), F32)], axis=1)
    g_w_ada = ada_grad_w("ada_grad_w", cond, rhs)
    dcc = ada_bwd_cond("ada_bwd_cond", jnp.pad(d_ctx[:, None], ((0, 0), (0, 7), (0, 0))), w_ada)[0]
    g4 = gather_flat("gather_dcc", [dcc])
    d_silu = g4[0, :d] + g4[2, :d] + g4[4, :d] + g4[6, :d]
    sg = _sigmoid(c_ctx)
    g_c_ctx = d_silu * (sg * (1.0 + c_ctx * (1.0 - sg)))

    def adam_rows(k, n):
        return _pick(k, max(8, ADAM_TILE_ELEMS // n), 8)

    def big(name, g, w, m, v):
        _, k, n = w.shape
        tr = adam_rows(k, n)
        return adamw(name, g, pl.BlockSpec((None, tr, n), lambda l, i: (l, i, 0)), w, m, v, tr)

    ada_tr = adam_rows(d, nada)
    res = {
        "c_ctx": adamw_small("adam_c_ctx", g_c_ctx, c_ctx, m_c_ctx, v_c_ctx),
        "w_ada": adamw("adam_w_ada", g_w_ada, pl.BlockSpec((None, ada_tr, nada), lambda l, i: (l, i, 0)), w_ada, m_w_ada, v_w_ada, ada_tr),
        "b_ada": adamw_small("adam_b_ada", g_b_ada, b_ada, m_b_ada, v_b_ada),
        "attn_w_qkv": big("adam_qkv", gf["qkv"], attn_w_qkv, m_attn_w_qkv, v_attn_w_qkv),
        "attn_w_o": big("adam_wo", gf["wo"], attn_w_o, m_attn_w_o, v_attn_w_o),
        "attn_q_gain": adamw_small("adam_q_gain", g_q_gain, attn_q_gain, m_attn_q_gain, v_attn_q_gain),
        "attn_k_gain": adamw_small("adam_k_gain", g_k_gain, attn_k_gain, m_attn_k_gain, v_attn_k_gain),
        "attn_sink": adamw_small("adam_sink", g_sink, attn_sink, m_attn_sink, v_attn_sink),
        "sc_w_in": big("adam_scin", gf["scin"], sc_w_in, m_sc_w_in, v_sc_w_in),
        "sc_conv": adamw_small("adam_sc_conv", g_sc_conv, sc_conv, m_sc_conv, v_sc_conv),
        "sc_w_out": big("adam_scout", gf["scout"], sc_w_out, m_sc_w_out, v_sc_w_out),
        "ffn_w_up": big("adam_up", gf["up"], ffn_w_up, m_ffn_w_up, v_ffn_w_up),
        "ffn_conv": adamw_small("adam_ffn_conv", g_ffn_conv, ffn_conv, m_ffn_conv, v_ffn_conv),
        "ffn_conv_b": adamw_small("adam_conv_b", g_conv_b, ffn_conv_b, m_ffn_conv_b, v_ffn_conv_b),
        "ffn_w_down": big("adam_down", gf["down"], ffn_w_down, m_ffn_w_down, v_ffn_w_down),
    }
    names = list(res)
    return (loss, grad_x, *[res[n][0] for n in names], *[res[n][1] for n in names],
            *[res[n][2] for n in names], *[res[n][3] for n in names])
```

```python
import functools

import jax
import jax.numpy as jnp
from jax import lax
from jax.experimental import pallas as pl
from jax.experimental.pallas import tpu as pltpu

F32, BF16 = jnp.float32, jnp.bfloat16
MESH = pl.DeviceIdType.MESH
VMEM_LIMIT = 56 * 1024 * 1024
LANE = 128
MXU = 256
WIDE = 2816
HALO = 16
HEAD = 128
GROUP = 4
CTX = 256
BLK = 128
WINDOW = 128
RB = 256
GRID_W = 64
ROPE_BASE = 10000.0
EPS = 1e-6
NEG = -1e30
N_MIX = 2
LR, B1, B2, ADAM_EPS, WD, STEP = 0.001, 0.9, 0.999, 1e-08, 0.01, 10
ADAM_TILE_ELEMS = 400 * 1024
NT = (((1,), (1,)), ((), ()))
TN = (((0,), (0,)), ((), ()))


def _pick(dim, target, mult=LANE):
    best = None
    for t in range(mult, min(dim, target) + 1, mult):
        if dim % t == 0:
            best = t
    return dim if best is None else best


def _cdiv(a, b):
    return -(-a // b)


def _params(sem):
    return pltpu.CompilerParams(dimension_semantics=sem, vmem_limit_bytes=VMEM_LIMIT)


def _sigmoid(g):
    return 1.0 / (1.0 + jnp.exp(-g))


def _row(v, r):
    rows = lax.broadcasted_iota(jnp.int32, v.shape, 0)
    return jnp.sum(jnp.where(rows == r, v, 0.0), axis=0, keepdims=True)


def _get_col(v, c):
    lanes = lax.broadcasted_iota(jnp.int32, v.shape, 1)
    return jnp.sum(jnp.where(lanes == c, v, 0.0), axis=1, keepdims=True)


def _put_col(v, c, col):
    lanes = lax.broadcasted_iota(jnp.int32, v.shape, 1)
    return jnp.where(lanes == c, col, v)


def _rows3(s0, s1, s2, width):
    rows = lax.broadcasted_iota(jnp.int32, (8, width), 0)
    z = jnp.zeros((8, width), F32)
    return jnp.where(rows == 0, s0, jnp.where(rows == 1, s1, jnp.where(rows == 2, s2, z)))


def _shift_rows(w, prev_row, next_row):
    n = w.shape[0]
    rows = lax.broadcasted_iota(jnp.int32, (n, 1), 0)
    down = jnp.where(rows == 0, prev_row, pltpu.roll(w, 1, 0))
    up = jnp.where(rows == n - 1, next_row, pltpu.roll(w, n - 1, 0))
    return down, up


def _seg_flags(i, nt):
    return i <= 1, (i == 0) | (i == nt - 1)


def _halo_specs(width, nrows):
    r = RB // HALO
    nh = nrows // HALO
    prev = pl.BlockSpec((HALO, width), lambda i: (jnp.maximum(i * r - 1, 0), 0))
    nxt = pl.BlockSpec((HALO, width), lambda i: (jnp.minimum((i + 1) * r, nh - 1), 0))
    return prev, nxt


class Layout:
    def __init__(self, d, ff, nqkv, depth):
        self.nb_up, self.kb_dn, self.kb_o = 2 * ff // 4, ff // 4, d // 4
        self.nb_mix = [(nqkv if l % N_MIX == 0 else 3 * d) // 4 for l in range(depth)]
        self.mix = [_cdiv(self.nb_up, nb) * nb for nb in self.nb_mix]
        self.ct = [m + nb for m, nb in zip(self.mix, self.nb_mix)]
        self.out = _cdiv(self.kb_dn, self.kb_o) * self.kb_o
        self.rt = _cdiv(self.out + self.kb_o, 2 * HALO) * 2 * HALO


def _mm_call(name, grid, in_specs, out_specs, out_shape, contract, operands, acc_shape, epilogue,
             n_extra=0, aliases=None, comm=None, merge_b=False):
    nk = grid[2]
    n_out = len(out_shape)
    n_cin = len(comm["ins"]) if comm else 0
    n_cout = len(comm["out_shape"]) if comm else 0
    aliases = dict(aliases or {})
    in_specs, out_specs, out_shape, operands = list(in_specs), list(out_specs), list(out_shape), list(operands)
    scratch = [] if nk == 1 else [pltpu.VMEM(acc_shape, F32)]
    if comm:
        for i_in, i_out in comm["aliases"].items():
            aliases[len(operands) + i_in] = n_out + i_out
        in_specs += [HBM_SPEC] * n_cin
        out_specs += [HBM_SPEC] * n_cout
        out_shape += comm["out_shape"]
        operands += comm["ins"]
        scratch += [pltpu.SemaphoreType.DMA((comm["n_sems"],)), pltpu.SemaphoreType.DMA((comm["n_sems"],))]

    def body(*refs):
        a_ref, b_ref = refs[0], refs[1]
        extra = refs[2:2 + n_extra]
        p = 2 + n_extra
        cin, outs = refs[p:p + n_cin], refs[p + n_cin:p + n_cin + n_out]
        couts = refs[p + n_cin + n_out:p + n_cin + n_out + n_cout]
        scr = refs[p + n_cin + n_out + n_cout:]
        ids = (pl.program_id(0), pl.program_id(1))
        k = pl.program_id(2)
        if comm:
            start, finish = comm["ops"](cin, couts, scr[-2], scr[-1])

            @pl.when((ids[0] == 0) & (ids[1] == 0) & (k == 0))
            def _():
                start()

        def part():
            b = b_ref[...]
            if merge_b:
                b = b.reshape(b.shape[0] * b.shape[1], b.shape[2])
            return lax.dot_general(a_ref[...], b, contract, preferred_element_type=F32)

        if nk == 1:
            epilogue(part(), extra, outs, ids)
        else:
            acc = scr[0]

            @pl.when(k == 0)
            def _():
                acc[...] = jnp.zeros_like(acc)

            acc[...] += part()

            @pl.when(k == nk - 1)
            def _():
                epilogue(acc[...], extra, outs, ids)

        if comm:
            @pl.when((ids[0] == grid[0] - 1) & (ids[1] == grid[1] - 1) & (k == nk - 1))
            def _():
                finish()

    sem = ("arbitrary",) * 3 if comm else ("parallel", "parallel", "arbitrary")
    res = pl.pallas_call(
        body, name=name, grid=grid, in_specs=in_specs, out_specs=out_specs, out_shape=out_shape,
        scratch_shapes=scratch, input_output_aliases=aliases, compiler_params=_params(sem))(*operands)
    return (res[:n_out], res[n_out:]) if comm else res


def _store(dtype):
    def epilogue(r, extra, outs, ids):
        outs[0][...] = r.astype(dtype)
    return epilogue


def _ret(res, comm, single=True):
    if comm:
        return (res[0][0] if single else res[0]), res[1]
    return res[0] if single else res


def _slots_per_step(s, kb):
    return max(n for n in (1, 2, 4) if s % n == 0 and (n == 1 or n * kb <= WIDE))


def mm_nn_col(name, a, wc, off, nb, comm=None):
    m, d = a.shape
    s = wc.shape[0]
    tn = _pick(nb, WIDE, MXU)
    tm = _pick(m, 528 if tn > 1536 else 1056, HALO)
    npb, ob = nb // tn, off // tn
    assert off % tn == 0
    return _ret(_mm_call(
        name, (s * npb, m // tm, 1),
        [pl.BlockSpec((tm, d), lambda j, i, k: (i, 0)),
         pl.BlockSpec((None, d, tn), lambda j, i, k: (j // npb, 0, ob + j % npb))],
        [pl.BlockSpec((tm, tn), lambda j, i, k: (i, j))],
        [jax.ShapeDtypeStruct((m, s * nb), BF16)],
        (((1,), (0,)), ((), ())), (a, wc), None, _store(BF16), comm=comm), comm)


def mm_nn_row(name, a, wr, off, kb, res, mod, gi, comm=None):
    m = a.shape[0]
    s, _, d = wr.shape
    sp = _slots_per_step(s, kb)
    tm, tn = _pick(m, 1056, HALO), _pick(d, 1024)
    ob = off // kb
    assert off % kb == 0

    def epilogue(r, extra, outs, ids):
        res_ref, mod_ref = extra
        rows = ids[0] * tm + lax.broadcasted_iota(jnp.int32, (tm, 1), 0)
        g = jnp.where(rows < CTX, mod_ref[0, gi:gi + 1, :], mod_ref[1, gi:gi + 1, :])
        outs[0][...] = res_ref[...] + g * r
        outs[1][...] = r.astype(BF16)

    return _ret(_mm_call(
        name, (m // tm, d // tn, s // sp),
        [pl.BlockSpec((tm, sp * kb), lambda i, j, k: (i, k)),
         pl.BlockSpec((sp, kb, tn), lambda i, j, k: (k, ob, j)),
         pl.BlockSpec((tm, tn), lambda i, j, k: (i, j)),
         pl.BlockSpec((2, 8, tn), lambda i, j, k: (0, 0, j))],
        [pl.BlockSpec((tm, tn), lambda i, j, k: (i, j)), pl.BlockSpec((tm, tn), lambda i, j, k: (i, j))],
        [jax.ShapeDtypeStruct((m, d), F32), jax.ShapeDtypeStruct((m, d), BF16)],
        (((1,), (0,)), ((), ())), (a, wr, res, mod), (tm, tn), epilogue, n_extra=2, comm=comm, merge_b=True),
        comm, single=False)


def mm_nt_col(name, dy, wc, off, nb):
    m = dy.shape[0]
    s, d, _ = wc.shape
    tc = _pick(nb, WIDE, MXU)
    tm = _pick(m, 528 if tc > 1536 else 768, HALO)
    npb, ob = nb // tc, off // tc
    return _mm_call(
        name, (m // tm, 1, s * npb),
        [pl.BlockSpec((tm, tc), lambda i, j, k: (i, k)),
         pl.BlockSpec((None, d, tc), lambda i, j, k: (k // npb, 0, ob + k % npb))],
        [pl.BlockSpec((tm, d), lambda i, j, k: (i, 0))],
        [jax.ShapeDtypeStruct((m, d), F32)],
        NT, (dy, wc), (tm, d), _store(F32))[0]


def mm_nt_row(name, dy, wr, off, kb):
    m, d = dy.shape
    s = wr.shape[0]
    sp = _slots_per_step(s, kb)
    tm = _pick(m, 528 if sp * kb > 2048 else 1056, HALO)
    ob = off // kb
    return _mm_call(
        name, (s // sp, m // tm, 1),
        [pl.BlockSpec((tm, d), lambda j, i, k: (i, 0)),
         pl.BlockSpec((sp, kb, d), lambda j, i, k: (j, ob, 0))],
        [pl.BlockSpec((tm, sp * kb), lambda j, i, k: (i, j))],
        [jax.ShapeDtypeStruct((m, s * kb), BF16)],
        NT, (dy, wr), None, _store(BF16), merge_b=True)[0]


def mm_tn_col(name, a, dy, gbuf, off, nb, comm=None):
    t, d = a.shape
    s = gbuf.shape[0]
    tn, tt = _pick(nb, WIDE, MXU), _pick(t, 768, MXU)
    tka = _pick(d, 1024 if tn > 1536 else 2048)
    npb, ob = nb // tn, off // tn
    return _ret(_mm_call(
        name, (d // tka, s * npb, t // tt),
        [pl.BlockSpec((tt, tka), lambda i, j, k: (k, i)),
         pl.BlockSpec((tt, tn), lambda i, j, k: (k, j)),
         pl.BlockSpec(memory_space=pl.ANY)],
        [pl.BlockSpec((None, tka, tn), lambda i, j, k: (j // npb, i, ob + j % npb))],
        [jax.ShapeDtypeStruct(gbuf.shape, BF16)],
        TN, (a, dy, gbuf), (tka, tn), _store(BF16), n_extra=1, aliases={2: 0}, comm=comm), comm)


def mm_tn_row(name, act, dy, gbuf, off, kb, comm=None):
    t, d = dy.shape
    s = gbuf.shape[0]
    tt = _pick(t, 768, MXU)
    ob = off // kb
    return _ret(_mm_call(
        name, (s, 1, t // tt),
        [pl.BlockSpec((tt, kb), lambda i, j, k: (k, i)),
         pl.BlockSpec((tt, d), lambda i, j, k: (k, 0)),
         pl.BlockSpec(memory_space=pl.ANY)],
        [pl.BlockSpec((None, kb, d), lambda i, j, k: (i, ob, 0))],
        [jax.ShapeDtypeStruct(gbuf.shape, BF16)],
        TN, (act, dy, gbuf), (kb, d), _store(BF16), n_extra=1, aliases={2: 0}, comm=comm), comm)


def cast_pack(name, w, l, buf, off, col, chip):
    _, k, n = w.shape
    if col:
        tr = _pick(k, 512, HALO)
        ob = off // n
        out_spec = pl.BlockSpec((None, tr, n), lambda i, s: (s[0], i, ob))
    else:
        tr = _pick(k, 704, HALO)
        ob = off // tr
        out_spec = pl.BlockSpec((None, tr, n), lambda i, s: (s[0], ob + i, 0))

    def body(s_ref, w_ref, buf_ref, out_ref):
        out_ref[...] = w_ref[...].astype(BF16)

    return pl.pallas_call(
        body, name=name,
        grid_spec=pltpu.PrefetchScalarGridSpec(
            num_scalar_prefetch=1, grid=(k // tr,),
            in_specs=[pl.BlockSpec((None, tr, n), lambda i, s: (l, i, 0)), pl.BlockSpec(memory_space=pl.ANY)],
            out_specs=out_spec),
        out_shape=jax.ShapeDtypeStruct(buf.shape, BF16),
        input_output_aliases={2: 0}, compiler_params=_params(("parallel",)))(chip, w, buf)


def norm_mod(name, x, mod, sh, sc):
    t, d = x.shape

    def body(x_ref, mod_ref, h_ref):
        seg = jnp.minimum(pl.program_id(0), 1)
        xv = x_ref[...]
        r = lax.rsqrt(jnp.mean(xv * xv, axis=-1, keepdims=True) + EPS)
        m = mod_ref[seg]
        h_ref[...] = ((xv * r) * (1.0 + m[sc:sc + 1, :]) + m[sh:sh + 1, :]).astype(BF16)

    return pl.pallas_call(
        body, name=name, grid=(t // RB,),
        in_specs=[pl.BlockSpec((RB, d), lambda i: (i, 0)), pl.BlockSpec((2, 8, d), lambda i: (0, 0, 0))],
        out_specs=pl.BlockSpec((RB, d), lambda i: (i, 0)),
        out_shape=jax.ShapeDtypeStruct((t, d), BF16), compiler_params=_params(("parallel",)))(x, mod)


def sc_gate_fwd(name, u, cw):
    t = u.shape[0]
    d = u.shape[1] // 3
    nt, tc = t // RB, _pick(d, 512)
    prev, nxt = _halo_specs(3 * d, t)

    def body(u_ref, up_ref, un_ref, cw_ref, z_ref):
        first, last = _seg_flags(pl.program_id(0), nt)
        for j in range(d // tc):
            c0 = j * tc
            gb = u_ref[:, c0:c0 + tc].astype(F32)
            w = u_ref[:, d + c0:d + c0 + tc].astype(F32) * u_ref[:, 2 * d + c0:2 * d + c0 + tc].astype(F32)
            pw = _row(up_ref[:, d + c0:d + c0 + tc].astype(F32) * up_ref[:, 2 * d + c0:2 * d + c0 + tc].astype(F32), HALO - 1)
            nw = _row(un_ref[:, d + c0:d + c0 + tc].astype(F32) * un_ref[:, 2 * d + c0:2 * d + c0 + tc].astype(F32), 0)
            wd, wu = _shift_rows(w, jnp.where(first, 0.0, pw), jnp.where(last, 0.0, nw))
            cwj = cw_ref[:, c0:c0 + tc]
            conv = wd * cwj[0:1] + w * cwj[1:2] + wu * cwj[2:3]
            z_ref[:, c0:c0 + tc] = (gb * conv).astype(BF16)

    return pl.pallas_call(
        body, name=name, grid=(nt,),
        in_specs=[pl.BlockSpec((RB, 3 * d), lambda i: (i, 0)), prev, nxt, pl.BlockSpec((8, d), lambda i: (0, 0))],
        out_specs=pl.BlockSpec((RB, d), lambda i: (i, 0)),
        out_shape=jax.ShapeDtypeStruct((t, d), BF16), compiler_params=_params(("parallel",)))(u, u, u, cw)


def sc_gate_bwd(name, u, dz, cw):
    t = u.shape[0]
    d = u.shape[1] // 3
    nt, tc = t // RB, _pick(d, 512)
    prev, nxt = _halo_specs(3 * d, t)
    dprev, dnxt = _halo_specs(d, t)

    def body(u_ref, up_ref, un_ref, dz_ref, dzp_ref, dzn_ref, cw_ref, du_ref, dcw_ref):
        i = pl.program_id(0)
        first, last = _seg_flags(i, nt)

        @pl.when(i == 0)
        def _():
            dcw_ref[...] = jnp.zeros_like(dcw_ref)

        for j in range(d // tc):
            c0 = j * tc
            sl0, sl1, sl2 = slice(c0, c0 + tc), slice(d + c0, d + c0 + tc), slice(2 * d + c0, 2 * d + c0 + tc)
            gb, gc, v = u_ref[:, sl0].astype(F32), u_ref[:, sl1].astype(F32), u_ref[:, sl2].astype(F32)
            w = gc * v
            pw = _row(up_ref[:, sl1].astype(F32) * up_ref[:, sl2].astype(F32), HALO - 1)
            nw = _row(un_ref[:, sl1].astype(F32) * un_ref[:, sl2].astype(F32), 0)
            wd, wu = _shift_rows(w, jnp.where(first, 0.0, pw), jnp.where(last, 0.0, nw))
            cwj = cw_ref[:, sl0]
            cw0, cw1, cw2 = cwj[0:1], cwj[1:2], cwj[2:3]
            dzv = dz_ref[:, sl0].astype(F32)
            e = dzv * gb
            pe = _row(dzp_ref[:, sl0].astype(F32) * up_ref[:, sl0].astype(F32), HALO - 1)
            ne = _row(dzn_ref[:, sl0].astype(F32) * un_ref[:, sl0].astype(F32), 0)
            ed, eu = _shift_rows(e, jnp.where(first, 0.0, pe), jnp.where(last, 0.0, ne))
            dw = cw0 * eu + cw1 * e + cw2 * ed
            du_ref[:, sl0] = (dzv * (wd * cw0 + w * cw1 + wu * cw2)).astype(BF16)
            du_ref[:, sl1] = (dw * v).astype(BF16)
            du_ref[:, sl2] = (dw * gc).astype(BF16)
            dcw_ref[:, sl0] += _rows3(jnp.sum(e * wd, axis=0, keepdims=True), jnp.sum(e * w, axis=0, keepdims=True),
                                      jnp.sum(e * wu, axis=0, keepdims=True), tc)

    return pl.pallas_call(
        body, name=name, grid=(nt,),
        in_specs=[pl.BlockSpec((RB, 3 * d), lambda i: (i, 0)), prev, nxt,
                  pl.BlockSpec((RB, d), lambda i: (i, 0)), dprev, dnxt, pl.BlockSpec((8, d), lambda i: (0, 0))],
        out_specs=[pl.BlockSpec((RB, 3 * d), lambda i: (i, 0)), pl.BlockSpec((8, d), lambda i: (0, 0))],
        out_shape=[jax.ShapeDtypeStruct((t, 3 * d), BF16), jax.ShapeDtypeStruct((8, d), F32)],
        compiler_params=_params(("arbitrary",)))(u, u, u, dz, dz, dz, cw)


def ffn_act_fwd(name, up, cw):
    t = up.shape[0]
    ff = up.shape[1] // 2
    nt, tc = t // RB, _pick(ff, 1408)
    prev, nxt = _halo_specs(2 * ff, t)

    def body(up_ref, upp_ref, upn_ref, cw_ref, a_ref):
        first, last = _seg_flags(pl.program_id(0), nt)
        for j in range(ff // tc):
            sg, sv = slice(j * tc, (j + 1) * tc), slice(ff + j * tc, ff + (j + 1) * tc)
            gate = up_ref[:, sg].astype(F32)
            pg = _row(upp_ref[:, sg].astype(F32), HALO - 1)
            ng = _row(upn_ref[:, sg].astype(F32), 0)
            gd, gu = _shift_rows(gate, jnp.where(first, 0.0, pg), jnp.where(last, 0.0, ng))
            cwj = cw_ref[:, sg]
            g = gd * cwj[0:1] + gate * cwj[1:2] + gu * cwj[2:3] + cwj[3:4]
            a_ref[:, sg] = (g * _sigmoid(g) * up_ref[:, sv].astype(F32)).astype(BF16)

    return pl.pallas_call(
        body, name=name, grid=(nt,),
        in_specs=[pl.BlockSpec((RB, 2 * ff), lambda i: (i, 0)), prev, nxt, pl.BlockSpec((8, ff), lambda i: (0, 0))],
        out_specs=pl.BlockSpec((RB, ff), lambda i: (i, 0)),
        out_shape=jax.ShapeDtypeStruct((t, ff), BF16), compiler_params=_params(("parallel",)))(up, up, up, cw)


def ffn_act_bwd(name, up, da, cw):
    t = up.shape[0]
    ff = up.shape[1] // 2
    nt, tc = t // RB, _pick(ff, 1408)
    prev, nxt = _halo_specs(2 * ff, t)
    dprev, dnxt = _halo_specs(ff, t)

    def dsilu(g):
        s = _sigmoid(g)
        return s * (1.0 + g * (1.0 - s))

    def body(up_ref, upp_ref, upn_ref, da_ref, dap_ref, dan_ref, cw_ref, dup_ref, acc_ref):
        i = pl.program_id(0)
        first, last = _seg_flags(i, nt)

        @pl.when(i == 0)
        def _():
            acc_ref[...] = jnp.zeros_like(acc_ref)

        for j in range(ff // tc):
            sg, sv = slice(j * tc, (j + 1) * tc), slice(ff + j * tc, ff + (j + 1) * tc)
            gate, val, dav = up_ref[:, sg].astype(F32), up_ref[:, sv].astype(F32), da_ref[:, sg].astype(F32)
            pgt, ngt = upp_ref[:, sg].astype(F32), upn_ref[:, sg].astype(F32)
            pg1, pg2 = _row(pgt, HALO - 1), _row(pgt, HALO - 2)
            ng1, ng2 = _row(ngt, 0), _row(ngt, 1)
            cwj = cw_ref[:, sg]
            cw0, cw1, cw2, b = cwj[0:1], cwj[1:2], cwj[2:3], cwj[3:4]
            gd, gu = _shift_rows(gate, jnp.where(first, 0.0, pg1), jnp.where(last, 0.0, ng1))
            g = gd * cw0 + gate * cw1 + gu * cw2 + b
            g_p = pg2 * cw0 + pg1 * cw1 + _row(gate, 0) * cw2 + b
            g_n = _row(gate, RB - 1) * cw0 + ng1 * cw1 + ng2 * cw2 + b
            dg = dav * val * dsilu(g)
            dg_p = _row(dap_ref[:, sg].astype(F32) * upp_ref[:, sv].astype(F32), HALO - 1) * dsilu(g_p)
            dg_n = _row(dan_ref[:, sg].astype(F32) * upn_ref[:, sv].astype(F32), 0) * dsilu(g_n)
            dgd, dgu = _shift_rows(dg, jnp.where(first, 0.0, dg_p), jnp.where(last, 0.0, dg_n))
            dup_ref[:, sg] = (cw0 * dgu + cw1 * dg + cw2 * dgd).astype(BF16)
            dup_ref[:, sv] = (dav * g * _sigmoid(g)).astype(BF16)
            rows = lax.broadcasted_iota(jnp.int32, (8, tc), 0)
            acc_ref[:, sg] += (_rows3(jnp.sum(dg * gd, axis=0, keepdims=True), jnp.sum(dg * gate, axis=0, keepdims=True),
                                      jnp.sum(dg * gu, axis=0, keepdims=True), tc)
                               + jnp.where(rows == 3, jnp.sum(dg, axis=0, keepdims=True), 0.0))

    return pl.pallas_call(
        body, name=name, grid=(nt,),
        in_specs=[pl.BlockSpec((RB, 2 * ff), lambda i: (i, 0)), prev, nxt,
                  pl.BlockSpec((RB, ff), lambda i: (i, 0)), dprev, dnxt, pl.BlockSpec((8, ff), lambda i: (0, 0))],
        out_specs=[pl.BlockSpec((RB, 2 * ff), lambda i: (i, 0)), pl.BlockSpec((8, ff), lambda i: (0, 0))],
        out_shape=[jax.ShapeDtypeStruct((t, 2 * ff), BF16), jax.ShapeDtypeStruct((8, ff), F32)],
        compiler_params=_params(("arbitrary",)))(up, up, up, da, da, da, cw)


def _rot(z):
    w = z.shape[1]
    lane = lax.broadcasted_iota(jnp.int32, z.shape, 1)
    return jnp.where((lane % 64) < 32, -pltpu.roll(z, w - 32, 1), pltpu.roll(z, 32, 1))


def rope_fwd(name, qkv, cos, sin, gains, dq, dkv):
    t, nqkv = qkv.shape
    nh, nkv = dq // HEAD, dkv // HEAD

    def body(qkv_ref, cos_ref, sin_ref, g_ref, qr_ref, kr_ref):
        cs, sn = cos_ref[...], sin_ref[...]
        for hd in range(nh + nkv):
            c0 = hd * HEAD
            xh = qkv_ref[:, c0:c0 + HEAD].astype(F32)
            r = lax.rsqrt(jnp.mean(xh * xh, axis=-1, keepdims=True) + EPS)
            y = xh * r * (g_ref[0:1, :] if hd < nh else g_ref[1:2, :])
            yr = (y * cs + _rot(y) * sn).astype(BF16)
            if hd < nh:
                qr_ref[:, c0:c0 + HEAD] = yr
            else:
                kr_ref[:, c0 - dq:c0 - dq + HEAD] = yr

    return pl.pallas_call(
        body, name=name, grid=(t // RB,),
        in_specs=[pl.BlockSpec((RB, nqkv), lambda i: (i, 0)), pl.BlockSpec((RB, HEAD), lambda i: (i, 0)),
                  pl.BlockSpec((RB, HEAD), lambda i: (i, 0)), pl.BlockSpec((8, HEAD), lambda i: (0, 0))],
        out_specs=[pl.BlockSpec((RB, dq), lambda i: (i, 0)), pl.BlockSpec((RB, dkv), lambda i: (i, 0))],
        out_shape=[jax.ShapeDtypeStruct((t, dq), BF16), jax.ShapeDtypeStruct((t, dkv), BF16)],
        compiler_params=_params(("parallel",)))(qkv, cos, sin, gains)


def rope_bwd(name, qkv, dqr, dkr, dv, cos, sin, gains):
    t, nqkv = qkv.shape
    dq, dkv = dqr.shape[1], dkr.shape[1]
    nh, nkv = dq // HEAD, dkv // HEAD

    def body(qkv_ref, dq_ref, dk_ref, dv_ref, cos_ref, sin_ref, g_ref, out_ref, dg_ref):
        @pl.when(pl.program_id(0) == 0)
        def _():
            dg_ref[...] = jnp.zeros_like(dg_ref)

        cs, sn = cos_ref[...], sin_ref[...]
        zero = jnp.zeros((1, HEAD), F32)
        gq, gk = zero, zero
        for hd in range(nh + nkv):
            c0 = hd * HEAD
            xh = qkv_ref[:, c0:c0 + HEAD].astype(F32)
            r = lax.rsqrt(jnp.mean(xh * xh, axis=-1, keepdims=True) + EPS)
            xhat = xh * r
            dy = dq_ref[:, c0:c0 + HEAD] if hd < nh else dk_ref[:, c0 - dq:c0 - dq + HEAD]
            tt = dy * cs - _rot(dy * sn)
            gsum = jnp.sum(tt * xhat, axis=0, keepdims=True)
            if hd < nh:
                gq = gq + gsum
            else:
                gk = gk + gsum
            dxh = tt * (g_ref[0:1, :] if hd < nh else g_ref[1:2, :])
            dx = r * (dxh - xhat * jnp.mean(dxh * xhat, axis=-1, keepdims=True))
            out_ref[:, c0:c0 + HEAD] = dx.astype(BF16)
        out_ref[:, dq + dkv:] = dv_ref[...].astype(BF16)
        dg_ref[...] += _rows3(gq, gk, zero, HEAD)

    return pl.pallas_call(
        body, name=name, grid=(t // RB,),
        in_specs=[pl.BlockSpec((RB, nqkv), lambda i: (i, 0)), pl.BlockSpec((RB, dq), lambda i: (i, 0)),
                  pl.BlockSpec((RB, dkv), lambda i: (i, 0)), pl.BlockSpec((RB, dkv), lambda i: (i, 0)),
                  pl.BlockSpec((RB, HEAD), lambda i: (i, 0)), pl.BlockSpec((RB, HEAD), lambda i: (i, 0)),
                  pl.BlockSpec((8, HEAD), lambda i: (0, 0))],
        out_specs=[pl.BlockSpec((RB, nqkv), lambda i: (i, 0)), pl.BlockSpec((8, HEAD), lambda i: (0, 0))],
        out_shape=[jax.ShapeDtypeStruct((t, nqkv), BF16), jax.ShapeDtypeStruct((8, HEAD), F32)],
        compiler_params=_params(("arbitrary",)))(qkv, dqr, dkr, dv, cos, sin, gains)


def resid_bwd(name, dx, dh, x, mod_n, sh, sc, y_prev=None, mod_g=None, gi=0):
    t, d = x.shape
    has_prev = y_prev is not None

    def body(*refs):
        if has_prev:
            dx_ref, dh_ref, x_ref, mn_ref, y_ref, mg_ref, dxo_ref, dy_ref, acc_ref = refs
        else:
            dx_ref, dh_ref, x_ref, mn_ref, dxo_ref, acc_ref = refs
        i = pl.program_id(0)
        seg = jnp.minimum(i, 1)

        @pl.when(i == 0)
        def _():
            acc_ref[...] = jnp.zeros_like(acc_ref)

        xv, dhv = x_ref[...], dh_ref[...]
        r = lax.rsqrt(jnp.mean(xv * xv, axis=-1, keepdims=True) + EPS)
        xhat = xv * r
        m = mn_ref[seg]
        dxh = dhv * (1.0 + m[sc:sc + 1, :])
        dxo = dx_ref[...] + r * (dxh - xhat * jnp.mean(dxh * xhat, axis=-1, keepdims=True))
        dxo_ref[...] = dxo
        s2 = jnp.zeros((1, d), F32)
        if has_prev:
            dy_ref[...] = (mg_ref[seg][gi:gi + 1, :] * dxo).astype(BF16)
            s2 = jnp.sum(dxo * y_ref[...].astype(F32), axis=0, keepdims=True)
        acc_ref[seg] = acc_ref[seg] + _rows3(jnp.sum(dhv, axis=0, keepdims=True),
                                             jnp.sum(dhv * xhat, axis=0, keepdims=True), s2, d)

    row = pl.BlockSpec((RB, d), lambda i: (i, 0))
    modspec = pl.BlockSpec((2, 8, d), lambda i: (0, 0, 0))
    in_specs, operands = [row, row, row, modspec], [dx, dh, x, mod_n]
    out_specs, out_shape = [row], [jax.ShapeDtypeStruct((t, d), F32)]
    if has_prev:
        in_specs += [row, modspec]
        operands += [y_prev, mod_g]
        out_specs.append(row)
        out_shape.append(jax.ShapeDtypeStruct((t, d), BF16))
    out_specs.append(modspec)
    out_shape.append(jax.ShapeDtypeStruct((2, 8, d), F32))
    return pl.pallas_call(body, name=name, grid=(t // RB,), in_specs=in_specs, out_specs=out_specs,
                          out_shape=out_shape, compiler_params=_params(("arbitrary",)))(*operands)


def loss_head(name, xf, target, y_last, mod, gi):
    t, d = xf.shape

    def body(x_ref, t_ref, y_ref, mod_ref, dx_ref, dy_ref, acc_ref, lp_ref):
        i = pl.program_id(0)
        seg = jnp.minimum(i, 1)

        @pl.when(i == 0)
        def _():
            acc_ref[...] = jnp.zeros_like(acc_ref)
            lp_ref[...] = jnp.zeros_like(lp_ref)

        lat = i >= 1
        err = jnp.where(lat, x_ref[...] - t_ref[...], 0.0)
        dxv = err / d
        dx_ref[...] = dxv
        dy_ref[...] = (mod_ref[seg][gi:gi + 1, :] * dxv).astype(BF16)
        zero = jnp.zeros((1, d), F32)
        lp_ref[...] += _rows3(jnp.sum(err * err, axis=0, keepdims=True), zero, zero, d)
        acc_ref[seg] = acc_ref[seg] + _rows3(zero, zero, jnp.sum(dxv * y_ref[...].astype(F32), axis=0, keepdims=True), d)

    row = pl.BlockSpec((RB, d), lambda i: (i, 0))
    modspec = pl.BlockSpec((2, 8, d), lambda i: (0, 0, 0))
    return pl.pallas_call(
        body, name=name, grid=(t // RB,),
        in_specs=[row, pl.BlockSpec((RB, d), lambda i: (jnp.maximum(i - 1, 0), 0)), row, modspec],
        out_specs=[row, row, modspec, pl.BlockSpec((8, d), lambda i: (0, 0))],
        out_shape=[jax.ShapeDtypeStruct((t, d), F32), jax.ShapeDtypeStruct((t, d), BF16),
                   jax.ShapeDtypeStruct((2, 8, d), F32), jax.ShapeDtypeStruct((8, d), F32)],
        compiler_params=_params(("arbitrary",)))(xf, target, y_last, mod)


def _kv_specs(width, colblk, nbk):
    return [pl.BlockSpec((CTX, width), lambda i: (0, colblk)),
            pl.BlockSpec((BLK, width), lambda i: (jnp.maximum(i - 1, 0), colblk)),
            pl.BlockSpec((BLK, width), lambda i: (i, colblk)),
            pl.BlockSpec((BLK, width), lambda i: (jnp.minimum(i + 1, nbk - 1), colblk))]


def _band_mask(i, seq):
    nk = CTX + 3 * BLK
    qrow = lax.broadcasted_iota(jnp.int32, (GROUP * BLK, nk), 0) % BLK
    col = lax.broadcasted_iota(jnp.int32, (GROUP * BLK, nk), 1)
    cb = col - CTX
    kpos = (i - 3) * BLK + cb
    band = (i >= 2) & (jnp.abs(BLK + qrow - cb) <= WINDOW) & (kpos >= 0) & (kpos < seq)
    return (col < CTX) | band


def _stack_heads(ref, h):
    return jnp.concatenate([ref[:, (h * GROUP + g) * HEAD:(h * GROUP + g + 1) * HEAD] for g in range(GROUP)], axis=0)


def _stack_cols(v, h):
    return jnp.concatenate([_get_col(v, h * GROUP + g) for g in range(GROUP)], axis=0)


def _sink_col(sink_ref, h):
    rowg = lax.broadcasted_iota(jnp.int32, (GROUP * BLK, 1), 0) // BLK
    sk = jnp.full((GROUP * BLK, 1), sink_ref[h * GROUP], F32)
    for g in range(1, GROUP):
        sk = jnp.where(rowg == g, sink_ref[h * GROUP + g], sk)
    return sk


def attn_fwd(name, qr, kr, qkv, sink, seq):
    t, dq = qr.shape
    dkv = kr.shape[1]
    nbk, nkv = t // BLK, dkv // HEAD
    vcol = (dq + dkv) // dkv
    scale = HEAD ** -0.5

    def body(sink_ref, q_ref, kc, kp, ko, kn, vc, vp, vo, vn, o_ref, lse_ref, lset_ref):
        i = pl.program_id(0)
        mask = _band_mask(i, seq)
        lse = jnp.zeros((BLK, LANE), F32)
        for h in range(nkv):
            hs = slice(h * HEAD, (h + 1) * HEAD)
            k = jnp.concatenate([kc[:, hs], kp[:, hs], ko[:, hs], kn[:, hs]], axis=0)
            v = jnp.concatenate([vc[:, hs], vp[:, hs], vo[:, hs], vn[:, hs]], axis=0)
            q4 = _stack_heads(q_ref, h)
            s = jnp.where(mask, lax.dot_general(q4, k, NT, preferred_element_type=F32) * scale, NEG)
            sk = _sink_col(sink_ref, h)
            m = jnp.maximum(jnp.max(s, axis=-1, keepdims=True), sk)
            e = jnp.exp(s - m)
            den = jnp.sum(e, axis=-1, keepdims=True) + jnp.exp(sk - m)
            o4 = jnp.dot((e / den).astype(BF16), v, preferred_element_type=F32)
            l4 = m + jnp.log(den)
            for g in range(GROUP):
                hg = h * GROUP + g
                o_ref[:, hg * HEAD:(hg + 1) * HEAD] = o4[g * BLK:(g + 1) * BLK].astype(BF16)
                lse = _put_col(lse, hg, l4[g * BLK:(g + 1) * BLK])
        lse_ref[...] = lse
        lset_ref[...] = lse.T[:nh]

    nh = dq // HEAD
    return pl.pallas_call(
        body, name=name, grid=(nbk,),
        in_specs=[pl.BlockSpec(memory_space=pltpu.SMEM), pl.BlockSpec((BLK, dq), lambda i: (i, 0))]
        + _kv_specs(dkv, 0, nbk) + _kv_specs(dkv, vcol, nbk),
        out_specs=[pl.BlockSpec((BLK, dq), lambda i: (i, 0)), pl.BlockSpec((BLK, LANE), lambda i: (i, 0)),
                   pl.BlockSpec((nh, BLK), lambda i: (0, i))],
        out_shape=[jax.ShapeDtypeStruct((t, dq), BF16), jax.ShapeDtypeStruct((t, LANE), F32),
                   jax.ShapeDtypeStruct((nh, t), F32)],
        compiler_params=_params(("parallel",)))(sink, qr, kr, kr, kr, kr, qkv, qkv, qkv, qkv)


def attn_bwd_q(name, qr, kr, qkv, sink, do, o, lse, seq):
    t, dq = qr.shape
    dkv = kr.shape[1]
    nbk, nkv = t // BLK, dkv // HEAD
    vcol = (dq + dkv) // dkv
    scale = HEAD ** -0.5

    def body(sink_ref, q_ref, kc, kp, ko, kn, vc, vp, vo, vn, do_ref, o_ref, lse_ref,
             dq_ref, dl_ref, dkc_ref, dvc_ref, ds_ref):
        i = pl.program_id(0)

        @pl.when(i == 0)
        def _():
            dkc_ref[...] = jnp.zeros_like(dkc_ref)
            dvc_ref[...] = jnp.zeros_like(dvc_ref)
            ds_ref[...] = jnp.zeros_like(ds_ref)

        mask = _band_mask(i, seq)
        lse = lse_ref[...]
        delta = jnp.zeros((BLK, LANE), F32)
        dsink = jnp.zeros((8, LANE), F32)
        for h in range(nkv):
            hs = slice(h * HEAD, (h + 1) * HEAD)
            k = jnp.concatenate([kc[:, hs], kp[:, hs], ko[:, hs], kn[:, hs]], axis=0)
            v = jnp.concatenate([vc[:, hs], vp[:, hs], vo[:, hs], vn[:, hs]], axis=0)
            q4, do4 = _stack_heads(q_ref, h), _stack_heads(do_ref, h)
            d4 = jnp.sum(do4.astype(F32) * _stack_heads(o_ref, h).astype(F32), axis=-1, keepdims=True)
            l4 = _stack_cols(lse, h)
            s = jnp.where(mask, lax.dot_general(q4, k, NT, preferred_element_type=F32) * scale, NEG)
            p = jnp.exp(s - l4)
            dp = lax.dot_general(do4, v, NT, preferred_element_type=F32)
            dsb = (p * (dp - d4) * scale).astype(BF16)
            pb = p.astype(BF16)
            dq4 = jnp.dot(dsb, k, preferred_element_type=F32)
            dkc_ref[:, hs] += lax.dot_general(dsb[:, :CTX], q4, TN, preferred_element_type=F32)
            dvc_ref[:, hs] += lax.dot_general(pb[:, :CTX], do4, TN, preferred_element_type=F32)
            dsk = -jnp.exp(_sink_col(sink_ref, h) - l4) * d4
            for g in range(GROUP):
                hg = h * GROUP + g
                rs = slice(g * BLK, (g + 1) * BLK)
                dq_ref[:, hg * HEAD:(hg + 1) * HEAD] = dq4[rs]
                delta = _put_col(delta, hg, d4[rs])
                dsink = _put_col(dsink, hg, jnp.sum(dsk[rs], axis=0, keepdims=True))
        dl_ref[...] = delta.T[:nh]
        rows = lax.broadcasted_iota(jnp.int32, (8, LANE), 0)
        ds_ref[...] += jnp.where(rows == 0, dsink, 0.0)

    nh = dq // HEAD
    blk = lambda w: pl.BlockSpec((BLK, w), lambda i: (i, 0))
    const = lambda r, w: pl.BlockSpec((r, w), lambda i: (0, 0))
    return pl.pallas_call(
        body, name=name, grid=(nbk,),
        in_specs=[pl.BlockSpec(memory_space=pltpu.SMEM), blk(dq)] + _kv_specs(dkv, 0, nbk) + _kv_specs(dkv, vcol, nbk)
        + [blk(dq), blk(dq), blk(LANE)],
        out_specs=[blk(dq), pl.BlockSpec((nh, BLK), lambda i: (0, i)), const(CTX, dkv), const(CTX, dkv), const(8, LANE)],
        out_shape=[jax.ShapeDtypeStruct((t, dq), F32), jax.ShapeDtypeStruct((nh, t), F32),
                   jax.ShapeDtypeStruct((CTX, dkv), F32), jax.ShapeDtypeStruct((CTX, dkv), F32),
                   jax.ShapeDtypeStruct((8, LANE), F32)],
        compiler_params=_params(("arbitrary",)))(sink, qr, kr, kr, kr, kr, qkv, qkv, qkv, qkv, do, o, lse)


def attn_bwd_kv(name, qr, kr, qkv, do, lset, deltat, seq):
    t, dq = qr.shape
    dkv = kr.shape[1]
    nbk, nbl, nkv, nh = t // BLK, seq // BLK, dkv // HEAD, dq // HEAD
    cb = CTX // BLK
    vcol = (dq + dkv) // dkv
    scale = HEAD ** -0.5

    def qspec(w, d):
        return pl.BlockSpec((BLK, w), lambda j: (jnp.clip(j + cb + d, cb, nbk - 1), 0))

    def tspec(d):
        return pl.BlockSpec((nh, BLK), lambda j: (0, jnp.clip(j + cb + d, cb, nbk - 1)))

    def stack_rows(v, h):
        return jnp.concatenate([v[h * GROUP + g:h * GROUP + g + 1, :] for g in range(GROUP)], axis=1)

    def body(k_ref, v_ref, *refs):
        dk_ref, dv_ref = refs[-2], refs[-1]
        j = pl.program_id(0)
        krow = lax.broadcasted_iota(jnp.int32, (BLK, GROUP * BLK), 0)
        qcol = lax.broadcasted_iota(jnp.int32, (BLK, GROUP * BLK), 1) % BLK
        for h in range(nkv):
            hs = slice(h * HEAD, (h + 1) * HEAD)
            kh, vh = k_ref[:, hs], v_ref[:, hs]
            dk_h = jnp.zeros((BLK, HEAD), F32)
            dv_h = jnp.zeros((BLK, HEAD), F32)
            for di, d in enumerate((-1, 0, 1)):
                q_ref, do_ref, lse_ref, dl_ref = refs[4 * di:4 * di + 4]
                n = j + d
                msk = (n >= 0) & (n < nbl) & (jnp.abs(d * BLK + qcol - krow) <= WINDOW)
                q4, do4 = _stack_heads(q_ref, h), _stack_heads(do_ref, h)
                l4, d4 = stack_rows(lse_ref[...], h), stack_rows(dl_ref[...], h)
                s = jnp.where(msk, lax.dot_general(kh, q4, NT, preferred_element_type=F32) * scale, NEG)
                p = jnp.exp(s - l4)
                dv_h += jnp.dot(p.astype(BF16), do4, preferred_element_type=F32)
                dp = lax.dot_general(vh, do4, NT, preferred_element_type=F32)
                dk_h += jnp.dot((p * (dp - d4) * scale).astype(BF16), q4, preferred_element_type=F32)
            dk_ref[:, hs] = dk_h
            dv_ref[:, hs] = dv_h

    in_specs = [pl.BlockSpec((BLK, dkv), lambda j: (j + cb, 0)), pl.BlockSpec((BLK, dkv), lambda j: (j + cb, vcol))]
    operands = [kr, qkv]
    for d in (-1, 0, 1):
        in_specs += [qspec(dq, d), qspec(dq, d), tspec(d), tspec(d)]
        operands += [qr, do, lset, deltat]
    return pl.pallas_call(
        body, name=name, grid=(nbl,), in_specs=in_specs,
        out_specs=[pl.BlockSpec((BLK, dkv), lambda j: (j, 0)), pl.BlockSpec((BLK, dkv), lambda j: (j, 0))],
        out_shape=[jax.ShapeDtypeStruct((seq, dkv), F32), jax.ShapeDtypeStruct((seq, dkv), F32)],
        compiler_params=_params(("parallel",)))(*operands)


def ada_fwd(name, cond, w_ada):
    nl, d, n = w_ada.shape
    tn = _pick(n, 1024)

    def body(c_ref, w_ref, out_ref):
        cv = c_ref[...]
        out_ref[...] = jnp.dot((cv * _sigmoid(cv)).astype(BF16), w_ref[...].astype(BF16), preferred_element_type=F32)

    return pl.pallas_call(
        body, name=name, grid=(nl, n // tn),
        in_specs=[pl.BlockSpec((16, d), lambda l, j: (0, 0)), pl.BlockSpec((None, d, tn), lambda l, j: (l, 0, j))],
        out_specs=pl.BlockSpec((None, 16, tn), lambda l, j: (l, 0, j)),
        out_shape=jax.ShapeDtypeStruct((nl, 16, n), F32), compiler_params=_params(("parallel", "parallel")))(cond, w_ada)


def ada_bwd_cond(name, dsum, w_ada):
    nl, d, n = w_ada.shape
    tn = _pick(n, 1024)

    def body(g_ref, w_ref, out_ref):
        @pl.when((pl.program_id(0) == 0) & (pl.program_id(1) == 0))
        def _():
            out_ref[...] = jnp.zeros_like(out_ref)

        out_ref[...] += lax.dot_general(g_ref[...].astype(BF16), w_ref[...].astype(BF16), NT, preferred_element_type=F32)

    return pl.pallas_call(
        body, name=name, grid=(nl, n // tn),
        in_specs=[pl.BlockSpec((None, 8, tn), lambda l, j: (l, 0, j)), pl.BlockSpec((None, d, tn), lambda l, j: (l, 0, j))],
        out_specs=pl.BlockSpec((8, d), lambda l, j: (0, 0)),
        out_shape=jax.ShapeDtypeStruct((8, d), F32), compiler_params=_params(("arbitrary", "arbitrary")))(dsum, w_ada)


def ada_grad_w(name, cond, rhs):
    nl, _, n = rhs.shape
    d = cond.shape[1]
    tr, tn = _pick(d, 512), _pick(n, 1024)

    def body(c_ref, r_ref, out_ref):
        cv = c_ref[...]
        out_ref[...] = lax.dot_general((cv * _sigmoid(cv)).astype(BF16), r_ref[...].astype(BF16), TN, preferred_element_type=F32)

    return pl.pallas_call(
        body, name=name, grid=(nl, d // tr, n // tn),
        in_specs=[pl.BlockSpec((16, tr), lambda l, i, j: (0, i)), pl.BlockSpec((None, 16, tn), lambda l, i, j: (l, 0, j))],
        out_specs=pl.BlockSpec((None, tr, tn), lambda l, i, j: (l, i, j)),
        out_shape=jax.ShapeDtypeStruct((nl, d, n), F32),
        compiler_params=_params(("parallel", "parallel", "parallel")))(cond, rhs)


def adamw(name, g, g_spec, w, m, v, tr):
    nl, r, c = w.shape
    spec = pl.BlockSpec((None, tr, c), lambda l, i: (l, i, 0))

    def body(g_ref, w_ref, m_ref, v_ref, go_ref, d_ref, mo_ref, vo_ref):
        gv = g_ref[...]
        mn = B1 * m_ref[...] + (1.0 - B1) * gv
        vn = B2 * v_ref[...] + (1.0 - B2) * (gv * gv)
        m_hat = mn / (1.0 - B1 ** STEP)
        v_hat = vn / (1.0 - B2 ** STEP)
        go_ref[...] = gv
        d_ref[...] = -LR * (m_hat / (jnp.sqrt(v_hat) + ADAM_EPS) + WD * w_ref[...])
        mo_ref[...] = mn
        vo_ref[...] = vn

    return pl.pallas_call(
        body, name=name, grid=(nl, r // tr), in_specs=[g_spec, spec, spec, spec], out_specs=[spec] * 4,
        out_shape=[jax.ShapeDtypeStruct(w.shape, F32)] * 4, compiler_params=_params(("parallel", "parallel")))(g, w, m, v)


def adamw_small(name, g, w, m, v):
    shape = w.shape
    r3 = lambda a: a.reshape(1, -1, shape[-1]).astype(F32)
    rows = r3(w).shape[1]
    outs = adamw(name, r3(g), pl.BlockSpec((None, rows, shape[-1]), lambda l, i: (l, i, 0)), r3(w), r3(m), r3(v), rows)
    return [o.reshape(shape) for o in outs]


def _place():
    x, y, c = lax.axis_index("x"), lax.axis_index("y"), lax.axis_index("c")
    return x, y, c, [(1 - x, y), (x, 1 - y), (1 - x, 1 - y)]


def small_allgather(name, v):
    r, w = v.shape

    def body(x_ref, out_ref, send_sems, recv_sems, local_sem):
        x, y, c, chips = _place()
        me, sibling = (x, y, c), (x, y, 1 - c)

        def slot(px, py, pc):
            return out_ref.at[4 * px + 2 * py + pc]

        def copy(k, block, to, src=None):
            return pltpu.make_async_remote_copy(
                src_ref=slot(*block) if src is None else src, dst_ref=slot(*block),
                send_sem=send_sems.at[k], recv_sem=recv_sems.at[k], device_id=to, device_id_type=MESH)

        mine = pltpu.make_async_copy(x_ref, slot(*me), local_sem)
        mine.start()
        first = [copy(0, me, sibling, src=x_ref)]
        first += [copy(1 + j, me, (*chip, c), src=x_ref) for j, chip in enumerate(chips)]
        for cp in first:
            cp.start()
        passed = [copy(4 + j, (*chip, c), sibling) for j, chip in enumerate(chips)]
        for j, chip in enumerate(chips):
            copy(1 + j, (*chip, c), me).wait_recv()
            passed[j].start()
        copy(0, sibling, me).wait_recv()
        for j, chip in enumerate(chips):
            copy(4 + j, (*chip, 1 - c), me).wait_recv()
        for cp in first + passed:
            cp.wait_send()
        mine.wait()

    return pl.pallas_call(
        body, name=name, out_shape=jax.ShapeDtypeStruct((8, r, w), v.dtype),
        in_specs=[pl.BlockSpec(memory_space=pltpu.VMEM)], out_specs=pl.BlockSpec(memory_space=pltpu.VMEM),
        scratch_shapes=[pltpu.SemaphoreType.DMA((7,)), pltpu.SemaphoreType.DMA((7,)), pltpu.SemaphoreType.DMA],
        compiler_params=pltpu.CompilerParams(vmem_limit_bytes=VMEM_LIMIT))(v)


def gather_flat(name, parts):
    flat = jnp.concatenate([p.reshape(-1).astype(F32) for p in parts])
    n = flat.shape[0]
    rows = _cdiv(n, MXU * LANE) * MXU
    flat = jnp.pad(flat, (0, rows * LANE - n))
    return small_allgather(name, flat.reshape(rows, LANE)).reshape(8, rows * LANE)


def sum8(name, g):
    p = g.shape[1]
    g3 = g.reshape(8, p // LANE, LANE)
    tr = _pick(p // LANE, 1024, MXU)

    def body(g_ref, out_ref):
        acc = g_ref[0]
        for k in range(1, 8):
            acc = acc + g_ref[k]
        out_ref[...] = acc

    return pl.pallas_call(
        body, name=name, grid=(p // LANE // tr,),
        in_specs=[pl.BlockSpec((8, tr, LANE), lambda i: (0, i, 0))], out_specs=pl.BlockSpec((tr, LANE), lambda i: (i, 0)),
        out_shape=jax.ShapeDtypeStruct((p // LANE, LANE), F32), compiler_params=_params(("parallel",)))(g3).reshape(p)


HBM_SPEC = pl.BlockSpec(memory_space=pltpu.HBM)


def _half(ref, lead, c, axis):
    h = ref.shape[axis] // 2
    return ref.at[lead, pl.ds(c * h, h), :] if axis == 1 else ref.at[lead, :, pl.ds(c * h, h)]


def _gather_ops(outs, axes, send_sems, recv_sems):
    x, y, c, chips = _place()
    me, sibling = (x, y, c), (x, y, 1 - c)

    def copy(a, k, chip, pc, to):
        blk = _half(outs[a], 2 * chip[0] + chip[1], pc, axes[a])
        return pltpu.make_async_remote_copy(src_ref=blk, dst_ref=blk, send_sem=send_sems.at[6 * a + k],
                                            recv_sem=recv_sems.at[6 * a + k], device_id=to, device_id_type=MESH)

    def start():
        for a in range(len(outs)):
            for j, chip in enumerate(chips):
                copy(a, j, (x, y), c, (*chip, c)).start()

    def finish():
        for a in range(len(outs)):
            for j, chip in enumerate(chips):
                copy(a, j, chip, c, me).wait_recv()
                copy(a, 3 + j, chip, c, sibling).start()
        for a in range(len(outs)):
            for j, chip in enumerate(chips):
                copy(a, 3 + j, chip, 1 - c, me).wait_recv()
            for j, chip in enumerate(chips):
                copy(a, j, (x, y), c, (*chip, c)).wait_send()
                copy(a, 3 + j, chip, c, sibling).wait_send()

    return start, finish


def gather_comm(bufs, axes):
    return dict(ins=list(bufs), out_shape=[jax.ShapeDtypeStruct(b.shape, b.dtype) for b in bufs],
                aliases={a: a for a in range(len(bufs))}, n_sems=6 * len(bufs),
                ops=lambda cin, couts, ss, rs: _gather_ops(couts, axes, ss, rs))


def gather_weights(name, bufs, axes):
    n = len(bufs)

    def body(*refs):
        start, finish = _gather_ops(refs[n:2 * n], axes, refs[2 * n], refs[2 * n + 1])
        start()
        finish()

    return pl.pallas_call(
        body, name=name, out_shape=[jax.ShapeDtypeStruct(b.shape, b.dtype) for b in bufs],
        in_specs=[HBM_SPEC] * n, out_specs=[HBM_SPEC] * n, input_output_aliases={a: a for a in range(n)},
        scratch_shapes=[pltpu.SemaphoreType.DMA((6 * n,)), pltpu.SemaphoreType.DMA((6 * n,))])(*bufs)


def _scatter_ops(ins, outs, send_sems, recv_sems):
    x, y, c, chips = _place()
    me = 2 * x + y

    def copy(a, j, chip):
        return pltpu.make_async_remote_copy(
            src_ref=ins[a].at[2 * chip[0] + chip[1]], dst_ref=outs[a].at[me], send_sem=send_sems.at[3 * a + j],
            recv_sem=recv_sems.at[3 * a + j], device_id=(*chip, c), device_id_type=MESH)

    def start():
        for a in range(len(ins)):
            for j, chip in enumerate(chips):
                copy(a, j, chip).start()

    def finish():
        for a in range(len(ins)):
            for j, chip in enumerate(chips):
                copy(a, j, chip).wait()

    return start, finish


def scatter_comm(bufs):
    return dict(ins=list(bufs), out_shape=[jax.ShapeDtypeStruct(b.shape, b.dtype) for b in bufs], aliases={},
                n_sems=3 * len(bufs), ops=_scatter_ops)


def chip_scatter(name, bufs):
    n = len(bufs)

    def body(*refs):
        start, finish = _scatter_ops(refs[:n], refs[n:2 * n], refs[2 * n], refs[2 * n + 1])
        start()
        finish()

    return pl.pallas_call(
        body, name=name, out_shape=[jax.ShapeDtypeStruct(b.shape, b.dtype) for b in bufs],
        in_specs=[HBM_SPEC] * n, out_specs=[HBM_SPEC] * n,
        scratch_shapes=[pltpu.SemaphoreType.DMA((3 * n,)), pltpu.SemaphoreType.DMA((3 * n,))])(*bufs)


def pair_exchange(name, bufs, axes):
    n = len(bufs)

    def body(*refs):
        ins, outs, (send_sems, recv_sems) = refs[:n], refs[n:2 * n], refs[2 * n:]
        x, y, c, _ = _place()
        cps = []
        for a, (src, out) in enumerate(zip(ins, outs)):
            cp = pltpu.make_async_remote_copy(
                src_ref=_half(src, slice(None), 1 - c, axes[a]), dst_ref=out, send_sem=send_sems.at[a],
                recv_sem=recv_sems.at[a], device_id=(x, y, 1 - c), device_id_type=MESH)
            cp.start()
            cps.append(cp)
        for cp in cps:
            cp.wait()

    def halved(b, axis):
        shape = list(b.shape)
        shape[axis] //= 2
        return jax.ShapeDtypeStruct(tuple(shape), b.dtype)

    return pl.pallas_call(
        body, name=name, out_shape=[halved(b, ax) for b, ax in zip(bufs, axes)],
        in_specs=[HBM_SPEC] * n, out_specs=[HBM_SPEC] * n,
        scratch_shapes=[pltpu.SemaphoreType.DMA((n,)), pltpu.SemaphoreType.DMA((n,))])(*bufs)


def pair_add(name, buf, got, cidx, axis):
    s, r, c = got.shape
    tr = _pick(r, max(HALO, (4 * 1024 * 1024) // (2 * c)), HALO)
    per = r // tr
    if axis == 1:
        mine = pl.BlockSpec((None, tr, c), lambda k, i, cr: (k, cr[0] * per + i, 0))
    else:
        mine = pl.BlockSpec((None, tr, c), lambda k, i, cr: (k, i, cr[0]))

    def body(c_ref, a_ref, b_ref, out_ref):
        out_ref[...] = (a_ref[...].astype(F32) + b_ref[...].astype(F32)).astype(BF16)

    return pl.pallas_call(
        body, name=name,
        grid_spec=pltpu.PrefetchScalarGridSpec(
            num_scalar_prefetch=1, grid=(s, per),
            in_specs=[mine, pl.BlockSpec((None, tr, c), lambda k, i, cr: (k, i, 0))],
            out_specs=pl.BlockSpec((None, tr, c), lambda k, i, cr: (k, i, 0))),
        out_shape=jax.ShapeDtypeStruct((s, r, c), BF16),
        compiler_params=_params(("parallel", "parallel")))(cidx, buf, got)


def chip_add(name, own, got, place, dst, l, off, size, col):
    s = got.shape[0]
    if col:
        h, n = got.shape[1], size
        tr = _pick(h, max(HALO, (2 * 1024 * 1024) // (2 * n)), HALO)
        per, ob = h // tr, off // n
        own_spec = pl.BlockSpec((None, tr, n), lambda i, p: (p[0], i, ob))
        got_spec = pl.BlockSpec((s, tr, n), lambda i, p: (0, i, ob))
        out_spec = pl.BlockSpec((None, tr, n), lambda i, p: (l, p[1] * per + i, 0))
        grid = (per,)
    else:
        n = got.shape[2]
        tr = _pick(size, max(HALO, (2 * 1024 * 1024) // (2 * n)), HALO)
        ob = off // tr
        own_spec = pl.BlockSpec((None, tr, n), lambda i, p: (p[0], ob + i, 0))
        got_spec = pl.BlockSpec((s, tr, n), lambda i, p: (0, ob + i, 0))
        out_spec = pl.BlockSpec((None, tr, n), lambda i, p: (l, i, p[1]))
        grid = (size // tr,)

    def body(p_ref, own_ref, g_ref, dst_ref, out_ref):
        acc = jnp.zeros((tr, n), F32)
        for k in range(s):
            acc = acc + jnp.where(p_ref[0] == k, own_ref[...], g_ref[k]).astype(F32)
        out_ref[...] = acc

    return pl.pallas_call(
        body, name=name,
        grid_spec=pltpu.PrefetchScalarGridSpec(
            num_scalar_prefetch=1, grid=grid,
            in_specs=[own_spec, got_spec, pl.BlockSpec(memory_space=pl.ANY)], out_specs=out_spec),
        out_shape=jax.ShapeDtypeStruct(dst.shape, F32), input_output_aliases={3: 0},
        compiler_params=_params(("parallel",)))(place, own, got, dst)


def pair_join(name, bufs, axes):
    n = len(bufs)

    def body(*refs):
        outs = refs[n:2 * n]
        send_sems, recv_sems = refs[2 * n:]
        x, y, c, _ = _place()
        started = []
        for a, out in enumerate(outs):
            blk = _half(out, slice(None), c, axes[a])
            cp = pltpu.make_async_remote_copy(src_ref=blk, dst_ref=blk, send_sem=send_sems.at[a], recv_sem=recv_sems.at[a],
                                              device_id=(x, y, 1 - c), device_id_type=MESH)
            cp.start()
            started.append(cp)
        for cp in started:
            cp.wait()

    return pl.pallas_call(
        body, name=name, out_shape=[jax.ShapeDtypeStruct(b.shape, b.dtype) for b in bufs],
        in_specs=[HBM_SPEC] * n, out_specs=[HBM_SPEC] * n, input_output_aliases={a: a for a in range(n)},
        scratch_shapes=[pltpu.SemaphoreType.DMA((n,)), pltpu.SemaphoreType.DMA((n,))])(*bufs)


def _rope_tables(seq):
    rows = seq // GRID_W
    row = jnp.repeat(jnp.arange(rows), GRID_W).astype(F32)
    col = jnp.tile(jnp.arange(GRID_W), rows).astype(F32)
    pairs = HEAD // 4
    inv = ROPE_BASE ** (-jnp.arange(pairs, dtype=F32) / pairs)
    ang = jnp.stack([row[:, None] * inv, col[:, None] * inv], axis=1)
    ang = jnp.broadcast_to(ang[:, :, None, :], (seq, 2, 2, pairs)).reshape(seq, HEAD)
    cos = jnp.concatenate([jnp.ones((CTX, HEAD), F32), jnp.cos(ang)], axis=0)
    sin = jnp.concatenate([jnp.zeros((CTX, HEAD), F32), jnp.sin(ang)], axis=0)
    return cos, sin


def _pad8(a):
    return jnp.pad(a, ((0, 8 - a.shape[0]), (0, 0)))


def kernel(x, c, ctx, c_ctx, w_ada, b_ada, attn_w_qkv, attn_w_o, attn_q_gain, attn_k_gain, attn_sink, sc_w_in, sc_conv, sc_w_out, ffn_w_up, ffn_conv, ffn_conv_b, ffn_w_down, loss_target, m_c_ctx, m_w_ada, m_b_ada, m_attn_w_qkv, m_attn_w_o, m_attn_q_gain, m_attn_k_gain, m_attn_sink, m_sc_w_in, m_sc_conv, m_sc_w_out, m_ffn_w_up, m_ffn_conv, m_ffn_conv_b, m_ffn_w_down, v_c_ctx, v_w_ada, v_b_ada, v_attn_w_qkv, v_attn_w_o, v_attn_q_gain, v_attn_k_gain, v_attn_sink, v_sc_w_in, v_sc_conv, v_sc_w_out, v_ffn_w_up, v_ffn_conv, v_ffn_conv_b, v_ffn_w_down):
    seq, d = x.shape[1], x.shape[2]
    depth, nada = w_ada.shape[0], w_ada.shape[2]
    n_attn, n_conv = attn_w_qkv.shape[0], sc_w_in.shape[0]
    ff = ffn_conv_b.shape[1]
    dq, dkv = d, d // GROUP
    nqkv = dq + 2 * dkv
    nh = dq // HEAD
    assert ctx.shape[1] == CTX and seq % RB == 0 and 6 * d == 4 * nada
    lay = Layout(d, ff, nqkv, depth)
    ax, ay, ac = lax.axis_index("x"), lax.axis_index("y"), lax.axis_index("c")
    chip, dev = 2 * ax + ay, 4 * ax + 2 * ay + ac
    cidx = jnp.reshape(ac, (1,)).astype(jnp.int32)

    chip1 = jnp.reshape(chip, (1,)).astype(jnp.int32)
    wcs, wrs = [], []
    for l in range(depth):
        j, is_attn = l // N_MIX, l % N_MIX == 0
        wc_l, wr_l = lax.empty((4, d, lay.ct[l]), BF16), lax.empty((4, lay.rt, d), BF16)
        wc_l = cast_pack(f"pack_up_{l}", ffn_w_up, l, wc_l, 0, True, chip1)
        wc_l = cast_pack(f"pack_mix_{l}", attn_w_qkv if is_attn else sc_w_in, j, wc_l, lay.mix[l], True, chip1)
        wr_l = cast_pack(f"pack_down_{l}", ffn_w_down, l, wr_l, 0, False, chip1)
        wr_l = cast_pack(f"pack_out_{l}", attn_w_o if is_attn else sc_w_out, j, wr_l, lay.out, False, chip1)
        wcs.append(wc_l)
        wrs.append(wr_l)
    wcs[0], wrs[0] = gather_weights("gather_w0", [wcs[0], wrs[0]], (1, 2))

    g1 = gather_flat("gather_cond", [c, sc_conv, ffn_conv])
    c_all = g1[:, :d]
    o1 = d + sc_conv.size
    sc_conv_full = jnp.concatenate([g1[2 * s, d:o1].reshape(sc_conv.shape) for s in range(4)], axis=-1)
    ffn_conv_full = jnp.concatenate([g1[2 * s, o1:o1 + ffn_conv.size].reshape(ffn_conv.shape) for s in range(4)], axis=-1)
    cond = jnp.concatenate([c_all, c_ctx[None, :], jnp.zeros((7, d), F32)], axis=0)
    ada_part = ada_fwd("ada_fwd", cond, w_ada)
    g2 = gather_flat("gather_ada", [ada_part])
    ada_all = jnp.concatenate([g2[2 * s, :ada_part.size].reshape(ada_part.shape) for s in range(4)], axis=-1)
    ada_own = jnp.stack([lax.dynamic_index_in_dim(ada_all, 8, 1, False),
                         lax.dynamic_index_in_dim(ada_all, dev, 1, False)], axis=1) + b_ada[:, None, :]
    mods = jnp.pad(ada_own.reshape(depth, 2, 6, d), ((0, 0), (0, 0), (0, 2), (0, 0)))

    cos, sin = _rope_tables(seq)
    xa = jnp.concatenate([ctx[0], x[0]], axis=0)
    cws = [_pad8(sc_conv_full[j]) for j in range(n_conv)]
    cwf = [_pad8(jnp.concatenate([ffn_conv_full[l], ffn_conv_b[l][None, :]], axis=0)) for l in range(depth)]
    gains = [_pad8(jnp.stack([attn_q_gain[j], attn_k_gain[j]])) for j in range(n_attn)]

    saved = []
    for l in range(depth):
        j, is_attn, mod = l // N_MIX, l % N_MIX == 0, mods[l]
        wc, wr, last = wcs[l], wrs[l], l == depth - 1
        sv = {"x_in": xa}
        h1 = norm_mod(f"norm1_{l}", xa, mod, 0, 1)
        if is_attn:
            qkv = mm_nn_col(f"qkv_{l}", h1, wc, lay.mix[l], lay.nb_mix[l])
            qr, kr = rope_fwd(f"rope_{l}", qkv, cos, sin, gains[j], dq, dkv)
            o, lse, lset = attn_fwd(f"attn_{l}", qr, kr, qkv, attn_sink[j], seq)
            xa, y_m = mm_nn_row(f"wo_{l}", o, wr, lay.out, lay.kb_o, xa, mod, 2)
            sv.update(qkv=qkv, qr=qr, kr=kr, o=o, lse=lse, lset=lset)
        else:
            u = mm_nn_col(f"scin_{l}", h1, wc, lay.mix[l], lay.nb_mix[l])
            z = sc_gate_fwd(f"scgate_{l}", u, cws[j])
            xa, y_m = mm_nn_row(f"scout_{l}", z, wr, lay.out, lay.kb_o, xa, mod, 2)
            sv.update(u=u, z=z)
        h2 = norm_mod(f"norm2_{l}", xa, mod, 3, 4)
        if last:
            up = mm_nn_col(f"up_{l}", h2, wc, 0, lay.nb_up)
        else:
            up, (wcs[l + 1],) = mm_nn_col(f"up_{l}", h2, wc, 0, lay.nb_up, comm=gather_comm([wcs[l + 1]], (1,)))
        act = ffn_act_fwd(f"act_{l}", up, cwf[l])
        sv.update(h1=h1, y_m=y_m, x_mid=xa, h2=h2, up=up, act=act)
        if last:
            xa, y_f = mm_nn_row(f"down_{l}", act, wr, 0, lay.kb_dn, xa, mod, 5)
        else:
            (xa, y_f), (wrs[l + 1],) = mm_nn_row(f"down_{l}", act, wr, 0, lay.kb_dn, xa, mod, 5,
                                                 comm=gather_comm([wrs[l + 1]], (2,)))
        sv["y_f"] = y_f
        saved.append(sv)

    dx, dy, acc, lp = loss_head("loss", xa, loss_target[0], saved[-1]["y_f"], mods[-1], 5)
    loss = lax.psum(0.5 * jnp.sum(lp[0]) / d, ("x", "y", "c"))
    d_mod = [jnp.zeros((2, 6, d), F32) for _ in range(depth)]
    place = jnp.stack([chip, ac]).astype(jnp.int32)
    gf = {"up": lax.empty(ffn_w_up.shape, F32), "down": lax.empty(ffn_w_down.shape, F32),
          "qkv": lax.empty(attn_w_qkv.shape, F32), "wo": lax.empty(attn_w_o.shape, F32),
          "scin": lax.empty(sc_w_in.shape, F32), "scout": lax.empty(sc_w_out.shape, F32)}

    def chip_adds(l, hc, hr, rb_c, rb_r):
        j, mix = l // N_MIX, ("qkv", "wo") if l % N_MIX == 0 else ("scin", "scout")
        gf["up"] = chip_add(f"sum_up_{l}", hc, rb_c, place, gf["up"], l, 0, lay.nb_up, True)
        gf[mix[0]] = chip_add(f"sum_mix_{l}", hc, rb_c, place, gf[mix[0]], j, lay.mix[l], lay.nb_mix[l], True)
        gf["down"] = chip_add(f"sum_down_{l}", hr, rb_r, place, gf["down"], l, 0, lay.kb_dn, False)
        gf[mix[1]] = chip_add(f"sum_out_{l}", hr, rb_r, place, gf[mix[1]], j, lay.out, lay.kb_o, False)

    pending = None

    def add_mod(l, acc, idx):
        upd = jnp.zeros((2, 6, d), F32)
        for row, k in idx:
            upd = upd.at[:, k, :].set(acc[:, row, :])
        d_mod[l] = d_mod[l] + upd

    add_mod(depth - 1, acc, [(2, 5)])
    d_conv_f, d_conv_s = [None] * depth, [None] * n_conv
    d_gq, d_gk, d_sink = [None] * n_attn, [None] * n_attn, [None] * n_attn
    for l in reversed(range(depth)):
        j, is_attn, sv = l // N_MIX, l % N_MIX == 0, saved[l]
        wc, wr = wcs[l], wrs[l]
        gc, gr = lax.empty((4, d, lay.ct[l]), BF16), lax.empty((4, lay.rt, d), BF16)
        if pending is None:
            gr = mm_tn_row(f"g_down_{l}", sv["act"], dy, gr, 0, lay.kb_dn)
        else:
            gr, (rb_r,) = mm_tn_row(f"g_down_{l}", sv["act"], dy, gr, 0, lay.kb_dn, comm=scatter_comm([pending[2]]))
        da = mm_nt_row(f"d_act_{l}", dy, wr, 0, lay.kb_dn)
        d_up, d_conv_f[l] = ffn_act_bwd(f"act_bwd_{l}", sv["up"], da, cwf[l])
        if pending is None:
            gc = mm_tn_col(f"g_up_{l}", sv["h2"], d_up, gc, 0, lay.nb_up)
        else:
            gc, (rb_c,) = mm_tn_col(f"g_up_{l}", sv["h2"], d_up, gc, 0, lay.nb_up, comm=scatter_comm([pending[1]]))
            chip_adds(pending[0], pending[1], pending[2], rb_c, rb_r)
        dh2 = mm_nt_col(f"d_h2_{l}", d_up, wc, 0, lay.nb_up)
        dx, dy, acc = resid_bwd(f"norm2_bwd_{l}", dx, dh2, sv["x_mid"], mods[l], 3, 4, sv["y_m"], mods[l], 2)
        add_mod(l, acc, [(0, 3), (1, 4), (2, 2)])
        if is_attn:
            gr = mm_tn_row(f"g_wo_{l}", sv["o"], dy, gr, lay.out, lay.kb_o)
            do = mm_nt_row(f"d_o_{l}", dy, wr, lay.out, lay.kb_o)
            dqr, deltat, dkc, dvc, dsk = attn_bwd_q(f"attn_bwd_q_{l}", sv["qr"], sv["kr"], sv["qkv"], attn_sink[j],
                                                    do, sv["o"], sv["lse"], seq)
            dkl, dvl = attn_bwd_kv(f"attn_bwd_kv_{l}", sv["qr"], sv["kr"], sv["qkv"], do, sv["lset"], deltat, seq)
            dqkv, dgn = rope_bwd(f"rope_bwd_{l}", sv["qkv"], dqr, jnp.concatenate([dkc, dkl], axis=0),
                                 jnp.concatenate([dvc, dvl], axis=0), cos, sin, gains[j])
            d_gq[j], d_gk[j], d_sink[j] = dgn[0], dgn[1], dsk[0, :nh]
            gc = mm_tn_col(f"g_qkv_{l}", sv["h1"], dqkv, gc, lay.mix[l], lay.nb_mix[l])
            dh1 = mm_nt_col(f"d_h1_{l}", dqkv, wc, lay.mix[l], lay.nb_mix[l])
        else:
            gr = mm_tn_row(f"g_scout_{l}", sv["z"], dy, gr, lay.out, lay.kb_o)
            dz = mm_nt_row(f"d_z_{l}", dy, wr, lay.out, lay.kb_o)
            du, dcw = sc_gate_bwd(f"scgate_bwd_{l}", sv["u"], dz, cws[j])
            d_conv_s[j] = dcw[:3]
            gc = mm_tn_col(f"g_scin_{l}", sv["h1"], du, gc, lay.mix[l], lay.nb_mix[l])
            dh1 = mm_nt_col(f"d_h1_{l}", du, wc, lay.mix[l], lay.nb_mix[l])
        if l > 0:
            dx, dy, acc = resid_bwd(f"norm1_bwd_{l}", dx, dh1, sv["x_in"], mods[l], 0, 1, saved[l - 1]["y_f"], mods[l - 1], 5)
            add_mod(l - 1, acc, [(2, 5)])
        else:
            dx, acc = resid_bwd(f"norm1_bwd_{l}", dx, dh1, sv["x_in"], mods[l], 0, 1)
        add_mod(l, acc, [(0, 0), (1, 1)])
        ra_c, ra_r = pair_exchange(f"pair_exchange_{l}", [gc, gr], (1, 2))
        pending = (l, pair_add(f"pair_add_c_{l}", gc, ra_c, cidx, 1), pair_add(f"pair_add_r_{l}", gr, ra_r, cidx, 2))
    grad_x = dx[CTX:][None]
    rb_c, rb_r = chip_scatter("chip_scatter_0", [pending[1], pending[2]])
    chip_adds(pending[0], pending[1], pending[2], rb_c, rb_r)
    order = ["up", "down", "qkv", "wo", "scin", "scout"]
    joined = pair_join("pair_join", [gf[k] for k in order], (1, 2, 1, 2, 1, 2))
    gf = dict(zip(order, joined))

    d_ada = jnp.stack(d_mod).reshape(depth, 2, 6 * d)
    small = [d_ada, jnp.stack(d_gq), jnp.stack(d_gk), jnp.stack(d_sink), jnp.stack(d_conv_s),
             jnp.stack([t[:3] for t in d_conv_f]), jnp.stack([t[3] for t in d_conv_f])]
    g3 = gather_flat("gather_small", small)
    tot = sum8("sum_small", g3)
    sizes = [s.size for s in small]
    offs = [sum(sizes[:k]) for k in range(len(sizes) + 1)]
    part = lambda k: tot[offs[k]:offs[k + 1]].reshape(small[k].shape)
    g_b_ada = part(0)[:, 0] + part(0)[:, 1]
    g_q_gain, g_k_gain, g_sink = part(1), part(2), part(3)
    g_sc_conv = lax.dynamic_slice_in_dim(part(4), chip * sc_conv.shape[2], sc_conv.shape[2], 2)
    g_ffn_conv = lax.dynamic_slice_in_dim(part(5), chip * ffn_conv.shape[2], ffn_conv.shape[2], 2)
    g_conv_b = part(6)

    d_ada_all = g3[:, :d_ada.size].reshape(8, depth, 2, 6 * d)
    cols = lambda a: lax.dynamic_slice_in_dim(a, chip * nada, nada, a.ndim - 1)
    d_lat = cols(jnp.moveaxis(d_ada_all[:, :, 1], 0, 1))
    d_ctx = cols(part(0)[:, 0])
    rhs = jnp.concatenate([d_lat, d_ctx[:, None], jnp.zeros((depth, 7, nada), F32)], axis=1)
    g_w_ada = ada_grad_w("ada_grad_w", cond, rhs)
    dcc = ada_bwd_cond("ada_bwd_cond", jnp.pad(d_ctx[:, None], ((0, 0), (0, 7), (0, 0))), w_ada)[0]
    g4 = gather_flat("gather_dcc", [dcc])
    d_silu = g4[0, :d] + g4[2, :d] + g4[4, :d] + g4[6, :d]
    sg = _sigmoid(c_ctx)
    g_c_ctx = d_silu * (sg * (1.0 + c_ctx * (1.0 - sg)))

    def adam_rows(k, n):
        return _pick(k, max(8, ADAM_TILE_ELEMS // n), 8)

    def big(name, g, w, m, v):
        _, k, n = w.shape
        tr = adam_rows(k, n)
        return adamw(name, g, pl.BlockSpec((None, tr, n), lambda l, i: (l, i, 0)), w, m, v, tr)

    ada_tr = adam_rows(d, nada)
    res = {
        "c_ctx": adamw_small("adam_c_ctx", g_c_ctx, c_ctx, m_c_ctx, v_c_ctx),
        "w_ada": adamw("adam_w_ada", g_w_ada, pl.BlockSpec((None, ada_tr, nada), lambda l, i: (l, i, 0)), w_ada, m_w_ada, v_w_ada, ada_tr),
        "b_ada": adamw_small("adam_b_ada", g_b_ada, b_ada, m_b_ada, v_b_ada),
        "attn_w_qkv": big("adam_qkv", gf["qkv"], attn_w_qkv, m_attn_w_qkv, v_attn_w_qkv),
        "attn_w_o": big("adam_wo", gf["wo"], attn_w_o, m_attn_w_o, v_attn_w_o),
        "attn_q_gain": adamw_small("adam_q_gain", g_q_gain, attn_q_gain, m_attn_q_gain, v_attn_q_gain),
        "attn_k_gain": adamw_small("adam_k_gain", g_k_gain, attn_k_gain, m_attn_k_gain, v_attn_k_gain),
        "attn_sink": adamw_small("adam_sink", g_sink, attn_sink, m_attn_sink, v_attn_sink),
        "sc_w_in": big("adam_scin", gf["scin"], sc_w_in, m_sc_w_in, v_sc_w_in),
        "sc_conv": adamw_small("adam_sc_conv", g_sc_conv, sc_conv, m_sc_conv, v_sc_conv),
        "sc_w_out": big("adam_scout", gf["scout"], sc_w_out, m_sc_w_out, v_sc_w_out),
        "ffn_w_up": big("adam_up", gf["up"], ffn_w_up, m_ffn_w_up, v_ffn_w_up),
        "ffn_conv": adamw_small("adam_ffn_conv", g_ffn_conv, ffn_conv, m_ffn_conv, v_ffn_conv),
        "ffn_conv_b": adamw_small("adam_conv_b", g_conv_b, ffn_conv_b, m_ffn_conv_b, v_ffn_conv_b),
        "ffn_w_down": big("adam_down", gf["down"], ffn_w_down, m_ffn_w_down, v_ffn_w_down),
    }
    names = list(res)
    return (loss, grad_x, *[res[n][0] for n in names], *[res[n][1] for n in names],
            *[res[n][2] for n in names], *[res[n][3] for n in names])
```

```python
import functools

import jax
import jax.numpy as jnp
from jax import lax
from jax.experimental import pallas as pl
from jax.experimental.pallas import tpu as pltpu

F32, BF16 = jnp.float32, jnp.bfloat16
MESH = pl.DeviceIdType.MESH
VMEM_LIMIT = 56 * 1024 * 1024
LANE = 128
MXU = 256
WIDE = 2816
HALO = 16
HEAD = 128
GROUP = 4
CTX = 256
BLK = 128
WINDOW = 128
RB = 256
GRID_W = 64
ROPE_BASE = 10000.0
EPS = 1e-6
NEG = -1e30
N_MIX = 2
LR, B1, B2, ADAM_EPS, WD, STEP = 0.001, 0.9, 0.999, 1e-08, 0.01, 10
ADAM_TILE_ELEMS = 400 * 1024
NT = (((1,), (1,)), ((), ()))
TN = (((0,), (0,)), ((), ()))


def _pick(dim, target, mult=LANE):
    best = None
    for t in range(mult, min(dim, target) + 1, mult):
        if dim % t == 0:
            best = t
    return dim if best is None else best


def _cdiv(a, b):
    return -(-a // b)


def _params(sem):
    return pltpu.CompilerParams(dimension_semantics=sem, vmem_limit_bytes=VMEM_LIMIT)


def _sigmoid(g):
    return 0.5 * jnp.tanh(0.5 * g) + 0.5


def _row(v, r):
    rows = lax.broadcasted_iota(jnp.int32, v.shape, 0)
    return jnp.sum(jnp.where(rows == r, v, 0.0), axis=0, keepdims=True)


def _get_col(v, c):
    lanes = lax.broadcasted_iota(jnp.int32, v.shape, 1)
    return jnp.sum(jnp.where(lanes == c, v, 0.0), axis=1, keepdims=True)


def _put_col(v, c, col):
    lanes = lax.broadcasted_iota(jnp.int32, v.shape, 1)
    return jnp.where(lanes == c, col, v)


def _rows3(s0, s1, s2, width):
    rows = lax.broadcasted_iota(jnp.int32, (8, width), 0)
    z = jnp.zeros((8, width), F32)
    return jnp.where(rows == 0, s0, jnp.where(rows == 1, s1, jnp.where(rows == 2, s2, z)))


def _shift_rows(w, prev_row, next_row):
    n = w.shape[0]
    rows = lax.broadcasted_iota(jnp.int32, (n, 1), 0)
    down = jnp.where(rows == 0, prev_row, pltpu.roll(w, 1, 0))
    up = jnp.where(rows == n - 1, next_row, pltpu.roll(w, n - 1, 0))
    return down, up


def _seg_flags(i, nt):
    return i <= 1, (i == 0) | (i == nt - 1)


def _halo_specs(width, nrows):
    r = RB // HALO
    nh = nrows // HALO
    prev = pl.BlockSpec((HALO, width), lambda i: (jnp.maximum(i * r - 1, 0), 0))
    nxt = pl.BlockSpec((HALO, width), lambda i: (jnp.minimum((i + 1) * r, nh - 1), 0))
    return prev, nxt


class Layout:
    def __init__(self, d, ff, nqkv, depth):
        self.nb_up, self.kb_dn, self.kb_o = 2 * ff // 4, ff // 4, d // 4
        self.nb_mix = [(nqkv if l % N_MIX == 0 else 3 * d) // 4 for l in range(depth)]
        self.mix = [_cdiv(self.nb_up, nb) * nb for nb in self.nb_mix]
        self.ct = [m + nb for m, nb in zip(self.mix, self.nb_mix)]
        self.out = _cdiv(self.kb_dn, self.kb_o) * self.kb_o
        self.rt = _cdiv(self.out + self.kb_o, 2 * HALO) * 2 * HALO


def _mm_call(name, grid, in_specs, out_specs, out_shape, contract, operands, acc_shape, epilogue,
             n_extra=0, aliases=None, comm=None, merge_b=False):
    nk = grid[2]
    n_out = len(out_shape)
    n_cin = len(comm["ins"]) if comm else 0
    n_cout = len(comm["out_shape"]) if comm else 0
    aliases = dict(aliases or {})
    in_specs, out_specs, out_shape, operands = list(in_specs), list(out_specs), list(out_shape), list(operands)
    scratch = [] if nk == 1 else [pltpu.VMEM(acc_shape, F32)]
    if comm:
        for i_in, i_out in comm["aliases"].items():
            aliases[len(operands) + i_in] = n_out + i_out
        in_specs += [HBM_SPEC] * n_cin
        out_specs += [HBM_SPEC] * n_cout
        out_shape += comm["out_shape"]
        operands += comm["ins"]
        scratch += [pltpu.SemaphoreType.DMA((comm["n_sems"],)), pltpu.SemaphoreType.DMA((comm["n_sems"],))]

    def body(*refs):
        a_ref, b_ref = refs[0], refs[1]
        extra = refs[2:2 + n_extra]
        p = 2 + n_extra
        cin, outs = refs[p:p + n_cin], refs[p + n_cin:p + n_cin + n_out]
        couts = refs[p + n_cin + n_out:p + n_cin + n_out + n_cout]
        scr = refs[p + n_cin + n_out + n_cout:]
        ids = (pl.program_id(0), pl.program_id(1))
        k = pl.program_id(2)
        if comm:
            start, finish = comm["ops"](cin, couts, scr[-2], scr[-1])

            @pl.when((ids[0] == 0) & (ids[1] == 0) & (k == 0))
            def _():
                start()

        def part():
            b = b_ref[...]
            if merge_b:
                b = b.reshape(b.shape[0] * b.shape[1], b.shape[2])
            return lax.dot_general(a_ref[...], b, contract, preferred_element_type=F32)

        if nk == 1:
            epilogue(part(), extra, outs, ids)
        else:
            acc = scr[0]

            @pl.when(k == 0)
            def _():
                acc[...] = jnp.zeros_like(acc)

            acc[...] += part()

            @pl.when(k == nk - 1)
            def _():
                epilogue(acc[...], extra, outs, ids)

        if comm:
            @pl.when((ids[0] == grid[0] - 1) & (ids[1] == grid[1] - 1) & (k == nk - 1))
            def _():
                finish()

    sem = ("arbitrary",) * 3 if comm else ("parallel", "parallel", "arbitrary")
    res = pl.pallas_call(
        body, name=name, grid=grid, in_specs=in_specs, out_specs=out_specs, out_shape=out_shape,
        scratch_shapes=scratch, input_output_aliases=aliases, compiler_params=_params(sem))(*operands)
    return (res[:n_out], res[n_out:]) if comm else res


def _store(dtype):
    def epilogue(r, extra, outs, ids):
        outs[0][...] = r.astype(dtype)
    return epilogue


def _ret(res, comm, single=True):
    if comm:
        return (res[0][0] if single else res[0]), res[1]
    return res[0] if single else res


def _slots_per_step(s, kb):
    return max(n for n in (1, 2, 4) if s % n == 0 and (n == 1 or n * kb <= WIDE))


def mm_nn_col(name, a, wc, off, nb, comm=None):
    m, d = a.shape
    s = wc.shape[0]
    tn = _pick(nb, WIDE, MXU)
    tm = _pick(m, 528 if tn > 1536 else 1056, HALO)
    npb, ob = nb // tn, off // tn
    assert off % tn == 0
    return _ret(_mm_call(
        name, (s * npb, m // tm, 1),
        [pl.BlockSpec((tm, d), lambda j, i, k: (i, 0)),
         pl.BlockSpec((None, d, tn), lambda j, i, k: (j // npb, 0, ob + j % npb))],
        [pl.BlockSpec((tm, tn), lambda j, i, k: (i, j))],
        [jax.ShapeDtypeStruct((m, s * nb), BF16)],
        (((1,), (0,)), ((), ())), (a, wc), None, _store(BF16), comm=comm), comm)


def mm_nn_row(name, a, wr, off, kb, res, mod, gi, comm=None):
    m = a.shape[0]
    s, _, d = wr.shape
    sp = _slots_per_step(s, kb)
    tm, tn = _pick(m, 1056, HALO), _pick(d, 1024)
    ob = off // kb
    assert off % kb == 0

    def epilogue(r, extra, outs, ids):
        res_ref, mod_ref = extra
        rows = ids[0] * tm + lax.broadcasted_iota(jnp.int32, (tm, 1), 0)
        g = jnp.where(rows < CTX, mod_ref[0, gi:gi + 1, :], mod_ref[1, gi:gi + 1, :])
        outs[0][...] = res_ref[...] + g * r
        outs[1][...] = r.astype(BF16)

    return _ret(_mm_call(
        name, (m // tm, d // tn, s // sp),
        [pl.BlockSpec((tm, sp * kb), lambda i, j, k: (i, k)),
         pl.BlockSpec((sp, kb, tn), lambda i, j, k: (k, ob, j)),
         pl.BlockSpec((tm, tn), lambda i, j, k: (i, j)),
         pl.BlockSpec((2, 8, tn), lambda i, j, k: (0, 0, j))],
        [pl.BlockSpec((tm, tn), lambda i, j, k: (i, j)), pl.BlockSpec((tm, tn), lambda i, j, k: (i, j))],
        [jax.ShapeDtypeStruct((m, d), F32), jax.ShapeDtypeStruct((m, d), BF16)],
        (((1,), (0,)), ((), ())), (a, wr, res, mod), (tm, tn), epilogue, n_extra=2, comm=comm, merge_b=True),
        comm, single=False)


def mm_nt_col(name, dy, wc, off, nb):
    m = dy.shape[0]
    s, d, _ = wc.shape
    tc = _pick(nb, WIDE, MXU)
    tm = _pick(m, 528 if tc > 1536 else 768, HALO)
    npb, ob = nb // tc, off // tc
    return _mm_call(
        name, (m // tm, 1, s * npb),
        [pl.BlockSpec((tm, tc), lambda i, j, k: (i, k)),
         pl.BlockSpec((None, d, tc), lambda i, j, k: (k // npb, 0, ob + k % npb))],
        [pl.BlockSpec((tm, d), lambda i, j, k: (i, 0))],
        [jax.ShapeDtypeStruct((m, d), F32)],
        NT, (dy, wc), (tm, d), _store(F32))[0]


def mm_nt_row(name, dy, wr, off, kb):
    m, d = dy.shape
    s = wr.shape[0]
    sp = _slots_per_step(s, kb)
    tm = _pick(m, 528 if sp * kb > 2048 else 1056, HALO)
    ob = off // kb
    return _mm_call(
        name, (s // sp, m // tm, 1),
        [pl.BlockSpec((tm, d), lambda j, i, k: (i, 0)),
         pl.BlockSpec((sp, kb, d), lambda j, i, k: (j, ob, 0))],
        [pl.BlockSpec((tm, sp * kb), lambda j, i, k: (i, j))],
        [jax.ShapeDtypeStruct((m, s * kb), BF16)],
        NT, (dy, wr), None, _store(BF16), merge_b=True)[0]


def mm_tn_col(name, a, dy, gbuf, off, nb, comm=None):
    t, d = a.shape
    s = gbuf.shape[0]
    tn, tt = _pick(nb, WIDE, MXU), _pick(t, 768, MXU)
    tka = _pick(d, 1024 if tn > 1536 else 2048)
    npb, ob = nb // tn, off // tn
    return _ret(_mm_call(
        name, (d // tka, s * npb, t // tt),
        [pl.BlockSpec((tt, tka), lambda i, j, k: (k, i)),
         pl.BlockSpec((tt, tn), lambda i, j, k: (k, j)),
         pl.BlockSpec(memory_space=pl.ANY)],
        [pl.BlockSpec((None, tka, tn), lambda i, j, k: (j // npb, i, ob + j % npb))],
        [jax.ShapeDtypeStruct(gbuf.shape, BF16)],
        TN, (a, dy, gbuf), (tka, tn), _store(BF16), n_extra=1, aliases={2: 0}, comm=comm), comm)


def mm_tn_row(name, act, dy, gbuf, off, kb, comm=None):
    t, d = dy.shape
    s = gbuf.shape[0]
    tt = _pick(t, 768, MXU)
    ob = off // kb
    return _ret(_mm_call(
        name, (s, 1, t // tt),
        [pl.BlockSpec((tt, kb), lambda i, j, k: (k, i)),
         pl.BlockSpec((tt, d), lambda i, j, k: (k, 0)),
         pl.BlockSpec(memory_space=pl.ANY)],
        [pl.BlockSpec((None, kb, d), lambda i, j, k: (i, ob, 0))],
        [jax.ShapeDtypeStruct(gbuf.shape, BF16)],
        TN, (act, dy, gbuf), (kb, d), _store(BF16), n_extra=1, aliases={2: 0}, comm=comm), comm)


def cast_pack(name, w, l, buf, off, col, chip):
    _, k, n = w.shape
    if col:
        tr = _pick(k, 512, HALO)
        ob = off // n
        out_spec = pl.BlockSpec((None, tr, n), lambda i, s: (s[0], i, ob))
    else:
        tr = _pick(k, 704, HALO)
        ob = off // tr
        out_spec = pl.BlockSpec((None, tr, n), lambda i, s: (s[0], ob + i, 0))

    def body(s_ref, w_ref, buf_ref, out_ref):
        out_ref[...] = w_ref[...].astype(BF16)

    return pl.pallas_call(
        body, name=name,
        grid_spec=pltpu.PrefetchScalarGridSpec(
            num_scalar_prefetch=1, grid=(k // tr,),
            in_specs=[pl.BlockSpec((None, tr, n), lambda i, s: (l, i, 0)), pl.BlockSpec(memory_space=pl.ANY)],
            out_specs=out_spec),
        out_shape=jax.ShapeDtypeStruct(buf.shape, BF16),
        input_output_aliases={2: 0}, compiler_params=_params(("parallel",)))(chip, w, buf)


def norm_mod(name, x, mod, sh, sc):
    t, d = x.shape

    def body(x_ref, mod_ref, h_ref):
        seg = jnp.minimum(pl.program_id(0), 1)
        xv = x_ref[...]
        r = lax.rsqrt(jnp.mean(xv * xv, axis=-1, keepdims=True) + EPS)
        m = mod_ref[seg]
        h_ref[...] = ((xv * r) * (1.0 + m[sc:sc + 1, :]) + m[sh:sh + 1, :]).astype(BF16)

    return pl.pallas_call(
        body, name=name, grid=(t // RB,),
        in_specs=[pl.BlockSpec((RB, d), lambda i: (i, 0)), pl.BlockSpec((2, 8, d), lambda i: (0, 0, 0))],
        out_specs=pl.BlockSpec((RB, d), lambda i: (i, 0)),
        out_shape=jax.ShapeDtypeStruct((t, d), BF16), compiler_params=_params(("parallel",)))(x, mod)


def sc_gate_fwd(name, u, cw):
    t = u.shape[0]
    d = u.shape[1] // 3
    nt, tc = t // RB, _pick(d, 512)
    prev, nxt = _halo_specs(3 * d, t)

    def body(u_ref, up_ref, un_ref, cw_ref, z_ref):
        first, last = _seg_flags(pl.program_id(0), nt)
        for j in range(d // tc):
            c0 = j * tc
            gb = u_ref[:, c0:c0 + tc].astype(F32)
            w = u_ref[:, d + c0:d + c0 + tc].astype(F32) * u_ref[:, 2 * d + c0:2 * d + c0 + tc].astype(F32)
            pw = _row(up_ref[:, d + c0:d + c0 + tc].astype(F32) * up_ref[:, 2 * d + c0:2 * d + c0 + tc].astype(F32), HALO - 1)
            nw = _row(un_ref[:, d + c0:d + c0 + tc].astype(F32) * un_ref[:, 2 * d + c0:2 * d + c0 + tc].astype(F32), 0)
            wd, wu = _shift_rows(w, jnp.where(first, 0.0, pw), jnp.where(last, 0.0, nw))
            cwj = cw_ref[:, c0:c0 + tc]
            conv = wd * cwj[0:1] + w * cwj[1:2] + wu * cwj[2:3]
            z_ref[:, c0:c0 + tc] = (gb * conv).astype(BF16)

    return pl.pallas_call(
        body, name=name, grid=(nt,),
        in_specs=[pl.BlockSpec((RB, 3 * d), lambda i: (i, 0)), prev, nxt, pl.BlockSpec((8, d), lambda i: (0, 0))],
        out_specs=pl.BlockSpec((RB, d), lambda i: (i, 0)),
        out_shape=jax.ShapeDtypeStruct((t, d), BF16), compiler_params=_params(("parallel",)))(u, u, u, cw)


def sc_gate_bwd(name, u, dz, cw):
    t = u.shape[0]
    d = u.shape[1] // 3
    nt, tc = t // RB, _pick(d, 512)
    prev, nxt = _halo_specs(3 * d, t)
    dprev, dnxt = _halo_specs(d, t)

    def body(u_ref, up_ref, un_ref, dz_ref, dzp_ref, dzn_ref, cw_ref, du_ref, dcw_ref):
        i = pl.program_id(0)
        first, last = _seg_flags(i, nt)

        @pl.when(i == 0)
        def _():
            dcw_ref[...] = jnp.zeros_like(dcw_ref)

        for j in range(d // tc):
            c0 = j * tc
            sl0, sl1, sl2 = slice(c0, c0 + tc), slice(d + c0, d + c0 + tc), slice(2 * d + c0, 2 * d + c0 + tc)
            gb, gc, v = u_ref[:, sl0].astype(F32), u_ref[:, sl1].astype(F32), u_ref[:, sl2].astype(F32)
            w = gc * v
            pw = _row(up_ref[:, sl1].astype(F32) * up_ref[:, sl2].astype(F32), HALO - 1)
            nw = _row(un_ref[:, sl1].astype(F32) * un_ref[:, sl2].astype(F32), 0)
            wd, wu = _shift_rows(w, jnp.where(first, 0.0, pw), jnp.where(last, 0.0, nw))
            cwj = cw_ref[:, sl0]
            cw0, cw1, cw2 = cwj[0:1], cwj[1:2], cwj[2:3]
            dzv = dz_ref[:, sl0].astype(F32)
            e = dzv * gb
            pe = _row(dzp_ref[:, sl0].astype(F32) * up_ref[:, sl0].astype(F32), HALO - 1)
            ne = _row(dzn_ref[:, sl0].astype(F32) * un_ref[:, sl0].astype(F32), 0)
            ed, eu = _shift_rows(e, jnp.where(first, 0.0, pe), jnp.where(last, 0.0, ne))
            dw = cw0 * eu + cw1 * e + cw2 * ed
            du_ref[:, sl0] = (dzv * (wd * cw0 + w * cw1 + wu * cw2)).astype(BF16)
            du_ref[:, sl1] = (dw * v).astype(BF16)
            du_ref[:, sl2] = (dw * gc).astype(BF16)
            dcw_ref[:, sl0] += _rows3(jnp.sum(e * wd, axis=0, keepdims=True), jnp.sum(e * w, axis=0, keepdims=True),
                                      jnp.sum(e * wu, axis=0, keepdims=True), tc)

    return pl.pallas_call(
        body, name=name, grid=(nt,),
        in_specs=[pl.BlockSpec((RB, 3 * d), lambda i: (i, 0)), prev, nxt,
                  pl.BlockSpec((RB, d), lambda i: (i, 0)), dprev, dnxt, pl.BlockSpec((8, d), lambda i: (0, 0))],
        out_specs=[pl.BlockSpec((RB, 3 * d), lambda i: (i, 0)), pl.BlockSpec((8, d), lambda i: (0, 0))],
        out_shape=[jax.ShapeDtypeStruct((t, 3 * d), BF16), jax.ShapeDtypeStruct((8, d), F32)],
        compiler_params=_params(("arbitrary",)))(u, u, u, dz, dz, dz, cw)


def ffn_act_fwd(name, up, cw):
    t = up.shape[0]
    ff = up.shape[1] // 2
    nt, tc = t // RB, _pick(ff, 1408)
    prev, nxt = _halo_specs(2 * ff, t)

    def body(up_ref, upp_ref, upn_ref, cw_ref, a_ref):
        first, last = _seg_flags(pl.program_id(0), nt)
        for j in range(ff // tc):
            sg, sv = slice(j * tc, (j + 1) * tc), slice(ff + j * tc, ff + (j + 1) * tc)
            gate = up_ref[:, sg].astype(F32)
            pg = _row(upp_ref[:, sg].astype(F32), HALO - 1)
            ng = _row(upn_ref[:, sg].astype(F32), 0)
            gd, gu = _shift_rows(gate, jnp.where(first, 0.0, pg), jnp.where(last, 0.0, ng))
            cwj = cw_ref[:, sg]
            g = gd * cwj[0:1] + gate * cwj[1:2] + gu * cwj[2:3] + cwj[3:4]
            a_ref[:, sg] = (g * _sigmoid(g) * up_ref[:, sv].astype(F32)).astype(BF16)

    return pl.pallas_call(
        body, name=name, grid=(nt,),
        in_specs=[pl.BlockSpec((RB, 2 * ff), lambda i: (i, 0)), prev, nxt, pl.BlockSpec((8, ff), lambda i: (0, 0))],
        out_specs=pl.BlockSpec((RB, ff), lambda i: (i, 0)),
        out_shape=jax.ShapeDtypeStruct((t, ff), BF16), compiler_params=_params(("parallel",)))(up, up, up, cw)


def ffn_act_bwd(name, up, da, cw):
    t = up.shape[0]
    ff = up.shape[1] // 2
    nt, tc = t // RB, _pick(ff, 1408)
    prev, nxt = _halo_specs(2 * ff, t)
    dprev, dnxt = _halo_specs(ff, t)

    def dsilu(g):
        s = _sigmoid(g)
        return s * (1.0 + g * (1.0 - s))

    def body(up_ref, upp_ref, upn_ref, da_ref, dap_ref, dan_ref, cw_ref, dup_ref, acc_ref):
        i = pl.program_id(0)
        first, last = _seg_flags(i, nt)

        @pl.when(i == 0)
        def _():
            acc_ref[...] = jnp.zeros_like(acc_ref)

        for j in range(ff // tc):
            sg, sv = slice(j * tc, (j + 1) * tc), slice(ff + j * tc, ff + (j + 1) * tc)
            gate, val, dav = up_ref[:, sg].astype(F32), up_ref[:, sv].astype(F32), da_ref[:, sg].astype(F32)
            pgt, ngt = upp_ref[:, sg].astype(F32), upn_ref[:, sg].astype(F32)
            pg1, pg2 = _row(pgt, HALO - 1), _row(pgt, HALO - 2)
            ng1, ng2 = _row(ngt, 0), _row(ngt, 1)
            cwj = cw_ref[:, sg]
            cw0, cw1, cw2, b = cwj[0:1], cwj[1:2], cwj[2:3], cwj[3:4]
            gd, gu = _shift_rows(gate, jnp.where(first, 0.0, pg1), jnp.where(last, 0.0, ng1))
            g = gd * cw0 + gate * cw1 + gu * cw2 + b
            g_p = pg2 * cw0 + pg1 * cw1 + _row(gate, 0) * cw2 + b
            g_n = _row(gate, RB - 1) * cw0 + ng1 * cw1 + ng2 * cw2 + b
            dg = dav * val * dsilu(g)
            dg_p = _row(dap_ref[:, sg].astype(F32) * upp_ref[:, sv].astype(F32), HALO - 1) * dsilu(g_p)
            dg_n = _row(dan_ref[:, sg].astype(F32) * upn_ref[:, sv].astype(F32), 0) * dsilu(g_n)
            dgd, dgu = _shift_rows(dg, jnp.where(first, 0.0, dg_p), jnp.where(last, 0.0, dg_n))
            dup_ref[:, sg] = (cw0 * dgu + cw1 * dg + cw2 * dgd).astype(BF16)
            dup_ref[:, sv] = (dav * g * _sigmoid(g)).astype(BF16)
            rows = lax.broadcasted_iota(jnp.int32, (8, tc), 0)
            acc_ref[:, sg] += (_rows3(jnp.sum(dg * gd, axis=0, keepdims=True), jnp.sum(dg * gate, axis=0, keepdims=True),
                                      jnp.sum(dg * gu, axis=0, keepdims=True), tc)
                               + jnp.where(rows == 3, jnp.sum(dg, axis=0, keepdims=True), 0.0))

    return pl.pallas_call(
        body, name=name, grid=(nt,),
        in_specs=[pl.BlockSpec((RB, 2 * ff), lambda i: (i, 0)), prev, nxt,
                  pl.BlockSpec((RB, ff), lambda i: (i, 0)), dprev, dnxt, pl.BlockSpec((8, ff), lambda i: (0, 0))],
        out_specs=[pl.BlockSpec((RB, 2 * ff), lambda i: (i, 0)), pl.BlockSpec((8, ff), lambda i: (0, 0))],
        out_shape=[jax.ShapeDtypeStruct((t, 2 * ff), BF16), jax.ShapeDtypeStruct((8, ff), F32)],
        compiler_params=_params(("arbitrary",)))(up, up, up, da, da, da, cw)


def _rot(z):
    w = z.shape[1]
    lane = lax.broadcasted_iota(jnp.int32, z.shape, 1)
    return jnp.where((lane % 64) < 32, -pltpu.roll(z, w - 32, 1), pltpu.roll(z, 32, 1))


def rope_fwd(name, qkv, cos, sin, gains, dq, dkv):
    t, nqkv = qkv.shape
    nh, nkv = dq // HEAD, dkv // HEAD

    def body(qkv_ref, cos_ref, sin_ref, g_ref, qr_ref, kr_ref):
        cs, sn = cos_ref[...], sin_ref[...]
        for hd in range(nh + nkv):
            c0 = hd * HEAD
            xh = qkv_ref[:, c0:c0 + HEAD].astype(F32)
            r = lax.rsqrt(jnp.mean(xh * xh, axis=-1, keepdims=True) + EPS)
            y = xh * r * (g_ref[0:1, :] if hd < nh else g_ref[1:2, :])
            yr = (y * cs + _rot(y) * sn).astype(BF16)
            if hd < nh:
                qr_ref[:, c0:c0 + HEAD] = yr
            else:
                kr_ref[:, c0 - dq:c0 - dq + HEAD] = yr

    return pl.pallas_call(
        body, name=name, grid=(t // RB,),
        in_specs=[pl.BlockSpec((RB, nqkv), lambda i: (i, 0)), pl.BlockSpec((RB, HEAD), lambda i: (i, 0)),
                  pl.BlockSpec((RB, HEAD), lambda i: (i, 0)), pl.BlockSpec((8, HEAD), lambda i: (0, 0))],
        out_specs=[pl.BlockSpec((RB, dq), lambda i: (i, 0)), pl.BlockSpec((RB, dkv), lambda i: (i, 0))],
        out_shape=[jax.ShapeDtypeStruct((t, dq), BF16), jax.ShapeDtypeStruct((t, dkv), BF16)],
        compiler_params=_params(("parallel",)))(qkv, cos, sin, gains)


def rope_bwd(name, qkv, dqr, dkr, dv, cos, sin, gains):
    t, nqkv = qkv.shape
    dq, dkv = dqr.shape[1], dkr.shape[1]
    nh, nkv = dq // HEAD, dkv // HEAD

    def body(qkv_ref, dq_ref, dk_ref, dv_ref, cos_ref, sin_ref, g_ref, out_ref, dg_ref):
        @pl.when(pl.program_id(0) == 0)
        def _():
            dg_ref[...] = jnp.zeros_like(dg_ref)

        cs, sn = cos_ref[...], sin_ref[...]
        zero = jnp.zeros((1, HEAD), F32)
        gq, gk = zero, zero
        for hd in range(nh + nkv):
            c0 = hd * HEAD
            xh = qkv_ref[:, c0:c0 + HEAD].astype(F32)
            r = lax.rsqrt(jnp.mean(xh * xh, axis=-1, keepdims=True) + EPS)
            xhat = xh * r
            dy = dq_ref[:, c0:c0 + HEAD] if hd < nh else dk_ref[:, c0 - dq:c0 - dq + HEAD]
            tt = dy * cs - _rot(dy * sn)
            gsum = jnp.sum(tt * xhat, axis=0, keepdims=True)
            if hd < nh:
                gq = gq + gsum
            else:
                gk = gk + gsum
            dxh = tt * (g_ref[0:1, :] if hd < nh else g_ref[1:2, :])
            dx = r * (dxh - xhat * jnp.mean(dxh * xhat, axis=-1, keepdims=True))
            out_ref[:, c0:c0 + HEAD] = dx.astype(BF16)
        out_ref[:, dq + dkv:] = dv_ref[...].astype(BF16)
        dg_ref[...] += _rows3(gq, gk, zero, HEAD)

    return pl.pallas_call(
        body, name=name, grid=(t // RB,),
        in_specs=[pl.BlockSpec((RB, nqkv), lambda i: (i, 0)), pl.BlockSpec((RB, dq), lambda i: (i, 0)),
                  pl.BlockSpec((RB, dkv), lambda i: (i, 0)), pl.BlockSpec((RB, dkv), lambda i: (i, 0)),
                  pl.BlockSpec((RB, HEAD), lambda i: (i, 0)), pl.BlockSpec((RB, HEAD), lambda i: (i, 0)),
                  pl.BlockSpec((8, HEAD), lambda i: (0, 0))],
        out_specs=[pl.BlockSpec((RB, nqkv), lambda i: (i, 0)), pl.BlockSpec((8, HEAD), lambda i: (0, 0))],
        out_shape=[jax.ShapeDtypeStruct((t, nqkv), BF16), jax.ShapeDtypeStruct((8, HEAD), F32)],
        compiler_params=_params(("arbitrary",)))(qkv, dqr, dkr, dv, cos, sin, gains)


def resid_bwd(name, dx, dh, x, mod_n, sh, sc, y_prev=None, mod_g=None, gi=0):
    t, d = x.shape
    has_prev = y_prev is not None

    def body(*refs):
        if has_prev:
            dx_ref, dh_ref, x_ref, mn_ref, y_ref, mg_ref, dxo_ref, dy_ref, acc_ref = refs
        else:
            dx_ref, dh_ref, x_ref, mn_ref, dxo_ref, acc_ref = refs
        i = pl.program_id(0)
        seg = jnp.minimum(i, 1)

        @pl.when(i == 0)
        def _():
            acc_ref[...] = jnp.zeros_like(acc_ref)

        xv, dhv = x_ref[...], dh_ref[...]
        r = lax.rsqrt(jnp.mean(xv * xv, axis=-1, keepdims=True) + EPS)
        xhat = xv * r
        m = mn_ref[seg]
        dxh = dhv * (1.0 + m[sc:sc + 1, :])
        dxo = dx_ref[...] + r * (dxh - xhat * jnp.mean(dxh * xhat, axis=-1, keepdims=True))
        dxo_ref[...] = dxo
        s2 = jnp.zeros((1, d), F32)
        if has_prev:
            dy_ref[...] = (mg_ref[seg][gi:gi + 1, :] * dxo).astype(BF16)
            s2 = jnp.sum(dxo * y_ref[...].astype(F32), axis=0, keepdims=True)
        acc_ref[seg] = acc_ref[seg] + _rows3(jnp.sum(dhv, axis=0, keepdims=True),
                                             jnp.sum(dhv * xhat, axis=0, keepdims=True), s2, d)

    row = pl.BlockSpec((RB, d), lambda i: (i, 0))
    modspec = pl.BlockSpec((2, 8, d), lambda i: (0, 0, 0))
    in_specs, operands = [row, row, row, modspec], [dx, dh, x, mod_n]
    out_specs, out_shape = [row], [jax.ShapeDtypeStruct((t, d), F32)]
    if has_prev:
        in_specs += [row, modspec]
        operands += [y_prev, mod_g]
        out_specs.append(row)
        out_shape.append(jax.ShapeDtypeStruct((t, d), BF16))
    out_specs.append(modspec)
    out_shape.append(jax.ShapeDtypeStruct((2, 8, d), F32))
    return pl.pallas_call(body, name=name, grid=(t // RB,), in_specs=in_specs, out_specs=out_specs,
                          out_shape=out_shape, compiler_params=_params(("arbitrary",)))(*operands)


def loss_head(name, xf, target, y_last, mod, gi):
    t, d = xf.shape

    def body(x_ref, t_ref, y_ref, mod_ref, dx_ref, dy_ref, acc_ref, lp_ref):
        i = pl.program_id(0)
        seg = jnp.minimum(i, 1)

        @pl.when(i == 0)
        def _():
            acc_ref[...] = jnp.zeros_like(acc_ref)
            lp_ref[...] = jnp.zeros_like(lp_ref)

        lat = i >= 1
        err = jnp.where(lat, x_ref[...] - t_ref[...], 0.0)
        dxv = err / d
        dx_ref[...] = dxv
        dy_ref[...] = (mod_ref[seg][gi:gi + 1, :] * dxv).astype(BF16)
        zero = jnp.zeros((1, d), F32)
        lp_ref[...] += _rows3(jnp.sum(err * err, axis=0, keepdims=True), zero, zero, d)
        acc_ref[seg] = acc_ref[seg] + _rows3(zero, zero, jnp.sum(dxv * y_ref[...].astype(F32), axis=0, keepdims=True), d)

    row = pl.BlockSpec((RB, d), lambda i: (i, 0))
    modspec = pl.BlockSpec((2, 8, d), lambda i: (0, 0, 0))
    return pl.pallas_call(
        body, name=name, grid=(t // RB,),
        in_specs=[row, pl.BlockSpec((RB, d), lambda i: (jnp.maximum(i - 1, 0), 0)), row, modspec],
        out_specs=[row, row, modspec, pl.BlockSpec((8, d), lambda i: (0, 0))],
        out_shape=[jax.ShapeDtypeStruct((t, d), F32), jax.ShapeDtypeStruct((t, d), BF16),
                   jax.ShapeDtypeStruct((2, 8, d), F32), jax.ShapeDtypeStruct((8, d), F32)],
        compiler_params=_params(("arbitrary",)))(xf, target, y_last, mod)


def _kv_specs(width, colblk, nbk):
    return [pl.BlockSpec((CTX, width), lambda i: (0, colblk)),
            pl.BlockSpec((BLK, width), lambda i: (jnp.maximum(i - 1, 0), colblk)),
            pl.BlockSpec((BLK, width), lambda i: (i, colblk)),
            pl.BlockSpec((BLK, width), lambda i: (jnp.minimum(i + 1, nbk - 1), colblk))]


def _band_mask(i, seq):
    nk = CTX + 3 * BLK
    qrow = lax.broadcasted_iota(jnp.int32, (GROUP * BLK, nk), 0) % BLK
    col = lax.broadcasted_iota(jnp.int32, (GROUP * BLK, nk), 1)
    cb = col - CTX
    kpos = (i - 3) * BLK + cb
    band = (i >= 2) & (jnp.abs(BLK + qrow - cb) <= WINDOW) & (kpos >= 0) & (kpos < seq)
    return (col < CTX) | band


def _stack_heads(ref, h):
    return jnp.concatenate([ref[:, (h * GROUP + g) * HEAD:(h * GROUP + g + 1) * HEAD] for g in range(GROUP)], axis=0)


def _stack_cols(v, h):
    return jnp.concatenate([_get_col(v, h * GROUP + g) for g in range(GROUP)], axis=0)


def _sink_col(sink_ref, h):
    rowg = lax.broadcasted_iota(jnp.int32, (GROUP * BLK, 1), 0) // BLK
    sk = jnp.full((GROUP * BLK, 1), sink_ref[h * GROUP], F32)
    for g in range(1, GROUP):
        sk = jnp.where(rowg == g, sink_ref[h * GROUP + g], sk)
    return sk


def attn_fwd(name, qr, kr, qkv, sink, seq):
    t, dq = qr.shape
    dkv = kr.shape[1]
    nbk, nkv = t // BLK, dkv // HEAD
    vcol = (dq + dkv) // dkv
    scale = HEAD ** -0.5

    def body(sink_ref, q_ref, kc, kp, ko, kn, vc, vp, vo, vn, o_ref, lse_ref, lset_ref):
        i = pl.program_id(0)
        mask = _band_mask(i, seq)
        lse = jnp.zeros((BLK, LANE), F32)
        for h in range(nkv):
            hs = slice(h * HEAD, (h + 1) * HEAD)
            k = jnp.concatenate([kc[:, hs], kp[:, hs], ko[:, hs], kn[:, hs]], axis=0)
            v = jnp.concatenate([vc[:, hs], vp[:, hs], vo[:, hs], vn[:, hs]], axis=0)
            q4 = _stack_heads(q_ref, h)
            s = jnp.where(mask, lax.dot_general(q4, k, NT, preferred_element_type=F32) * scale, NEG)
            sk = _sink_col(sink_ref, h)
            m = jnp.maximum(jnp.max(s, axis=-1, keepdims=True), sk)
            e = jnp.exp(s - m)
            den = jnp.sum(e, axis=-1, keepdims=True) + jnp.exp(sk - m)
            o4 = jnp.dot((e * (1.0 / den)).astype(BF16), v, preferred_element_type=F32)
            l4 = m + jnp.log(den)
            for g in range(GROUP):
                hg = h * GROUP + g
                o_ref[:, hg * HEAD:(hg + 1) * HEAD] = o4[g * BLK:(g + 1) * BLK].astype(BF16)
                lse = _put_col(lse, hg, l4[g * BLK:(g + 1) * BLK])
        lse_ref[...] = lse
        lset_ref[...] = lse.T[:nh]

    nh = dq // HEAD
    return pl.pallas_call(
        body, name=name, grid=(nbk,),
        in_specs=[pl.BlockSpec(memory_space=pltpu.SMEM), pl.BlockSpec((BLK, dq), lambda i: (i, 0))]
        + _kv_specs(dkv, 0, nbk) + _kv_specs(dkv, vcol, nbk),
        out_specs=[pl.BlockSpec((BLK, dq), lambda i: (i, 0)), pl.BlockSpec((BLK, LANE), lambda i: (i, 0)),
                   pl.BlockSpec((nh, BLK), lambda i: (0, i))],
        out_shape=[jax.ShapeDtypeStruct((t, dq), BF16), jax.ShapeDtypeStruct((t, LANE), F32),
                   jax.ShapeDtypeStruct((nh, t), F32)],
        compiler_params=_params(("parallel",)))(sink, qr, kr, kr, kr, kr, qkv, qkv, qkv, qkv)


def attn_bwd_q(name, qr, kr, qkv, sink, do, o, lse, seq):
    t, dq = qr.shape
    dkv = kr.shape[1]
    nbk, nkv = t // BLK, dkv // HEAD
    vcol = (dq + dkv) // dkv
    scale = HEAD ** -0.5

    def body(sink_ref, q_ref, kc, kp, ko, kn, vc, vp, vo, vn, do_ref, o_ref, lse_ref,
             dq_ref, dl_ref, dkc_ref, dvc_ref, ds_ref):
        i = pl.program_id(0)

        @pl.when(i == 0)
        def _():
            dkc_ref[...] = jnp.zeros_like(dkc_ref)
            dvc_ref[...] = jnp.zeros_like(dvc_ref)
            ds_ref[...] = jnp.zeros_like(ds_ref)

        mask = _band_mask(i, seq)
        lse = lse_ref[...]
        delta = jnp.zeros((BLK, LANE), F32)
        dsink = jnp.zeros((8, LANE), F32)
        for h in range(nkv):
            hs = slice(h * HEAD, (h + 1) * HEAD)
            k = jnp.concatenate([kc[:, hs], kp[:, hs], ko[:, hs], kn[:, hs]], axis=0)
            v = jnp.concatenate([vc[:, hs], vp[:, hs], vo[:, hs], vn[:, hs]], axis=0)
            q4, do4 = _stack_heads(q_ref, h), _stack_heads(do_ref, h)
            d4 = jnp.sum(do4.astype(F32) * _stack_heads(o_ref, h).astype(F32), axis=-1, keepdims=True)
            l4 = _stack_cols(lse, h)
            s = jnp.where(mask, lax.dot_general(q4, k, NT, preferred_element_type=F32) * scale, NEG)
            p = jnp.exp(s - l4)
            dp = lax.dot_general(do4, v, NT, preferred_element_type=F32)
            dsb = (p * (dp - d4) * scale).astype(BF16)
            pb = p.astype(BF16)
            dq4 = jnp.dot(dsb, k, preferred_element_type=F32)
            dkc_ref[:, hs] += lax.dot_general(dsb[:, :CTX], q4, TN, preferred_element_type=F32)
            dvc_ref[:, hs] += lax.dot_general(pb[:, :CTX], do4, TN, preferred_element_type=F32)
            dsk = -jnp.exp(_sink_col(sink_ref, h) - l4) * d4
            for g in range(GROUP):
                hg = h * GROUP + g
                rs = slice(g * BLK, (g + 1) * BLK)
                dq_ref[:, hg * HEAD:(hg + 1) * HEAD] = dq4[rs]
                delta = _put_col(delta, hg, d4[rs])
                dsink = _put_col(dsink, hg, jnp.sum(dsk[rs], axis=0, keepdims=True))
        dl_ref[...] = delta.T[:nh]
        rows = lax.broadcasted_iota(jnp.int32, (8, LANE), 0)
        ds_ref[...] += jnp.where(rows == 0, dsink, 0.0)

    nh = dq // HEAD
    blk = lambda w: pl.BlockSpec((BLK, w), lambda i: (i, 0))
    const = lambda r, w: pl.BlockSpec((r, w), lambda i: (0, 0))
    return pl.pallas_call(
        body, name=name, grid=(nbk,),
        in_specs=[pl.BlockSpec(memory_space=pltpu.SMEM), blk(dq)] + _kv_specs(dkv, 0, nbk) + _kv_specs(dkv, vcol, nbk)
        + [blk(dq), blk(dq), blk(LANE)],
        out_specs=[blk(dq), pl.BlockSpec((nh, BLK), lambda i: (0, i)), const(CTX, dkv), const(CTX, dkv), const(8, LANE)],
        out_shape=[jax.ShapeDtypeStruct((t, dq), F32), jax.ShapeDtypeStruct((nh, t), F32),
                   jax.ShapeDtypeStruct((CTX, dkv), F32), jax.ShapeDtypeStruct((CTX, dkv), F32),
                   jax.ShapeDtypeStruct((8, LANE), F32)],
        compiler_params=_params(("arbitrary",)))(sink, qr, kr, kr, kr, kr, qkv, qkv, qkv, qkv, do, o, lse)


def attn_bwd_kv(name, qr, kr, qkv, do, lset, deltat, seq):
    t, dq = qr.shape
    dkv = kr.shape[1]
    nbk, nbl, nkv, nh = t // BLK, seq // BLK, dkv // HEAD, dq // HEAD
    cb = CTX // BLK
    vcol = (dq + dkv) // dkv
    scale = HEAD ** -0.5

    def qspec(w, d):
        return pl.BlockSpec((BLK, w), lambda j: (jnp.clip(j + cb + d, cb, nbk - 1), 0))

    def tspec(d):
        return pl.BlockSpec((nh, BLK), lambda j: (0, jnp.clip(j + cb + d, cb, nbk - 1)))

    def stack_rows(v, h):
        return jnp.concatenate([v[h * GROUP + g:h * GROUP + g + 1, :] for g in range(GROUP)], axis=1)

    def body(k_ref, v_ref, *refs):
        dk_ref, dv_ref = refs[-2], refs[-1]
        j = pl.program_id(0)
        krow = lax.broadcasted_iota(jnp.int32, (BLK, GROUP * BLK), 0)
        qcol = lax.broadcasted_iota(jnp.int32, (BLK, GROUP * BLK), 1) % BLK
        for h in range(nkv):
            hs = slice(h * HEAD, (h + 1) * HEAD)
            kh, vh = k_ref[:, hs], v_ref[:, hs]
            dk_h = jnp.zeros((BLK, HEAD), F32)
            dv_h = jnp.zeros((BLK, HEAD), F32)
            for di, d in enumerate((-1, 0, 1)):
                q_ref, do_ref, lse_ref, dl_ref = refs[4 * di:4 * di + 4]
                n = j + d
                msk = (n >= 0) & (n < nbl) & (jnp.abs(d * BLK + qcol - krow) <= WINDOW)
                q4, do4 = _stack_heads(q_ref, h), _stack_heads(do_ref, h)
                l4, d4 = stack_rows(lse_ref[...], h), stack_rows(dl_ref[...], h)
                s = jnp.where(msk, lax.dot_general(kh, q4, NT, preferred_element_type=F32) * scale, NEG)
                p = jnp.exp(s - l4)
                dv_h += jnp.dot(p.astype(BF16), do4, preferred_element_type=F32)
                dp = lax.dot_general(vh, do4, NT, preferred_element_type=F32)
                dk_h += jnp.dot((p * (dp - d4) * scale).astype(BF16), q4, preferred_element_type=F32)
            dk_ref[:, hs] = dk_h
            dv_ref[:, hs] = dv_h

    in_specs = [pl.BlockSpec((BLK, dkv), lambda j: (j + cb, 0)), pl.BlockSpec((BLK, dkv), lambda j: (j + cb, vcol))]
    operands = [kr, qkv]
    for d in (-1, 0, 1):
        in_specs += [qspec(dq, d), qspec(dq, d), tspec(d), tspec(d)]
        operands += [qr, do, lset, deltat]
    return pl.pallas_call(
        body, name=name, grid=(nbl,), in_specs=in_specs,
        out_specs=[pl.BlockSpec((BLK, dkv), lambda j: (j, 0)), pl.BlockSpec((BLK, dkv), lambda j: (j, 0))],
        out_shape=[jax.ShapeDtypeStruct((seq, dkv), F32), jax.ShapeDtypeStruct((seq, dkv), F32)],
        compiler_params=_params(("parallel",)))(*operands)


def ada_fwd(name, cond, w_ada):
    nl, d, n = w_ada.shape
    tn = _pick(n, 1024)

    def body(c_ref, w_ref, out_ref):
        cv = c_ref[...]
        out_ref[...] = jnp.dot((cv * _sigmoid(cv)).astype(BF16), w_ref[...].astype(BF16), preferred_element_type=F32)

    return pl.pallas_call(
        body, name=name, grid=(nl, n // tn),
        in_specs=[pl.BlockSpec((16, d), lambda l, j: (0, 0)), pl.BlockSpec((None, d, tn), lambda l, j: (l, 0, j))],
        out_specs=pl.BlockSpec((None, 16, tn), lambda l, j: (l, 0, j)),
        out_shape=jax.ShapeDtypeStruct((nl, 16, n), F32), compiler_params=_params(("parallel", "parallel")))(cond, w_ada)


def ada_bwd_cond(name, dsum, w_ada):
    nl, d, n = w_ada.shape
    tn = _pick(n, 1024)

    def body(g_ref, w_ref, out_ref):
        @pl.when((pl.program_id(0) == 0) & (pl.program_id(1) == 0))
        def _():
            out_ref[...] = jnp.zeros_like(out_ref)

        out_ref[...] += lax.dot_general(g_ref[...].astype(BF16), w_ref[...].astype(BF16), NT, preferred_element_type=F32)

    return pl.pallas_call(
        body, name=name, grid=(nl, n // tn),
        in_specs=[pl.BlockSpec((None, 8, tn), lambda l, j: (l, 0, j)), pl.BlockSpec((None, d, tn), lambda l, j: (l, 0, j))],
        out_specs=pl.BlockSpec((8, d), lambda l, j: (0, 0)),
        out_shape=jax.ShapeDtypeStruct((8, d), F32), compiler_params=_params(("arbitrary", "arbitrary")))(dsum, w_ada)


def ada_grad_w(name, cond, rhs):
    nl, _, n = rhs.shape
    d = cond.shape[1]
    tr, tn = _pick(d, 512), _pick(n, 1024)

    def body(c_ref, r_ref, out_ref):
        cv = c_ref[...]
        out_ref[...] = lax.dot_general((cv * _sigmoid(cv)).astype(BF16), r_ref[...].astype(BF16), TN, preferred_element_type=F32)

    return pl.pallas_call(
        body, name=name, grid=(nl, d // tr, n // tn),
        in_specs=[pl.BlockSpec((16, tr), lambda l, i, j: (0, i)), pl.BlockSpec((None, 16, tn), lambda l, i, j: (l, 0, j))],
        out_specs=pl.BlockSpec((None, tr, tn), lambda l, i, j: (l, i, j)),
        out_shape=jax.ShapeDtypeStruct((nl, d, n), F32),
        compiler_params=_params(("parallel", "parallel", "parallel")))(cond, rhs)


def adamw(name, g, g_spec, w, m, v, tr):
    nl, r, c = w.shape
    spec = pl.BlockSpec((None, tr, c), lambda l, i: (l, i, 0))

    def body(g_ref, w_ref, m_ref, v_ref, go_ref, d_ref, mo_ref, vo_ref):
        gv = g_ref[...]
        mn = B1 * m_ref[...] + (1.0 - B1) * gv
        vn = B2 * v_ref[...] + (1.0 - B2) * (gv * gv)
        m_hat = mn / (1.0 - B1 ** STEP)
        v_hat = vn / (1.0 - B2 ** STEP)
        go_ref[...] = gv
        d_ref[...] = -LR * (m_hat / (jnp.sqrt(v_hat) + ADAM_EPS) + WD * w_ref[...])
        mo_ref[...] = mn
        vo_ref[...] = vn

    return pl.pallas_call(
        body, name=name, grid=(nl, r // tr), in_specs=[g_spec, spec, spec, spec], out_specs=[spec] * 4,
        out_shape=[jax.ShapeDtypeStruct(w.shape, F32)] * 4, compiler_params=_params(("parallel", "parallel")))(g, w, m, v)


def adamw_small(name, g, w, m, v):
    shape = w.shape
    r3 = lambda a: a.reshape(1, -1, shape[-1]).astype(F32)
    rows = r3(w).shape[1]
    outs = adamw(name, r3(g), pl.BlockSpec((None, rows, shape[-1]), lambda l, i: (l, i, 0)), r3(w), r3(m), r3(v), rows)
    return [o.reshape(shape) for o in outs]


def _place():
    x, y, c = lax.axis_index("x"), lax.axis_index("y"), lax.axis_index("c")
    return x, y, c, [(1 - x, y), (x, 1 - y), (1 - x, 1 - y)]


def small_allgather(name, v):
    r, w = v.shape

    def body(x_ref, out_ref, send_sems, recv_sems, local_sem):
        x, y, c, chips = _place()
        me, sibling = (x, y, c), (x, y, 1 - c)

        def slot(px, py, pc):
            return out_ref.at[4 * px + 2 * py + pc]

        def copy(k, block, to, src=None):
            return pltpu.make_async_remote_copy(
                src_ref=slot(*block) if src is None else src, dst_ref=slot(*block),
                send_sem=send_sems.at[k], recv_sem=recv_sems.at[k], device_id=to, device_id_type=MESH)

        mine = pltpu.make_async_copy(x_ref, slot(*me), local_sem)
        mine.start()
        first = [copy(0, me, sibling, src=x_ref)]
        first += [copy(1 + j, me, (*chip, c), src=x_ref) for j, chip in enumerate(chips)]
        for cp in first:
            cp.start()
        passed = [copy(4 + j, (*chip, c), sibling) for j, chip in enumerate(chips)]
        for j, chip in enumerate(chips):
            copy(1 + j, (*chip, c), me).wait_recv()
            passed[j].start()
        copy(0, sibling, me).wait_recv()
        for j, chip in enumerate(chips):
            copy(4 + j, (*chip, 1 - c), me).wait_recv()
        for cp in first + passed:
            cp.wait_send()
        mine.wait()

    return pl.pallas_call(
        body, name=name, out_shape=jax.ShapeDtypeStruct((8, r, w), v.dtype),
        in_specs=[pl.BlockSpec(memory_space=pltpu.VMEM)], out_specs=pl.BlockSpec(memory_space=pltpu.VMEM),
        scratch_shapes=[pltpu.SemaphoreType.DMA((7,)), pltpu.SemaphoreType.DMA((7,)), pltpu.SemaphoreType.DMA],
        compiler_params=pltpu.CompilerParams(vmem_limit_bytes=VMEM_LIMIT))(v)


def gather_flat(name, parts):
    flat = jnp.concatenate([p.reshape(-1).astype(F32) for p in parts])
    n = flat.shape[0]
    rows = _cdiv(n, MXU * LANE) * MXU
    flat = jnp.pad(flat, (0, rows * LANE - n))
    return small_allgather(name, flat.reshape(rows, LANE)).reshape(8, rows * LANE)


def sum8(name, g):
    p = g.shape[1]
    g3 = g.reshape(8, p // LANE, LANE)
    tr = _pick(p // LANE, 1024, MXU)

    def body(g_ref, out_ref):
        acc = g_ref[0]
        for k in range(1, 8):
            acc = acc + g_ref[k]
        out_ref[...] = acc

    return pl.pallas_call(
        body, name=name, grid=(p // LANE // tr,),
        in_specs=[pl.BlockSpec((8, tr, LANE), lambda i: (0, i, 0))], out_specs=pl.BlockSpec((tr, LANE), lambda i: (i, 0)),
        out_shape=jax.ShapeDtypeStruct((p // LANE, LANE), F32), compiler_params=_params(("parallel",)))(g3).reshape(p)


HBM_SPEC = pl.BlockSpec(memory_space=pltpu.HBM)


def _half(ref, lead, c, axis):
    h = ref.shape[axis] // 2
    return ref.at[lead, pl.ds(c * h, h), :] if axis == 1 else ref.at[lead, :, pl.ds(c * h, h)]


def _gather_ops(outs, axes, send_sems, recv_sems):
    x, y, c, chips = _place()
    me, sibling = (x, y, c), (x, y, 1 - c)

    def copy(a, k, chip, pc, to):
        blk = _half(outs[a], 2 * chip[0] + chip[1], pc, axes[a])
        return pltpu.make_async_remote_copy(src_ref=blk, dst_ref=blk, send_sem=send_sems.at[6 * a + k],
                                            recv_sem=recv_sems.at[6 * a + k], device_id=to, device_id_type=MESH)

    def start():
        for a in range(len(outs)):
            for j, chip in enumerate(chips):
                copy(a, j, (x, y), c, (*chip, c)).start()

    def finish():
        for a in range(len(outs)):
            for j, chip in enumerate(chips):
                copy(a, j, chip, c, me).wait_recv()
                copy(a, 3 + j, chip, c, sibling).start()
        for a in range(len(outs)):
            for j, chip in enumerate(chips):
                copy(a, 3 + j, chip, 1 - c, me).wait_recv()
            for j, chip in enumerate(chips):
                copy(a, j, (x, y), c, (*chip, c)).wait_send()
                copy(a, 3 + j, chip, c, sibling).wait_send()

    return start, finish


def gather_comm(bufs, axes):
    return dict(ins=list(bufs), out_shape=[jax.ShapeDtypeStruct(b.shape, b.dtype) for b in bufs],
                aliases={a: a for a in range(len(bufs))}, n_sems=6 * len(bufs),
                ops=lambda cin, couts, ss, rs: _gather_ops(couts, axes, ss, rs))


def gather_weights(name, bufs, axes):
    n = len(bufs)

    def body(*refs):
        start, finish = _gather_ops(refs[n:2 * n], axes, refs[2 * n], refs[2 * n + 1])
        start()
        finish()

    return pl.pallas_call(
        body, name=name, out_shape=[jax.ShapeDtypeStruct(b.shape, b.dtype) for b in bufs],
        in_specs=[HBM_SPEC] * n, out_specs=[HBM_SPEC] * n, input_output_aliases={a: a for a in range(n)},
        scratch_shapes=[pltpu.SemaphoreType.DMA((6 * n,)), pltpu.SemaphoreType.DMA((6 * n,))])(*bufs)


def _scatter_ops(ins, outs, send_sems, recv_sems):
    x, y, c, chips = _place()
    me = 2 * x + y

    def copy(a, j, chip):
        return pltpu.make_async_remote_copy(
            src_ref=ins[a].at[2 * chip[0] + chip[1]], dst_ref=outs[a].at[me], send_sem=send_sems.at[3 * a + j],
            recv_sem=recv_sems.at[3 * a + j], device_id=(*chip, c), device_id_type=MESH)

    def start():
        for a in range(len(ins)):
            for j, chip in enumerate(chips):
                copy(a, j, chip).start()

    def finish():
        for a in range(len(ins)):
            for j, chip in enumerate(chips):
                copy(a, j, chip).wait()

    return start, finish


def scatter_comm(bufs):
    return dict(ins=list(bufs), out_shape=[jax.ShapeDtypeStruct(b.shape, b.dtype) for b in bufs], aliases={},
                n_sems=3 * len(bufs), ops=_scatter_ops)


def chip_scatter(name, bufs):
    n = len(bufs)

    def body(*refs):
        start, finish = _scatter_ops(refs[:n], refs[n:2 * n], refs[2 * n], refs[2 * n + 1])
        start()
        finish()

    return pl.pallas_call(
        body, name=name, out_shape=[jax.ShapeDtypeStruct(b.shape, b.dtype) for b in bufs],
        in_specs=[HBM_SPEC] * n, out_specs=[HBM_SPEC] * n,
        scratch_shapes=[pltpu.SemaphoreType.DMA((3 * n,)), pltpu.SemaphoreType.DMA((3 * n,))])(*bufs)


def pair_exchange(name, bufs, axes):
    n = len(bufs)

    def body(*refs):
        ins, outs, (send_sems, recv_sems) = refs[:n], refs[n:2 * n], refs[2 * n:]
        x, y, c, _ = _place()
        cps = []
        for a, (src, out) in enumerate(zip(ins, outs)):
            cp = pltpu.make_async_remote_copy(
                src_ref=_half(src, slice(None), 1 - c, axes[a]), dst_ref=out, send_sem=send_sems.at[a],
                recv_sem=recv_sems.at[a], device_id=(x, y, 1 - c), device_id_type=MESH)
            cp.start()
            cps.append(cp)
        for cp in cps:
            cp.wait()

    def halved(b, axis):
        shape = list(b.shape)
        shape[axis] //= 2
        return jax.ShapeDtypeStruct(tuple(shape), b.dtype)

    return pl.pallas_call(
        body, name=name, out_shape=[halved(b, ax) for b, ax in zip(bufs, axes)],
        in_specs=[HBM_SPEC] * n, out_specs=[HBM_SPEC] * n,
        scratch_shapes=[pltpu.SemaphoreType.DMA((n,)), pltpu.SemaphoreType.DMA((n,))])(*bufs)


def pair_add(name, buf, got, cidx, axis):
    s, r, c = got.shape
    tr = _pick(r, max(HALO, (4 * 1024 * 1024) // (2 * c)), HALO)
    per = r // tr
    if axis == 1:
        mine = pl.BlockSpec((None, tr, c), lambda k, i, cr: (k, cr[0] * per + i, 0))
    else:
        mine = pl.BlockSpec((None, tr, c), lambda k, i, cr: (k, i, cr[0]))

    def body(c_ref, a_ref, b_ref, out_ref):
        out_ref[...] = (a_ref[...].astype(F32) + b_ref[...].astype(F32)).astype(BF16)

    return pl.pallas_call(
        body, name=name,
        grid_spec=pltpu.PrefetchScalarGridSpec(
            num_scalar_prefetch=1, grid=(s, per),
            in_specs=[mine, pl.BlockSpec((None, tr, c), lambda k, i, cr: (k, i, 0))],
            out_specs=pl.BlockSpec((None, tr, c), lambda k, i, cr: (k, i, 0))),
        out_shape=jax.ShapeDtypeStruct((s, r, c), BF16),
        compiler_params=_params(("parallel", "parallel")))(cidx, buf, got)


def chip_add(name, own, got, place, dst, l, off, size, col):
    s = got.shape[0]
    if col:
        h, n = got.shape[1], size
        tr = _pick(h, max(HALO, (2 * 1024 * 1024) // (2 * n)), HALO)
        per, ob = h // tr, off // n
        own_spec = pl.BlockSpec((None, tr, n), lambda i, p: (p[0], i, ob))
        got_spec = pl.BlockSpec((s, tr, n), lambda i, p: (0, i, ob))
        out_spec = pl.BlockSpec((None, tr, n), lambda i, p: (l, p[1] * per + i, 0))
        grid = (per,)
    else:
        n = got.shape[2]
        tr = _pick(size, max(HALO, (2 * 1024 * 1024) // (2 * n)), HALO)
        ob = off // tr
        own_spec = pl.BlockSpec((None, tr, n), lambda i, p: (p[0], ob + i, 0))
        got_spec = pl.BlockSpec((s, tr, n), lambda i, p: (0, ob + i, 0))
        out_spec = pl.BlockSpec((None, tr, n), lambda i, p: (l, i, p[1]))
        grid = (size // tr,)

    def body(p_ref, own_ref, g_ref, dst_ref, out_ref):
        acc = jnp.zeros((tr, n), F32)
        for k in range(s):
            acc = acc + jnp.where(p_ref[0] == k, own_ref[...], g_ref[k]).astype(F32)
        out_ref[...] = acc

    return pl.pallas_call(
        body, name=name,
        grid_spec=pltpu.PrefetchScalarGridSpec(
            num_scalar_prefetch=1, grid=grid,
            in_specs=[own_spec, got_spec, pl.BlockSpec(memory_space=pl.ANY)], out_specs=out_spec),
        out_shape=jax.ShapeDtypeStruct(dst.shape, F32), input_output_aliases={3: 0},
        compiler_params=_params(("parallel",)))(place, own, got, dst)


def pair_join(name, bufs, axes):
    n = len(bufs)

    def body(*refs):
        outs = refs[n:2 * n]
        send_sems, recv_sems = refs[2 * n:]
        x, y, c, _ = _place()
        started = []
        for a, out in enumerate(outs):
            blk = _half(out, slice(None), c, axes[a])
            cp = pltpu.make_async_remote_copy(src_ref=blk, dst_ref=blk, send_sem=send_sems.at[a], recv_sem=recv_sems.at[a],
                                              device_id=(x, y, 1 - c), device_id_type=MESH)
            cp.start()
            started.append(cp)
        for cp in started:
            cp.wait()

    return pl.pallas_call(
        body, name=name, out_shape=[jax.ShapeDtypeStruct(b.shape, b.dtype) for b in bufs],
        in_specs=[HBM_SPEC] * n, out_specs=[HBM_SPEC] * n, input_output_aliases={a: a for a in range(n)},
        scratch_shapes=[pltpu.SemaphoreType.DMA((n,)), pltpu.SemaphoreType.DMA((n,))])(*bufs)


def _rope_tables(seq):
    rows = seq // GRID_W
    row = jnp.repeat(jnp.arange(rows), GRID_W).astype(F32)
    col = jnp.tile(jnp.arange(GRID_W), rows).astype(F32)
    pairs = HEAD // 4
    inv = ROPE_BASE ** (-jnp.arange(pairs, dtype=F32) / pairs)
    ang = jnp.stack([row[:, None] * inv, col[:, None] * inv], axis=1)
    ang = jnp.broadcast_to(ang[:, :, None, :], (seq, 2, 2, pairs)).reshape(seq, HEAD)
    cos = jnp.concatenate([jnp.ones((CTX, HEAD), F32), jnp.cos(ang)], axis=0)
    sin = jnp.concatenate([jnp.zeros((CTX, HEAD), F32), jnp.sin(ang)], axis=0)
    return cos, sin


def _pad8(a):
    return jnp.pad(a, ((0, 8 - a.shape[0]), (0, 0)))


def kernel(x, c, ctx, c_ctx, w_ada, b_ada, attn_w_qkv, attn_w_o, attn_q_gain, attn_k_gain, attn_sink, sc_w_in, sc_conv, sc_w_out, ffn_w_up, ffn_conv, ffn_conv_b, ffn_w_down, loss_target, m_c_ctx, m_w_ada, m_b_ada, m_attn_w_qkv, m_attn_w_o, m_attn_q_gain, m_attn_k_gain, m_attn_sink, m_sc_w_in, m_sc_conv, m_sc_w_out, m_ffn_w_up, m_ffn_conv, m_ffn_conv_b, m_ffn_w_down, v_c_ctx, v_w_ada, v_b_ada, v_attn_w_qkv, v_attn_w_o, v_attn_q_gain, v_attn_k_gain, v_attn_sink, v_sc_w_in, v_sc_conv, v_sc_w_out, v_ffn_w_up, v_ffn_conv, v_ffn_conv_b, v_ffn_w_down):
    seq, d = x.shape[1], x.shape[2]
    depth, nada = w_ada.shape[0], w_ada.shape[2]
    n_attn, n_conv = attn_w_qkv.shape[0], sc_w_in.shape[0]
    ff = ffn_conv_b.shape[1]
    dq, dkv = d, d // GROUP
    nqkv = dq + 2 * dkv
    nh = dq // HEAD
    assert ctx.shape[1] == CTX and seq % RB == 0 and 6 * d == 4 * nada
    lay = Layout(d, ff, nqkv, depth)
    ax, ay, ac = lax.axis_index("x"), lax.axis_index("y"), lax.axis_index("c")
    chip, dev = 2 * ax + ay, 4 * ax + 2 * ay + ac
    cidx = jnp.reshape(ac, (1,)).astype(jnp.int32)

    chip1 = jnp.reshape(chip, (1,)).astype(jnp.int32)
    wcs, wrs = [], []
    for l in range(depth):
        j, is_attn = l // N_MIX, l % N_MIX == 0
        wc_l, wr_l = lax.empty((4, d, lay.ct[l]), BF16), lax.empty((4, lay.rt, d), BF16)
        wc_l = cast_pack(f"pack_up_{l}", ffn_w_up, l, wc_l, 0, True, chip1)
        wc_l = cast_pack(f"pack_mix_{l}", attn_w_qkv if is_attn else sc_w_in, j, wc_l, lay.mix[l], True, chip1)
        wr_l = cast_pack(f"pack_down_{l}", ffn_w_down, l, wr_l, 0, False, chip1)
        wr_l = cast_pack(f"pack_out_{l}", attn_w_o if is_attn else sc_w_out, j, wr_l, lay.out, False, chip1)
        wcs.append(wc_l)
        wrs.append(wr_l)
    (wcs[0],) = gather_weights("gather_w0", [wcs[0]], (1,))

    g1 = gather_flat("gather_cond", [c, sc_conv, ffn_conv])
    c_all = g1[:, :d]
    o1 = d + sc_conv.size
    sc_conv_full = jnp.concatenate([g1[2 * s, d:o1].reshape(sc_conv.shape) for s in range(4)], axis=-1)
    ffn_conv_full = jnp.concatenate([g1[2 * s, o1:o1 + ffn_conv.size].reshape(ffn_conv.shape) for s in range(4)], axis=-1)
    cond = jnp.concatenate([c_all, c_ctx[None, :], jnp.zeros((7, d), F32)], axis=0)
    ada_part = ada_fwd("ada_fwd", cond, w_ada)
    g2 = gather_flat("gather_ada", [ada_part])
    ada_all = jnp.concatenate([g2[2 * s, :ada_part.size].reshape(ada_part.shape) for s in range(4)], axis=-1)
    ada_own = jnp.stack([lax.dynamic_index_in_dim(ada_all, 8, 1, False),
                         lax.dynamic_index_in_dim(ada_all, dev, 1, False)], axis=1) + b_ada[:, None, :]
    mods = jnp.pad(ada_own.reshape(depth, 2, 6, d), ((0, 0), (0, 0), (0, 2), (0, 0)))

    cos, sin = _rope_tables(seq)
    xa = jnp.concatenate([ctx[0], x[0]], axis=0)
    cws = [_pad8(sc_conv_full[j]) for j in range(n_conv)]
    cwf = [_pad8(jnp.concatenate([ffn_conv_full[l], ffn_conv_b[l][None, :]], axis=0)) for l in range(depth)]
    gains = [_pad8(jnp.stack([attn_q_gain[j], attn_k_gain[j]])) for j in range(n_attn)]

    saved = []
    for l in range(depth):
        j, is_attn, mod = l // N_MIX, l % N_MIX == 0, mods[l]
        wc, wr, last = wcs[l], wrs[l], l == depth - 1
        sv = {"x_in": xa}
        h1 = norm_mod(f"norm1_{l}", xa, mod, 0, 1)
        if is_attn:
            if l == 0:
                qkv, (wr,) = mm_nn_col(f"qkv_{l}", h1, wc, lay.mix[l], lay.nb_mix[l], comm=gather_comm([wr], (2,)))
                wrs[0] = wr
            else:
                qkv = mm_nn_col(f"qkv_{l}", h1, wc, lay.mix[l], lay.nb_mix[l])
            qr, kr = rope_fwd(f"rope_{l}", qkv, cos, sin, gains[j], dq, dkv)
            o, lse, lset = attn_fwd(f"attn_{l}", qr, kr, qkv, attn_sink[j], seq)
            xa, y_m = mm_nn_row(f"wo_{l}", o, wr, lay.out, lay.kb_o, xa, mod, 2)
            sv.update(qkv=qkv, qr=qr, kr=kr, o=o, lse=lse, lset=lset)
        else:
            u = mm_nn_col(f"scin_{l}", h1, wc, lay.mix[l], lay.nb_mix[l])
            z = sc_gate_fwd(f"scgate_{l}", u, cws[j])
            xa, y_m = mm_nn_row(f"scout_{l}", z, wr, lay.out, lay.kb_o, xa, mod, 2)
            sv.update(u=u, z=z)
        h2 = norm_mod(f"norm2_{l}", xa, mod, 3, 4)
        if last:
            up = mm_nn_col(f"up_{l}", h2, wc, 0, lay.nb_up)
        else:
            up, (wcs[l + 1],) = mm_nn_col(f"up_{l}", h2, wc, 0, lay.nb_up, comm=gather_comm([wcs[l + 1]], (1,)))
        act = ffn_act_fwd(f"act_{l}", up, cwf[l])
        sv.update(h1=h1, y_m=y_m, x_mid=xa, h2=h2, up=up, act=act)
        if last:
            xa, y_f = mm_nn_row(f"down_{l}", act, wr, 0, lay.kb_dn, xa, mod, 5)
        else:
            (xa, y_f), (wrs[l + 1],) = mm_nn_row(f"down_{l}", act, wr, 0, lay.kb_dn, xa, mod, 5,
                                                 comm=gather_comm([wrs[l + 1]], (2,)))
        sv["y_f"] = y_f
        saved.append(sv)

    dx, dy, acc, lp = loss_head("loss", xa, loss_target[0], saved[-1]["y_f"], mods[-1], 5)
    loss = lax.psum(0.5 * jnp.sum(lp[0]) / d, ("x", "y", "c"))
    d_mod = [jnp.zeros((2, 6, d), F32) for _ in range(depth)]
    place = jnp.stack([chip, ac]).astype(jnp.int32)
    gf = {"up": lax.empty(ffn_w_up.shape, F32), "down": lax.empty(ffn_w_down.shape, F32),
          "qkv": lax.empty(attn_w_qkv.shape, F32), "wo": lax.empty(attn_w_o.shape, F32),
          "scin": lax.empty(sc_w_in.shape, F32), "scout": lax.empty(sc_w_out.shape, F32)}

    def chip_adds(l, hc, hr, rb_c, rb_r):
        j, mix = l // N_MIX, ("qkv", "wo") if l % N_MIX == 0 else ("scin", "scout")
        gf["up"] = chip_add(f"sum_up_{l}", hc, rb_c, place, gf["up"], l, 0, lay.nb_up, True)
        gf[mix[0]] = chip_add(f"sum_mix_{l}", hc, rb_c, place, gf[mix[0]], j, lay.mix[l], lay.nb_mix[l], True)
        gf["down"] = chip_add(f"sum_down_{l}", hr, rb_r, place, gf["down"], l, 0, lay.kb_dn, False)
        gf[mix[1]] = chip_add(f"sum_out_{l}", hr, rb_r, place, gf[mix[1]], j, lay.out, lay.kb_o, False)

    pending = None

    def add_mod(l, acc, idx):
        upd = jnp.zeros((2, 6, d), F32)
        for row, k in idx:
            upd = upd.at[:, k, :].set(acc[:, row, :])
        d_mod[l] = d_mod[l] + upd

    add_mod(depth - 1, acc, [(2, 5)])
    d_conv_f, d_conv_s = [None] * depth, [None] * n_conv
    d_gq, d_gk, d_sink = [None] * n_attn, [None] * n_attn, [None] * n_attn
    for l in reversed(range(depth)):
        j, is_attn, sv = l // N_MIX, l % N_MIX == 0, saved[l]
        wc, wr = wcs[l], wrs[l]
        gc, gr = lax.empty((4, d, lay.ct[l]), BF16), lax.empty((4, lay.rt, d), BF16)
        if pending is None:
            gr = mm_tn_row(f"g_down_{l}", sv["act"], dy, gr, 0, lay.kb_dn)
        else:
            gr, (rb_r,) = mm_tn_row(f"g_down_{l}", sv["act"], dy, gr, 0, lay.kb_dn, comm=scatter_comm([pending[2]]))
        da = mm_nt_row(f"d_act_{l}", dy, wr, 0, lay.kb_dn)
        d_up, d_conv_f[l] = ffn_act_bwd(f"act_bwd_{l}", sv["up"], da, cwf[l])
        if pending is None:
            gc = mm_tn_col(f"g_up_{l}", sv["h2"], d_up, gc, 0, lay.nb_up)
        else:
            gc, (rb_c,) = mm_tn_col(f"g_up_{l}", sv["h2"], d_up, gc, 0, lay.nb_up, comm=scatter_comm([pending[1]]))
            chip_adds(pending[0], pending[1], pending[2], rb_c, rb_r)
        dh2 = mm_nt_col(f"d_h2_{l}", d_up, wc, 0, lay.nb_up)
        dx, dy, acc = resid_bwd(f"norm2_bwd_{l}", dx, dh2, sv["x_mid"], mods[l], 3, 4, sv["y_m"], mods[l], 2)
        add_mod(l, acc, [(0, 3), (1, 4), (2, 2)])
        if is_attn:
            gr = mm_tn_row(f"g_wo_{l}", sv["o"], dy, gr, lay.out, lay.kb_o)
            do = mm_nt_row(f"d_o_{l}", dy, wr, lay.out, lay.kb_o)
            dqr, deltat, dkc, dvc, dsk = attn_bwd_q(f"attn_bwd_q_{l}", sv["qr"], sv["kr"], sv["qkv"], attn_sink[j],
                                                    do, sv["o"], sv["lse"], seq)
            dkl, dvl = attn_bwd_kv(f"attn_bwd_kv_{l}", sv["qr"], sv["kr"], sv["qkv"], do, sv["lset"], deltat, seq)
            dqkv, dgn = rope_bwd(f"rope_bwd_{l}", sv["qkv"], dqr, jnp.concatenate([dkc, dkl], axis=0),
                                 jnp.concatenate([dvc, dvl], axis=0), cos, sin, gains[j])
            d_gq[j], d_gk[j], d_sink[j] = dgn[0], dgn[1], dsk[0, :nh]
            gc = mm_tn_col(f"g_qkv_{l}", sv["h1"], dqkv, gc, lay.mix[l], lay.nb_mix[l])
            dh1 = mm_nt_col(f"d_h1_{l}", dqkv, wc, lay.mix[l], lay.nb_mix[l])
        else:
            gr = mm_tn_row(f"g_scout_{l}", sv["z"], dy, gr, lay.out, lay.kb_o)
            dz = mm_nt_row(f"d_z_{l}", dy, wr, lay.out, lay.kb_o)
            du, dcw = sc_gate_bwd(f"scgate_bwd_{l}", sv["u"], dz, cws[j])
            d_conv_s[j] = dcw[:3]
            gc = mm_tn_col(f"g_scin_{l}", sv["h1"], du, gc, lay.mix[l], lay.nb_mix[l])
            dh1 = mm_nt_col(f"d_h1_{l}", du, wc, lay.mix[l], lay.nb_mix[l])
        if l > 0:
            dx, dy, acc = resid_bwd(f"norm1_bwd_{l}", dx, dh1, sv["x_in"], mods[l], 0, 1, saved[l - 1]["y_f"], mods[l - 1], 5)
            add_mod(l - 1, acc, [(2, 5)])
        else:
            dx, acc = resid_bwd(f"norm1_bwd_{l}", dx, dh1, sv["x_in"], mods[l], 0, 1)
        add_mod(l, acc, [(0, 0), (1, 1)])
        ra_c, ra_r = pair_exchange(f"pair_exchange_{l}", [gc, gr], (1, 2))
        pending = (l, pair_add(f"pair_add_c_{l}", gc, ra_c, cidx, 1), pair_add(f"pair_add_r_{l}", gr, ra_r, cidx, 2))
    grad_x = dx[CTX:][None]
    rb_c, rb_r = chip_scatter("chip_scatter_0", [pending[1], pending[2]])
    chip_adds(pending[0], pending[1], pending[2], rb_c, rb_r)
    order = ["up", "down", "qkv", "wo", "scin", "scout"]
    joined = pair_join("pair_join", [gf[k] for k in order], (1, 2, 1, 2, 1, 2))
    gf = dict(zip(order, joined))

    d_ada = jnp.stack(d_mod).reshape(depth, 2, 6 * d)
    small = [d_ada, jnp.stack(d_gq), jnp.stack(d_gk), jnp.stack(d_sink), jnp.stack(d_conv_s),
             jnp.stack([t[:3] for t in d_conv_f]), jnp.stack([t[3] for t in d_conv_f])]
    g3 = gather_flat("gather_small", small)
    tot = sum8("sum_small", g3)
    sizes = [s.size for s in small]
    offs = [sum(sizes[:k]) for k in range(len(sizes) + 1)]
    part = lambda k: tot[offs[k]:offs[k + 1]].reshape(small[k].shape)
    g_b_ada = part(0)[:, 0] + part(0)[:, 1]
    g_q_gain, g_k_gain, g_sink = part(1), part(2), part(3)
    g_sc_conv = lax.dynamic_slice_in_dim(part(4), chip * sc_conv.shape[2], sc_conv.shape[2], 2)
    g_ffn_conv = lax.dynamic_slice_in_dim(part(5), chip * ffn_conv.shape[2], ffn_conv.shape[2], 2)
    g_conv_b = part(6)

    d_ada_all = g3[:, :d_ada.size].reshape(8, depth, 2, 6 * d)
    cols = lambda a: lax.dynamic_slice_in_dim(a, chip * nada, nada, a.ndim - 1)
    d_lat = cols(jnp.moveaxis(d_ada_all[:, :, 1], 0, 1))
    d_ctx = cols(part(0)[:, 0])
    rhs = jnp.concatenate([d_lat, d_ctx[:, None], jnp.zeros((depth, 7, nada), F32)], axis=1)
    g_w_ada = ada_grad_w("ada_grad_w", cond, rhs)
    dcc = ada_bwd_cond("ada_bwd_cond", jnp.pad(d_ctx[:, None], ((0, 0), (0, 7), (0, 0))), w_ada)[0]
    g4 = gather_flat("gather_dcc", [dcc])
    d_silu = g4[0, :d] + g4[2, :d] + g4[4, :d] + g4[6, :d]
    sg = _sigmoid(c_ctx)
    g_c_ctx = d_silu * (sg * (1.0 + c_ctx * (1.0 - sg)))

    def adam_rows(k, n):
        return _pick(k, max(8, ADAM_TILE_ELEMS // n), 8)

    def big(name, g, w, m, v):
        _, k, n = w.shape
        tr = adam_rows(k, n)
        return adamw(name, g, pl.BlockSpec((None, tr, n), lambda l, i: (l, i, 0)), w, m, v, tr)

    ada_tr = adam_rows(d, nada)
    res = {
        "c_ctx": adamw_small("adam_c_ctx", g_c_ctx, c_ctx, m_c_ctx, v_c_ctx),
        "w_ada": adamw("adam_w_ada", g_w_ada, pl.BlockSpec((None, ada_tr, nada), lambda l, i: (l, i, 0)), w_ada, m_w_ada, v_w_ada, ada_tr),
        "b_ada": adamw_small("adam_b_ada", g_b_ada, b_ada, m_b_ada, v_b_ada),
        "attn_w_qkv": big("adam_qkv", gf["qkv"], attn_w_qkv, m_attn_w_qkv, v_attn_w_qkv),
        "attn_w_o": big("adam_wo", gf["wo"], attn_w_o, m_attn_w_o, v_attn_w_o),
        "attn_q_gain": adamw_small("adam_q_gain", g_q_gain, attn_q_gain, m_attn_q_gain, v_attn_q_gain),
        "attn_k_gain": adamw_small("adam_k_gain", g_k_gain, attn_k_gain, m_attn_k_gain, v_attn_k_gain),
        "attn_sink": adamw_small("adam_sink", g_sink, attn_sink, m_attn_sink, v_attn_sink),
        "sc_w_in": big("adam_scin", gf["scin"], sc_w_in, m_sc_w_in, v_sc_w_in),
        "sc_conv": adamw_small("adam_sc_conv", g_sc_conv, sc_conv, m_sc_conv, v_sc_conv),
        "sc_w_out": big("adam_scout", gf["scout"], sc_w_out, m_sc_w_out, v_sc_w_out),
        "ffn_w_up": big("adam_up", gf["up"], ffn_w_up, m_ffn_w_up, v_ffn_w_up),
        "ffn_conv": adamw_small("adam_ffn_conv", g_ffn_conv, ffn_conv, m_ffn_conv, v_ffn_conv),
        "ffn_conv_b": adamw_small("adam_conv_b", g_conv_b, ffn_conv_b, m_ffn_conv_b, v_ffn_conv_b),
        "ffn_w_down": big("adam_down", gf["down"], ffn_w_down, m_ffn_w_down, v_ffn_w_down),
    }
    names = list(res)
    return (loss, grad_x, *[res[n][0] for n in names], *[res[n][1] for n in names],
            *[res[n][2] for n in names], *[res[n][3] for n in names])
```

```python
import functools

import jax
import jax.numpy as jnp
from jax import lax
from jax.experimental import pallas as pl
from jax.experimental.pallas import tpu as pltpu

F32, BF16 = jnp.float32, jnp.bfloat16
MESH = pl.DeviceIdType.MESH
VMEM_LIMIT = 56 * 1024 * 1024
LANE = 128
MXU = 256
WIDE = 2816
HALO = 16
HEAD = 128
GROUP = 4
CTX = 256
BLK = 128
WINDOW = 128
RB = 256
GRID_W = 64
ROPE_BASE = 10000.0
EPS = 1e-6
NEG = -1e30
N_MIX = 2
LR, B1, B2, ADAM_EPS, WD, STEP = 0.001, 0.9, 0.999, 1e-08, 0.01, 10
ADAM_TILE_ELEMS = 400 * 1024
NT = (((1,), (1,)), ((), ()))
TN = (((0,), (0,)), ((), ()))


def _pick(dim, target, mult=LANE):
    best = None
    for t in range(mult, min(dim, target) + 1, mult):
        if dim % t == 0:
            best = t
    return dim if best is None else best


def _cdiv(a, b):
    return -(-a // b)


def _params(sem):
    return pltpu.CompilerParams(dimension_semantics=sem, vmem_limit_bytes=VMEM_LIMIT)


def _sigmoid(g):
    return 0.5 * jnp.tanh(0.5 * g) + 0.5


def _row(v, r):
    rows = lax.broadcasted_iota(jnp.int32, v.shape, 0)
    return jnp.sum(jnp.where(rows == r, v, 0.0), axis=0, keepdims=True)


def _get_col(v, c):
    lanes = lax.broadcasted_iota(jnp.int32, v.shape, 1)
    return jnp.sum(jnp.where(lanes == c, v, 0.0), axis=1, keepdims=True)


def _put_col(v, c, col):
    lanes = lax.broadcasted_iota(jnp.int32, v.shape, 1)
    return jnp.where(lanes == c, col, v)


def _rows3(s0, s1, s2, width):
    rows = lax.broadcasted_iota(jnp.int32, (8, width), 0)
    z = jnp.zeros((8, width), F32)
    return jnp.where(rows == 0, s0, jnp.where(rows == 1, s1, jnp.where(rows == 2, s2, z)))


def _shift_rows(w, prev_row, next_row):
    n = w.shape[0]
    rows = lax.broadcasted_iota(jnp.int32, (n, 1), 0)
    down = jnp.where(rows == 0, prev_row, pltpu.roll(w, 1, 0))
    up = jnp.where(rows == n - 1, next_row, pltpu.roll(w, n - 1, 0))
    return down, up


def _seg_flags(i, nt):
    return i <= 1, (i == 0) | (i == nt - 1)


def _halo_specs(width, nrows):
    r = RB // HALO
    nh = nrows // HALO
    prev = pl.BlockSpec((HALO, width), lambda i: (jnp.maximum(i * r - 1, 0), 0))
    nxt = pl.BlockSpec((HALO, width), lambda i: (jnp.minimum((i + 1) * r, nh - 1), 0))
    return prev, nxt


class Layout:
    def __init__(self, d, ff, nqkv, depth):
        self.nb_up, self.kb_dn, self.kb_o = 2 * ff // 4, ff // 4, d // 4
        self.nb_mix = [(nqkv if l % N_MIX == 0 else 3 * d) // 4 for l in range(depth)]
        self.mix = [_cdiv(self.nb_up, nb) * nb for nb in self.nb_mix]
        self.ct = [m + nb for m, nb in zip(self.mix, self.nb_mix)]
        self.out = _cdiv(self.kb_dn, self.kb_o) * self.kb_o
        self.rt = _cdiv(self.out + self.kb_o, 2 * HALO) * 2 * HALO


def _mm_call(name, grid, in_specs, out_specs, out_shape, contract, operands, acc_shape, epilogue,
             n_extra=0, aliases=None, comm=None, merge_b=False):
    nk = grid[2]
    n_out = len(out_shape)
    n_cin = len(comm["ins"]) if comm else 0
    n_cout = len(comm["out_shape"]) if comm else 0
    aliases = dict(aliases or {})
    in_specs, out_specs, out_shape, operands = list(in_specs), list(out_specs), list(out_shape), list(operands)
    scratch = [] if nk == 1 else [pltpu.VMEM(acc_shape, F32)]
    if comm:
        for i_in, i_out in comm["aliases"].items():
            aliases[len(operands) + i_in] = n_out + i_out
        in_specs += [HBM_SPEC] * n_cin
        out_specs += [HBM_SPEC] * n_cout
        out_shape += comm["out_shape"]
        operands += comm["ins"]
        scratch += [pltpu.SemaphoreType.DMA((comm["n_sems"],)), pltpu.SemaphoreType.DMA((comm["n_sems"],))]

    def body(*refs):
        a_ref, b_ref = refs[0], refs[1]
        extra = refs[2:2 + n_extra]
        p = 2 + n_extra
        cin, outs = refs[p:p + n_cin], refs[p + n_cin:p + n_cin + n_out]
        couts = refs[p + n_cin + n_out:p + n_cin + n_out + n_cout]
        scr = refs[p + n_cin + n_out + n_cout:]
        ids = (pl.program_id(0), pl.program_id(1))
        k = pl.program_id(2)
        if comm:
            start, finish = comm["ops"](cin, couts, scr[-2], scr[-1])

            @pl.when((ids[0] == 0) & (ids[1] == 0) & (k == 0))
            def _():
                start()

        def part():
            b = b_ref[...]
            if merge_b:
                b = b.reshape(b.shape[0] * b.shape[1], b.shape[2])
            return lax.dot_general(a_ref[...], b, contract, preferred_element_type=F32)

        if nk == 1:
            epilogue(part(), extra, outs, ids)
        else:
            acc = scr[0]

            @pl.when(k == 0)
            def _():
                acc[...] = jnp.zeros_like(acc)

            acc[...] += part()

            @pl.when(k == nk - 1)
            def _():
                epilogue(acc[...], extra, outs, ids)

        if comm:
            @pl.when((ids[0] == grid[0] - 1) & (ids[1] == grid[1] - 1) & (k == nk - 1))
            def _():
                finish()

    sem = ("arbitrary",) * 3 if comm else ("parallel", "parallel", "arbitrary")
    res = pl.pallas_call(
        body, name=name, grid=grid, in_specs=in_specs, out_specs=out_specs, out_shape=out_shape,
        scratch_shapes=scratch, input_output_aliases=aliases, compiler_params=_params(sem))(*operands)
    return (res[:n_out], res[n_out:]) if comm else res


def _store(dtype):
    def epilogue(r, extra, outs, ids):
        outs[0][...] = r.astype(dtype)
    return epilogue


def _ret(res, comm, single=True):
    if comm:
        return (res[0][0] if single else res[0]), res[1]
    return res[0] if single else res


def _slots_per_step(s, kb):
    return max(n for n in (1, 2, 4) if s % n == 0 and (n == 1 or n * kb <= WIDE))


def mm_nn_col(name, a, wc, off, nb, comm=None):
    m, d = a.shape
    s = wc.shape[0]
    tn = _pick(nb, WIDE, MXU)
    tm = _pick(m, 528 if tn > 1536 else 1056, HALO)
    npb, ob = nb // tn, off // tn
    assert off % tn == 0
    return _ret(_mm_call(
        name, (s * npb, m // tm, 1),
        [pl.BlockSpec((tm, d), lambda j, i, k: (i, 0)),
         pl.BlockSpec((None, d, tn), lambda j, i, k: (j // npb, 0, ob + j % npb))],
        [pl.BlockSpec((tm, tn), lambda j, i, k: (i, j))],
        [jax.ShapeDtypeStruct((m, s * nb), BF16)],
        (((1,), (0,)), ((), ())), (a, wc), None, _store(BF16), comm=comm), comm)


def mm_nn_row(name, a, wr, off, kb, res, mod, gi, comm=None):
    m = a.shape[0]
    s, _, d = wr.shape
    sp = _slots_per_step(s, kb)
    tm, tn = _pick(m, 1056, HALO), _pick(d, 1024)
    ob = off // kb
    assert off % kb == 0

    def epilogue(r, extra, outs, ids):
        res_ref, mod_ref = extra
        rows = ids[0] * tm + lax.broadcasted_iota(jnp.int32, (tm, 1), 0)
        g = jnp.where(rows < CTX, mod_ref[0, gi:gi + 1, :], mod_ref[1, gi:gi + 1, :])
        outs[0][...] = res_ref[...] + g * r
        outs[1][...] = r.astype(BF16)

    return _ret(_mm_call(
        name, (m // tm, d // tn, s // sp),
        [pl.BlockSpec((tm, sp * kb), lambda i, j, k: (i, k)),
         pl.BlockSpec((sp, kb, tn), lambda i, j, k: (k, ob, j)),
         pl.BlockSpec((tm, tn), lambda i, j, k: (i, j)),
         pl.BlockSpec((2, 8, tn), lambda i, j, k: (0, 0, j))],
        [pl.BlockSpec((tm, tn), lambda i, j, k: (i, j)), pl.BlockSpec((tm, tn), lambda i, j, k: (i, j))],
        [jax.ShapeDtypeStruct((m, d), F32), jax.ShapeDtypeStruct((m, d), BF16)],
        (((1,), (0,)), ((), ())), (a, wr, res, mod), (tm, tn), epilogue, n_extra=2, comm=comm, merge_b=True),
        comm, single=False)


def mm_nt_col(name, dy, wc, off, nb, comm=None):
    m = dy.shape[0]
    s, d, _ = wc.shape
    tc = _pick(nb, WIDE, MXU)
    tm = _pick(m, 528 if tc > 1536 else 768, HALO)
    npb, ob = nb // tc, off // tc
    return _ret(_mm_call(
        name, (m // tm, 1, s * npb),
        [pl.BlockSpec((tm, tc), lambda i, j, k: (i, k)),
         pl.BlockSpec((None, d, tc), lambda i, j, k: (k // npb, 0, ob + k % npb))],
        [pl.BlockSpec((tm, d), lambda i, j, k: (i, 0))],
        [jax.ShapeDtypeStruct((m, d), F32)],
        NT, (dy, wc), (tm, d), _store(F32), comm=comm), comm)


def mm_nt_row(name, dy, wr, off, kb):
    m, d = dy.shape
    s = wr.shape[0]
    sp = _slots_per_step(s, kb)
    tm = _pick(m, 528 if sp * kb > 2048 else 1056, HALO)
    ob = off // kb
    return _mm_call(
        name, (s // sp, m // tm, 1),
        [pl.BlockSpec((tm, d), lambda j, i, k: (i, 0)),
         pl.BlockSpec((sp, kb, d), lambda j, i, k: (j, ob, 0))],
        [pl.BlockSpec((tm, sp * kb), lambda j, i, k: (i, j))],
        [jax.ShapeDtypeStruct((m, s * kb), BF16)],
        NT, (dy, wr), None, _store(BF16), merge_b=True)[0]


def mm_tn_col(name, a, dy, gbuf, off, nb, comm=None):
    t, d = a.shape
    s = gbuf.shape[0]
    tn, tt = _pick(nb, WIDE, MXU), _pick(t, 768, MXU)
    tka = _pick(d, 1024 if tn > 1536 else 2048)
    npb, ob = nb // tn, off // tn
    return _ret(_mm_call(
        name, (d // tka, s * npb, t // tt),
        [pl.BlockSpec((tt, tka), lambda i, j, k: (k, i)),
         pl.BlockSpec((tt, tn), lambda i, j, k: (k, j)),
         pl.BlockSpec(memory_space=pl.ANY)],
        [pl.BlockSpec((None, tka, tn), lambda i, j, k: (j // npb, i, ob + j % npb))],
        [jax.ShapeDtypeStruct(gbuf.shape, BF16)],
        TN, (a, dy, gbuf), (tka, tn), _store(BF16), n_extra=1, aliases={2: 0}, comm=comm), comm)


def mm_tn_row(name, act, dy, gbuf, off, kb, comm=None):
    t, d = dy.shape
    s = gbuf.shape[0]
    tt = _pick(t, 768, MXU)
    ob = off // kb
    return _ret(_mm_call(
        name, (s, 1, t // tt),
        [pl.BlockSpec((tt, kb), lambda i, j, k: (k, i)),
         pl.BlockSpec((tt, d), lambda i, j, k: (k, 0)),
         pl.BlockSpec(memory_space=pl.ANY)],
        [pl.BlockSpec((None, kb, d), lambda i, j, k: (i, ob, 0))],
        [jax.ShapeDtypeStruct(gbuf.shape, BF16)],
        TN, (act, dy, gbuf), (kb, d), _store(BF16), n_extra=1, aliases={2: 0}, comm=comm), comm)


def cast_pack(name, w, l, buf, off, col, chip):
    _, k, n = w.shape
    if col:
        tr = _pick(k, 512, HALO)
        ob = off // n
        out_spec = pl.BlockSpec((None, tr, n), lambda i, s: (s[0], i, ob))
    else:
        tr = _pick(k, 704, HALO)
        ob = off // tr
        out_spec = pl.BlockSpec((None, tr, n), lambda i, s: (s[0], ob + i, 0))

    def body(s_ref, w_ref, buf_ref, out_ref):
        out_ref[...] = w_ref[...].astype(BF16)

    return pl.pallas_call(
        body, name=name,
        grid_spec=pltpu.PrefetchScalarGridSpec(
            num_scalar_prefetch=1, grid=(k // tr,),
            in_specs=[pl.BlockSpec((None, tr, n), lambda i, s: (l, i, 0)), pl.BlockSpec(memory_space=pl.ANY)],
            out_specs=out_spec),
        out_shape=jax.ShapeDtypeStruct(buf.shape, BF16),
        input_output_aliases={2: 0}, compiler_params=_params(("parallel",)))(chip, w, buf)


def norm_mod(name, x, mod, sh, sc):
    t, d = x.shape

    def body(x_ref, mod_ref, h_ref):
        seg = jnp.minimum(pl.program_id(0), 1)
        xv = x_ref[...]
        r = lax.rsqrt(jnp.mean(xv * xv, axis=-1, keepdims=True) + EPS)
        m = mod_ref[seg]
        h_ref[...] = ((xv * r) * (1.0 + m[sc:sc + 1, :]) + m[sh:sh + 1, :]).astype(BF16)

    return pl.pallas_call(
        body, name=name, grid=(t // RB,),
        in_specs=[pl.BlockSpec((RB, d), lambda i: (i, 0)), pl.BlockSpec((2, 8, d), lambda i: (0, 0, 0))],
        out_specs=pl.BlockSpec((RB, d), lambda i: (i, 0)),
        out_shape=jax.ShapeDtypeStruct((t, d), BF16), compiler_params=_params(("parallel",)))(x, mod)


def sc_gate_fwd(name, u, cw):
    t = u.shape[0]
    d = u.shape[1] // 3
    nt, tc = t // RB, _pick(d, 512)
    prev, nxt = _halo_specs(3 * d, t)

    def body(u_ref, up_ref, un_ref, cw_ref, z_ref):
        first, last = _seg_flags(pl.program_id(0), nt)
        for j in range(d // tc):
            c0 = j * tc
            gb = u_ref[:, c0:c0 + tc].astype(F32)
            w = u_ref[:, d + c0:d + c0 + tc].astype(F32) * u_ref[:, 2 * d + c0:2 * d + c0 + tc].astype(F32)
            pw = _row(up_ref[:, d + c0:d + c0 + tc].astype(F32) * up_ref[:, 2 * d + c0:2 * d + c0 + tc].astype(F32), HALO - 1)
            nw = _row(un_ref[:, d + c0:d + c0 + tc].astype(F32) * un_ref[:, 2 * d + c0:2 * d + c0 + tc].astype(F32), 0)
            wd, wu = _shift_rows(w, jnp.where(first, 0.0, pw), jnp.where(last, 0.0, nw))
            cwj = cw_ref[:, c0:c0 + tc]
            conv = wd * cwj[0:1] + w * cwj[1:2] + wu * cwj[2:3]
            z_ref[:, c0:c0 + tc] = (gb * conv).astype(BF16)

    return pl.pallas_call(
        body, name=name, grid=(nt,),
        in_specs=[pl.BlockSpec((RB, 3 * d), lambda i: (i, 0)), prev, nxt, pl.BlockSpec((8, d), lambda i: (0, 0))],
        out_specs=pl.BlockSpec((RB, d), lambda i: (i, 0)),
        out_shape=jax.ShapeDtypeStruct((t, d), BF16), compiler_params=_params(("parallel",)))(u, u, u, cw)


def sc_gate_bwd(name, u, dz, cw):
    t = u.shape[0]
    d = u.shape[1] // 3
    nt, tc = t // RB, _pick(d, 512)
    prev, nxt = _halo_specs(3 * d, t)
    dprev, dnxt = _halo_specs(d, t)

    def body(u_ref, up_ref, un_ref, dz_ref, dzp_ref, dzn_ref, cw_ref, du_ref, dcw_ref):
        i = pl.program_id(0)
        first, last = _seg_flags(i, nt)

        @pl.when(i == 0)
        def _():
            dcw_ref[...] = jnp.zeros_like(dcw_ref)

        for j in range(d // tc):
            c0 = j * tc
            sl0, sl1, sl2 = slice(c0, c0 + tc), slice(d + c0, d + c0 + tc), slice(2 * d + c0, 2 * d + c0 + tc)
            gb, gc, v = u_ref[:, sl0].astype(F32), u_ref[:, sl1].astype(F32), u_ref[:, sl2].astype(F32)
            w = gc * v
            pw = _row(up_ref[:, sl1].astype(F32) * up_ref[:, sl2].astype(F32), HALO - 1)
            nw = _row(un_ref[:, sl1].astype(F32) * un_ref[:, sl2].astype(F32), 0)
            wd, wu = _shift_rows(w, jnp.where(first, 0.0, pw), jnp.where(last, 0.0, nw))
            cwj = cw_ref[:, sl0]
            cw0, cw1, cw2 = cwj[0:1], cwj[1:2], cwj[2:3]
            dzv = dz_ref[:, sl0].astype(F32)
            e = dzv * gb
            pe = _row(dzp_ref[:, sl0].astype(F32) * up_ref[:, sl0].astype(F32), HALO - 1)
            ne = _row(dzn_ref[:, sl0].astype(F32) * un_ref[:, sl0].astype(F32), 0)
            ed, eu = _shift_rows(e, jnp.where(first, 0.0, pe), jnp.where(last, 0.0, ne))
            dw = cw0 * eu + cw1 * e + cw2 * ed
            du_ref[:, sl0] = (dzv * (wd * cw0 + w * cw1 + wu * cw2)).astype(BF16)
            du_ref[:, sl1] = (dw * v).astype(BF16)
            du_ref[:, sl2] = (dw * gc).astype(BF16)
            dcw_ref[:, sl0] += _rows3(jnp.sum(e * wd, axis=0, keepdims=True), jnp.sum(e * w, axis=0, keepdims=True),
                                      jnp.sum(e * wu, axis=0, keepdims=True), tc)

    return pl.pallas_call(
        body, name=name, grid=(nt,),
        in_specs=[pl.BlockSpec((RB, 3 * d), lambda i: (i, 0)), prev, nxt,
                  pl.BlockSpec((RB, d), lambda i: (i, 0)), dprev, dnxt, pl.BlockSpec((8, d), lambda i: (0, 0))],
        out_specs=[pl.BlockSpec((RB, 3 * d), lambda i: (i, 0)), pl.BlockSpec((8, d), lambda i: (0, 0))],
        out_shape=[jax.ShapeDtypeStruct((t, 3 * d), BF16), jax.ShapeDtypeStruct((8, d), F32)],
        compiler_params=_params(("arbitrary",)))(u, u, u, dz, dz, dz, cw)


def ffn_act_fwd(name, up, cw):
    t = up.shape[0]
    ff = up.shape[1] // 2
    nt, tc = t // RB, _pick(ff, 1408)
    prev, nxt = _halo_specs(2 * ff, t)

    def body(up_ref, upp_ref, upn_ref, cw_ref, a_ref):
        first, last = _seg_flags(pl.program_id(0), nt)
        for j in range(ff // tc):
            sg, sv = slice(j * tc, (j + 1) * tc), slice(ff + j * tc, ff + (j + 1) * tc)
            gate = up_ref[:, sg].astype(F32)
            pg = _row(upp_ref[:, sg].astype(F32), HALO - 1)
            ng = _row(upn_ref[:, sg].astype(F32), 0)
            gd, gu = _shift_rows(gate, jnp.where(first, 0.0, pg), jnp.where(last, 0.0, ng))
            cwj = cw_ref[:, sg]
            g = gd * cwj[0:1] + gate * cwj[1:2] + gu * cwj[2:3] + cwj[3:4]
            a_ref[:, sg] = (g * _sigmoid(g) * up_ref[:, sv].astype(F32)).astype(BF16)

    return pl.pallas_call(
        body, name=name, grid=(nt,),
        in_specs=[pl.BlockSpec((RB, 2 * ff), lambda i: (i, 0)), prev, nxt, pl.BlockSpec((8, ff), lambda i: (0, 0))],
        out_specs=pl.BlockSpec((RB, ff), lambda i: (i, 0)),
        out_shape=jax.ShapeDtypeStruct((t, ff), BF16), compiler_params=_params(("parallel",)))(up, up, up, cw)


def ffn_act_bwd(name, up, da, cw):
    t = up.shape[0]
    ff = up.shape[1] // 2
    nt, tc = t // RB, _pick(ff, 1408)
    prev, nxt = _halo_specs(2 * ff, t)
    dprev, dnxt = _halo_specs(ff, t)

    def dsilu(g):
        s = _sigmoid(g)
        return s * (1.0 + g * (1.0 - s))

    def body(up_ref, upp_ref, upn_ref, da_ref, dap_ref, dan_ref, cw_ref, dup_ref, acc_ref):
        i = pl.program_id(0)
        first, last = _seg_flags(i, nt)

        @pl.when(i == 0)
        def _():
            acc_ref[...] = jnp.zeros_like(acc_ref)

        for j in range(ff // tc):
            sg, sv = slice(j * tc, (j + 1) * tc), slice(ff + j * tc, ff + (j + 1) * tc)
            gate, val, dav = up_ref[:, sg].astype(F32), up_ref[:, sv].astype(F32), da_ref[:, sg].astype(F32)
            pgt, ngt = upp_ref[:, sg].astype(F32), upn_ref[:, sg].astype(F32)
            pg1, pg2 = _row(pgt, HALO - 1), _row(pgt, HALO - 2)
            ng1, ng2 = _row(ngt, 0), _row(ngt, 1)
            cwj = cw_ref[:, sg]
            cw0, cw1, cw2, b = cwj[0:1], cwj[1:2], cwj[2:3], cwj[3:4]
            gd, gu = _shift_rows(gate, jnp.where(first, 0.0, pg1), jnp.where(last, 0.0, ng1))
            g = gd * cw0 + gate * cw1 + gu * cw2 + b
            g_p = pg2 * cw0 + pg1 * cw1 + _row(gate, 0) * cw2 + b
            g_n = _row(gate, RB - 1) * cw0 + ng1 * cw1 + ng2 * cw2 + b
            dg = dav * val * dsilu(g)
            dg_p = _row(dap_ref[:, sg].astype(F32) * upp_ref[:, sv].astype(F32), HALO - 1) * dsilu(g_p)
            dg_n = _row(dan_ref[:, sg].astype(F32) * upn_ref[:, sv].astype(F32), 0) * dsilu(g_n)
            dgd, dgu = _shift_rows(dg, jnp.where(first, 0.0, dg_p), jnp.where(last, 0.0, dg_n))
            dup_ref[:, sg] = (cw0 * dgu + cw1 * dg + cw2 * dgd).astype(BF16)
            dup_ref[:, sv] = (dav * g * _sigmoid(g)).astype(BF16)
            rows = lax.broadcasted_iota(jnp.int32, (8, tc), 0)
            acc_ref[:, sg] += (_rows3(jnp.sum(dg * gd, axis=0, keepdims=True), jnp.sum(dg * gate, axis=0, keepdims=True),
                                      jnp.sum(dg * gu, axis=0, keepdims=True), tc)
                               + jnp.where(rows == 3, jnp.sum(dg, axis=0, keepdims=True), 0.0))

    return pl.pallas_call(
        body, name=name, grid=(nt,),
        in_specs=[pl.BlockSpec((RB, 2 * ff), lambda i: (i, 0)), prev, nxt,
                  pl.BlockSpec((RB, ff), lambda i: (i, 0)), dprev, dnxt, pl.BlockSpec((8, ff), lambda i: (0, 0))],
        out_specs=[pl.BlockSpec((RB, 2 * ff), lambda i: (i, 0)), pl.BlockSpec((8, ff), lambda i: (0, 0))],
        out_shape=[jax.ShapeDtypeStruct((t, 2 * ff), BF16), jax.ShapeDtypeStruct((8, ff), F32)],
        compiler_params=_params(("arbitrary",)))(up, up, up, da, da, da, cw)


def _rot(z):
    w = z.shape[1]
    lane = lax.broadcasted_iota(jnp.int32, z.shape, 1)
    return jnp.where((lane % 64) < 32, -pltpu.roll(z, w - 32, 1), pltpu.roll(z, 32, 1))


def rope_fwd(name, qkv, cos, sin, gains, dq, dkv):
    t, nqkv = qkv.shape
    nh, nkv = dq // HEAD, dkv // HEAD

    def body(qkv_ref, cos_ref, sin_ref, g_ref, qr_ref, kr_ref):
        cs, sn = cos_ref[...], sin_ref[...]
        for hd in range(nh + nkv):
            c0 = hd * HEAD
            xh = qkv_ref[:, c0:c0 + HEAD].astype(F32)
            r = lax.rsqrt(jnp.mean(xh * xh, axis=-1, keepdims=True) + EPS)
            y = xh * r * (g_ref[0:1, :] if hd < nh else g_ref[1:2, :])
            yr = (y * cs + _rot(y) * sn).astype(BF16)
            if hd < nh:
                qr_ref[:, c0:c0 + HEAD] = yr
            else:
                kr_ref[:, c0 - dq:c0 - dq + HEAD] = yr

    return pl.pallas_call(
        body, name=name, grid=(t // RB,),
        in_specs=[pl.BlockSpec((RB, nqkv), lambda i: (i, 0)), pl.BlockSpec((RB, HEAD), lambda i: (i, 0)),
                  pl.BlockSpec((RB, HEAD), lambda i: (i, 0)), pl.BlockSpec((8, HEAD), lambda i: (0, 0))],
        out_specs=[pl.BlockSpec((RB, dq), lambda i: (i, 0)), pl.BlockSpec((RB, dkv), lambda i: (i, 0))],
        out_shape=[jax.ShapeDtypeStruct((t, dq), BF16), jax.ShapeDtypeStruct((t, dkv), BF16)],
        compiler_params=_params(("parallel",)))(qkv, cos, sin, gains)


def rope_bwd(name, qkv, dqr, dkr, dv, cos, sin, gains):
    t, nqkv = qkv.shape
    dq, dkv = dqr.shape[1], dkr.shape[1]
    nh, nkv = dq // HEAD, dkv // HEAD

    def body(qkv_ref, dq_ref, dk_ref, dv_ref, cos_ref, sin_ref, g_ref, out_ref, dg_ref):
        @pl.when(pl.program_id(0) == 0)
        def _():
            dg_ref[...] = jnp.zeros_like(dg_ref)

        cs, sn = cos_ref[...], sin_ref[...]
        zero = jnp.zeros((1, HEAD), F32)
        gq, gk = zero, zero
        for hd in range(nh + nkv):
            c0 = hd * HEAD
            xh = qkv_ref[:, c0:c0 + HEAD].astype(F32)
            r = lax.rsqrt(jnp.mean(xh * xh, axis=-1, keepdims=True) + EPS)
            xhat = xh * r
            dy = dq_ref[:, c0:c0 + HEAD] if hd < nh else dk_ref[:, c0 - dq:c0 - dq + HEAD]
            tt = dy * cs - _rot(dy * sn)
            gsum = jnp.sum(tt * xhat, axis=0, keepdims=True)
            if hd < nh:
                gq = gq + gsum
            else:
                gk = gk + gsum
            dxh = tt * (g_ref[0:1, :] if hd < nh else g_ref[1:2, :])
            dx = r * (dxh - xhat * jnp.mean(dxh * xhat, axis=-1, keepdims=True))
            out_ref[:, c0:c0 + HEAD] = dx.astype(BF16)
        out_ref[:, dq + dkv:] = dv_ref[...].astype(BF16)
        dg_ref[...] += _rows3(gq, gk, zero, HEAD)

    return pl.pallas_call(
        body, name=name, grid=(t // RB,),
        in_specs=[pl.BlockSpec((RB, nqkv), lambda i: (i, 0)), pl.BlockSpec((RB, dq), lambda i: (i, 0)),
                  pl.BlockSpec((RB, dkv), lambda i: (i, 0)), pl.BlockSpec((RB, dkv), lambda i: (i, 0)),
                  pl.BlockSpec((RB, HEAD), lambda i: (i, 0)), pl.BlockSpec((RB, HEAD), lambda i: (i, 0)),
                  pl.BlockSpec((8, HEAD), lambda i: (0, 0))],
        out_specs=[pl.BlockSpec((RB, nqkv), lambda i: (i, 0)), pl.BlockSpec((8, HEAD), lambda i: (0, 0))],
        out_shape=[jax.ShapeDtypeStruct((t, nqkv), BF16), jax.ShapeDtypeStruct((8, HEAD), F32)],
        compiler_params=_params(("arbitrary",)))(qkv, dqr, dkr, dv, cos, sin, gains)


def resid_bwd(name, dx, dh, x, mod_n, sh, sc, y_prev=None, mod_g=None, gi=0):
    t, d = x.shape
    has_prev = y_prev is not None

    def body(*refs):
        if has_prev:
            dx_ref, dh_ref, x_ref, mn_ref, y_ref, mg_ref, dxo_ref, dy_ref, acc_ref = refs
        else:
            dx_ref, dh_ref, x_ref, mn_ref, dxo_ref, acc_ref = refs
        i = pl.program_id(0)
        seg = jnp.minimum(i, 1)

        @pl.when(i == 0)
        def _():
            acc_ref[...] = jnp.zeros_like(acc_ref)

        xv, dhv = x_ref[...], dh_ref[...]
        r = lax.rsqrt(jnp.mean(xv * xv, axis=-1, keepdims=True) + EPS)
        xhat = xv * r
        m = mn_ref[seg]
        dxh = dhv * (1.0 + m[sc:sc + 1, :])
        dxo = dx_ref[...] + r * (dxh - xhat * jnp.mean(dxh * xhat, axis=-1, keepdims=True))
        dxo_ref[...] = dxo
        s2 = jnp.zeros((1, d), F32)
        if has_prev:
            dy_ref[...] = (mg_ref[seg][gi:gi + 1, :] * dxo).astype(BF16)
            s2 = jnp.sum(dxo * y_ref[...].astype(F32), axis=0, keepdims=True)
        acc_ref[seg] = acc_ref[seg] + _rows3(jnp.sum(dhv, axis=0, keepdims=True),
                                             jnp.sum(dhv * xhat, axis=0, keepdims=True), s2, d)

    row = pl.BlockSpec((RB, d), lambda i: (i, 0))
    modspec = pl.BlockSpec((2, 8, d), lambda i: (0, 0, 0))
    in_specs, operands = [row, row, row, modspec], [dx, dh, x, mod_n]
    out_specs, out_shape = [row], [jax.ShapeDtypeStruct((t, d), F32)]
    if has_prev:
        in_specs += [row, modspec]
        operands += [y_prev, mod_g]
        out_specs.append(row)
        out_shape.append(jax.ShapeDtypeStruct((t, d), BF16))
    out_specs.append(modspec)
    out_shape.append(jax.ShapeDtypeStruct((2, 8, d), F32))
    return pl.pallas_call(body, name=name, grid=(t // RB,), in_specs=in_specs, out_specs=out_specs,
                          out_shape=out_shape, compiler_params=_params(("arbitrary",)))(*operands)


def loss_head(name, xf, target, y_last, mod, gi):
    t, d = xf.shape

    def body(x_ref, t_ref, y_ref, mod_ref, dx_ref, dy_ref, acc_ref, lp_ref):
        i = pl.program_id(0)
        seg = jnp.minimum(i, 1)

        @pl.when(i == 0)
        def _():
            acc_ref[...] = jnp.zeros_like(acc_ref)
            lp_ref[...] = jnp.zeros_like(lp_ref)

        lat = i >= 1
        err = jnp.where(lat, x_ref[...] - t_ref[...], 0.0)
        dxv = err / d
        dx_ref[...] = dxv
        dy_ref[...] = (mod_ref[seg][gi:gi + 1, :] * dxv).astype(BF16)
        zero = jnp.zeros((1, d), F32)
        lp_ref[...] += _rows3(jnp.sum(err * err, axis=0, keepdims=True), zero, zero, d)
        acc_ref[seg] = acc_ref[seg] + _rows3(zero, zero, jnp.sum(dxv * y_ref[...].astype(F32), axis=0, keepdims=True), d)

    row = pl.BlockSpec((RB, d), lambda i: (i, 0))
    modspec = pl.BlockSpec((2, 8, d), lambda i: (0, 0, 0))
    return pl.pallas_call(
        body, name=name, grid=(t // RB,),
        in_specs=[row, pl.BlockSpec((RB, d), lambda i: (jnp.maximum(i - 1, 0), 0)), row, modspec],
        out_specs=[row, row, modspec, pl.BlockSpec((8, d), lambda i: (0, 0))],
        out_shape=[jax.ShapeDtypeStruct((t, d), F32), jax.ShapeDtypeStruct((t, d), BF16),
                   jax.ShapeDtypeStruct((2, 8, d), F32), jax.ShapeDtypeStruct((8, d), F32)],
        compiler_params=_params(("arbitrary",)))(xf, target, y_last, mod)


def _kv_specs(width, colblk, nbk):
    return [pl.BlockSpec((CTX, width), lambda i: (0, colblk)),
            pl.BlockSpec((BLK, width), lambda i: (jnp.maximum(i - 1, 0), colblk)),
            pl.BlockSpec((BLK, width), lambda i: (i, colblk)),
            pl.BlockSpec((BLK, width), lambda i: (jnp.minimum(i + 1, nbk - 1), colblk))]


def _band_mask(i, seq):
    nk = CTX + 3 * BLK
    qrow = lax.broadcasted_iota(jnp.int32, (GROUP * BLK, nk), 0) % BLK
    col = lax.broadcasted_iota(jnp.int32, (GROUP * BLK, nk), 1)
    cb = col - CTX
    kpos = (i - 3) * BLK + cb
    band = (i >= 2) & (jnp.abs(BLK + qrow - cb) <= WINDOW) & (kpos >= 0) & (kpos < seq)
    return (col < CTX) | band


def _stack_heads(ref, h):
    return jnp.concatenate([ref[:, (h * GROUP + g) * HEAD:(h * GROUP + g + 1) * HEAD] for g in range(GROUP)], axis=0)


def _stack_cols(v, h):
    return jnp.concatenate([_get_col(v, h * GROUP + g) for g in range(GROUP)], axis=0)


def _sink_col(sink_ref, h):
    rowg = lax.broadcasted_iota(jnp.int32, (GROUP * BLK, 1), 0) // BLK
    sk = jnp.full((GROUP * BLK, 1), sink_ref[h * GROUP], F32)
    for g in range(1, GROUP):
        sk = jnp.where(rowg == g, sink_ref[h * GROUP + g], sk)
    return sk


def attn_fwd(name, qr, kr, qkv, sink, seq):
    t, dq = qr.shape
    dkv = kr.shape[1]
    nbk, nkv = t // BLK, dkv // HEAD
    vcol = (dq + dkv) // dkv
    scale = HEAD ** -0.5

    def body(sink_ref, q_ref, kc, kp, ko, kn, vc, vp, vo, vn, o_ref, lse_ref, lset_ref):
        i = pl.program_id(0)
        mask = _band_mask(i, seq)
        lse = jnp.zeros((BLK, LANE), F32)
        for h in range(nkv):
            hs = slice(h * HEAD, (h + 1) * HEAD)
            k = jnp.concatenate([kc[:, hs], kp[:, hs], ko[:, hs], kn[:, hs]], axis=0)
            v = jnp.concatenate([vc[:, hs], vp[:, hs], vo[:, hs], vn[:, hs]], axis=0)
            q4 = _stack_heads(q_ref, h)
            s = jnp.where(mask, lax.dot_general(q4, k, NT, preferred_element_type=F32) * scale, NEG)
            sk = _sink_col(sink_ref, h)
            m = jnp.maximum(jnp.max(s, axis=-1, keepdims=True), sk)
            e = jnp.exp(s - m)
            den = jnp.sum(e, axis=-1, keepdims=True) + jnp.exp(sk - m)
            o4 = jnp.dot((e * (1.0 / den)).astype(BF16), v, preferred_element_type=F32)
            l4 = m + jnp.log(den)
            for g in range(GROUP):
                hg = h * GROUP + g
                o_ref[:, hg * HEAD:(hg + 1) * HEAD] = o4[g * BLK:(g + 1) * BLK].astype(BF16)
                lse = _put_col(lse, hg, l4[g * BLK:(g + 1) * BLK])
        lse_ref[...] = lse
        lset_ref[...] = lse.T[:nh]

    nh = dq // HEAD
    return pl.pallas_call(
        body, name=name, grid=(nbk,),
        in_specs=[pl.BlockSpec(memory_space=pltpu.SMEM), pl.BlockSpec((BLK, dq), lambda i: (i, 0))]
        + _kv_specs(dkv, 0, nbk) + _kv_specs(dkv, vcol, nbk),
        out_specs=[pl.BlockSpec((BLK, dq), lambda i: (i, 0)), pl.BlockSpec((BLK, LANE), lambda i: (i, 0)),
                   pl.BlockSpec((nh, BLK), lambda i: (0, i))],
        out_shape=[jax.ShapeDtypeStruct((t, dq), BF16), jax.ShapeDtypeStruct((t, LANE), F32),
                   jax.ShapeDtypeStruct((nh, t), F32)],
        compiler_params=_params(("parallel",)))(sink, qr, kr, kr, kr, kr, qkv, qkv, qkv, qkv)


def attn_bwd_q(name, qr, kr, qkv, sink, do, o, lse, seq):
    t, dq = qr.shape
    dkv = kr.shape[1]
    nbk, nkv = t // BLK, dkv // HEAD
    vcol = (dq + dkv) // dkv
    scale = HEAD ** -0.5

    def body(sink_ref, q_ref, kc, kp, ko, kn, vc, vp, vo, vn, do_ref, o_ref, lse_ref,
             dq_ref, dl_ref, dkc_ref, dvc_ref, ds_ref):
        i = pl.program_id(0)

        @pl.when(i == 0)
        def _():
            dkc_ref[...] = jnp.zeros_like(dkc_ref)
            dvc_ref[...] = jnp.zeros_like(dvc_ref)
            ds_ref[...] = jnp.zeros_like(ds_ref)

        mask = _band_mask(i, seq)
        lse = lse_ref[...]
        delta = jnp.zeros((BLK, LANE), F32)
        dsink = jnp.zeros((8, LANE), F32)
        for h in range(nkv):
            hs = slice(h * HEAD, (h + 1) * HEAD)
            k = jnp.concatenate([kc[:, hs], kp[:, hs], ko[:, hs], kn[:, hs]], axis=0)
            v = jnp.concatenate([vc[:, hs], vp[:, hs], vo[:, hs], vn[:, hs]], axis=0)
            q4, do4 = _stack_heads(q_ref, h), _stack_heads(do_ref, h)
            d4 = jnp.sum(do4.astype(F32) * _stack_heads(o_ref, h).astype(F32), axis=-1, keepdims=True)
            l4 = _stack_cols(lse, h)
            s = jnp.where(mask, lax.dot_general(q4, k, NT, preferred_element_type=F32) * scale, NEG)
            p = jnp.exp(s - l4)
            dp = lax.dot_general(do4, v, NT, preferred_element_type=F32)
            dsb = (p * (dp - d4) * scale).astype(BF16)
            pb = p.astype(BF16)
            dq4 = jnp.dot(dsb, k, preferred_element_type=F32)
            dkc_ref[:, hs] += lax.dot_general(dsb[:, :CTX], q4, TN, preferred_element_type=F32)
            dvc_ref[:, hs] += lax.dot_general(pb[:, :CTX], do4, TN, preferred_element_type=F32)
            dsk = -jnp.exp(_sink_col(sink_ref, h) - l4) * d4
            for g in range(GROUP):
                hg = h * GROUP + g
                rs = slice(g * BLK, (g + 1) * BLK)
                dq_ref[:, hg * HEAD:(hg + 1) * HEAD] = dq4[rs]
                delta = _put_col(delta, hg, d4[rs])
                dsink = _put_col(dsink, hg, jnp.sum(dsk[rs], axis=0, keepdims=True))
        dl_ref[...] = delta.T[:nh]
        rows = lax.broadcasted_iota(jnp.int32, (8, LANE), 0)
        ds_ref[...] += jnp.where(rows == 0, dsink, 0.0)

    nh = dq // HEAD
    blk = lambda w: pl.BlockSpec((BLK, w), lambda i: (i, 0))
    const = lambda r, w: pl.BlockSpec((r, w), lambda i: (0, 0))
    return pl.pallas_call(
        body, name=name, grid=(nbk,),
        in_specs=[pl.BlockSpec(memory_space=pltpu.SMEM), blk(dq)] + _kv_specs(dkv, 0, nbk) + _kv_specs(dkv, vcol, nbk)
        + [blk(dq), blk(dq), blk(LANE)],
        out_specs=[blk(dq), pl.BlockSpec((nh, BLK), lambda i: (0, i)), const(CTX, dkv), const(CTX, dkv), const(8, LANE)],
        out_shape=[jax.ShapeDtypeStruct((t, dq), F32), jax.ShapeDtypeStruct((nh, t), F32),
                   jax.ShapeDtypeStruct((CTX, dkv), F32), jax.ShapeDtypeStruct((CTX, dkv), F32),
                   jax.ShapeDtypeStruct((8, LANE), F32)],
        compiler_params=_params(("arbitrary",)))(sink, qr, kr, kr, kr, kr, qkv, qkv, qkv, qkv, do, o, lse)


def attn_bwd_kv(name, qr, kr, qkv, do, lset, deltat, seq):
    t, dq = qr.shape
    dkv = kr.shape[1]
    nbk, nbl, nkv, nh = t // BLK, seq // BLK, dkv // HEAD, dq // HEAD
    cb = CTX // BLK
    vcol = (dq + dkv) // dkv
    scale = HEAD ** -0.5

    def qspec(w, d):
        return pl.BlockSpec((BLK, w), lambda j: (jnp.clip(j + cb + d, cb, nbk - 1), 0))

    def tspec(d):
        return pl.BlockSpec((nh, BLK), lambda j: (0, jnp.clip(j + cb + d, cb, nbk - 1)))

    def stack_rows(v, h):
        return jnp.concatenate([v[h * GROUP + g:h * GROUP + g + 1, :] for g in range(GROUP)], axis=1)

    def body(k_ref, v_ref, *refs):
        dk_ref, dv_ref = refs[-2], refs[-1]
        j = pl.program_id(0)
        krow = lax.broadcasted_iota(jnp.int32, (BLK, GROUP * BLK), 0)
        qcol = lax.broadcasted_iota(jnp.int32, (BLK, GROUP * BLK), 1) % BLK
        for h in range(nkv):
            hs = slice(h * HEAD, (h + 1) * HEAD)
            kh, vh = k_ref[:, hs], v_ref[:, hs]
            dk_h = jnp.zeros((BLK, HEAD), F32)
            dv_h = jnp.zeros((BLK, HEAD), F32)
            for di, d in enumerate((-1, 0, 1)):
                q_ref, do_ref, lse_ref, dl_ref = refs[4 * di:4 * di + 4]
                n = j + d
                msk = (n >= 0) & (n < nbl) & (jnp.abs(d * BLK + qcol - krow) <= WINDOW)
                q4, do4 = _stack_heads(q_ref, h), _stack_heads(do_ref, h)
                l4, d4 = stack_rows(lse_ref[...], h), stack_rows(dl_ref[...], h)
                s = jnp.where(msk, lax.dot_general(kh, q4, NT, preferred_element_type=F32) * scale, NEG)
                p = jnp.exp(s - l4)
                dv_h += jnp.dot(p.astype(BF16), do4, preferred_element_type=F32)
                dp = lax.dot_general(vh, do4, NT, preferred_element_type=F32)
                dk_h += jnp.dot((p * (dp - d4) * scale).astype(BF16), q4, preferred_element_type=F32)
            dk_ref[:, hs] = dk_h
            dv_ref[:, hs] = dv_h

    in_specs = [pl.BlockSpec((BLK, dkv), lambda j: (j + cb, 0)), pl.BlockSpec((BLK, dkv), lambda j: (j + cb, vcol))]
    operands = [kr, qkv]
    for d in (-1, 0, 1):
        in_specs += [qspec(dq, d), qspec(dq, d), tspec(d), tspec(d)]
        operands += [qr, do, lset, deltat]
    return pl.pallas_call(
        body, name=name, grid=(nbl,), in_specs=in_specs,
        out_specs=[pl.BlockSpec((BLK, dkv), lambda j: (j, 0)), pl.BlockSpec((BLK, dkv), lambda j: (j, 0))],
        out_shape=[jax.ShapeDtypeStruct((seq, dkv), F32), jax.ShapeDtypeStruct((seq, dkv), F32)],
        compiler_params=_params(("parallel",)))(*operands)


def ada_fwd(name, cond, w_ada):
    nl, d, n = w_ada.shape
    tn = _pick(n, 1024)

    def body(c_ref, w_ref, out_ref):
        cv = c_ref[...]
        out_ref[...] = jnp.dot((cv * _sigmoid(cv)).astype(BF16), w_ref[...].astype(BF16), preferred_element_type=F32)

    return pl.pallas_call(
        body, name=name, grid=(nl, n // tn),
        in_specs=[pl.BlockSpec((16, d), lambda l, j: (0, 0)), pl.BlockSpec((None, d, tn), lambda l, j: (l, 0, j))],
        out_specs=pl.BlockSpec((None, 16, tn), lambda l, j: (l, 0, j)),
        out_shape=jax.ShapeDtypeStruct((nl, 16, n), F32), compiler_params=_params(("parallel", "parallel")))(cond, w_ada)


def ada_bwd_cond(name, dsum, w_ada):
    nl, d, n = w_ada.shape
    tn = _pick(n, 1024)

    def body(g_ref, w_ref, out_ref):
        @pl.when((pl.program_id(0) == 0) & (pl.program_id(1) == 0))
        def _():
            out_ref[...] = jnp.zeros_like(out_ref)

        out_ref[...] += lax.dot_general(g_ref[...].astype(BF16), w_ref[...].astype(BF16), NT, preferred_element_type=F32)

    return pl.pallas_call(
        body, name=name, grid=(nl, n // tn),
        in_specs=[pl.BlockSpec((None, 8, tn), lambda l, j: (l, 0, j)), pl.BlockSpec((None, d, tn), lambda l, j: (l, 0, j))],
        out_specs=pl.BlockSpec((8, d), lambda l, j: (0, 0)),
        out_shape=jax.ShapeDtypeStruct((8, d), F32), compiler_params=_params(("arbitrary", "arbitrary")))(dsum, w_ada)


def ada_grad_w(name, cond, rhs):
    nl, _, n = rhs.shape
    d = cond.shape[1]
    tr, tn = _pick(d, 512), _pick(n, 1024)

    def body(c_ref, r_ref, out_ref):
        cv = c_ref[...]
        out_ref[...] = lax.dot_general((cv * _sigmoid(cv)).astype(BF16), r_ref[...].astype(BF16), TN, preferred_element_type=F32)

    return pl.pallas_call(
        body, name=name, grid=(nl, d // tr, n // tn),
        in_specs=[pl.BlockSpec((16, tr), lambda l, i, j: (0, i)), pl.BlockSpec((None, 16, tn), lambda l, i, j: (l, 0, j))],
        out_specs=pl.BlockSpec((None, tr, tn), lambda l, i, j: (l, i, j)),
        out_shape=jax.ShapeDtypeStruct((nl, d, n), F32),
        compiler_params=_params(("parallel", "parallel", "parallel")))(cond, rhs)


def adamw(name, g, g_spec, w, m, v, tr):
    nl, r, c = w.shape
    spec = pl.BlockSpec((None, tr, c), lambda l, i: (l, i, 0))

    def body(g_ref, w_ref, m_ref, v_ref, go_ref, d_ref, mo_ref, vo_ref):
        gv = g_ref[...]
        mn = B1 * m_ref[...] + (1.0 - B1) * gv
        vn = B2 * v_ref[...] + (1.0 - B2) * (gv * gv)
        m_hat = mn / (1.0 - B1 ** STEP)
        v_hat = vn / (1.0 - B2 ** STEP)
        go_ref[...] = gv
        d_ref[...] = -LR * (m_hat / (jnp.sqrt(v_hat) + ADAM_EPS) + WD * w_ref[...])
        mo_ref[...] = mn
        vo_ref[...] = vn

    return pl.pallas_call(
        body, name=name, grid=(nl, r // tr), in_specs=[g_spec, spec, spec, spec], out_specs=[spec] * 4,
        out_shape=[jax.ShapeDtypeStruct(w.shape, F32)] * 4, compiler_params=_params(("parallel", "parallel")))(g, w, m, v)


def adamw_small(name, g, w, m, v):
    shape = w.shape
    r3 = lambda a: a.reshape(1, -1, shape[-1]).astype(F32)
    rows = r3(w).shape[1]
    outs = adamw(name, r3(g), pl.BlockSpec((None, rows, shape[-1]), lambda l, i: (l, i, 0)), r3(w), r3(m), r3(v), rows)
    return [o.reshape(shape) for o in outs]


def _place():
    x, y, c = lax.axis_index("x"), lax.axis_index("y"), lax.axis_index("c")
    return x, y, c, [(1 - x, y), (x, 1 - y), (1 - x, 1 - y)]


def small_allgather(name, v):
    r, w = v.shape

    def body(x_ref, out_ref, send_sems, recv_sems, local_sem):
        x, y, c, chips = _place()
        me, sibling = (x, y, c), (x, y, 1 - c)

        def slot(px, py, pc):
            return out_ref.at[4 * px + 2 * py + pc]

        def copy(k, block, to, src=None):
            return pltpu.make_async_remote_copy(
                src_ref=slot(*block) if src is None else src, dst_ref=slot(*block),
                send_sem=send_sems.at[k], recv_sem=recv_sems.at[k], device_id=to, device_id_type=MESH)

        mine = pltpu.make_async_copy(x_ref, slot(*me), local_sem)
        mine.start()
        first = [copy(0, me, sibling, src=x_ref)]
        first += [copy(1 + j, me, (*chip, c), src=x_ref) for j, chip in enumerate(chips)]
        for cp in first:
            cp.start()
        passed = [copy(4 + j, (*chip, c), sibling) for j, chip in enumerate(chips)]
        for j, chip in enumerate(chips):
            copy(1 + j, (*chip, c), me).wait_recv()
            passed[j].start()
        copy(0, sibling, me).wait_recv()
        for j, chip in enumerate(chips):
            copy(4 + j, (*chip, 1 - c), me).wait_recv()
        for cp in first + passed:
            cp.wait_send()
        mine.wait()

    return pl.pallas_call(
        body, name=name, out_shape=jax.ShapeDtypeStruct((8, r, w), v.dtype),
        in_specs=[pl.BlockSpec(memory_space=pltpu.VMEM)], out_specs=pl.BlockSpec(memory_space=pltpu.VMEM),
        scratch_shapes=[pltpu.SemaphoreType.DMA((7,)), pltpu.SemaphoreType.DMA((7,)), pltpu.SemaphoreType.DMA],
        compiler_params=pltpu.CompilerParams(vmem_limit_bytes=VMEM_LIMIT))(v)


def gather_flat(name, parts):
    flat = jnp.concatenate([p.reshape(-1).astype(F32) for p in parts])
    n = flat.shape[0]
    rows = _cdiv(n, MXU * LANE) * MXU
    flat = jnp.pad(flat, (0, rows * LANE - n))
    return small_allgather(name, flat.reshape(rows, LANE)).reshape(8, rows * LANE)


def sum8(name, g):
    p = g.shape[1]
    g3 = g.reshape(8, p // LANE, LANE)
    tr = _pick(p // LANE, 1024, MXU)

    def body(g_ref, out_ref):
        acc = g_ref[0]
        for k in range(1, 8):
            acc = acc + g_ref[k]
        out_ref[...] = acc

    return pl.pallas_call(
        body, name=name, grid=(p // LANE // tr,),
        in_specs=[pl.BlockSpec((8, tr, LANE), lambda i: (0, i, 0))], out_specs=pl.BlockSpec((tr, LANE), lambda i: (i, 0)),
        out_shape=jax.ShapeDtypeStruct((p // LANE, LANE), F32), compiler_params=_params(("parallel",)))(g3).reshape(p)


HBM_SPEC = pl.BlockSpec(memory_space=pltpu.HBM)


def _half(ref, lead, c, axis):
    h = ref.shape[axis] // 2
    return ref.at[lead, pl.ds(c * h, h), :] if axis == 1 else ref.at[lead, :, pl.ds(c * h, h)]


def _gather_ops(outs, axes, send_sems, recv_sems):
    x, y, c, chips = _place()
    me, sibling = (x, y, c), (x, y, 1 - c)

    def copy(a, k, chip, pc, to):
        blk = _half(outs[a], 2 * chip[0] + chip[1], pc, axes[a])
        return pltpu.make_async_remote_copy(src_ref=blk, dst_ref=blk, send_sem=send_sems.at[6 * a + k],
                                            recv_sem=recv_sems.at[6 * a + k], device_id=to, device_id_type=MESH)

    def start():
        for a in range(len(outs)):
            for j, chip in enumerate(chips):
                copy(a, j, (x, y), c, (*chip, c)).start()

    def finish():
        for a in range(len(outs)):
            for j, chip in enumerate(chips):
                copy(a, j, chip, c, me).wait_recv()
                copy(a, 3 + j, chip, c, sibling).start()
        for a in range(len(outs)):
            for j, chip in enumerate(chips):
                copy(a, 3 + j, chip, 1 - c, me).wait_recv()
            for j, chip in enumerate(chips):
                copy(a, j, (x, y), c, (*chip, c)).wait_send()
                copy(a, 3 + j, chip, c, sibling).wait_send()

    return start, finish


def gather_comm(bufs, axes):
    return dict(ins=list(bufs), out_shape=[jax.ShapeDtypeStruct(b.shape, b.dtype) for b in bufs],
                aliases={a: a for a in range(len(bufs))}, n_sems=6 * len(bufs),
                ops=lambda cin, couts, ss, rs: _gather_ops(couts, axes, ss, rs))


def gather_weights(name, bufs, axes):
    n = len(bufs)

    def body(*refs):
        start, finish = _gather_ops(refs[n:2 * n], axes, refs[2 * n], refs[2 * n + 1])
        start()
        finish()

    return pl.pallas_call(
        body, name=name, out_shape=[jax.ShapeDtypeStruct(b.shape, b.dtype) for b in bufs],
        in_specs=[HBM_SPEC] * n, out_specs=[HBM_SPEC] * n, input_output_aliases={a: a for a in range(n)},
        scratch_shapes=[pltpu.SemaphoreType.DMA((6 * n,)), pltpu.SemaphoreType.DMA((6 * n,))])(*bufs)


def _scatter_ops(ins, outs, send_sems, recv_sems):
    x, y, c, chips = _place()
    me = 2 * x + y

    def copy(a, j, chip):
        return pltpu.make_async_remote_copy(
            src_ref=ins[a].at[2 * chip[0] + chip[1]], dst_ref=outs[a].at[me], send_sem=send_sems.at[3 * a + j],
            recv_sem=recv_sems.at[3 * a + j], device_id=(*chip, c), device_id_type=MESH)

    def start():
        for a in range(len(ins)):
            for j, chip in enumerate(chips):
                copy(a, j, chip).start()

    def finish():
        for a in range(len(ins)):
            for j, chip in enumerate(chips):
                copy(a, j, chip).wait()

    return start, finish


def scatter_comm(bufs):
    return dict(ins=list(bufs), out_shape=[jax.ShapeDtypeStruct(b.shape, b.dtype) for b in bufs], aliases={},
                n_sems=3 * len(bufs), ops=_scatter_ops)


def chip_scatter(name, bufs):
    n = len(bufs)

    def body(*refs):
        start, finish = _scatter_ops(refs[:n], refs[n:2 * n], refs[2 * n], refs[2 * n + 1])
        start()
        finish()

    return pl.pallas_call(
        body, name=name, out_shape=[jax.ShapeDtypeStruct(b.shape, b.dtype) for b in bufs],
        in_specs=[HBM_SPEC] * n, out_specs=[HBM_SPEC] * n,
        scratch_shapes=[pltpu.SemaphoreType.DMA((3 * n,)), pltpu.SemaphoreType.DMA((3 * n,))])(*bufs)


def pair_exchange(name, bufs, axes):
    n = len(bufs)

    def body(*refs):
        ins, outs, (send_sems, recv_sems) = refs[:n], refs[n:2 * n], refs[2 * n:]
        x, y, c, _ = _place()
        cps = []
        for a, (src, out) in enumerate(zip(ins, outs)):
            cp = pltpu.make_async_remote_copy(
                src_ref=_half(src, slice(None), 1 - c, axes[a]), dst_ref=out, send_sem=send_sems.at[a],
                recv_sem=recv_sems.at[a], device_id=(x, y, 1 - c), device_id_type=MESH)
            cp.start()
            cps.append(cp)
        for cp in cps:
            cp.wait()

    def halved(b, axis):
        shape = list(b.shape)
        shape[axis] //= 2
        return jax.ShapeDtypeStruct(tuple(shape), b.dtype)

    return pl.pallas_call(
        body, name=name, out_shape=[halved(b, ax) for b, ax in zip(bufs, axes)],
        in_specs=[HBM_SPEC] * n, out_specs=[HBM_SPEC] * n,
        scratch_shapes=[pltpu.SemaphoreType.DMA((n,)), pltpu.SemaphoreType.DMA((n,))])(*bufs)


def pair_add(name, buf, got, cidx, axis):
    s, r, c = got.shape
    tr = _pick(r, max(HALO, (4 * 1024 * 1024) // (2 * c)), HALO)
    per = r // tr
    if axis == 1:
        mine = pl.BlockSpec((None, tr, c), lambda k, i, cr: (k, cr[0] * per + i, 0))
    else:
        mine = pl.BlockSpec((None, tr, c), lambda k, i, cr: (k, i, cr[0]))

    def body(c_ref, a_ref, b_ref, out_ref):
        out_ref[...] = (a_ref[...].astype(F32) + b_ref[...].astype(F32)).astype(BF16)

    return pl.pallas_call(
        body, name=name,
        grid_spec=pltpu.PrefetchScalarGridSpec(
            num_scalar_prefetch=1, grid=(s, per),
            in_specs=[mine, pl.BlockSpec((None, tr, c), lambda k, i, cr: (k, i, 0))],
            out_specs=pl.BlockSpec((None, tr, c), lambda k, i, cr: (k, i, 0))),
        out_shape=jax.ShapeDtypeStruct((s, r, c), BF16),
        compiler_params=_params(("parallel", "parallel")))(cidx, buf, got)


def chip_add(name, own, got, place, dst, l, off, size, col):
    s = got.shape[0]
    if col:
        h, n = got.shape[1], size
        tr = _pick(h, max(HALO, (2 * 1024 * 1024) // (2 * n)), HALO)
        per, ob = h // tr, off // n
        own_spec = pl.BlockSpec((None, tr, n), lambda i, p: (p[0], i, ob))
        got_spec = pl.BlockSpec((s, tr, n), lambda i, p: (0, i, ob))
        out_spec = pl.BlockSpec((None, tr, n), lambda i, p: (l, p[1] * per + i, 0))
        grid = (per,)
    else:
        n = got.shape[2]
        tr = _pick(size, max(HALO, (2 * 1024 * 1024) // (2 * n)), HALO)
        ob = off // tr
        own_spec = pl.BlockSpec((None, tr, n), lambda i, p: (p[0], ob + i, 0))
        got_spec = pl.BlockSpec((s, tr, n), lambda i, p: (0, ob + i, 0))
        out_spec = pl.BlockSpec((None, tr, n), lambda i, p: (l, i, p[1]))
        grid = (size // tr,)

    def body(p_ref, own_ref, g_ref, dst_ref, out_ref):
        acc = jnp.zeros((tr, n), F32)
        for k in range(s):
            acc = acc + jnp.where(p_ref[0] == k, own_ref[...], g_ref[k]).astype(F32)
        out_ref[...] = acc

    return pl.pallas_call(
        body, name=name,
        grid_spec=pltpu.PrefetchScalarGridSpec(
            num_scalar_prefetch=1, grid=grid,
            in_specs=[own_spec, got_spec, pl.BlockSpec(memory_space=pl.ANY)], out_specs=out_spec),
        out_shape=jax.ShapeDtypeStruct(dst.shape, F32), input_output_aliases={3: 0},
        compiler_params=_params(("parallel",)))(place, own, got, dst)


def pair_join(name, bufs, axes):
    n = len(bufs)

    def body(*refs):
        outs = refs[n:2 * n]
        send_sems, recv_sems = refs[2 * n:]
        x, y, c, _ = _place()
        started = []
        for a, out in enumerate(outs):
            blk = _half(out, slice(None), c, axes[a])
            cp = pltpu.make_async_remote_copy(src_ref=blk, dst_ref=blk, send_sem=send_sems.at[a], recv_sem=recv_sems.at[a],
                                              device_id=(x, y, 1 - c), device_id_type=MESH)
            cp.start()
            started.append(cp)
        for cp in started:
            cp.wait()

    return pl.pallas_call(
        body, name=name, out_shape=[jax.ShapeDtypeStruct(b.shape, b.dtype) for b in bufs],
        in_specs=[HBM_SPEC] * n, out_specs=[HBM_SPEC] * n, input_output_aliases={a: a for a in range(n)},
        scratch_shapes=[pltpu.SemaphoreType.DMA((n,)), pltpu.SemaphoreType.DMA((n,))])(*bufs)


def _rope_tables(seq):
    rows = seq // GRID_W
    row = jnp.repeat(jnp.arange(rows), GRID_W).astype(F32)
    col = jnp.tile(jnp.arange(GRID_W), rows).astype(F32)
    pairs = HEAD // 4
    inv = ROPE_BASE ** (-jnp.arange(pairs, dtype=F32) / pairs)
    ang = jnp.stack([row[:, None] * inv, col[:, None] * inv], axis=1)
    ang = jnp.broadcast_to(ang[:, :, None, :], (seq, 2, 2, pairs)).reshape(seq, HEAD)
    cos = jnp.concatenate([jnp.ones((CTX, HEAD), F32), jnp.cos(ang)], axis=0)
    sin = jnp.concatenate([jnp.zeros((CTX, HEAD), F32), jnp.sin(ang)], axis=0)
    return cos, sin


def _pad8(a):
    return jnp.pad(a, ((0, 8 - a.shape[0]), (0, 0)))


def kernel(x, c, ctx, c_ctx, w_ada, b_ada, attn_w_qkv, attn_w_o, attn_q_gain, attn_k_gain, attn_sink, sc_w_in, sc_conv, sc_w_out, ffn_w_up, ffn_conv, ffn_conv_b, ffn_w_down, loss_target, m_c_ctx, m_w_ada, m_b_ada, m_attn_w_qkv, m_attn_w_o, m_attn_q_gain, m_attn_k_gain, m_attn_sink, m_sc_w_in, m_sc_conv, m_sc_w_out, m_ffn_w_up, m_ffn_conv, m_ffn_conv_b, m_ffn_w_down, v_c_ctx, v_w_ada, v_b_ada, v_attn_w_qkv, v_attn_w_o, v_attn_q_gain, v_attn_k_gain, v_attn_sink, v_sc_w_in, v_sc_conv, v_sc_w_out, v_ffn_w_up, v_ffn_conv, v_ffn_conv_b, v_ffn_w_down):
    seq, d = x.shape[1], x.shape[2]
    depth, nada = w_ada.shape[0], w_ada.shape[2]
    n_attn, n_conv = attn_w_qkv.shape[0], sc_w_in.shape[0]
    ff = ffn_conv_b.shape[1]
    dq, dkv = d, d // GROUP
    nqkv = dq + 2 * dkv
    nh = dq // HEAD
    assert ctx.shape[1] == CTX and seq % RB == 0 and 6 * d == 4 * nada
    lay = Layout(d, ff, nqkv, depth)
    ax, ay, ac = lax.axis_index("x"), lax.axis_index("y"), lax.axis_index("c")
    chip, dev = 2 * ax + ay, 4 * ax + 2 * ay + ac
    cidx = jnp.reshape(ac, (1,)).astype(jnp.int32)

    chip1 = jnp.reshape(chip, (1,)).astype(jnp.int32)
    wcs, wrs = [], []
    for l in range(depth):
        j, is_attn = l // N_MIX, l % N_MIX == 0
        wc_l, wr_l = lax.empty((4, d, lay.ct[l]), BF16), lax.empty((4, lay.rt, d), BF16)
        wc_l = cast_pack(f"pack_up_{l}", ffn_w_up, l, wc_l, 0, True, chip1)
        wc_l = cast_pack(f"pack_mix_{l}", attn_w_qkv if is_attn else sc_w_in, j, wc_l, lay.mix[l], True, chip1)
        wr_l = cast_pack(f"pack_down_{l}", ffn_w_down, l, wr_l, 0, False, chip1)
        wr_l = cast_pack(f"pack_out_{l}", attn_w_o if is_attn else sc_w_out, j, wr_l, lay.out, False, chip1)
        wcs.append(wc_l)
        wrs.append(wr_l)
    (wcs[0],) = gather_weights("gather_w0", [wcs[0]], (1,))

    g1 = gather_flat("gather_cond", [c, sc_conv, ffn_conv])
    c_all = g1[:, :d]
    o1 = d + sc_conv.size
    sc_conv_full = jnp.concatenate([g1[2 * s, d:o1].reshape(sc_conv.shape) for s in range(4)], axis=-1)
    ffn_conv_full = jnp.concatenate([g1[2 * s, o1:o1 + ffn_conv.size].reshape(ffn_conv.shape) for s in range(4)], axis=-1)
    cond = jnp.concatenate([c_all, c_ctx[None, :], jnp.zeros((7, d), F32)], axis=0)
    ada_part = ada_fwd("ada_fwd", cond, w_ada)
    g2 = gather_flat("gather_ada", [ada_part])
    ada_all = jnp.concatenate([g2[2 * s, :ada_part.size].reshape(ada_part.shape) for s in range(4)], axis=-1)
    ada_own = jnp.stack([lax.dynamic_index_in_dim(ada_all, 8, 1, False),
                         lax.dynamic_index_in_dim(ada_all, dev, 1, False)], axis=1) + b_ada[:, None, :]
    mods = jnp.pad(ada_own.reshape(depth, 2, 6, d), ((0, 0), (0, 0), (0, 2), (0, 0)))

    cos, sin = _rope_tables(seq)
    xa = jnp.concatenate([ctx[0], x[0]], axis=0)
    cws = [_pad8(sc_conv_full[j]) for j in range(n_conv)]
    cwf = [_pad8(jnp.concatenate([ffn_conv_full[l], ffn_conv_b[l][None, :]], axis=0)) for l in range(depth)]
    gains = [_pad8(jnp.stack([attn_q_gain[j], attn_k_gain[j]])) for j in range(n_attn)]

    saved = []
    for l in range(depth):
        j, is_attn, mod = l // N_MIX, l % N_MIX == 0, mods[l]
        wc, wr, last = wcs[l], wrs[l], l == depth - 1
        sv = {"x_in": xa}
        h1 = norm_mod(f"norm1_{l}", xa, mod, 0, 1)
        if is_attn:
            if l == 0:
                qkv, (wr,) = mm_nn_col(f"qkv_{l}", h1, wc, lay.mix[l], lay.nb_mix[l], comm=gather_comm([wr], (2,)))
                wrs[0] = wr
            else:
                qkv = mm_nn_col(f"qkv_{l}", h1, wc, lay.mix[l], lay.nb_mix[l])
            qr, kr = rope_fwd(f"rope_{l}", qkv, cos, sin, gains[j], dq, dkv)
            o, lse, lset = attn_fwd(f"attn_{l}", qr, kr, qkv, attn_sink[j], seq)
            xa, y_m = mm_nn_row(f"wo_{l}", o, wr, lay.out, lay.kb_o, xa, mod, 2)
            sv.update(qkv=qkv, qr=qr, kr=kr, o=o, lse=lse, lset=lset)
        else:
            u = mm_nn_col(f"scin_{l}", h1, wc, lay.mix[l], lay.nb_mix[l])
            z = sc_gate_fwd(f"scgate_{l}", u, cws[j])
            xa, y_m = mm_nn_row(f"scout_{l}", z, wr, lay.out, lay.kb_o, xa, mod, 2)
            sv.update(u=u, z=z)
        h2 = norm_mod(f"norm2_{l}", xa, mod, 3, 4)
        if last:
            up = mm_nn_col(f"up_{l}", h2, wc, 0, lay.nb_up)
        else:
            up, (wcs[l + 1],) = mm_nn_col(f"up_{l}", h2, wc, 0, lay.nb_up, comm=gather_comm([wcs[l + 1]], (1,)))
        act = ffn_act_fwd(f"act_{l}", up, cwf[l])
        sv.update(h1=h1, y_m=y_m, x_mid=xa, h2=h2, up=up, act=act)
        if last:
            xa, y_f = mm_nn_row(f"down_{l}", act, wr, 0, lay.kb_dn, xa, mod, 5)
        else:
            (xa, y_f), (wrs[l + 1],) = mm_nn_row(f"down_{l}", act, wr, 0, lay.kb_dn, xa, mod, 5,
                                                 comm=gather_comm([wrs[l + 1]], (2,)))
        sv["y_f"] = y_f
        saved.append(sv)

    dx, dy, acc, lp = loss_head("loss", xa, loss_target[0], saved[-1]["y_f"], mods[-1], 5)
    loss = lax.psum(0.5 * jnp.sum(lp[0]) / d, ("x", "y", "c"))
    d_mod = [jnp.zeros((2, 6, d), F32) for _ in range(depth)]
    place = jnp.stack([chip, ac]).astype(jnp.int32)
    gf = {"up": lax.empty(ffn_w_up.shape, F32), "down": lax.empty(ffn_w_down.shape, F32),
          "qkv": lax.empty(attn_w_qkv.shape, F32), "wo": lax.empty(attn_w_o.shape, F32),
          "scin": lax.empty(sc_w_in.shape, F32), "scout": lax.empty(sc_w_out.shape, F32)}

    def chip_adds(l, hc, hr, rb_c, rb_r):
        j, mix = l // N_MIX, ("qkv", "wo") if l % N_MIX == 0 else ("scin", "scout")
        gf["up"] = chip_add(f"sum_up_{l}", hc, rb_c, place, gf["up"], l, 0, lay.nb_up, True)
        gf[mix[0]] = chip_add(f"sum_mix_{l}", hc, rb_c, place, gf[mix[0]], j, lay.mix[l], lay.nb_mix[l], True)
        gf["down"] = chip_add(f"sum_down_{l}", hr, rb_r, place, gf["down"], l, 0, lay.kb_dn, False)
        gf[mix[1]] = chip_add(f"sum_out_{l}", hr, rb_r, place, gf[mix[1]], j, lay.out, lay.kb_o, False)

    pending = None

    def add_mod(l, acc, idx):
        upd = jnp.zeros((2, 6, d), F32)
        for row, k in idx:
            upd = upd.at[:, k, :].set(acc[:, row, :])
        d_mod[l] = d_mod[l] + upd

    add_mod(depth - 1, acc, [(2, 5)])
    d_conv_f, d_conv_s = [None] * depth, [None] * n_conv
    d_gq, d_gk, d_sink = [None] * n_attn, [None] * n_attn, [None] * n_attn
    for l in reversed(range(depth)):
        j, is_attn, sv = l // N_MIX, l % N_MIX == 0, saved[l]
        wc, wr = wcs[l], wrs[l]
        gc, gr = lax.empty((4, d, lay.ct[l]), BF16), lax.empty((4, lay.rt, d), BF16)
        if pending is None:
            gr = mm_tn_row(f"g_down_{l}", sv["act"], dy, gr, 0, lay.kb_dn)
        else:
            gr, (rb_r,) = mm_tn_row(f"g_down_{l}", sv["act"], dy, gr, 0, lay.kb_dn, comm=scatter_comm([pending[2]]))
        da = mm_nt_row(f"d_act_{l}", dy, wr, 0, lay.kb_dn)
        d_up, d_conv_f[l] = ffn_act_bwd(f"act_bwd_{l}", sv["up"], da, cwf[l])
        if pending is None:
            gc = mm_tn_col(f"g_up_{l}", sv["h2"], d_up, gc, 0, lay.nb_up)
        else:
            gc, (rb_c,) = mm_tn_col(f"g_up_{l}", sv["h2"], d_up, gc, 0, lay.nb_up, comm=scatter_comm([pending[1]]))
            chip_adds(pending[0], pending[1], pending[2], rb_c, rb_r)
        dh2 = mm_nt_col(f"d_h2_{l}", d_up, wc, 0, lay.nb_up)
        dx, dy, acc = resid_bwd(f"norm2_bwd_{l}", dx, dh2, sv["x_mid"], mods[l], 3, 4, sv["y_m"], mods[l], 2)
        add_mod(l, acc, [(0, 3), (1, 4), (2, 2)])
        if is_attn:
            gr = mm_tn_row(f"g_wo_{l}", sv["o"], dy, gr, lay.out, lay.kb_o)
            if l == 0:
                (ra_r,) = pair_exchange("pair_exchange_r_0", [gr], (2,))
                hr0 = pair_add("pair_add_r_0", gr, ra_r, cidx, 2)
            do = mm_nt_row(f"d_o_{l}", dy, wr, lay.out, lay.kb_o)
            dqr, deltat, dkc, dvc, dsk = attn_bwd_q(f"attn_bwd_q_{l}", sv["qr"], sv["kr"], sv["qkv"], attn_sink[j],
                                                    do, sv["o"], sv["lse"], seq)
            dkl, dvl = attn_bwd_kv(f"attn_bwd_kv_{l}", sv["qr"], sv["kr"], sv["qkv"], do, sv["lset"], deltat, seq)
            dqkv, dgn = rope_bwd(f"rope_bwd_{l}", sv["qkv"], dqr, jnp.concatenate([dkc, dkl], axis=0),
                                 jnp.concatenate([dvc, dvl], axis=0), cos, sin, gains[j])
            d_gq[j], d_gk[j], d_sink[j] = dgn[0], dgn[1], dsk[0, :nh]
            if l == 0:
                gc, (rb_r0,) = mm_tn_col(f"g_qkv_{l}", sv["h1"], dqkv, gc, lay.mix[l], lay.nb_mix[l],
                                         comm=scatter_comm([hr0]))
                (ra_c,) = pair_exchange("pair_exchange_c_0", [gc], (1,))
                hc0 = pair_add("pair_add_c_0", gc, ra_c, cidx, 1)
                dh1, (rb_c0,) = mm_nt_col(f"d_h1_{l}", dqkv, wc, lay.mix[l], lay.nb_mix[l], comm=scatter_comm([hc0]))
                chip_adds(0, hc0, hr0, rb_c0, rb_r0)
            else:
                gc = mm_tn_col(f"g_qkv_{l}", sv["h1"], dqkv, gc, lay.mix[l], lay.nb_mix[l])
                dh1 = mm_nt_col(f"d_h1_{l}", dqkv, wc, lay.mix[l], lay.nb_mix[l])
        else:
            gr = mm_tn_row(f"g_scout_{l}", sv["z"], dy, gr, lay.out, lay.kb_o)
            dz = mm_nt_row(f"d_z_{l}", dy, wr, lay.out, lay.kb_o)
            du, dcw = sc_gate_bwd(f"scgate_bwd_{l}", sv["u"], dz, cws[j])
            d_conv_s[j] = dcw[:3]
            gc = mm_tn_col(f"g_scin_{l}", sv["h1"], du, gc, lay.mix[l], lay.nb_mix[l])
            dh1 = mm_nt_col(f"d_h1_{l}", du, wc, lay.mix[l], lay.nb_mix[l])
        if l > 0:
            dx, dy, acc = resid_bwd(f"norm1_bwd_{l}", dx, dh1, sv["x_in"], mods[l], 0, 1, saved[l - 1]["y_f"], mods[l - 1], 5)
            add_mod(l - 1, acc, [(2, 5)])
        else:
            dx, acc = resid_bwd(f"norm1_bwd_{l}", dx, dh1, sv["x_in"], mods[l], 0, 1)
        add_mod(l, acc, [(0, 0), (1, 1)])
        if l > 0:
            ra_c, ra_r = pair_exchange(f"pair_exchange_{l}", [gc, gr], (1, 2))
            pending = (l, pair_add(f"pair_add_c_{l}", gc, ra_c, cidx, 1), pair_add(f"pair_add_r_{l}", gr, ra_r, cidx, 2))
    grad_x = dx[CTX:][None]
    order = ["up", "down", "qkv", "wo", "scin", "scout"]
    joined = pair_join("pair_join", [gf[k] for k in order], (1, 2, 1, 2, 1, 2))
    gf = dict(zip(order, joined))

    d_ada = jnp.stack(d_mod).reshape(depth, 2, 6 * d)
    small = [d_ada, jnp.stack(d_gq), jnp.stack(d_gk), jnp.stack(d_sink), jnp.stack(d_conv_s),
             jnp.stack([t[:3] for t in d_conv_f]), jnp.stack([t[3] for t in d_conv_f])]
    g3 = gather_flat("gather_small", small)
    tot = sum8("sum_small", g3)
    sizes = [s.size for s in small]
    offs = [sum(sizes[:k]) for k in range(len(sizes) + 1)]
    part = lambda k: tot[offs[k]:offs[k + 1]].reshape(small[k].shape)
    g_b_ada = part(0)[:, 0] + part(0)[:, 1]
    g_q_gain, g_k_gain, g_sink = part(1), part(2), part(3)
    g_sc_conv = lax.dynamic_slice_in_dim(part(4), chip * sc_conv.shape[2], sc_conv.shape[2], 2)
    g_ffn_conv = lax.dynamic_slice_in_dim(part(5), chip * ffn_conv.shape[2], ffn_conv.shape[2], 2)
    g_conv_b = part(6)

    d_ada_all = g3[:, :d_ada.size].reshape(8, depth, 2, 6 * d)
    cols = lambda a: lax.dynamic_slice_in_dim(a, chip * nada, nada, a.ndim - 1)
    d_lat = cols(jnp.moveaxis(d_ada_all[:, :, 1], 0, 1))
    d_ctx = cols(part(0)[:, 0])
    rhs = jnp.concatenate([d_lat, d_ctx[:, None], jnp.zeros((depth, 7, nada), F32)], axis=1)
    g_w_ada = ada_grad_w("ada_grad_w", cond, rhs)
    dcc = ada_bwd_cond("ada_bwd_cond", jnp.pad(d_ctx[:, None], ((0, 0), (0, 7), (0, 0))), w_ada)[0]
    g4 = gather_flat("gather_dcc", [dcc])
    d_silu = g4[0, :d] + g4[2, :d] + g4[4, :d] + g4[6, :d]
    sg = _sigmoid(c_ctx)
    g_c_ctx = d_silu * (sg * (1.0 + c_ctx * (1.0 - sg)))

    def adam_rows(k, n):
        return _pick(k, max(8, ADAM_TILE_ELEMS // n), 8)

    def big(name, g, w, m, v):
        _, k, n = w.shape
        tr = adam_rows(k, n)
        return adamw(name, g, pl.BlockSpec((None, tr, n), lambda l, i: (l, i, 0)), w, m, v, tr)

    ada_tr = adam_rows(d, nada)
    res = {
        "c_ctx": adamw_small("adam_c_ctx", g_c_ctx, c_ctx, m_c_ctx, v_c_ctx),
        "w_ada": adamw("adam_w_ada", g_w_ada, pl.BlockSpec((None, ada_tr, nada), lambda l, i: (l, i, 0)), w_ada, m_w_ada, v_w_ada, ada_tr),
        "b_ada": adamw_small("adam_b_ada", g_b_ada, b_ada, m_b_ada, v_b_ada),
        "attn_w_qkv": big("adam_qkv", gf["qkv"], attn_w_qkv, m_attn_w_qkv, v_attn_w_qkv),
        "attn_w_o": big("adam_wo", gf["wo"], attn_w_o, m_attn_w_o, v_attn_w_o),
        "attn_q_gain": adamw_small("adam_q_gain", g_q_gain, attn_q_gain, m_attn_q_gain, v_attn_q_gain),
        "attn_k_gain": adamw_small("adam_k_gain", g_k_gain, attn_k_gain, m_attn_k_gain, v_attn_k_gain),
        "attn_sink": adamw_small("adam_sink", g_sink, attn_sink, m_attn_sink, v_attn_sink),
        "sc_w_in": big("adam_scin", gf["scin"], sc_w_in, m_sc_w_in, v_sc_w_in),
        "sc_conv": adamw_small("adam_sc_conv", g_sc_conv, sc_conv, m_sc_conv, v_sc_conv),
        "sc_w_out": big("adam_scout", gf["scout"], sc_w_out, m_sc_w_out, v_sc_w_out),
        "ffn_w_up": big("adam_up", gf["up"], ffn_w_up, m_ffn_w_up, v_ffn_w_up),
        "ffn_conv": adamw_small("adam_ffn_conv", g_ffn_conv, ffn_conv, m_ffn_conv, v_ffn_conv),
        "ffn_conv_b": adamw_small("adam_conv_b", g_conv_b, ffn_conv_b, m_ffn_conv_b, v_ffn_conv_b),
        "ffn_w_down": big("adam_down", gf["down"], ffn_w_down, m_ffn_w_down, v_ffn_w_down),
    }
    names = list(res)
    return (loss, grad_x, *[res[n][0] for n in names], *[res[n][1] for n in names],
            *[res[n][2] for n in names], *[res[n][3] for n in names])
```

```python
import functools

import jax
import jax.numpy as jnp
from jax import lax
from jax.experimental import pallas as pl
from jax.experimental.pallas import tpu as pltpu

F32, BF16 = jnp.float32, jnp.bfloat16
MESH = pl.DeviceIdType.MESH
VMEM_LIMIT = 56 * 1024 * 1024
LANE = 128
MXU = 256
WIDE = 2816
HALO = 16
HEAD = 128
GROUP = 4
CTX = 256
BLK = 128
WINDOW = 128
RB = 256
GRID_W = 64
ROPE_BASE = 10000.0
EPS = 1e-6
NEG = -1e30
N_MIX = 2
LR, B1, B2, ADAM_EPS, WD, STEP = 0.001, 0.9, 0.999, 1e-08, 0.01, 10
ADAM_TILE_ELEMS = 400 * 1024
NT = (((1,), (1,)), ((), ()))
TN = (((0,), (0,)), ((), ()))


def _pick(dim, target, mult=LANE):
    best = None
    for t in range(mult, min(dim, target) + 1, mult):
        if dim % t == 0:
            best = t
    return dim if best is None else best


def _cdiv(a, b):
    return -(-a // b)


def _params(sem):
    return pltpu.CompilerParams(dimension_semantics=sem, vmem_limit_bytes=VMEM_LIMIT)


def _sigmoid(g):
    return 0.5 * jnp.tanh(0.5 * g) + 0.5


def _row(v, r):
    rows = lax.broadcasted_iota(jnp.int32, v.shape, 0)
    return jnp.sum(jnp.where(rows == r, v, 0.0), axis=0, keepdims=True)


def _get_col(v, c):
    lanes = lax.broadcasted_iota(jnp.int32, v.shape, 1)
    return jnp.sum(jnp.where(lanes == c, v, 0.0), axis=1, keepdims=True)


def _put_col(v, c, col):
    lanes = lax.broadcasted_iota(jnp.int32, v.shape, 1)
    return jnp.where(lanes == c, col, v)


def _rows3(s0, s1, s2, width):
    rows = lax.broadcasted_iota(jnp.int32, (8, width), 0)
    z = jnp.zeros((8, width), F32)
    return jnp.where(rows == 0, s0, jnp.where(rows == 1, s1, jnp.where(rows == 2, s2, z)))


def _shift_rows(w, prev_row, next_row):
    n = w.shape[0]
    rows = lax.broadcasted_iota(jnp.int32, (n, 1), 0)
    down = jnp.where(rows == 0, prev_row, pltpu.roll(w, 1, 0))
    up = jnp.where(rows == n - 1, next_row, pltpu.roll(w, n - 1, 0))
    return down, up


def _seg_flags(i, nt):
    return i <= 1, (i == 0) | (i == nt - 1)


def _halo_specs(width, nrows):
    r = RB // HALO
    nh = nrows // HALO
    prev = pl.BlockSpec((HALO, width), lambda i: (jnp.maximum(i * r - 1, 0), 0))
    nxt = pl.BlockSpec((HALO, width), lambda i: (jnp.minimum((i + 1) * r, nh - 1), 0))
    return prev, nxt


class Layout:
    def __init__(self, d, ff, nqkv, depth):
        self.nb_up, self.kb_dn, self.kb_o = 2 * ff // 4, ff // 4, d // 4
        self.nb_mix = [(nqkv if l % N_MIX == 0 else 3 * d) // 4 for l in range(depth)]
        self.mix = [_cdiv(self.nb_up, nb) * nb for nb in self.nb_mix]
        self.ct = [m + nb for m, nb in zip(self.mix, self.nb_mix)]
        self.out = _cdiv(self.kb_dn, self.kb_o) * self.kb_o
        self.rt = _cdiv(self.out + self.kb_o, 2 * HALO) * 2 * HALO


def _mm_call(name, grid, in_specs, out_specs, out_shape, contract, operands, acc_shape, epilogue,
             n_extra=0, aliases=None, comm=None, merge_b=False):
    nk = grid[2]
    n_out = len(out_shape)
    n_cin = len(comm["ins"]) if comm else 0
    n_cout = len(comm["out_shape"]) if comm else 0
    aliases = dict(aliases or {})
    in_specs, out_specs, out_shape, operands = list(in_specs), list(out_specs), list(out_shape), list(operands)
    scratch = [] if nk == 1 else [pltpu.VMEM(acc_shape, F32)]
    if comm:
        for i_in, i_out in comm["aliases"].items():
            aliases[len(operands) + i_in] = n_out + i_out
        in_specs += [HBM_SPEC] * n_cin
        out_specs += [HBM_SPEC] * n_cout
        out_shape += comm["out_shape"]
        operands += comm["ins"]
        scratch += [pltpu.SemaphoreType.DMA((comm["n_sems"],)), pltpu.SemaphoreType.DMA((comm["n_sems"],))]

    def body(*refs):
        a_ref, b_ref = refs[0], refs[1]
        extra = refs[2:2 + n_extra]
        p = 2 + n_extra
        cin, outs = refs[p:p + n_cin], refs[p + n_cin:p + n_cin + n_out]
        couts = refs[p + n_cin + n_out:p + n_cin + n_out + n_cout]
        scr = refs[p + n_cin + n_out + n_cout:]
        ids = (pl.program_id(0), pl.program_id(1))
        k = pl.program_id(2)
        if comm:
            start, finish = comm["ops"](cin, couts, scr[-2], scr[-1])

            @pl.when((ids[0] == 0) & (ids[1] == 0) & (k == 0))
            def _():
                start()

        def part():
            b = b_ref[...]
            if merge_b:
                b = b.reshape(b.shape[0] * b.shape[1], b.shape[2])
            return lax.dot_general(a_ref[...], b, contract, preferred_element_type=F32)

        if nk == 1:
            epilogue(part(), extra, outs, ids)
        else:
            acc = scr[0]

            @pl.when(k == 0)
            def _():
                acc[...] = jnp.zeros_like(acc)

            acc[...] += part()

            @pl.when(k == nk - 1)
            def _():
                epilogue(acc[...], extra, outs, ids)

        if comm:
            @pl.when((ids[0] == grid[0] - 1) & (ids[1] == grid[1] - 1) & (k == nk - 1))
            def _():
                finish()

    sem = ("arbitrary",) * 3 if comm else ("parallel", "parallel", "arbitrary")
    res = pl.pallas_call(
        body, name=name, grid=grid, in_specs=in_specs, out_specs=out_specs, out_shape=out_shape,
        scratch_shapes=scratch, input_output_aliases=aliases, compiler_params=_params(sem))(*operands)
    return (res[:n_out], res[n_out:]) if comm else res


def _store(dtype):
    def epilogue(r, extra, outs, ids):
        outs[0][...] = r.astype(dtype)
    return epilogue


def _ret(res, comm, single=True):
    if comm:
        return (res[0][0] if single else res[0]), res[1]
    return res[0] if single else res


def _slots_per_step(s, kb):
    return max(n for n in (1, 2, 4) if s % n == 0 and (n == 1 or n * kb <= WIDE))


def mm_nn_col(name, a, wc, off, nb, comm=None):
    m, d = a.shape
    s = wc.shape[0]
    tn = _pick(nb, WIDE, MXU)
    tm = _pick(m, 528 if tn > 1536 else 1056, HALO)
    npb, ob = nb // tn, off // tn
    assert off % tn == 0
    return _ret(_mm_call(
        name, (s * npb, m // tm, 1),
        [pl.BlockSpec((tm, d), lambda j, i, k: (i, 0)),
         pl.BlockSpec((None, d, tn), lambda j, i, k: (j // npb, 0, ob + j % npb))],
        [pl.BlockSpec((tm, tn), lambda j, i, k: (i, j))],
        [jax.ShapeDtypeStruct((m, s * nb), BF16)],
        (((1,), (0,)), ((), ())), (a, wc), None, _store(BF16), comm=comm), comm)


def mm_nn_row(name, a, wr, off, kb, res, mod, gi, comm=None):
    m = a.shape[0]
    s, _, d = wr.shape
    sp = _slots_per_step(s, kb)
    tm, tn = _pick(m, 1056, HALO), _pick(d, 1024)
    ob = off // kb
    assert off % kb == 0

    def epilogue(r, extra, outs, ids):
        res_ref, mod_ref = extra
        rows = ids[0] * tm + lax.broadcasted_iota(jnp.int32, (tm, 1), 0)
        g = jnp.where(rows < CTX, mod_ref[0, gi:gi + 1, :], mod_ref[1, gi:gi + 1, :])
        outs[0][...] = res_ref[...] + g * r
        outs[1][...] = r.astype(BF16)

    return _ret(_mm_call(
        name, (m // tm, d // tn, s // sp),
        [pl.BlockSpec((tm, sp * kb), lambda i, j, k: (i, k)),
         pl.BlockSpec((sp, kb, tn), lambda i, j, k: (k, ob, j)),
         pl.BlockSpec((tm, tn), lambda i, j, k: (i, j)),
         pl.BlockSpec((2, 8, tn), lambda i, j, k: (0, 0, j))],
        [pl.BlockSpec((tm, tn), lambda i, j, k: (i, j)), pl.BlockSpec((tm, tn), lambda i, j, k: (i, j))],
        [jax.ShapeDtypeStruct((m, d), F32), jax.ShapeDtypeStruct((m, d), BF16)],
        (((1,), (0,)), ((), ())), (a, wr, res, mod), (tm, tn), epilogue, n_extra=2, comm=comm, merge_b=True),
        comm, single=False)


def mm_nt_col(name, dy, wc, off, nb, comm=None):
    m = dy.shape[0]
    s, d, _ = wc.shape
    tc = _pick(nb, WIDE, MXU)
    tm = _pick(m, 528 if tc > 1536 else 768, HALO)
    npb, ob = nb // tc, off // tc
    return _ret(_mm_call(
        name, (m // tm, 1, s * npb),
        [pl.BlockSpec((tm, tc), lambda i, j, k: (i, k)),
         pl.BlockSpec((None, d, tc), lambda i, j, k: (k // npb, 0, ob + k % npb))],
        [pl.BlockSpec((tm, d), lambda i, j, k: (i, 0))],
        [jax.ShapeDtypeStruct((m, d), F32)],
        NT, (dy, wc), (tm, d), _store(F32), comm=comm), comm)


def mm_nt_row(name, dy, wr, off, kb):
    m, d = dy.shape
    s = wr.shape[0]
    sp = _slots_per_step(s, kb)
    tm = _pick(m, 528 if sp * kb > 2048 else 1056, HALO)
    ob = off // kb
    return _mm_call(
        name, (s // sp, m // tm, 1),
        [pl.BlockSpec((tm, d), lambda j, i, k: (i, 0)),
         pl.BlockSpec((sp, kb, d), lambda j, i, k: (j, ob, 0))],
        [pl.BlockSpec((tm, sp * kb), lambda j, i, k: (i, j))],
        [jax.ShapeDtypeStruct((m, s * kb), BF16)],
        NT, (dy, wr), None, _store(BF16), merge_b=True)[0]


def mm_tn_col(name, a, dy, gbuf, off, nb, comm=None):
    t, d = a.shape
    s = gbuf.shape[0]
    tn, tt = _pick(nb, WIDE, MXU), _pick(t, 768, MXU)
    tka = _pick(d, 1024 if tn > 1536 else 2048)
    npb, ob = nb // tn, off // tn
    return _ret(_mm_call(
        name, (d // tka, s * npb, t // tt),
        [pl.BlockSpec((tt, tka), lambda i, j, k: (k, i)),
         pl.BlockSpec((tt, tn), lambda i, j, k: (k, j)),
         pl.BlockSpec(memory_space=pl.ANY)],
        [pl.BlockSpec((None, tka, tn), lambda i, j, k: (j // npb, i, ob + j % npb))],
        [jax.ShapeDtypeStruct(gbuf.shape, BF16)],
        TN, (a, dy, gbuf), (tka, tn), _store(BF16), n_extra=1, aliases={2: 0}, comm=comm), comm)


def mm_tn_row(name, act, dy, gbuf, off, kb, comm=None):
    t, d = dy.shape
    s = gbuf.shape[0]
    tt = _pick(t, 768, MXU)
    ob = off // kb
    return _ret(_mm_call(
        name, (s, 1, t // tt),
        [pl.BlockSpec((tt, kb), lambda i, j, k: (k, i)),
         pl.BlockSpec((tt, d), lambda i, j, k: (k, 0)),
         pl.BlockSpec(memory_space=pl.ANY)],
        [pl.BlockSpec((None, kb, d), lambda i, j, k: (i, ob, 0))],
        [jax.ShapeDtypeStruct(gbuf.shape, BF16)],
        TN, (act, dy, gbuf), (kb, d), _store(BF16), n_extra=1, aliases={2: 0}, comm=comm), comm)


def cast_pack(name, w, l, buf, off, col, chip):
    _, k, n = w.shape
    if col:
        tr = _pick(k, 512, HALO)
        ob = off // n
        out_spec = pl.BlockSpec((None, tr, n), lambda i, s: (s[0], i, ob))
    else:
        tr = _pick(k, 704, HALO)
        ob = off // tr
        out_spec = pl.BlockSpec((None, tr, n), lambda i, s: (s[0], ob + i, 0))

    def body(s_ref, w_ref, buf_ref, out_ref):
        out_ref[...] = w_ref[...].astype(BF16)

    return pl.pallas_call(
        body, name=name,
        grid_spec=pltpu.PrefetchScalarGridSpec(
            num_scalar_prefetch=1, grid=(k // tr,),
            in_specs=[pl.BlockSpec((None, tr, n), lambda i, s: (l, i, 0)), pl.BlockSpec(memory_space=pl.ANY)],
            out_specs=out_spec),
        out_shape=jax.ShapeDtypeStruct(buf.shape, BF16),
        input_output_aliases={2: 0}, compiler_params=_params(("parallel",)))(chip, w, buf)


def norm_mod(name, x, mod, sh, sc):
    t, d = x.shape

    def body(x_ref, mod_ref, h_ref):
        seg = jnp.minimum(pl.program_id(0), 1)
        xv = x_ref[...]
        r = lax.rsqrt(jnp.mean(xv * xv, axis=-1, keepdims=True) + EPS)
        m = mod_ref[seg]
        h_ref[...] = ((xv * r) * (1.0 + m[sc:sc + 1, :]) + m[sh:sh + 1, :]).astype(BF16)

    return pl.pallas_call(
        body, name=name, grid=(t // RB,),
        in_specs=[pl.BlockSpec((RB, d), lambda i: (i, 0)), pl.BlockSpec((2, 8, d), lambda i: (0, 0, 0))],
        out_specs=pl.BlockSpec((RB, d), lambda i: (i, 0)),
        out_shape=jax.ShapeDtypeStruct((t, d), BF16), compiler_params=_params(("parallel",)))(x, mod)


def sc_gate_fwd(name, u, cw):
    t = u.shape[0]
    d = u.shape[1] // 3
    nt, tc = t // RB, _pick(d, 512)
    prev, nxt = _halo_specs(3 * d, t)

    def body(u_ref, up_ref, un_ref, cw_ref, z_ref):
        first, last = _seg_flags(pl.program_id(0), nt)
        for j in range(d // tc):
            c0 = j * tc
            gb = u_ref[:, c0:c0 + tc].astype(F32)
            w = u_ref[:, d + c0:d + c0 + tc].astype(F32) * u_ref[:, 2 * d + c0:2 * d + c0 + tc].astype(F32)
            pw = _row(up_ref[:, d + c0:d + c0 + tc].astype(F32) * up_ref[:, 2 * d + c0:2 * d + c0 + tc].astype(F32), HALO - 1)
            nw = _row(un_ref[:, d + c0:d + c0 + tc].astype(F32) * un_ref[:, 2 * d + c0:2 * d + c0 + tc].astype(F32), 0)
            wd, wu = _shift_rows(w, jnp.where(first, 0.0, pw), jnp.where(last, 0.0, nw))
            cwj = cw_ref[:, c0:c0 + tc]
            conv = wd * cwj[0:1] + w * cwj[1:2] + wu * cwj[2:3]
            z_ref[:, c0:c0 + tc] = (gb * conv).astype(BF16)

    return pl.pallas_call(
        body, name=name, grid=(nt,),
        in_specs=[pl.BlockSpec((RB, 3 * d), lambda i: (i, 0)), prev, nxt, pl.BlockSpec((8, d), lambda i: (0, 0))],
        out_specs=pl.BlockSpec((RB, d), lambda i: (i, 0)),
        out_shape=jax.ShapeDtypeStruct((t, d), BF16), compiler_params=_params(("parallel",)))(u, u, u, cw)


def sc_gate_bwd(name, u, dz, cw):
    t = u.shape[0]
    d = u.shape[1] // 3
    nt, tc = t // RB, _pick(d, 512)
    prev, nxt = _halo_specs(3 * d, t)
    dprev, dnxt = _halo_specs(d, t)

    def body(u_ref, up_ref, un_ref, dz_ref, dzp_ref, dzn_ref, cw_ref, du_ref, dcw_ref):
        i = pl.program_id(0)
        first, last = _seg_flags(i, nt)

        @pl.when(i == 0)
        def _():
            dcw_ref[...] = jnp.zeros_like(dcw_ref)

        for j in range(d // tc):
            c0 = j * tc
            sl0, sl1, sl2 = slice(c0, c0 + tc), slice(d + c0, d + c0 + tc), slice(2 * d + c0, 2 * d + c0 + tc)
            gb, gc, v = u_ref[:, sl0].astype(F32), u_ref[:, sl1].astype(F32), u_ref[:, sl2].astype(F32)
            w = gc * v
            pw = _row(up_ref[:, sl1].astype(F32) * up_ref[:, sl2].astype(F32), HALO - 1)
            nw = _row(un_ref[:, sl1].astype(F32) * un_ref[:, sl2].astype(F32), 0)
            wd, wu = _shift_rows(w, jnp.where(first, 0.0, pw), jnp.where(last, 0.0, nw))
            cwj = cw_ref[:, sl0]
            cw0, cw1, cw2 = cwj[0:1], cwj[1:2], cwj[2:3]
            dzv = dz_ref[:, sl0].astype(F32)
            e = dzv * gb
            pe = _row(dzp_ref[:, sl0].astype(F32) * up_ref[:, sl0].astype(F32), HALO - 1)
            ne = _row(dzn_ref[:, sl0].astype(F32) * un_ref[:, sl0].astype(F32), 0)
            ed, eu = _shift_rows(e, jnp.where(first, 0.0, pe), jnp.where(last, 0.0, ne))
            dw = cw0 * eu + cw1 * e + cw2 * ed
            du_ref[:, sl0] = (dzv * (wd * cw0 + w * cw1 + wu * cw2)).astype(BF16)
            du_ref[:, sl1] = (dw * v).astype(BF16)
            du_ref[:, sl2] = (dw * gc).astype(BF16)
            dcw_ref[:, sl0] += _rows3(jnp.sum(e * wd, axis=0, keepdims=True), jnp.sum(e * w, axis=0, keepdims=True),
                                      jnp.sum(e * wu, axis=0, keepdims=True), tc)

    return pl.pallas_call(
        body, name=name, grid=(nt,),
        in_specs=[pl.BlockSpec((RB, 3 * d), lambda i: (i, 0)), prev, nxt,
                  pl.BlockSpec((RB, d), lambda i: (i, 0)), dprev, dnxt, pl.BlockSpec((8, d), lambda i: (0, 0))],
        out_specs=[pl.BlockSpec((RB, 3 * d), lambda i: (i, 0)), pl.BlockSpec((8, d), lambda i: (0, 0))],
        out_shape=[jax.ShapeDtypeStruct((t, 3 * d), BF16), jax.ShapeDtypeStruct((8, d), F32)],
        compiler_params=_params(("arbitrary",)))(u, u, u, dz, dz, dz, cw)


def ffn_act_fwd(name, up, cw):
    t = up.shape[0]
    ff = up.shape[1] // 2
    nt, tc = t // RB, _pick(ff, 1408)
    prev, nxt = _halo_specs(2 * ff, t)

    def body(up_ref, upp_ref, upn_ref, cw_ref, a_ref):
        first, last = _seg_flags(pl.program_id(0), nt)
        for j in range(ff // tc):
            sg, sv = slice(j * tc, (j + 1) * tc), slice(ff + j * tc, ff + (j + 1) * tc)
            gate = up_ref[:, sg].astype(F32)
            pg = _row(upp_ref[:, sg].astype(F32), HALO - 1)
            ng = _row(upn_ref[:, sg].astype(F32), 0)
            gd, gu = _shift_rows(gate, jnp.where(first, 0.0, pg), jnp.where(last, 0.0, ng))
            cwj = cw_ref[:, sg]
            g = gd * cwj[0:1] + gate * cwj[1:2] + gu * cwj[2:3] + cwj[3:4]
            a_ref[:, sg] = (g * _sigmoid(g) * up_ref[:, sv].astype(F32)).astype(BF16)

    return pl.pallas_call(
        body, name=name, grid=(nt,),
        in_specs=[pl.BlockSpec((RB, 2 * ff), lambda i: (i, 0)), prev, nxt, pl.BlockSpec((8, ff), lambda i: (0, 0))],
        out_specs=pl.BlockSpec((RB, ff), lambda i: (i, 0)),
        out_shape=jax.ShapeDtypeStruct((t, ff), BF16), compiler_params=_params(("parallel",)))(up, up, up, cw)


def ffn_act_bwd(name, up, da, cw):
    t = up.shape[0]
    ff = up.shape[1] // 2
    nt, tc = t // RB, _pick(ff, 1408)
    prev, nxt = _halo_specs(2 * ff, t)
    dprev, dnxt = _halo_specs(ff, t)

    def dsilu(g):
        s = _sigmoid(g)
        return s * (1.0 + g * (1.0 - s))

    def body(up_ref, upp_ref, upn_ref, da_ref, dap_ref, dan_ref, cw_ref, dup_ref, acc_ref):
        i = pl.program_id(0)
        first, last = _seg_flags(i, nt)

        @pl.when(i == 0)
        def _():
            acc_ref[...] = jnp.zeros_like(acc_ref)

        for j in range(ff // tc):
            sg, sv = slice(j * tc, (j + 1) * tc), slice(ff + j * tc, ff + (j + 1) * tc)
            gate, val, dav = up_ref[:, sg].astype(F32), up_ref[:, sv].astype(F32), da_ref[:, sg].astype(F32)
            pgt, ngt = upp_ref[:, sg].astype(F32), upn_ref[:, sg].astype(F32)
            pg1, pg2 = _row(pgt, HALO - 1), _row(pgt, HALO - 2)
            ng1, ng2 = _row(ngt, 0), _row(ngt, 1)
            cwj = cw_ref[:, sg]
            cw0, cw1, cw2, b = cwj[0:1], cwj[1:2], cwj[2:3], cwj[3:4]
            gd, gu = _shift_rows(gate, jnp.where(first, 0.0, pg1), jnp.where(last, 0.0, ng1))
            g = gd * cw0 + gate * cw1 + gu * cw2 + b
            g_p = pg2 * cw0 + pg1 * cw1 + _row(gate, 0) * cw2 + b
            g_n = _row(gate, RB - 1) * cw0 + ng1 * cw1 + ng2 * cw2 + b
            dg = dav * val * dsilu(g)
            dg_p = _row(dap_ref[:, sg].astype(F32) * upp_ref[:, sv].astype(F32), HALO - 1) * dsilu(g_p)
            dg_n = _row(dan_ref[:, sg].astype(F32) * upn_ref[:, sv].astype(F32), 0) * dsilu(g_n)
            dgd, dgu = _shift_rows(dg, jnp.where(first, 0.0, dg_p), jnp.where(last, 0.0, dg_n))
            dup_ref[:, sg] = (cw0 * dgu + cw1 * dg + cw2 * dgd).astype(BF16)
            dup_ref[:, sv] = (dav * g * _sigmoid(g)).astype(BF16)
            rows = lax.broadcasted_iota(jnp.int32, (8, tc), 0)
            acc_ref[:, sg] += (_rows3(jnp.sum(dg * gd, axis=0, keepdims=True), jnp.sum(dg * gate, axis=0, keepdims=True),
                                      jnp.sum(dg * gu, axis=0, keepdims=True), tc)
                               + jnp.where(rows == 3, jnp.sum(dg, axis=0, keepdims=True), 0.0))

    return pl.pallas_call(
        body, name=name, grid=(nt,),
        in_specs=[pl.BlockSpec((RB, 2 * ff), lambda i: (i, 0)), prev, nxt,
                  pl.BlockSpec((RB, ff), lambda i: (i, 0)), dprev, dnxt, pl.BlockSpec((8, ff), lambda i: (0, 0))],
        out_specs=[pl.BlockSpec((RB, 2 * ff), lambda i: (i, 0)), pl.BlockSpec((8, ff), lambda i: (0, 0))],
        out_shape=[jax.ShapeDtypeStruct((t, 2 * ff), BF16), jax.ShapeDtypeStruct((8, ff), F32)],
        compiler_params=_params(("arbitrary",)))(up, up, up, da, da, da, cw)


def _rot(z):
    w = z.shape[1]
    lane = lax.broadcasted_iota(jnp.int32, z.shape, 1)
    return jnp.where((lane % 64) < 32, -pltpu.roll(z, w - 32, 1), pltpu.roll(z, 32, 1))


def rope_fwd(name, qkv, cos, sin, gains, dq, dkv):
    t, nqkv = qkv.shape
    nh, nkv = dq // HEAD, dkv // HEAD

    def body(qkv_ref, cos_ref, sin_ref, g_ref, qr_ref, kr_ref):
        cs, sn = cos_ref[...], sin_ref[...]
        for hd in range(nh + nkv):
            c0 = hd * HEAD
            xh = qkv_ref[:, c0:c0 + HEAD].astype(F32)
            r = lax.rsqrt(jnp.mean(xh * xh, axis=-1, keepdims=True) + EPS)
            y = xh * r * (g_ref[0:1, :] if hd < nh else g_ref[1:2, :])
            yr = (y * cs + _rot(y) * sn).astype(BF16)
            if hd < nh:
                qr_ref[:, c0:c0 + HEAD] = yr
            else:
                kr_ref[:, c0 - dq:c0 - dq + HEAD] = yr

    return pl.pallas_call(
        body, name=name, grid=(t // RB,),
        in_specs=[pl.BlockSpec((RB, nqkv), lambda i: (i, 0)), pl.BlockSpec((RB, HEAD), lambda i: (i, 0)),
                  pl.BlockSpec((RB, HEAD), lambda i: (i, 0)), pl.BlockSpec((8, HEAD), lambda i: (0, 0))],
        out_specs=[pl.BlockSpec((RB, dq), lambda i: (i, 0)), pl.BlockSpec((RB, dkv), lambda i: (i, 0))],
        out_shape=[jax.ShapeDtypeStruct((t, dq), BF16), jax.ShapeDtypeStruct((t, dkv), BF16)],
        compiler_params=_params(("parallel",)))(qkv, cos, sin, gains)


def rope_bwd(name, qkv, dqr, dkr, dv, cos, sin, gains):
    t, nqkv = qkv.shape
    dq, dkv = dqr.shape[1], dkr.shape[1]
    nh, nkv = dq // HEAD, dkv // HEAD

    def body(qkv_ref, dq_ref, dk_ref, dv_ref, cos_ref, sin_ref, g_ref, out_ref, dg_ref):
        @pl.when(pl.program_id(0) == 0)
        def _():
            dg_ref[...] = jnp.zeros_like(dg_ref)

        cs, sn = cos_ref[...], sin_ref[...]
        zero = jnp.zeros((1, HEAD), F32)
        gq, gk = zero, zero
        for hd in range(nh + nkv):
            c0 = hd * HEAD
            xh = qkv_ref[:, c0:c0 + HEAD].astype(F32)
            r = lax.rsqrt(jnp.mean(xh * xh, axis=-1, keepdims=True) + EPS)
            xhat = xh * r
            dy = dq_ref[:, c0:c0 + HEAD] if hd < nh else dk_ref[:, c0 - dq:c0 - dq + HEAD]
            tt = dy * cs - _rot(dy * sn)
            gsum = jnp.sum(tt * xhat, axis=0, keepdims=True)
            if hd < nh:
                gq = gq + gsum
            else:
                gk = gk + gsum
            dxh = tt * (g_ref[0:1, :] if hd < nh else g_ref[1:2, :])
            dx = r * (dxh - xhat * jnp.mean(dxh * xhat, axis=-1, keepdims=True))
            out_ref[:, c0:c0 + HEAD] = dx.astype(BF16)
        out_ref[:, dq + dkv:] = dv_ref[...].astype(BF16)
        dg_ref[...] += _rows3(gq, gk, zero, HEAD)

    return pl.pallas_call(
        body, name=name, grid=(t // RB,),
        in_specs=[pl.BlockSpec((RB, nqkv), lambda i: (i, 0)), pl.BlockSpec((RB, dq), lambda i: (i, 0)),
                  pl.BlockSpec((RB, dkv), lambda i: (i, 0)), pl.BlockSpec((RB, dkv), lambda i: (i, 0)),
                  pl.BlockSpec((RB, HEAD), lambda i: (i, 0)), pl.BlockSpec((RB, HEAD), lambda i: (i, 0)),
                  pl.BlockSpec((8, HEAD), lambda i: (0, 0))],
        out_specs=[pl.BlockSpec((RB, nqkv), lambda i: (i, 0)), pl.BlockSpec((8, HEAD), lambda i: (0, 0))],
        out_shape=[jax.ShapeDtypeStruct((t, nqkv), BF16), jax.ShapeDtypeStruct((8, HEAD), F32)],
        compiler_params=_params(("arbitrary",)))(qkv, dqr, dkr, dv, cos, sin, gains)


def resid_bwd(name, dx, dh, x, mod_n, sh, sc, y_prev=None, mod_g=None, gi=0):
    t, d = x.shape
    has_prev = y_prev is not None

    def body(*refs):
        if has_prev:
            dx_ref, dh_ref, x_ref, mn_ref, y_ref, mg_ref, dxo_ref, dy_ref, acc_ref = refs
        else:
            dx_ref, dh_ref, x_ref, mn_ref, dxo_ref, acc_ref = refs
        i = pl.program_id(0)
        seg = jnp.minimum(i, 1)

        @pl.when(i == 0)
        def _():
            acc_ref[...] = jnp.zeros_like(acc_ref)

        xv, dhv = x_ref[...], dh_ref[...]
        r = lax.rsqrt(jnp.mean(xv * xv, axis=-1, keepdims=True) + EPS)
        xhat = xv * r
        m = mn_ref[seg]
        dxh = dhv * (1.0 + m[sc:sc + 1, :])
        dxo = dx_ref[...] + r * (dxh - xhat * jnp.mean(dxh * xhat, axis=-1, keepdims=True))
        dxo_ref[...] = dxo
        s2 = jnp.zeros((1, d), F32)
        if has_prev:
            dy_ref[...] = (mg_ref[seg][gi:gi + 1, :] * dxo).astype(BF16)
            s2 = jnp.sum(dxo * y_ref[...].astype(F32), axis=0, keepdims=True)
        acc_ref[seg] = acc_ref[seg] + _rows3(jnp.sum(dhv, axis=0, keepdims=True),
                                             jnp.sum(dhv * xhat, axis=0, keepdims=True), s2, d)

    row = pl.BlockSpec((RB, d), lambda i: (i, 0))
    modspec = pl.BlockSpec((2, 8, d), lambda i: (0, 0, 0))
    in_specs, operands = [row, row, row, modspec], [dx, dh, x, mod_n]
    out_specs, out_shape = [row], [jax.ShapeDtypeStruct((t, d), F32)]
    if has_prev:
        in_specs += [row, modspec]
        operands += [y_prev, mod_g]
        out_specs.append(row)
        out_shape.append(jax.ShapeDtypeStruct((t, d), BF16))
    out_specs.append(modspec)
    out_shape.append(jax.ShapeDtypeStruct((2, 8, d), F32))
    return pl.pallas_call(body, name=name, grid=(t // RB,), in_specs=in_specs, out_specs=out_specs,
                          out_shape=out_shape, compiler_params=_params(("arbitrary",)))(*operands)


def loss_head(name, xf, target, y_last, mod, gi):
    t, d = xf.shape

    def body(x_ref, t_ref, y_ref, mod_ref, dx_ref, dy_ref, acc_ref, lp_ref):
        i = pl.program_id(0)
        seg = jnp.minimum(i, 1)

        @pl.when(i == 0)
        def _():
            acc_ref[...] = jnp.zeros_like(acc_ref)
            lp_ref[...] = jnp.zeros_like(lp_ref)

        lat = i >= 1
        err = jnp.where(lat, x_ref[...] - t_ref[...], 0.0)
        dxv = err / d
        dx_ref[...] = dxv
        dy_ref[...] = (mod_ref[seg][gi:gi + 1, :] * dxv).astype(BF16)
        zero = jnp.zeros((1, d), F32)
        lp_ref[...] += _rows3(jnp.sum(err * err, axis=0, keepdims=True), zero, zero, d)
        acc_ref[seg] = acc_ref[seg] + _rows3(zero, zero, jnp.sum(dxv * y_ref[...].astype(F32), axis=0, keepdims=True), d)

    row = pl.BlockSpec((RB, d), lambda i: (i, 0))
    modspec = pl.BlockSpec((2, 8, d), lambda i: (0, 0, 0))
    return pl.pallas_call(
        body, name=name, grid=(t // RB,),
        in_specs=[row, pl.BlockSpec((RB, d), lambda i: (jnp.maximum(i - 1, 0), 0)), row, modspec],
        out_specs=[row, row, modspec, pl.BlockSpec((8, d), lambda i: (0, 0))],
        out_shape=[jax.ShapeDtypeStruct((t, d), F32), jax.ShapeDtypeStruct((t, d), BF16),
                   jax.ShapeDtypeStruct((2, 8, d), F32), jax.ShapeDtypeStruct((8, d), F32)],
        compiler_params=_params(("arbitrary",)))(xf, target, y_last, mod)


def _kv_specs(width, colblk, nbk):
    return [pl.BlockSpec((CTX, width), lambda i: (0, colblk)),
            pl.BlockSpec((BLK, width), lambda i: (jnp.maximum(i - 1, 0), colblk)),
            pl.BlockSpec((BLK, width), lambda i: (i, colblk)),
            pl.BlockSpec((BLK, width), lambda i: (jnp.minimum(i + 1, nbk - 1), colblk))]


def _band_mask(i, seq):
    nk = CTX + 3 * BLK
    qrow = lax.broadcasted_iota(jnp.int32, (GROUP * BLK, nk), 0) % BLK
    col = lax.broadcasted_iota(jnp.int32, (GROUP * BLK, nk), 1)
    cb = col - CTX
    kpos = (i - 3) * BLK + cb
    band = (i >= 2) & (jnp.abs(BLK + qrow - cb) <= WINDOW) & (kpos >= 0) & (kpos < seq)
    return (col < CTX) | band


def _stack_heads(ref, h):
    return jnp.concatenate([ref[:, (h * GROUP + g) * HEAD:(h * GROUP + g + 1) * HEAD] for g in range(GROUP)], axis=0)


def _stack_cols(v, h):
    return jnp.concatenate([_get_col(v, h * GROUP + g) for g in range(GROUP)], axis=0)


def _sink_col(sink_ref, h):
    rowg = lax.broadcasted_iota(jnp.int32, (GROUP * BLK, 1), 0) // BLK
    sk = jnp.full((GROUP * BLK, 1), sink_ref[h * GROUP], F32)
    for g in range(1, GROUP):
        sk = jnp.where(rowg == g, sink_ref[h * GROUP + g], sk)
    return sk


def attn_fwd(name, qr, kr, qkv, sink, seq):
    t, dq = qr.shape
    dkv = kr.shape[1]
    nbk, nkv = t // BLK, dkv // HEAD
    vcol = (dq + dkv) // dkv
    scale = HEAD ** -0.5

    def body(sink_ref, q_ref, kc, kp, ko, kn, vc, vp, vo, vn, o_ref, lse_ref, lset_ref):
        i = pl.program_id(0)
        mask = _band_mask(i, seq)
        lse = jnp.zeros((BLK, LANE), F32)
        for h in range(nkv):
            hs = slice(h * HEAD, (h + 1) * HEAD)
            k = jnp.concatenate([kc[:, hs], kp[:, hs], ko[:, hs], kn[:, hs]], axis=0)
            v = jnp.concatenate([vc[:, hs], vp[:, hs], vo[:, hs], vn[:, hs]], axis=0)
            q4 = _stack_heads(q_ref, h)
            s = jnp.where(mask, lax.dot_general(q4, k, NT, preferred_element_type=F32) * scale, NEG)
            sk = _sink_col(sink_ref, h)
            m = jnp.maximum(jnp.max(s, axis=-1, keepdims=True), sk)
            e = jnp.exp(s - m)
            den = jnp.sum(e, axis=-1, keepdims=True) + jnp.exp(sk - m)
            o4 = jnp.dot((e * (1.0 / den)).astype(BF16), v, preferred_element_type=F32)
            l4 = m + jnp.log(den)
            for g in range(GROUP):
                hg = h * GROUP + g
                o_ref[:, hg * HEAD:(hg + 1) * HEAD] = o4[g * BLK:(g + 1) * BLK].astype(BF16)
                lse = _put_col(lse, hg, l4[g * BLK:(g + 1) * BLK])
        lse_ref[...] = lse
        lset_ref[...] = lse.T[:nh]

    nh = dq // HEAD
    return pl.pallas_call(
        body, name=name, grid=(nbk,),
        in_specs=[pl.BlockSpec(memory_space=pltpu.SMEM), pl.BlockSpec((BLK, dq), lambda i: (i, 0))]
        + _kv_specs(dkv, 0, nbk) + _kv_specs(dkv, vcol, nbk),
        out_specs=[pl.BlockSpec((BLK, dq), lambda i: (i, 0)), pl.BlockSpec((BLK, LANE), lambda i: (i, 0)),
                   pl.BlockSpec((nh, BLK), lambda i: (0, i))],
        out_shape=[jax.ShapeDtypeStruct((t, dq), BF16), jax.ShapeDtypeStruct((t, LANE), F32),
                   jax.ShapeDtypeStruct((nh, t), F32)],
        compiler_params=_params(("parallel",)))(sink, qr, kr, kr, kr, kr, qkv, qkv, qkv, qkv)


def attn_bwd_q(name, qr, kr, qkv, sink, do, o, lse, seq):
    t, dq = qr.shape
    dkv = kr.shape[1]
    nbk, nkv = t // BLK, dkv // HEAD
    vcol = (dq + dkv) // dkv
    scale = HEAD ** -0.5

    def body(sink_ref, q_ref, kc, kp, ko, kn, vc, vp, vo, vn, do_ref, o_ref, lse_ref,
             dq_ref, dl_ref, dkc_ref, dvc_ref, ds_ref):
        i = pl.program_id(0)

        @pl.when(i == 0)
        def _():
            dkc_ref[...] = jnp.zeros_like(dkc_ref)
            dvc_ref[...] = jnp.zeros_like(dvc_ref)
            ds_ref[...] = jnp.zeros_like(ds_ref)

        mask = _band_mask(i, seq)
        lse = lse_ref[...]
        delta = jnp.zeros((BLK, LANE), F32)
        dsink = jnp.zeros((8, LANE), F32)
        for h in range(nkv):
            hs = slice(h * HEAD, (h + 1) * HEAD)
            k = jnp.concatenate([kc[:, hs], kp[:, hs], ko[:, hs], kn[:, hs]], axis=0)
            v = jnp.concatenate([vc[:, hs], vp[:, hs], vo[:, hs], vn[:, hs]], axis=0)
            q4, do4 = _stack_heads(q_ref, h), _stack_heads(do_ref, h)
            d4 = jnp.sum(do4.astype(F32) * _stack_heads(o_ref, h).astype(F32), axis=-1, keepdims=True)
            l4 = _stack_cols(lse, h)
            s = jnp.where(mask, lax.dot_general(q4, k, NT, preferred_element_type=F32) * scale, NEG)
            p = jnp.exp(s - l4)
            dp = lax.dot_general(do4, v, NT, preferred_element_type=F32)
            dsb = (p * (dp - d4) * scale).astype(BF16)
            pb = p.astype(BF16)
            dq4 = jnp.dot(dsb, k, preferred_element_type=F32)
            dkc_ref[:, hs] += lax.dot_general(dsb[:, :CTX], q4, TN, preferred_element_type=F32)
            dvc_ref[:, hs] += lax.dot_general(pb[:, :CTX], do4, TN, preferred_element_type=F32)
            dsk = -jnp.exp(_sink_col(sink_ref, h) - l4) * d4
            for g in range(GROUP):
                hg = h * GROUP + g
                rs = slice(g * BLK, (g + 1) * BLK)
                dq_ref[:, hg * HEAD:(hg + 1) * HEAD] = dq4[rs]
                delta = _put_col(delta, hg, d4[rs])
                dsink = _put_col(dsink, hg, jnp.sum(dsk[rs], axis=0, keepdims=True))
        dl_ref[...] = delta.T[:nh]
        rows = lax.broadcasted_iota(jnp.int32, (8, LANE), 0)
        ds_ref[...] += jnp.where(rows == 0, dsink, 0.0)

    nh = dq // HEAD
    blk = lambda w: pl.BlockSpec((BLK, w), lambda i: (i, 0))
    const = lambda r, w: pl.BlockSpec((r, w), lambda i: (0, 0))
    return pl.pallas_call(
        body, name=name, grid=(nbk,),
        in_specs=[pl.BlockSpec(memory_space=pltpu.SMEM), blk(dq)] + _kv_specs(dkv, 0, nbk) + _kv_specs(dkv, vcol, nbk)
        + [blk(dq), blk(dq), blk(LANE)],
        out_specs=[blk(dq), pl.BlockSpec((nh, BLK), lambda i: (0, i)), const(CTX, dkv), const(CTX, dkv), const(8, LANE)],
        out_shape=[jax.ShapeDtypeStruct((t, dq), F32), jax.ShapeDtypeStruct((nh, t), F32),
                   jax.ShapeDtypeStruct((CTX, dkv), F32), jax.ShapeDtypeStruct((CTX, dkv), F32),
                   jax.ShapeDtypeStruct((8, LANE), F32)],
        compiler_params=_params(("arbitrary",)))(sink, qr, kr, kr, kr, kr, qkv, qkv, qkv, qkv, do, o, lse)


def attn_bwd_kv(name, qr, kr, qkv, do, lset, deltat, seq):
    t, dq = qr.shape
    dkv = kr.shape[1]
    nbk, nbl, nkv, nh = t // BLK, seq // BLK, dkv // HEAD, dq // HEAD
    cb = CTX // BLK
    vcol = (dq + dkv) // dkv
    scale = HEAD ** -0.5

    def qspec(w, d):
        return pl.BlockSpec((BLK, w), lambda j: (jnp.clip(j + cb + d, cb, nbk - 1), 0))

    def tspec(d):
        return pl.BlockSpec((nh, BLK), lambda j: (0, jnp.clip(j + cb + d, cb, nbk - 1)))

    def stack_rows(v, h):
        return jnp.concatenate([v[h * GROUP + g:h * GROUP + g + 1, :] for g in range(GROUP)], axis=1)

    def body(k_ref, v_ref, *refs):
        dk_ref, dv_ref = refs[-2], refs[-1]
        j = pl.program_id(0)
        krow = lax.broadcasted_iota(jnp.int32, (BLK, GROUP * BLK), 0)
        qcol = lax.broadcasted_iota(jnp.int32, (BLK, GROUP * BLK), 1) % BLK
        for h in range(nkv):
            hs = slice(h * HEAD, (h + 1) * HEAD)
            kh, vh = k_ref[:, hs], v_ref[:, hs]
            dk_h = jnp.zeros((BLK, HEAD), F32)
            dv_h = jnp.zeros((BLK, HEAD), F32)
            for di, d in enumerate((-1, 0, 1)):
                q_ref, do_ref, lse_ref, dl_ref = refs[4 * di:4 * di + 4]
                n = j + d
                msk = (n >= 0) & (n < nbl) & (jnp.abs(d * BLK + qcol - krow) <= WINDOW)
                q4, do4 = _stack_heads(q_ref, h), _stack_heads(do_ref, h)
                l4, d4 = stack_rows(lse_ref[...], h), stack_rows(dl_ref[...], h)
                s = jnp.where(msk, lax.dot_general(kh, q4, NT, preferred_element_type=F32) * scale, NEG)
                p = jnp.exp(s - l4)
                dv_h += jnp.dot(p.astype(BF16), do4, preferred_element_type=F32)
                dp = lax.dot_general(vh, do4, NT, preferred_element_type=F32)
                dk_h += jnp.dot((p * (dp - d4) * scale).astype(BF16), q4, preferred_element_type=F32)
            dk_ref[:, hs] = dk_h
            dv_ref[:, hs] = dv_h

    in_specs = [pl.BlockSpec((BLK, dkv), lambda j: (j + cb, 0)), pl.BlockSpec((BLK, dkv), lambda j: (j + cb, vcol))]
    operands = [kr, qkv]
    for d in (-1, 0, 1):
        in_specs += [qspec(dq, d), qspec(dq, d), tspec(d), tspec(d)]
        operands += [qr, do, lset, deltat]
    return pl.pallas_call(
        body, name=name, grid=(nbl,), in_specs=in_specs,
        out_specs=[pl.BlockSpec((BLK, dkv), lambda j: (j, 0)), pl.BlockSpec((BLK, dkv), lambda j: (j, 0))],
        out_shape=[jax.ShapeDtypeStruct((seq, dkv), F32), jax.ShapeDtypeStruct((seq, dkv), F32)],
        compiler_params=_params(("parallel",)))(*operands)


def ada_fwd(name, cond, w_ada):
    nl, d, n = w_ada.shape
    tn = _pick(n, 1024)

    def body(c_ref, w_ref, out_ref):
        cv = c_ref[...]
        out_ref[...] = jnp.dot((cv * _sigmoid(cv)).astype(BF16), w_ref[...].astype(BF16), preferred_element_type=F32)

    return pl.pallas_call(
        body, name=name, grid=(nl, n // tn),
        in_specs=[pl.BlockSpec((16, d), lambda l, j: (0, 0)), pl.BlockSpec((None, d, tn), lambda l, j: (l, 0, j))],
        out_specs=pl.BlockSpec((None, 16, tn), lambda l, j: (l, 0, j)),
        out_shape=jax.ShapeDtypeStruct((nl, 16, n), F32), compiler_params=_params(("parallel", "parallel")))(cond, w_ada)


def ada_bwd_cond(name, dsum, w_ada):
    nl, d, n = w_ada.shape
    tn = _pick(n, 1024)

    def body(g_ref, w_ref, out_ref):
        @pl.when((pl.program_id(0) == 0) & (pl.program_id(1) == 0))
        def _():
            out_ref[...] = jnp.zeros_like(out_ref)

        out_ref[...] += lax.dot_general(g_ref[...].astype(BF16), w_ref[...].astype(BF16), NT, preferred_element_type=F32)

    return pl.pallas_call(
        body, name=name, grid=(nl, n // tn),
        in_specs=[pl.BlockSpec((None, 8, tn), lambda l, j: (l, 0, j)), pl.BlockSpec((None, d, tn), lambda l, j: (l, 0, j))],
        out_specs=pl.BlockSpec((8, d), lambda l, j: (0, 0)),
        out_shape=jax.ShapeDtypeStruct((8, d), F32), compiler_params=_params(("arbitrary", "arbitrary")))(dsum, w_ada)


def ada_grad_w(name, cond, rhs):
    nl, _, n = rhs.shape
    d = cond.shape[1]
    tr, tn = _pick(d, 512), _pick(n, 1024)

    def body(c_ref, r_ref, out_ref):
        cv = c_ref[...]
        out_ref[...] = lax.dot_general((cv * _sigmoid(cv)).astype(BF16), r_ref[...].astype(BF16), TN, preferred_element_type=F32)

    return pl.pallas_call(
        body, name=name, grid=(nl, d // tr, n // tn),
        in_specs=[pl.BlockSpec((16, tr), lambda l, i, j: (0, i)), pl.BlockSpec((None, 16, tn), lambda l, i, j: (l, 0, j))],
        out_specs=pl.BlockSpec((None, tr, tn), lambda l, i, j: (l, i, j)),
        out_shape=jax.ShapeDtypeStruct((nl, d, n), F32),
        compiler_params=_params(("parallel", "parallel", "parallel")))(cond, rhs)


def adamw(name, g, g_spec, w, m, v, tr):
    nl, r, c = w.shape
    spec = pl.BlockSpec((None, tr, c), lambda l, i: (l, i, 0))

    def body(g_ref, w_ref, m_ref, v_ref, go_ref, d_ref, mo_ref, vo_ref):
        gv = g_ref[...]
        mn = B1 * m_ref[...] + (1.0 - B1) * gv
        vn = B2 * v_ref[...] + (1.0 - B2) * (gv * gv)
        m_hat = mn / (1.0 - B1 ** STEP)
        v_hat = vn / (1.0 - B2 ** STEP)
        go_ref[...] = gv
        d_ref[...] = -LR * (m_hat / (jnp.sqrt(v_hat) + ADAM_EPS) + WD * w_ref[...])
        mo_ref[...] = mn
        vo_ref[...] = vn

    return pl.pallas_call(
        body, name=name, grid=(nl, r // tr), in_specs=[g_spec, spec, spec, spec], out_specs=[spec] * 4,
        out_shape=[jax.ShapeDtypeStruct(w.shape, F32)] * 4, compiler_params=_params(("parallel", "parallel")))(g, w, m, v)


def adamw_ada(name, cond, rhs, w, m, v, tr):
    nl, r, c = w.shape
    spec = pl.BlockSpec((None, tr, c), lambda l, i: (l, i, 0))

    def body(c_ref, r_ref, w_ref, m_ref, v_ref, go_ref, d_ref, mo_ref, vo_ref):
        cv = c_ref[...]
        gv = lax.dot_general((cv * _sigmoid(cv)).astype(BF16), r_ref[...].astype(BF16), TN, preferred_element_type=F32)
        mn = B1 * m_ref[...] + (1.0 - B1) * gv
        vn = B2 * v_ref[...] + (1.0 - B2) * (gv * gv)
        m_hat = mn / (1.0 - B1 ** STEP)
        v_hat = vn / (1.0 - B2 ** STEP)
        go_ref[...] = gv
        d_ref[...] = -LR * (m_hat / (jnp.sqrt(v_hat) + ADAM_EPS) + WD * w_ref[...])
        mo_ref[...] = mn
        vo_ref[...] = vn

    return pl.pallas_call(
        body, name=name, grid=(nl, r // tr),
        in_specs=[pl.BlockSpec((16, tr), lambda l, i: (0, i)), pl.BlockSpec((None, 16, c), lambda l, i: (l, 0, 0)),
                  spec, spec, spec],
        out_specs=[spec] * 4, out_shape=[jax.ShapeDtypeStruct(w.shape, F32)] * 4,
        compiler_params=_params(("parallel", "parallel")))(cond, rhs, w, m, v)


def adamw_small(name, g, w, m, v):
    shape = w.shape
    r3 = lambda a: a.reshape(1, -1, shape[-1]).astype(F32)
    rows = r3(w).shape[1]
    outs = adamw(name, r3(g), pl.BlockSpec((None, rows, shape[-1]), lambda l, i: (l, i, 0)), r3(w), r3(m), r3(v), rows)
    return [o.reshape(shape) for o in outs]


def _place():
    x, y, c = lax.axis_index("x"), lax.axis_index("y"), lax.axis_index("c")
    return x, y, c, [(1 - x, y), (x, 1 - y), (1 - x, 1 - y)]


def small_allgather(name, v):
    r, w = v.shape

    def body(x_ref, out_ref, send_sems, recv_sems, local_sem):
        x, y, c, chips = _place()
        me, sibling = (x, y, c), (x, y, 1 - c)

        def slot(px, py, pc):
            return out_ref.at[4 * px + 2 * py + pc]

        def copy(k, block, to, src=None):
            return pltpu.make_async_remote_copy(
                src_ref=slot(*block) if src is None else src, dst_ref=slot(*block),
                send_sem=send_sems.at[k], recv_sem=recv_sems.at[k], device_id=to, device_id_type=MESH)

        mine = pltpu.make_async_copy(x_ref, slot(*me), local_sem)
        mine.start()
        first = [copy(0, me, sibling, src=x_ref)]
        first += [copy(1 + j, me, (*chip, c), src=x_ref) for j, chip in enumerate(chips)]
        for cp in first:
            cp.start()
        passed = [copy(4 + j, (*chip, c), sibling) for j, chip in enumerate(chips)]
        for j, chip in enumerate(chips):
            copy(1 + j, (*chip, c), me).wait_recv()
            passed[j].start()
        copy(0, sibling, me).wait_recv()
        for j, chip in enumerate(chips):
            copy(4 + j, (*chip, 1 - c), me).wait_recv()
        for cp in first + passed:
            cp.wait_send()
        mine.wait()

    return pl.pallas_call(
        body, name=name, out_shape=jax.ShapeDtypeStruct((8, r, w), v.dtype),
        in_specs=[pl.BlockSpec(memory_space=pltpu.VMEM)], out_specs=pl.BlockSpec(memory_space=pltpu.VMEM),
        scratch_shapes=[pltpu.SemaphoreType.DMA((7,)), pltpu.SemaphoreType.DMA((7,)), pltpu.SemaphoreType.DMA],
        compiler_params=pltpu.CompilerParams(vmem_limit_bytes=VMEM_LIMIT))(v)


def gather_flat(name, parts):
    flat = jnp.concatenate([p.reshape(-1).astype(F32) for p in parts])
    n = flat.shape[0]
    rows = _cdiv(n, MXU * LANE) * MXU
    flat = jnp.pad(flat, (0, rows * LANE - n))
    return small_allgather(name, flat.reshape(rows, LANE)).reshape(8, rows * LANE)


def sum8(name, g):
    p = g.shape[1]
    g3 = g.reshape(8, p // LANE, LANE)
    tr = _pick(p // LANE, 1024, MXU)

    def body(g_ref, out_ref):
        acc = g_ref[0]
        for k in range(1, 8):
            acc = acc + g_ref[k]
        out_ref[...] = acc

    return pl.pallas_call(
        body, name=name, grid=(p // LANE // tr,),
        in_specs=[pl.BlockSpec((8, tr, LANE), lambda i: (0, i, 0))], out_specs=pl.BlockSpec((tr, LANE), lambda i: (i, 0)),
        out_shape=jax.ShapeDtypeStruct((p // LANE, LANE), F32), compiler_params=_params(("parallel",)))(g3).reshape(p)


HBM_SPEC = pl.BlockSpec(memory_space=pltpu.HBM)


def _half(ref, lead, c, axis):
    h = ref.shape[axis] // 2
    return ref.at[lead, pl.ds(c * h, h), :] if axis == 1 else ref.at[lead, :, pl.ds(c * h, h)]


def _gather_ops(outs, axes, send_sems, recv_sems):
    x, y, c, chips = _place()
    me, sibling = (x, y, c), (x, y, 1 - c)

    def copy(a, k, chip, pc, to):
        blk = _half(outs[a], 2 * chip[0] + chip[1], pc, axes[a])
        return pltpu.make_async_remote_copy(src_ref=blk, dst_ref=blk, send_sem=send_sems.at[6 * a + k],
                                            recv_sem=recv_sems.at[6 * a + k], device_id=to, device_id_type=MESH)

    def start():
        for a in range(len(outs)):
            for j, chip in enumerate(chips):
                copy(a, j, (x, y), c, (*chip, c)).start()

    def finish():
        for a in range(len(outs)):
            for j, chip in enumerate(chips):
                copy(a, j, chip, c, me).wait_recv()
                copy(a, 3 + j, chip, c, sibling).start()
        for a in range(len(outs)):
            for j, chip in enumerate(chips):
                copy(a, 3 + j, chip, 1 - c, me).wait_recv()
            for j, chip in enumerate(chips):
                copy(a, j, (x, y), c, (*chip, c)).wait_send()
                copy(a, 3 + j, chip, c, sibling).wait_send()

    return start, finish


def gather_comm(bufs, axes):
    return dict(ins=list(bufs), out_shape=[jax.ShapeDtypeStruct(b.shape, b.dtype) for b in bufs],
                aliases={a: a for a in range(len(bufs))}, n_sems=6 * len(bufs),
                ops=lambda cin, couts, ss, rs: _gather_ops(couts, axes, ss, rs))


def gather_weights(name, bufs, axes):
    n = len(bufs)

    def body(*refs):
        start, finish = _gather_ops(refs[n:2 * n], axes, refs[2 * n], refs[2 * n + 1])
        start()
        finish()

    return pl.pallas_call(
        body, name=name, out_shape=[jax.ShapeDtypeStruct(b.shape, b.dtype) for b in bufs],
        in_specs=[HBM_SPEC] * n, out_specs=[HBM_SPEC] * n, input_output_aliases={a: a for a in range(n)},
        scratch_shapes=[pltpu.SemaphoreType.DMA((6 * n,)), pltpu.SemaphoreType.DMA((6 * n,))])(*bufs)


def _scatter_ops(ins, outs, send_sems, recv_sems):
    x, y, c, chips = _place()
    me = 2 * x + y

    def copy(a, j, chip):
        return pltpu.make_async_remote_copy(
            src_ref=ins[a].at[2 * chip[0] + chip[1]], dst_ref=outs[a].at[me], send_sem=send_sems.at[3 * a + j],
            recv_sem=recv_sems.at[3 * a + j], device_id=(*chip, c), device_id_type=MESH)

    def start():
        for a in range(len(ins)):
            for j, chip in enumerate(chips):
                copy(a, j, chip).start()

    def finish():
        for a in range(len(ins)):
            for j, chip in enumerate(chips):
                copy(a, j, chip).wait()

    return start, finish


def scatter_comm(bufs):
    return dict(ins=list(bufs), out_shape=[jax.ShapeDtypeStruct(b.shape, b.dtype) for b in bufs], aliases={},
                n_sems=3 * len(bufs), ops=_scatter_ops)


def chip_scatter(name, bufs):
    n = len(bufs)

    def body(*refs):
        start, finish = _scatter_ops(refs[:n], refs[n:2 * n], refs[2 * n], refs[2 * n + 1])
        start()
        finish()

    return pl.pallas_call(
        body, name=name, out_shape=[jax.ShapeDtypeStruct(b.shape, b.dtype) for b in bufs],
        in_specs=[HBM_SPEC] * n, out_specs=[HBM_SPEC] * n,
        scratch_shapes=[pltpu.SemaphoreType.DMA((3 * n,)), pltpu.SemaphoreType.DMA((3 * n,))])(*bufs)


def pair_exchange(name, bufs, axes):
    n = len(bufs)

    def body(*refs):
        ins, outs, (send_sems, recv_sems) = refs[:n], refs[n:2 * n], refs[2 * n:]
        x, y, c, _ = _place()
        cps = []
        for a, (src, out) in enumerate(zip(ins, outs)):
            cp = pltpu.make_async_remote_copy(
                src_ref=_half(src, slice(None), 1 - c, axes[a]), dst_ref=out, send_sem=send_sems.at[a],
                recv_sem=recv_sems.at[a], device_id=(x, y, 1 - c), device_id_type=MESH)
            cp.start()
            cps.append(cp)
        for cp in cps:
            cp.wait()

    def halved(b, axis):
        shape = list(b.shape)
        shape[axis] //= 2
        return jax.ShapeDtypeStruct(tuple(shape), b.dtype)

    return pl.pallas_call(
        body, name=name, out_shape=[halved(b, ax) for b, ax in zip(bufs, axes)],
        in_specs=[HBM_SPEC] * n, out_specs=[HBM_SPEC] * n,
        scratch_shapes=[pltpu.SemaphoreType.DMA((n,)), pltpu.SemaphoreType.DMA((n,))])(*bufs)


def pair_add(name, buf, got, cidx, axis):
    s, r, c = got.shape
    tr = _pick(r, max(HALO, (4 * 1024 * 1024) // (2 * c)), HALO)
    per = r // tr
    if axis == 1:
        mine = pl.BlockSpec((None, tr, c), lambda k, i, cr: (k, cr[0] * per + i, 0))
    else:
        mine = pl.BlockSpec((None, tr, c), lambda k, i, cr: (k, i, cr[0]))

    def body(c_ref, a_ref, b_ref, out_ref):
        out_ref[...] = (a_ref[...].astype(F32) + b_ref[...].astype(F32)).astype(BF16)

    return pl.pallas_call(
        body, name=name,
        grid_spec=pltpu.PrefetchScalarGridSpec(
            num_scalar_prefetch=1, grid=(s, per),
            in_specs=[mine, pl.BlockSpec((None, tr, c), lambda k, i, cr: (k, i, 0))],
            out_specs=pl.BlockSpec((None, tr, c), lambda k, i, cr: (k, i, 0))),
        out_shape=jax.ShapeDtypeStruct((s, r, c), BF16),
        compiler_params=_params(("parallel", "parallel")))(cidx, buf, got)


def chip_add(name, own, got, place, dst, l, off, size, col):
    s = got.shape[0]
    if col:
        h, n = got.shape[1], size
        tr = _pick(h, max(HALO, (2 * 1024 * 1024) // (2 * n)), HALO)
        per, ob = h // tr, off // n
        own_spec = pl.BlockSpec((None, tr, n), lambda i, p: (p[0], i, ob))
        got_spec = pl.BlockSpec((s, tr, n), lambda i, p: (0, i, ob))
        out_spec = pl.BlockSpec((None, tr, n), lambda i, p: (l, p[1] * per + i, 0))
        grid = (per,)
    else:
        n = got.shape[2]
        tr = _pick(size, max(HALO, (2 * 1024 * 1024) // (2 * n)), HALO)
        ob = off // tr
        own_spec = pl.BlockSpec((None, tr, n), lambda i, p: (p[0], ob + i, 0))
        got_spec = pl.BlockSpec((s, tr, n), lambda i, p: (0, ob + i, 0))
        out_spec = pl.BlockSpec((None, tr, n), lambda i, p: (l, i, p[1]))
        grid = (size // tr,)

    def body(p_ref, own_ref, g_ref, dst_ref, out_ref):
        acc = jnp.zeros((tr, n), F32)
        for k in range(s):
            acc = acc + jnp.where(p_ref[0] == k, own_ref[...], g_ref[k]).astype(F32)
        out_ref[...] = acc

    return pl.pallas_call(
        body, name=name,
        grid_spec=pltpu.PrefetchScalarGridSpec(
            num_scalar_prefetch=1, grid=grid,
            in_specs=[own_spec, got_spec, pl.BlockSpec(memory_space=pl.ANY)], out_specs=out_spec),
        out_shape=jax.ShapeDtypeStruct(dst.shape, F32), input_output_aliases={3: 0},
        compiler_params=_params(("parallel",)))(place, own, got, dst)


def pair_join(name, bufs, axes):
    n = len(bufs)

    def body(*refs):
        outs = refs[n:2 * n]
        send_sems, recv_sems = refs[2 * n:]
        x, y, c, _ = _place()
        started = []
        for a, out in enumerate(outs):
            blk = _half(out, slice(None), c, axes[a])
            cp = pltpu.make_async_remote_copy(src_ref=blk, dst_ref=blk, send_sem=send_sems.at[a], recv_sem=recv_sems.at[a],
                                              device_id=(x, y, 1 - c), device_id_type=MESH)
            cp.start()
            started.append(cp)
        for cp in started:
            cp.wait()

    return pl.pallas_call(
        body, name=name, out_shape=[jax.ShapeDtypeStruct(b.shape, b.dtype) for b in bufs],
        in_specs=[HBM_SPEC] * n, out_specs=[HBM_SPEC] * n, input_output_aliases={a: a for a in range(n)},
        scratch_shapes=[pltpu.SemaphoreType.DMA((n,)), pltpu.SemaphoreType.DMA((n,))])(*bufs)


def _rope_tables(seq):
    rows = seq // GRID_W
    row = jnp.repeat(jnp.arange(rows), GRID_W).astype(F32)
    col = jnp.tile(jnp.arange(GRID_W), rows).astype(F32)
    pairs = HEAD // 4
    inv = ROPE_BASE ** (-jnp.arange(pairs, dtype=F32) / pairs)
    ang = jnp.stack([row[:, None] * inv, col[:, None] * inv], axis=1)
    ang = jnp.broadcast_to(ang[:, :, None, :], (seq, 2, 2, pairs)).reshape(seq, HEAD)
    cos = jnp.concatenate([jnp.ones((CTX, HEAD), F32), jnp.cos(ang)], axis=0)
    sin = jnp.concatenate([jnp.zeros((CTX, HEAD), F32), jnp.sin(ang)], axis=0)
    return cos, sin


def _pad8(a):
    return jnp.pad(a, ((0, 8 - a.shape[0]), (0, 0)))


def kernel(x, c, ctx, c_ctx, w_ada, b_ada, attn_w_qkv, attn_w_o, attn_q_gain, attn_k_gain, attn_sink, sc_w_in, sc_conv, sc_w_out, ffn_w_up, ffn_conv, ffn_conv_b, ffn_w_down, loss_target, m_c_ctx, m_w_ada, m_b_ada, m_attn_w_qkv, m_attn_w_o, m_attn_q_gain, m_attn_k_gain, m_attn_sink, m_sc_w_in, m_sc_conv, m_sc_w_out, m_ffn_w_up, m_ffn_conv, m_ffn_conv_b, m_ffn_w_down, v_c_ctx, v_w_ada, v_b_ada, v_attn_w_qkv, v_attn_w_o, v_attn_q_gain, v_attn_k_gain, v_attn_sink, v_sc_w_in, v_sc_conv, v_sc_w_out, v_ffn_w_up, v_ffn_conv, v_ffn_conv_b, v_ffn_w_down):
    seq, d = x.shape[1], x.shape[2]
    depth, nada = w_ada.shape[0], w_ada.shape[2]
    n_attn, n_conv = attn_w_qkv.shape[0], sc_w_in.shape[0]
    ff = ffn_conv_b.shape[1]
    dq, dkv = d, d // GROUP
    nqkv = dq + 2 * dkv
    nh = dq // HEAD
    assert ctx.shape[1] == CTX and seq % RB == 0 and 6 * d == 4 * nada
    lay = Layout(d, ff, nqkv, depth)
    ax, ay, ac = lax.axis_index("x"), lax.axis_index("y"), lax.axis_index("c")
    chip, dev = 2 * ax + ay, 4 * ax + 2 * ay + ac
    cidx = jnp.reshape(ac, (1,)).astype(jnp.int32)

    chip1 = jnp.reshape(chip, (1,)).astype(jnp.int32)
    wcs, wrs = [], []
    for l in range(depth):
        j, is_attn = l // N_MIX, l % N_MIX == 0
        wc_l, wr_l = lax.empty((4, d, lay.ct[l]), BF16), lax.empty((4, lay.rt, d), BF16)
        wc_l = cast_pack(f"pack_up_{l}", ffn_w_up, l, wc_l, 0, True, chip1)
        wc_l = cast_pack(f"pack_mix_{l}", attn_w_qkv if is_attn else sc_w_in, j, wc_l, lay.mix[l], True, chip1)
        wr_l = cast_pack(f"pack_down_{l}", ffn_w_down, l, wr_l, 0, False, chip1)
        wr_l = cast_pack(f"pack_out_{l}", attn_w_o if is_attn else sc_w_out, j, wr_l, lay.out, False, chip1)
        wcs.append(wc_l)
        wrs.append(wr_l)
    (wcs[0],) = gather_weights("gather_w0", [wcs[0]], (1,))

    g1 = gather_flat("gather_cond", [c, sc_conv, ffn_conv])
    c_all = g1[:, :d]
    o1 = d + sc_conv.size
    sc_conv_full = jnp.concatenate([g1[2 * s, d:o1].reshape(sc_conv.shape) for s in range(4)], axis=-1)
    ffn_conv_full = jnp.concatenate([g1[2 * s, o1:o1 + ffn_conv.size].reshape(ffn_conv.shape) for s in range(4)], axis=-1)
    cond = jnp.concatenate([c_all, c_ctx[None, :], jnp.zeros((7, d), F32)], axis=0)
    ada_part = ada_fwd("ada_fwd", cond, w_ada)
    g2 = gather_flat("gather_ada", [ada_part])
    ada_all = jnp.concatenate([g2[2 * s, :ada_part.size].reshape(ada_part.shape) for s in range(4)], axis=-1)
    ada_own = jnp.stack([lax.dynamic_index_in_dim(ada_all, 8, 1, False),
                         lax.dynamic_index_in_dim(ada_all, dev, 1, False)], axis=1) + b_ada[:, None, :]
    mods = jnp.pad(ada_own.reshape(depth, 2, 6, d), ((0, 0), (0, 0), (0, 2), (0, 0)))

    cos, sin = _rope_tables(seq)
    xa = jnp.concatenate([ctx[0], x[0]], axis=0)
    cws = [_pad8(sc_conv_full[j]) for j in range(n_conv)]
    cwf = [_pad8(jnp.concatenate([ffn_conv_full[l], ffn_conv_b[l][None, :]], axis=0)) for l in range(depth)]
    gains = [_pad8(jnp.stack([attn_q_gain[j], attn_k_gain[j]])) for j in range(n_attn)]

    saved = []
    for l in range(depth):
        j, is_attn, mod = l // N_MIX, l % N_MIX == 0, mods[l]
        wc, wr, last = wcs[l], wrs[l], l == depth - 1
        sv = {"x_in": xa}
        h1 = norm_mod(f"norm1_{l}", xa, mod, 0, 1)
        if is_attn:
            if l == 0:
                qkv, (wr,) = mm_nn_col(f"qkv_{l}", h1, wc, lay.mix[l], lay.nb_mix[l], comm=gather_comm([wr], (2,)))
                wrs[0] = wr
            else:
                qkv = mm_nn_col(f"qkv_{l}", h1, wc, lay.mix[l], lay.nb_mix[l])
            qr, kr = rope_fwd(f"rope_{l}", qkv, cos, sin, gains[j], dq, dkv)
            o, lse, lset = attn_fwd(f"attn_{l}", qr, kr, qkv, attn_sink[j], seq)
            xa, y_m = mm_nn_row(f"wo_{l}", o, wr, lay.out, lay.kb_o, xa, mod, 2)
            sv.update(qkv=qkv, qr=qr, kr=kr, o=o, lse=lse, lset=lset)
        else:
            u = mm_nn_col(f"scin_{l}", h1, wc, lay.mix[l], lay.nb_mix[l])
            z = sc_gate_fwd(f"scgate_{l}", u, cws[j])
            xa, y_m = mm_nn_row(f"scout_{l}", z, wr, lay.out, lay.kb_o, xa, mod, 2)
            sv.update(u=u, z=z)
        h2 = norm_mod(f"norm2_{l}", xa, mod, 3, 4)
        if last:
            up = mm_nn_col(f"up_{l}", h2, wc, 0, lay.nb_up)
        else:
            up, (wcs[l + 1],) = mm_nn_col(f"up_{l}", h2, wc, 0, lay.nb_up, comm=gather_comm([wcs[l + 1]], (1,)))
        act = ffn_act_fwd(f"act_{l}", up, cwf[l])
        sv.update(h1=h1, y_m=y_m, x_mid=xa, h2=h2, up=up, act=act)
        if last:
            xa, y_f = mm_nn_row(f"down_{l}", act, wr, 0, lay.kb_dn, xa, mod, 5)
        else:
            (xa, y_f), (wrs[l + 1],) = mm_nn_row(f"down_{l}", act, wr, 0, lay.kb_dn, xa, mod, 5,
                                                 comm=gather_comm([wrs[l + 1]], (2,)))
        sv["y_f"] = y_f
        saved.append(sv)

    dx, dy, acc, lp = loss_head("loss", xa, loss_target[0], saved[-1]["y_f"], mods[-1], 5)
    loss = lax.psum(0.5 * jnp.sum(lp[0]) / d, ("x", "y", "c"))
    d_mod = [jnp.zeros((2, 6, d), F32) for _ in range(depth)]
    place = jnp.stack([chip, ac]).astype(jnp.int32)
    gf = {"up": lax.empty(ffn_w_up.shape, F32), "down": lax.empty(ffn_w_down.shape, F32),
          "qkv": lax.empty(attn_w_qkv.shape, F32), "wo": lax.empty(attn_w_o.shape, F32),
          "scin": lax.empty(sc_w_in.shape, F32), "scout": lax.empty(sc_w_out.shape, F32)}

    def chip_adds(l, hc, hr, rb_c, rb_r):
        j, mix = l // N_MIX, ("qkv", "wo") if l % N_MIX == 0 else ("scin", "scout")
        gf["up"] = chip_add(f"sum_up_{l}", hc, rb_c, place, gf["up"], l, 0, lay.nb_up, True)
        gf[mix[0]] = chip_add(f"sum_mix_{l}", hc, rb_c, place, gf[mix[0]], j, lay.mix[l], lay.nb_mix[l], True)
        gf["down"] = chip_add(f"sum_down_{l}", hr, rb_r, place, gf["down"], l, 0, lay.kb_dn, False)
        gf[mix[1]] = chip_add(f"sum_out_{l}", hr, rb_r, place, gf[mix[1]], j, lay.out, lay.kb_o, False)

    pending = None

    def add_mod(l, acc, idx):
        upd = jnp.zeros((2, 6, d), F32)
        for row, k in idx:
            upd = upd.at[:, k, :].set(acc[:, row, :])
        d_mod[l] = d_mod[l] + upd

    add_mod(depth - 1, acc, [(2, 5)])
    d_conv_f, d_conv_s = [None] * depth, [None] * n_conv
    d_gq, d_gk, d_sink = [None] * n_attn, [None] * n_attn, [None] * n_attn
    for l in reversed(range(depth)):
        j, is_attn, sv = l // N_MIX, l % N_MIX == 0, saved[l]
        wc, wr = wcs[l], wrs[l]
        gc, gr = lax.empty((4, d, lay.ct[l]), BF16), lax.empty((4, lay.rt, d), BF16)
        if pending is None:
            gr = mm_tn_row(f"g_down_{l}", sv["act"], dy, gr, 0, lay.kb_dn)
        else:
            gr, (rb_r,) = mm_tn_row(f"g_down_{l}", sv["act"], dy, gr, 0, lay.kb_dn, comm=scatter_comm([pending[2]]))
        da = mm_nt_row(f"d_act_{l}", dy, wr, 0, lay.kb_dn)
        d_up, d_conv_f[l] = ffn_act_bwd(f"act_bwd_{l}", sv["up"], da, cwf[l])
        if pending is None:
            gc = mm_tn_col(f"g_up_{l}", sv["h2"], d_up, gc, 0, lay.nb_up)
        else:
            gc, (rb_c,) = mm_tn_col(f"g_up_{l}", sv["h2"], d_up, gc, 0, lay.nb_up, comm=scatter_comm([pending[1]]))
            chip_adds(pending[0], pending[1], pending[2], rb_c, rb_r)
        dh2 = mm_nt_col(f"d_h2_{l}", d_up, wc, 0, lay.nb_up)
        dx, dy, acc = resid_bwd(f"norm2_bwd_{l}", dx, dh2, sv["x_mid"], mods[l], 3, 4, sv["y_m"], mods[l], 2)
        add_mod(l, acc, [(0, 3), (1, 4), (2, 2)])
        if is_attn:
            gr = mm_tn_row(f"g_wo_{l}", sv["o"], dy, gr, lay.out, lay.kb_o)
            if l == 0:
                (ra_r,) = pair_exchange("pair_exchange_r_0", [gr], (2,))
                hr0 = pair_add("pair_add_r_0", gr, ra_r, cidx, 2)
            do = mm_nt_row(f"d_o_{l}", dy, wr, lay.out, lay.kb_o)
            dqr, deltat, dkc, dvc, dsk = attn_bwd_q(f"attn_bwd_q_{l}", sv["qr"], sv["kr"], sv["qkv"], attn_sink[j],
                                                    do, sv["o"], sv["lse"], seq)
            dkl, dvl = attn_bwd_kv(f"attn_bwd_kv_{l}", sv["qr"], sv["kr"], sv["qkv"], do, sv["lset"], deltat, seq)
            dqkv, dgn = rope_bwd(f"rope_bwd_{l}", sv["qkv"], dqr, jnp.concatenate([dkc, dkl], axis=0),
                                 jnp.concatenate([dvc, dvl], axis=0), cos, sin, gains[j])
            d_gq[j], d_gk[j], d_sink[j] = dgn[0], dgn[1], dsk[0, :nh]
            if l == 0:
                gc, (rb_r0,) = mm_tn_col(f"g_qkv_{l}", sv["h1"], dqkv, gc, lay.mix[l], lay.nb_mix[l],
                                         comm=scatter_comm([hr0]))
                (ra_c,) = pair_exchange("pair_exchange_c_0", [gc], (1,))
                hc0 = pair_add("pair_add_c_0", gc, ra_c, cidx, 1)
                dh1, (rb_c0,) = mm_nt_col(f"d_h1_{l}", dqkv, wc, lay.mix[l], lay.nb_mix[l], comm=scatter_comm([hc0]))
                chip_adds(0, hc0, hr0, rb_c0, rb_r0)
            else:
                gc = mm_tn_col(f"g_qkv_{l}", sv["h1"], dqkv, gc, lay.mix[l], lay.nb_mix[l])
                dh1 = mm_nt_col(f"d_h1_{l}", dqkv, wc, lay.mix[l], lay.nb_mix[l])
        else:
            gr = mm_tn_row(f"g_scout_{l}", sv["z"], dy, gr, lay.out, lay.kb_o)
            dz = mm_nt_row(f"d_z_{l}", dy, wr, lay.out, lay.kb_o)
            du, dcw = sc_gate_bwd(f"scgate_bwd_{l}", sv["u"], dz, cws[j])
            d_conv_s[j] = dcw[:3]
            gc = mm_tn_col(f"g_scin_{l}", sv["h1"], du, gc, lay.mix[l], lay.nb_mix[l])
            dh1 = mm_nt_col(f"d_h1_{l}", du, wc, lay.mix[l], lay.nb_mix[l])
        if l > 0:
            dx, dy, acc = resid_bwd(f"norm1_bwd_{l}", dx, dh1, sv["x_in"], mods[l], 0, 1, saved[l - 1]["y_f"], mods[l - 1], 5)
            add_mod(l - 1, acc, [(2, 5)])
        else:
            dx, acc = resid_bwd(f"norm1_bwd_{l}", dx, dh1, sv["x_in"], mods[l], 0, 1)
        add_mod(l, acc, [(0, 0), (1, 1)])
        if l > 0:
            ra_c, ra_r = pair_exchange(f"pair_exchange_{l}", [gc, gr], (1, 2))
            pending = (l, pair_add(f"pair_add_c_{l}", gc, ra_c, cidx, 1), pair_add(f"pair_add_r_{l}", gr, ra_r, cidx, 2))
    grad_x = dx[CTX:][None]
    order = ["up", "down", "qkv", "wo", "scin", "scout"]
    joined = pair_join("pair_join", [gf[k] for k in order], (1, 2, 1, 2, 1, 2))
    gf = dict(zip(order, joined))

    d_ada = jnp.stack(d_mod).reshape(depth, 2, 6 * d)
    small = [d_ada, jnp.stack(d_gq), jnp.stack(d_gk), jnp.stack(d_sink), jnp.stack(d_conv_s),
             jnp.stack([t[:3] for t in d_conv_f]), jnp.stack([t[3] for t in d_conv_f])]
    g3 = gather_flat("gather_small", small)
    tot = sum8("sum_small", g3)
    sizes = [s.size for s in small]
    offs = [sum(sizes[:k]) for k in range(len(sizes) + 1)]
    part = lambda k: tot[offs[k]:offs[k + 1]].reshape(small[k].shape)
    g_b_ada = part(0)[:, 0] + part(0)[:, 1]
    g_q_gain, g_k_gain, g_sink = part(1), part(2), part(3)
    g_sc_conv = lax.dynamic_slice_in_dim(part(4), chip * sc_conv.shape[2], sc_conv.shape[2], 2)
    g_ffn_conv = lax.dynamic_slice_in_dim(part(5), chip * ffn_conv.shape[2], ffn_conv.shape[2], 2)
    g_conv_b = part(6)

    d_ada_all = g3[:, :d_ada.size].reshape(8, depth, 2, 6 * d)
    cols = lambda a: lax.dynamic_slice_in_dim(a, chip * nada, nada, a.ndim - 1)
    d_lat = cols(jnp.moveaxis(d_ada_all[:, :, 1], 0, 1))
    d_ctx = cols(part(0)[:, 0])
    rhs = jnp.concatenate([d_lat, d_ctx[:, None], jnp.zeros((depth, 7, nada), F32)], axis=1)
    dcc = ada_bwd_cond("ada_bwd_cond", jnp.pad(d_ctx[:, None], ((0, 0), (0, 7), (0, 0))), w_ada)[0]
    g4 = gather_flat("gather_dcc", [dcc])
    d_silu = g4[0, :d] + g4[2, :d] + g4[4, :d] + g4[6, :d]
    sg = _sigmoid(c_ctx)
    g_c_ctx = d_silu * (sg * (1.0 + c_ctx * (1.0 - sg)))

    def adam_rows(k, n):
        return _pick(k, max(8, ADAM_TILE_ELEMS // n), 8)

    def big(name, g, w, m, v):
        _, k, n = w.shape
        tr = adam_rows(k, n)
        return adamw(name, g, pl.BlockSpec((None, tr, n), lambda l, i: (l, i, 0)), w, m, v, tr)

    ada_tr = adam_rows(d, nada)
    res = {
        "c_ctx": adamw_small("adam_c_ctx", g_c_ctx, c_ctx, m_c_ctx, v_c_ctx),
        "w_ada": adamw_ada("adam_w_ada", cond, rhs, w_ada, m_w_ada, v_w_ada, ada_tr),
        "b_ada": adamw_small("adam_b_ada", g_b_ada, b_ada, m_b_ada, v_b_ada),
        "attn_w_qkv": big("adam_qkv", gf["qkv"], attn_w_qkv, m_attn_w_qkv, v_attn_w_qkv),
        "attn_w_o": big("adam_wo", gf["wo"], attn_w_o, m_attn_w_o, v_attn_w_o),
        "attn_q_gain": adamw_small("adam_q_gain", g_q_gain, attn_q_gain, m_attn_q_gain, v_attn_q_gain),
        "attn_k_gain": adamw_small("adam_k_gain", g_k_gain, attn_k_gain, m_attn_k_gain, v_attn_k_gain),
        "attn_sink": adamw_small("adam_sink", g_sink, attn_sink, m_attn_sink, v_attn_sink),
        "sc_w_in": big("adam_scin", gf["scin"], sc_w_in, m_sc_w_in, v_sc_w_in),
        "sc_conv": adamw_small("adam_sc_conv", g_sc_conv, sc_conv, m_sc_conv, v_sc_conv),
        "sc_w_out": big("adam_scout", gf["scout"], sc_w_out, m_sc_w_out, v_sc_w_out),
        "ffn_w_up": big("adam_up", gf["up"], ffn_w_up, m_ffn_w_up, v_ffn_w_up),
        "ffn_conv": adamw_small("adam_ffn_conv", g_ffn_conv, ffn_conv, m_ffn_conv, v_ffn_conv),
        "ffn_conv_b": adamw_small("adam_conv_b", g_conv_b, ffn_conv_b, m_ffn_conv_b, v_ffn_conv_b),
        "ffn_w_down": big("adam_down", gf["down"], ffn_w_down, m_ffn_w_down, v_ffn_w_down),
    }
    names = list(res)
    return (loss, grad_x, *[res[n][0] for n in names], *[res[n][1] for n in names],
            *[res[n][2] for n in names], *[res[n][3] for n in names])
```
